```python
import functools
import math
import jax
import jax.numpy as jnp
from jax import lax
import numpy as np

D_MODEL = 1024
BATCH = 4
SEQ = 4096
DEPTH = 4

GRID_W = 64
CTX_LEN = 256
CHUNK = 64
CONV_K = 3
HEAD_DIM = D_MODEL // 8
GDN_HEADS = 4
RET_HEADS = 4
MLSTM_HEADS = 4
GDN_W = GDN_HEADS * HEAD_DIM
RET_W = RET_HEADS * HEAD_DIM
MLSTM_W = MLSTM_HEADS * HEAD_DIM
S5_CH = D_MODEL // 2
S5_GROUP = 16
S5_GROUPS = S5_CH // S5_GROUP
S5_STATE = 64
N_EXPERTS = 16
EXPERT_FF = 2 * D_MODEL
CAPACITY_FACTOR = 2
ALPHA = (2 * DEPTH) ** 0.25
BETA = (8 * DEPTH) ** -0.25
N_EVEN = (DEPTH + 1) // 2
N_ODD = DEPTH // 2
EPS = 1e-5
EVEN_COLS = (GDN_W, GDN_W, GDN_W, GDN_W, GDN_HEADS, GDN_HEADS, GDN_HEADS, GDN_HEADS,
             RET_W, RET_W, RET_W, RET_W)
ODD_COLS = (MLSTM_W, MLSTM_W, MLSTM_W, MLSTM_W, MLSTM_HEADS, MLSTM_HEADS, MLSTM_HEADS, MLSTM_HEADS, S5_CH)
EVEN_IN = sum(EVEN_COLS)
ODD_IN = sum(ODD_COLS)
EVEN_MIX = GDN_W + RET_W
ODD_MIX = MLSTM_W + S5_CH
F32 = jnp.float32

kernel_name = 'hybrid_gdn_retention_mlstm_s5_ecmoe_dit'


def split_cols(t, sizes):
    return jnp.split(t, np.cumsum(sizes)[:-1].tolist(), axis=-1)


def heads(t, n):
    return t.reshape(t.shape[:-1] + (n, t.shape[-1] // n))


def flip_t(t):
    return None if t is None else jnp.flip(t, axis=1)


def l2norm(t):
    return t * lax.rsqrt(jnp.sum(t * t, axis=-1, keepdims=True) + 1e-6)


def layer_norm(t, g, b):
    tf = t.astype(F32)
    mu = tf.mean(-1, keepdims=True)
    var = jnp.square(tf - mu).mean(-1, keepdims=True)
    return ((tf - mu) * lax.rsqrt(var + EPS)).astype(t.dtype) * g + b


def rms_norm_heads(o, g):
    y = o * lax.rsqrt(jnp.mean(o * o, axis=-1, keepdims=True) + EPS) * g
    return y.reshape(o.shape[:2] + (-1,))


def group_norm_heads(o, g):
    mu = o.mean(-1, keepdims=True)
    var = jnp.square(o - mu).mean(-1, keepdims=True)
    return ((o - mu) * lax.rsqrt(var + EPS)).reshape(o.shape[:2] + (-1,)) * g


def short_conv(t, w, on_grid):
    ch = t.shape[-1]
    w = w.astype(t.dtype)
    if on_grid:
        b, n = t.shape[:2]
        rows = n // GRID_W
        tg = t.reshape(b, rows, GRID_W, ch)
        y = lax.conv_general_dilated(tg, w[:, :, None, :], (1, 1), 'SAME',
                                     dimension_numbers=('NHWC', 'HWIO', 'NHWC'), feature_group_count=ch)
        return y.reshape(b, n, ch)
    return lax.conv_general_dilated(t, w[CONV_K // 2][:, None, :], (1,), 'SAME',
                                    dimension_numbers=('NWC', 'WIO', 'NWC'), feature_group_count=ch)


def to_chunks(t):
    b, n, h = t.shape[:3]
    t = t.reshape((b, n // CHUNK, CHUNK, h) + t.shape[3:])
    return jnp.moveaxis(jnp.moveaxis(t, 3, 1), 2, 0)


def from_chunks(t):
    t = jnp.moveaxis(jnp.moveaxis(t, 0, 2), 1, 3)
    return t.reshape((t.shape[0], -1) + t.shape[3:])


def linear_scan(q, k, v, g, beta, s0, want_out):
    b, _, h, dk = k.shape
    dv = v.shape[-1]
    kc, vc = to_chunks(k), to_chunks(v)
    gcum = jnp.cumsum(to_chunks(g), axis=-1)
    diff = gcum[..., :, None] - gcum[..., None, :]
    incl = jnp.tril(jnp.ones((CHUNK, CHUNK), bool))
    if beta is None:
        u, w = vc, None
    else:
        bc = to_chunks(beta)[..., None]
        kb = kc * bc
        strict = jnp.tril(jnp.ones((CHUNK, CHUNK), bool), -1)
        a = jnp.where(strict, jnp.einsum('...ik,...jk->...ij', kb, kc) * jnp.exp(jnp.where(strict, diff, 0.0)), 0.0)
        rhs = jnp.concatenate([vc * bc, kb * jnp.exp(gcum)[..., None]], axis=-1)
        sol = lax.linalg.triangular_solve(a + jnp.eye(CHUNK, dtype=a.dtype), rhs, left_side=True,
                                          lower=True, unit_diagonal=True)
        u, w = sol[..., :dv], sol[..., dv:]
    k_end = kc * jnp.exp(gcum[..., -1:] - gcum)[..., None]
    g_end = jnp.exp(gcum[..., -1])[..., None, None]
    if want_out:
        qc = to_chunks(q)
        q_dec = qc * jnp.exp(gcum)[..., None]
        a_qk = jnp.where(incl, jnp.einsum('...ik,...jk->...ij', qc, kc) * jnp.exp(jnp.where(incl, diff, 0.0)), 0.0)
    else:
        q_dec, a_qk = None, None
    if s0 is None:
        s0 = jnp.zeros((b, h, dk, dv), F32)

    def step(s, inp):
        qd, ke, uc, wc, aqk, ge = inp
        vn = uc if wc is None else uc - jnp.einsum('bhck,bhkv->bhcv', wc, s)
        s_new = ge * s + jnp.einsum('bhck,bhcv->bhkv', ke, vn)
        if qd is None:
            return s_new, None
        return s_new, jnp.einsum('bhck,bhkv->bhcv', qd, s) + jnp.einsum('bhij,bhjv->bhiv', aqk, vn)

    s_fin, o = lax.scan(step, s0, (q_dec, k_end, u, w, a_qk, g_end))
    return (from_chunks(o) if want_out else None), s_fin


def mlstm_scan(q, k, v, log_i, log_f, s0, want_out):
    b, _, h, dk = k.shape
    dv = v.shape[-1]
    if s0 is None:
        s0 = (jnp.zeros((b, h, dk, dv), F32), jnp.zeros((b, h, dk), F32), jnp.zeros((b, h), F32))
    incl = jnp.tril(jnp.ones((CHUNK, CHUNK), bool))

    def step(carry, inp):
        c_prev, n_prev, m_prev = carry
        qc, kc, vc, ic, fc = inp
        bcum = jnp.cumsum(fc, axis=-1)
        b_end = bcum[..., -1]
        a = b_end[..., None] - bcum + ic
        m_new = jnp.maximum(b_end + m_prev, a.max(-1))
        w_state = jnp.exp(a - m_new[..., None])
        decay = jnp.exp(b_end + m_prev - m_new)
        c_new = decay[..., None, None] * c_prev + jnp.einsum('bhck,bhcv->bhkv', kc * w_state[..., None], vc)
        n_new = decay[..., None] * n_prev + jnp.einsum('bhck,bhc->bhk', kc, w_state)
        carry_new = (c_new, n_new, m_new)
        if qc is None:
            return carry_new, None
        dlog = jnp.where(incl, bcum[..., :, None] - bcum[..., None, :] + ic[..., None, :], -jnp.inf)
        inter = bcum + m_prev[..., None]
        m_t = jnp.maximum(inter, dlog.max(-1))
        s = jnp.einsum('bhik,bhjk->bhij', qc, kc) * jnp.exp(dlog - m_t[..., None])
        w_inter = jnp.exp(inter - m_t)[..., None]
        num = jnp.einsum('bhij,bhjv->bhiv', s, vc) + w_inter * jnp.einsum('bhik,bhkv->bhiv', qc, c_prev)
        den = s.sum(-1, keepdims=True) + w_inter * jnp.einsum('bhik,bhk->bhi', qc, n_prev)[..., None]
        return carry_new, num / jnp.maximum(jnp.abs(den), jnp.exp(-m_t)[..., None])

    xs = (to_chunks(q) if want_out else None, to_chunks(k), to_chunks(v), to_chunks(log_i), to_chunks(log_f))
    s_fin, hs = lax.scan(step, s0, xs)
    return (from_chunks(hs) if want_out else None), s_fin


def s5_discretize(lam_re, lam_im, log_dt, b_re, b_im):
    lr = jnp.minimum(lam_re.astype(F32), -1e-4)
    li = lam_im.astype(F32)
    dt = jnp.exp(log_dt.astype(F32))[:, None]
    mag = jnp.exp(lr * dt)
    ab_re, ab_im = mag * jnp.cos(li * dt), mag * jnp.sin(li * dt)
    xr, xi, den = ab_re - 1.0, ab_im, lr * lr + li * li
    f_re = (xr * lr + xi * li) / den
    f_im = (xi * lr - xr * li) / den
    br, bi = b_re.astype(F32), b_im.astype(F32)
    bb_re = f_re[..., None] * br - f_im[..., None] * bi
    bb_im = f_re[..., None] * bi + f_im[..., None] * br
    return ab_re, ab_im, bb_re, bb_im


def complex_affine_combine(e1, e2):
    a1r, a1i, b1r, b1i = e1
    a2r, a2i, b2r, b2i = e2
    return (a1r * a2r - a1i * a2i, a1r * a2i + a1i * a2r,
            a2r * b1r - a2i * b1i + b2r, a2r * b1i + a2i * b1r + b2i)


def s5_scan(ab_re, ab_im, bb_re, bb_im, c_re, c_im, u, s0, want_out):
    bu_re = jnp.einsum('btgh,gph->btgp', u, bb_re)
    bu_im = jnp.einsum('btgh,gph->btgp', u, bb_im)
    if s0 is not None:
        h0r, h0i = s0
        bu_re = bu_re.at[:, 0].add(ab_re * h0r - ab_im * h0i)
        bu_im = bu_im.at[:, 0].add(ab_re * h0i + ab_im * h0r)
    n = u.shape[1]
    a_re = jnp.broadcast_to(ab_re, (1, n) + ab_re.shape)
    a_im = jnp.broadcast_to(ab_im, (1, n) + ab_im.shape)
    _, _, hr, hi = lax.associative_scan(complex_affine_combine, (a_re, a_im, bu_re, bu_im), axis=1)
    s_fin = (hr[:, -1], hi[:, -1])
    if not want_out:
        return None, s_fin
    y = jnp.einsum('btgp,ghp->btgh', hr, c_re) - jnp.einsum('btgp,ghp->btgh', hi, c_im)
    return y, s_fin


def two_pass(scan_fn, ctx_args, lat_args, ctx_out, reverse):
    if reverse:
        ctx_args = [flip_t(t) for t in ctx_args]
        lat_args = [flip_t(t) for t in lat_args]
    o_ctx, s_ctx = scan_fn(*ctx_args, None, ctx_out)
    o_lat, _ = scan_fn(*lat_args, s_ctx, True)
    if reverse:
        o_ctx, o_lat = flip_t(o_ctx), flip_t(o_lat)
    return o_ctx, o_lat


def retention_log_decay(direction):
    expo = 5.0 + 2.0 * jnp.arange(RET_HEADS, dtype=F32) + direction
    return jnp.log1p(-jnp.exp2(-expo))


def gdn_retention_mixer(h_ctx, h_lat, w_in, w_out, conv_w, a_log, dt_bias, gdn_gain, ret_gain, ctx_out):
    def prep(h, on_grid):
        qa, ka, va, za, af, ab, bf, bb, qr, kr, vr, zr = split_cols(h @ w_in, EVEN_COLS)
        qkv = jax.nn.silu(short_conv(jnp.concatenate([qa, ka, va], -1), conv_w, on_grid)).astype(F32)
        qa, ka, va = split_cols(qkv, EVEN_COLS[:3])
        q_g = l2norm(heads(qa, GDN_HEADS)) * HEAD_DIM ** -0.5
        k_g = l2norm(heads(ka, GDN_HEADS))
        v_g = heads(va, GDN_HEADS)
        gdn_dirs = []
        for d, (a_pre, b_pre) in enumerate(((af, bf), (ab, bb))):
            g = -jnp.exp(a_log[d].astype(F32)) * jax.nn.softplus(a_pre.astype(F32) + dt_bias[d].astype(F32))
            gdn_dirs.append((q_g, k_g, v_g, g, jax.nn.sigmoid(b_pre.astype(F32))))
        q_r = heads(qr, RET_HEADS).astype(F32)
        k_r = heads(kr, RET_HEADS).astype(F32) * HEAD_DIM ** -0.5
        v_r = heads(vr, RET_HEADS).astype(F32)
        ret_dirs = [(q_r, k_r, v_r, jnp.broadcast_to(retention_log_decay(d), v_r.shape[:3]), None) for d in range(2)]
        return gdn_dirs, ret_dirs, za, zr

    gc, rc, za_c, zr_c = prep(h_ctx, False)
    gl, rl, za_l, zr_l = prep(h_lat, True)
    gdn = [two_pass(linear_scan, gc[d], gl[d], ctx_out, d == 1) for d in range(2)]
    ret = [two_pass(linear_scan, rc[d], rl[d], ctx_out, d == 1) for d in range(2)]

    def merge(o_g, o_r, za, zr, dtype):
        y_g = rms_norm_heads(o_g, gdn_gain) * jax.nn.silu(za.astype(F32))
        y_r = group_norm_heads(o_r, ret_gain) * jax.nn.silu(zr.astype(F32))
        return jnp.concatenate([y_g, y_r], -1).astype(dtype) @ w_out

    y_lat = merge(gdn[0][1] + gdn[1][1], ret[0][1] + ret[1][1], za_l, zr_l, h_lat.dtype)
    y_ctx = merge(gdn[0][0] + gdn[1][0], ret[0][0] + ret[1][0], za_c, zr_c, h_ctx.dtype) if ctx_out else None
    return y_ctx, y_lat


def mlstm_s5_mixer(h_ctx, h_lat, w_in, w_out, conv_w, gate_bias, mlstm_gain, lam_re, lam_im, log_dt,
                   b_re, b_im, c_re, c_im, d_skip, w_glu, b_glu, ctx_out):
    gb = gate_bias.astype(F32)

    def prep(h, on_grid):
        qm, km, vm, om, i_f, i_b, f_f, f_b, u = split_cols(h @ w_in, ODD_COLS)
        qk = jax.nn.silu(short_conv(jnp.concatenate([qm, km], -1), conv_w, on_grid)).astype(F32)
        qm, km = split_cols(qk, ODD_COLS[:2])
        q = heads(qm, MLSTM_HEADS)
        k = heads(km, MLSTM_HEADS) * HEAD_DIM ** -0.5
        v = heads(vm, MLSTM_HEADS).astype(F32)
        dirs = [(q, k, v, i_pre.astype(F32) + gb[d, 0], jax.nn.log_sigmoid(f_pre.astype(F32) + gb[d, 1]))
                for d, (i_pre, f_pre) in enumerate(((i_f, f_f), (i_b, f_b)))]
        u = u.astype(F32)
        return dirs, u, om

    mc, u_c, o_c = prep(h_ctx, False)
    ml, u_l, o_l = prep(h_lat, True)
    grp = lambda t: t.reshape(t.shape[:2] + (S5_GROUPS, S5_GROUP))
    cr, ci = c_re.astype(F32), c_im.astype(F32)
    mls = [two_pass(mlstm_scan, mc[d], ml[d], ctx_out, d == 1) for d in range(2)]
    ssm = [two_pass(functools.partial(s5_scan, *s5_discretize(lam_re[d], lam_im[d], log_dt[d], b_re, b_im), cr, ci),
                    [grp(u_c)], [grp(u_l)], ctx_out, d == 1) for d in range(2)]

    def merge(h_m, y_s, u, og, dtype):
        y_m = group_norm_heads(h_m, mlstm_gain) * jax.nn.sigmoid(og.astype(F32))
        y = jax.nn.gelu(y_s.reshape(u.shape) + d_skip * u)
        y = y * jax.nn.sigmoid(y @ w_glu + b_glu)
        return jnp.concatenate([y_m, y], -1).astype(dtype) @ w_out

    y_lat = merge(mls[0][1] + mls[1][1], ssm[0][1] + ssm[1][1], u_l, o_l, h_lat.dtype)
    y_ctx = merge(mls[0][0] + mls[1][0], ssm[0][0] + ssm[1][0], u_c, o_c, h_ctx.dtype) if ctx_out else None
    return y_ctx, y_lat


def expert_choice_ffn(h, w_router, w_gate, w_up, w_down):
    n, dm = h.shape[1], h.shape[2]
    cap = CAPACITY_FACTOR * n // N_EXPERTS
    aff = jax.nn.softmax((h @ w_router).astype(F32), axis=-1)
    gate, idx = lax.top_k(jnp.swapaxes(aff, 1, 2), cap)
    xs = jax.vmap(lambda hb, ib: hb[ib])(h, idx)
    hid = jax.nn.silu(jnp.einsum('becd,edf->becf', xs, w_gate)) * jnp.einsum('becd,edf->becf', xs, w_up)
    ys = jnp.einsum('becf,efd->becd', hid, w_down) * gate[..., None].astype(h.dtype)
    return jax.vmap(lambda ib, yb: jnp.zeros((n, dm), yb.dtype).at[ib.reshape(-1)].add(yb.reshape(-1, dm)))(idx, ys)


def setup_inputs(seed: int = 0) -> dict:
    key = jax.random.key(seed)
    ks = iter(jax.random.split(key, 48))

    def nrm(shape, scale):
        return jax.random.normal(next(ks), shape, jnp.float32) * scale

    def unif(shape, lo, hi):
        return jax.random.uniform(next(ks), shape, jnp.float32, lo, hi)

    D = D_MODEL
    L = DEPTH
    dt_gdn = jnp.exp(unif((N_EVEN, 2, GDN_HEADS), math.log(1e-3), math.log(1e-1)))
    gate_bias = jnp.stack([nrm((N_ODD, 2, MLSTM_HEADS), 0.1),
                           jnp.linspace(3.0, 6.0, MLSTM_HEADS) + nrm((N_ODD, 2, MLSTM_HEADS), 0.1)], axis=2)
    lam_im = jnp.pi * jnp.arange(S5_STATE, dtype=jnp.float32) + nrm((N_ODD, 2, S5_GROUPS, S5_STATE), 0.01)
    return {
        'x': nrm((BATCH, SEQ, D), 1.0),
        'c': nrm((BATCH, D), 1.0),
        'ctx': nrm((BATCH, CTX_LEN, D), 1.0),
        'c_ctx': nrm((D,), 1.0),
        'w_mod': nrm((L, D, 6 * D), 0.5 * D ** -0.5),
        'b_mod': nrm((L, 6 * D), 0.02),
        'ln1_g': 1.0 + nrm((L, D), 0.02),
        'ln1_b': nrm((L, D), 0.02),
        'ln2_g': 1.0 + nrm((L, D), 0.02),
        'ln2_b': nrm((L, D), 0.02),
        'w_router': nrm((L, D, N_EXPERTS), D ** -0.5),
        'w_gate': nrm((L, N_EXPERTS, D, EXPERT_FF), D ** -0.5),
        'w_up': nrm((L, N_EXPERTS, D, EXPERT_FF), D ** -0.5),
        'w_down': nrm((L, N_EXPERTS, EXPERT_FF, D), BETA * EXPERT_FF ** -0.5),
        'ev_w_in': nrm((N_EVEN, D, EVEN_IN), D ** -0.5),
        'ev_w_out': nrm((N_EVEN, EVEN_MIX, D), BETA * EVEN_MIX ** -0.5),
        'ev_conv': nrm((N_EVEN, CONV_K, CONV_K, 3 * GDN_W), 1.0 / CONV_K),
        'ev_a_log': jnp.log(unif((N_EVEN, 2, GDN_HEADS), 1.0, 16.0)),
        'ev_dt_bias': dt_gdn + jnp.log(-jnp.expm1(-dt_gdn)),
        'ev_gdn_norm': 1.0 + nrm((N_EVEN, HEAD_DIM), 0.02),
        'ev_ret_norm': 1.0 + nrm((N_EVEN, RET_W), 0.02),
        'od_w_in': nrm((N_ODD, D, ODD_IN), D ** -0.5),
        'od_w_out': nrm((N_ODD, ODD_MIX, D), BETA * ODD_MIX ** -0.5),
        'od_conv': nrm((N_ODD, CONV_K, CONV_K, 2 * MLSTM_W), 1.0 / CONV_K),
        'od_gate_bias': gate_bias,
        'od_mlstm_norm': 1.0 + nrm((N_ODD, MLSTM_W), 0.02),
        'od_lam_re': -0.5 + nrm((N_ODD, 2, S5_GROUPS, S5_STATE), 0.01),
        'od_lam_im': lam_im,
        'od_log_dt': unif((N_ODD, 2, S5_GROUPS), math.log(1e-3), math.log(1e-1)),
        'od_b_re': nrm((N_ODD, S5_GROUPS, S5_STATE, S5_GROUP), (2 * S5_GROUP) ** -0.5),
        'od_b_im': nrm((N_ODD, S5_GROUPS, S5_STATE, S5_GROUP), (2 * S5_GROUP) ** -0.5),
        'od_c_re': nrm((N_ODD, S5_GROUPS, S5_GROUP, S5_STATE), S5_STATE ** -0.5),
        'od_c_im': nrm((N_ODD, S5_GROUPS, S5_GROUP, S5_STATE), S5_STATE ** -0.5),
        'od_d_skip': nrm((N_ODD, S5_CH), 1.0),
        'od_w_glu': nrm((N_ODD, S5_CH, S5_CH), S5_CH ** -0.5),
        'od_b_glu': nrm((N_ODD, S5_CH), 0.02),
    }


def reference(x, c, ctx, c_ctx, w_mod, b_mod, ln1_g, ln1_b, ln2_g, ln2_b, w_router, w_gate, w_up, w_down,
              ev_w_in, ev_w_out, ev_conv, ev_a_log, ev_dt_bias, ev_gdn_norm, ev_ret_norm,
              od_w_in, od_w_out, od_conv, od_gate_bias, od_mlstm_norm, od_lam_re, od_lam_im, od_log_dt,
              od_b_re, od_b_im, od_c_re, od_c_im, od_d_skip, od_w_glu, od_b_glu):
    h_lat, h_ctx = x, ctx
    s_lat = jax.nn.silu(c)
    s_ctx = jax.nn.silu(c_ctx)
    for l in range(DEPTH):
        last = l == DEPTH - 1
        sh1, sc1, g1, sh2, sc2, g2 = jnp.split((s_lat @ w_mod[l] + b_mod[l])[:, None, :], 6, axis=-1)
        csh1, csc1, cg1, csh2, csc2, cg2 = jnp.split(s_ctx @ w_mod[l] + b_mod[l], 6, axis=-1)
        in_lat = h_lat * (1.0 + sc1) + sh1
        in_ctx = h_ctx * (1.0 + csc1) + csh1
        if l % 2 == 0:
            e = l // 2
            y_ctx, y_lat = gdn_retention_mixer(in_ctx, in_lat, ev_w_in[e], ev_w_out[e], ev_conv[e], ev_a_log[e],
                                               ev_dt_bias[e], ev_gdn_norm[e], ev_ret_norm[e], not last)
        else:
            o = l // 2
            y_ctx, y_lat = mlstm_s5_mixer(in_ctx, in_lat, od_w_in[o], od_w_out[o], od_conv[o], od_gate_bias[o],
                                          od_mlstm_norm[o], od_lam_re[o], od_lam_im[o], od_log_dt[o],
                                          od_b_re[o], od_b_im[o], od_c_re[o], od_c_im[o], od_d_skip[o],
                                          od_w_glu[o], od_b_glu[o], not last)
        h_lat = layer_norm(ALPHA * h_lat + g1 * y_lat, ln1_g[l], ln1_b[l])
        f_lat = expert_choice_ffn(h_lat * (1.0 + sc2) + sh2, w_router[l], w_gate[l], w_up[l], w_down[l])
        h_lat = layer_norm(ALPHA * h_lat + g2 * f_lat, ln2_g[l], ln2_b[l])
        if not last:
            h_ctx = layer_norm(ALPHA * h_ctx + cg1 * y_ctx, ln1_g[l], ln1_b[l])
            f_ctx = expert_choice_ffn(h_ctx * (1.0 + csc2) + csh2, w_router[l], w_gate[l], w_up[l], w_down[l])
            h_ctx = layer_norm(ALPHA * h_ctx + cg2 * f_ctx, ln2_g[l], ln2_b[l])
    return h_lat
```

```python
import functools
import math

import jax
import jax.numpy as jnp
import numpy as np
from jax import lax
from jax.experimental import pallas as pl
from jax.experimental.pallas import tpu as pltpu

D_MODEL = 1024
BATCH = 4
SEQ = 4096
DEPTH = 4
GRID_W = 64
CTX_LEN = 256
CHUNK = 64
CONV_K = 3
HEAD_DIM = D_MODEL // 8
GDN_HEADS = 4
RET_HEADS = 4
MLSTM_HEADS = 4
GDN_W = GDN_HEADS * HEAD_DIM
RET_W = RET_HEADS * HEAD_DIM
MLSTM_W = MLSTM_HEADS * HEAD_DIM
S5_CH = D_MODEL // 2
S5_GROUP = 16
S5_GROUPS = S5_CH // S5_GROUP
S5_STATE = 64
N_EXPERTS = 16
EXPERT_FF = 2 * D_MODEL
CAPACITY_FACTOR = 2
ALPHA = (2 * DEPTH) ** 0.25
EPS = 1e-5
EVEN_COLS = (GDN_W, GDN_W, GDN_W, GDN_W, GDN_HEADS, GDN_HEADS, GDN_HEADS, GDN_HEADS,
             RET_W, RET_W, RET_W, RET_W)
ODD_COLS = (MLSTM_W, MLSTM_W, MLSTM_W, MLSTM_W, MLSTM_HEADS, MLSTM_HEADS, MLSTM_HEADS, MLSTM_HEADS, S5_CH)
F32 = jnp.float32
BF16 = jnp.bfloat16

LANE = 128


def _mm_kernel(a_ref, b_ref, o_ref):
    o_ref[...] = jnp.dot(a_ref[...].astype(BF16), b_ref[...].astype(BF16), preferred_element_type=F32)


def pmatmul(a, b, tm=512, tn=512):
    m, k = a.shape
    n = b.shape[1]
    n_pad = -n % LANE
    if n_pad:
        b = jnp.pad(b, ((0, 0), (0, n_pad)))
    np_ = n + n_pad
    tn = min(tn, np_)
    while np_ % tn:
        tn -= LANE
    tm = min(tm, m)
    assert m % tm == 0
    out = pl.pallas_call(
        _mm_kernel,
        grid=(m // tm, np_ // tn),
        in_specs=[pl.BlockSpec((tm, k), lambda i, j: (i, 0)),
                  pl.BlockSpec((k, tn), lambda i, j: (0, j))],
        out_specs=pl.BlockSpec((tm, tn), lambda i, j: (i, j)),
        out_shape=jax.ShapeDtypeStruct((m, np_), F32),
        name="pmatmul",
    )(a, b)
    return out[:, :n] if n_pad else out


def mm(x, w):
    lead = x.shape[:-1]
    return pmatmul(x.reshape(-1, x.shape[-1]), w).reshape(lead + (w.shape[-1],))


def split_cols(t, sizes):
    return jnp.split(t, np.cumsum(sizes)[:-1].tolist(), axis=-1)


def heads(t, n):
    return t.reshape(t.shape[:-1] + (n, t.shape[-1] // n))


def flip_t(t):
    return None if t is None else jnp.flip(t, axis=1)


def l2norm(t):
    return t * lax.rsqrt(jnp.sum(t * t, axis=-1, keepdims=True) + 1e-6)


def layer_norm(t, g, b):
    mu = t.mean(-1, keepdims=True)
    var = jnp.square(t - mu).mean(-1, keepdims=True)
    return ((t - mu) * lax.rsqrt(var + EPS)) * g + b


def rms_norm_heads(o, g):
    y = o * lax.rsqrt(jnp.mean(o * o, axis=-1, keepdims=True) + EPS) * g
    return y.reshape(o.shape[:2] + (-1,))


def group_norm_heads(o, g):
    mu = o.mean(-1, keepdims=True)
    var = jnp.square(o - mu).mean(-1, keepdims=True)
    return ((o - mu) * lax.rsqrt(var + EPS)).reshape(o.shape[:2] + (-1,)) * g


def short_conv(t, w, on_grid):
    ch = t.shape[-1]
    if on_grid:
        b, n = t.shape[:2]
        rows = n // GRID_W
        tg = t.reshape(b, rows, GRID_W, ch)
        y = lax.conv_general_dilated(tg, w[:, :, None, :], (1, 1), 'SAME',
                                     dimension_numbers=('NHWC', 'HWIO', 'NHWC'), feature_group_count=ch)
        return y.reshape(b, n, ch)
    return lax.conv_general_dilated(t, w[CONV_K // 2][:, None, :], (1,), 'SAME',
                                    dimension_numbers=('NWC', 'WIO', 'NWC'), feature_group_count=ch)


def to_chunks(t):
    b, n, h = t.shape[:3]
    t = t.reshape((b, n // CHUNK, CHUNK, h) + t.shape[3:])
    return jnp.moveaxis(jnp.moveaxis(t, 3, 1), 2, 0)


def from_chunks(t):
    t = jnp.moveaxis(jnp.moveaxis(t, 0, 2), 1, 3)
    return t.reshape((t.shape[0], -1) + t.shape[3:])


def linear_scan(q, k, v, g, beta, s0, want_out):
    b, _, h, dk = k.shape
    dv = v.shape[-1]
    kc, vc = to_chunks(k), to_chunks(v)
    gcum = jnp.cumsum(to_chunks(g), axis=-1)
    diff = gcum[..., :, None] - gcum[..., None, :]
    incl = jnp.tril(jnp.ones((CHUNK, CHUNK), bool))
    if beta is None:
        u, w = vc, None
    else:
        bc = to_chunks(beta)[..., None]
        kb = kc * bc
        strict = jnp.tril(jnp.ones((CHUNK, CHUNK), bool), -1)
        a = jnp.where(strict, jnp.einsum('...ik,...jk->...ij', kb, kc) * jnp.exp(jnp.where(strict, diff, 0.0)), 0.0)
        rhs = jnp.concatenate([vc * bc, kb * jnp.exp(gcum)[..., None]], axis=-1)
        sol = lax.linalg.triangular_solve(a + jnp.eye(CHUNK, dtype=a.dtype), rhs, left_side=True,
                                          lower=True, unit_diagonal=True)
        u, w = sol[..., :dv], sol[..., dv:]
    k_end = kc * jnp.exp(gcum[..., -1:] - gcum)[..., None]
    g_end = jnp.exp(gcum[..., -1])[..., None, None]
    if want_out:
        qc = to_chunks(q)
        q_dec = qc * jnp.exp(gcum)[..., None]
        a_qk = jnp.where(incl, jnp.einsum('...ik,...jk->...ij', qc, kc) * jnp.exp(jnp.where(incl, diff, 0.0)), 0.0)
    else:
        q_dec, a_qk = None, None
    if s0 is None:
        s0 = jnp.zeros((b, h, dk, dv), F32)

    def step(s, inp):
        qd, ke, uc, wc, aqk, ge = inp
        vn = uc if wc is None else uc - jnp.einsum('bhck,bhkv->bhcv', wc, s)
        s_new = ge * s + jnp.einsum('bhck,bhcv->bhkv', ke, vn)
        if qd is None:
            return s_new, None
        return s_new, jnp.einsum('bhck,bhkv->bhcv', qd, s) + jnp.einsum('bhij,bhjv->bhiv', aqk, vn)

    s_fin, o = lax.scan(step, s0, (q_dec, k_end, u, w, a_qk, g_end))
    return (from_chunks(o) if want_out else None), s_fin


def mlstm_scan(q, k, v, log_i, log_f, s0, want_out):
    b, _, h, dk = k.shape
    dv = v.shape[-1]
    if s0 is None:
        s0 = (jnp.zeros((b, h, dk, dv), F32), jnp.zeros((b, h, dk), F32), jnp.zeros((b, h), F32))
    incl = jnp.tril(jnp.ones((CHUNK, CHUNK), bool))

    def step(carry, inp):
        c_prev, n_prev, m_prev = carry
        qc, kc, vc, ic, fc = inp
        bcum = jnp.cumsum(fc, axis=-1)
        b_end = bcum[..., -1]
        a = b_end[..., None] - bcum + ic
        m_new = jnp.maximum(b_end + m_prev, a.max(-1))
        w_state = jnp.exp(a - m_new[..., None])
        decay = jnp.exp(b_end + m_prev - m_new)
        c_new = decay[..., None, None] * c_prev + jnp.einsum('bhck,bhcv->bhkv', kc * w_state[..., None], vc)
        n_new = decay[..., None] * n_prev + jnp.einsum('bhck,bhc->bhk', kc, w_state)
        carry_new = (c_new, n_new, m_new)
        if qc is None:
            return carry_new, None
        dlog = jnp.where(incl, bcum[..., :, None] - bcum[..., None, :] + ic[..., None, :], -jnp.inf)
        inter = bcum + m_prev[..., None]
        m_t = jnp.maximum(inter, dlog.max(-1))
        s = jnp.einsum('bhik,bhjk->bhij', qc, kc) * jnp.exp(dlog - m_t[..., None])
        w_inter = jnp.exp(inter - m_t)[..., None]
        num = jnp.einsum('bhij,bhjv->bhiv', s, vc) + w_inter * jnp.einsum('bhik,bhkv->bhiv', qc, c_prev)
        den = s.sum(-1, keepdims=True) + w_inter * jnp.einsum('bhik,bhk->bhi', qc, n_prev)[..., None]
        return carry_new, num / jnp.maximum(jnp.abs(den), jnp.exp(-m_t)[..., None])

    xs = (to_chunks(q) if want_out else None, to_chunks(k), to_chunks(v), to_chunks(log_i), to_chunks(log_f))
    s_fin, hs = lax.scan(step, s0, xs)
    return (from_chunks(hs) if want_out else None), s_fin


def s5_discretize(lam_re, lam_im, log_dt, b_re, b_im):
    lr = jnp.minimum(lam_re, -1e-4)
    li = lam_im
    dt = jnp.exp(log_dt)[:, None]
    mag = jnp.exp(lr * dt)
    ab_re, ab_im = mag * jnp.cos(li * dt), mag * jnp.sin(li * dt)
    xr, xi, den = ab_re - 1.0, ab_im, lr * lr + li * li
    f_re = (xr * lr + xi * li) / den
    f_im = (xi * lr - xr * li) / den
    bb_re = f_re[..., None] * b_re - f_im[..., None] * b_im
    bb_im = f_re[..., None] * b_im + f_im[..., None] * b_re
    return ab_re, ab_im, bb_re, bb_im


def complex_affine_combine(e1, e2):
    a1r, a1i, b1r, b1i = e1
    a2r, a2i, b2r, b2i = e2
    return (a1r * a2r - a1i * a2i, a1r * a2i + a1i * a2r,
            a2r * b1r - a2i * b1i + b2r, a2r * b1i + a2i * b1r + b2i)


def s5_scan(ab_re, ab_im, bb_re, bb_im, c_re, c_im, u, s0, want_out):
    bu_re = jnp.einsum('btgh,gph->btgp', u, bb_re)
    bu_im = jnp.einsum('btgh,gph->btgp', u, bb_im)
    if s0 is not None:
        h0r, h0i = s0
        bu_re = bu_re.at[:, 0].add(ab_re * h0r - ab_im * h0i)
        bu_im = bu_im.at[:, 0].add(ab_re * h0i + ab_im * h0r)
    n = u.shape[1]
    a_re = jnp.broadcast_to(ab_re, (1, n) + ab_re.shape)
    a_im = jnp.broadcast_to(ab_im, (1, n) + ab_im.shape)
    _, _, hr, hi = lax.associative_scan(complex_affine_combine, (a_re, a_im, bu_re, bu_im), axis=1)
    s_fin = (hr[:, -1], hi[:, -1])
    if not want_out:
        return None, s_fin
    y = jnp.einsum('btgp,ghp->btgh', hr, c_re) - jnp.einsum('btgp,ghp->btgh', hi, c_im)
    return y, s_fin


def two_pass(scan_fn, ctx_args, lat_args, ctx_out, reverse):
    if reverse:
        ctx_args = [flip_t(t) for t in ctx_args]
        lat_args = [flip_t(t) for t in lat_args]
    o_ctx, s_ctx = scan_fn(*ctx_args, None, ctx_out)
    o_lat, _ = scan_fn(*lat_args, s_ctx, True)
    if reverse:
        o_ctx, o_lat = flip_t(o_ctx), flip_t(o_lat)
    return o_ctx, o_lat


def retention_log_decay(direction):
    expo = 5.0 + 2.0 * jnp.arange(RET_HEADS, dtype=F32) + direction
    return jnp.log1p(-jnp.exp2(-expo))


def gdn_retention_mixer(h_ctx, h_lat, w_in, w_out, conv_w, a_log, dt_bias, gdn_gain, ret_gain, ctx_out):
    def prep(h, on_grid):
        qa, ka, va, za, af, ab, bf, bb, qr, kr, vr, zr = split_cols(mm(h, w_in), EVEN_COLS)
        qkv = jax.nn.silu(short_conv(jnp.concatenate([qa, ka, va], -1), conv_w, on_grid))
        qa, ka, va = split_cols(qkv, EVEN_COLS[:3])
        q_g = l2norm(heads(qa, GDN_HEADS)) * HEAD_DIM ** -0.5
        k_g = l2norm(heads(ka, GDN_HEADS))
        v_g = heads(va, GDN_HEADS)
        gdn_dirs = []
        for d, (a_pre, b_pre) in enumerate(((af, bf), (ab, bb))):
            g = -jnp.exp(a_log[d]) * jax.nn.softplus(a_pre + dt_bias[d])
            gdn_dirs.append((q_g, k_g, v_g, g, jax.nn.sigmoid(b_pre)))
        q_r = heads(qr, RET_HEADS)
        k_r = heads(kr, RET_HEADS) * HEAD_DIM ** -0.5
        v_r = heads(vr, RET_HEADS)
        ret_dirs = [(q_r, k_r, v_r, jnp.broadcast_to(retention_log_decay(d), v_r.shape[:3]), None) for d in range(2)]
        return gdn_dirs, ret_dirs, za, zr

    gc, rc, za_c, zr_c = prep(h_ctx, False)
    gl, rl, za_l, zr_l = prep(h_lat, True)
    gdn = [two_pass(linear_scan, gc[d], gl[d], ctx_out, d == 1) for d in range(2)]
    ret = [two_pass(linear_scan, rc[d], rl[d], ctx_out, d == 1) for d in range(2)]

    def merge(o_g, o_r, za, zr):
        y_g = rms_norm_heads(o_g, gdn_gain) * jax.nn.silu(za)
        y_r = group_norm_heads(o_r, ret_gain) * jax.nn.silu(zr)
        return mm(jnp.concatenate([y_g, y_r], -1), w_out)

    y_lat = merge(gdn[0][1] + gdn[1][1], ret[0][1] + ret[1][1], za_l, zr_l)
    y_ctx = merge(gdn[0][0] + gdn[1][0], ret[0][0] + ret[1][0], za_c, zr_c) if ctx_out else None
    return y_ctx, y_lat


def mlstm_s5_mixer(h_ctx, h_lat, w_in, w_out, conv_w, gate_bias, mlstm_gain, lam_re, lam_im, log_dt,
                   b_re, b_im, c_re, c_im, d_skip, w_glu, b_glu, ctx_out):
    gb = gate_bias

    def prep(h, on_grid):
        qm, km, vm, om, i_f, i_b, f_f, f_b, u = split_cols(mm(h, w_in), ODD_COLS)
        qk = jax.nn.silu(short_conv(jnp.concatenate([qm, km], -1), conv_w, on_grid))
        qm, km = split_cols(qk, ODD_COLS[:2])
        q = heads(qm, MLSTM_HEADS)
        k = heads(km, MLSTM_HEADS) * HEAD_DIM ** -0.5
        v = heads(vm, MLSTM_HEADS)
        dirs = [(q, k, v, i_pre + gb[d, 0], jax.nn.log_sigmoid(f_pre + gb[d, 1]))
                for d, (i_pre, f_pre) in enumerate(((i_f, f_f), (i_b, f_b)))]
        return dirs, u, om

    mc, u_c, o_c = prep(h_ctx, False)
    ml, u_l, o_l = prep(h_lat, True)
    grp = lambda t: t.reshape(t.shape[:2] + (S5_GROUPS, S5_GROUP))
    mls = [two_pass(mlstm_scan, mc[d], ml[d], ctx_out, d == 1) for d in range(2)]
    ssm = [two_pass(functools.partial(s5_scan, *s5_discretize(lam_re[d], lam_im[d], log_dt[d], b_re, b_im), c_re, c_im),
                    [grp(u_c)], [grp(u_l)], ctx_out, d == 1) for d in range(2)]

    def merge(h_m, y_s, u, og):
        y_m = group_norm_heads(h_m, mlstm_gain) * jax.nn.sigmoid(og)
        y = jax.nn.gelu(y_s.reshape(u.shape) + d_skip * u)
        y = y * jax.nn.sigmoid(mm(y, w_glu) + b_glu)
        return mm(jnp.concatenate([y_m, y], -1), w_out)

    y_lat = merge(mls[0][1] + mls[1][1], ssm[0][1] + ssm[1][1], u_l, o_l)
    y_ctx = merge(mls[0][0] + mls[1][0], ssm[0][0] + ssm[1][0], u_c, o_c) if ctx_out else None
    return y_ctx, y_lat


def expert_choice_ffn(h, w_router, w_gate, w_up, w_down):
    n, dm = h.shape[1], h.shape[2]
    cap = CAPACITY_FACTOR * n // N_EXPERTS
    aff = jax.nn.softmax(jnp.einsum('bnd,de->bne', h, w_router, precision=lax.Precision.HIGHEST), axis=-1)
    gate, idx = lax.top_k(jnp.swapaxes(aff, 1, 2), cap)
    xs = jax.vmap(lambda hb, ib: hb[ib])(h, idx)
    hid = jax.nn.silu(jnp.einsum('becd,edf->becf', xs, w_gate)) * jnp.einsum('becd,edf->becf', xs, w_up)
    ys = jnp.einsum('becf,efd->becd', hid, w_down) * gate[..., None]
    return jax.vmap(lambda ib, yb: jnp.zeros((n, dm), yb.dtype).at[ib.reshape(-1)].add(yb.reshape(-1, dm)))(idx, ys)


def kernel(x, c, ctx, c_ctx, w_mod, b_mod, ln1_g, ln1_b, ln2_g, ln2_b, w_router, w_gate, w_up, w_down,
           ev_w_in, ev_w_out, ev_conv, ev_a_log, ev_dt_bias, ev_gdn_norm, ev_ret_norm,
           od_w_in, od_w_out, od_conv, od_gate_bias, od_mlstm_norm, od_lam_re, od_lam_im, od_log_dt,
           od_b_re, od_b_im, od_c_re, od_c_im, od_d_skip, od_w_glu, od_b_glu):
    h_lat, h_ctx = x, ctx
    s_lat = jax.nn.silu(c)
    s_ctx = jax.nn.silu(c_ctx)
    for l in range(DEPTH):
        last = l == DEPTH - 1
        sh1, sc1, g1, sh2, sc2, g2 = jnp.split((s_lat @ w_mod[l] + b_mod[l])[:, None, :], 6, axis=-1)
        csh1, csc1, cg1, csh2, csc2, cg2 = jnp.split(s_ctx @ w_mod[l] + b_mod[l], 6, axis=-1)
        in_lat = h_lat * (1.0 + sc1) + sh1
        in_ctx = h_ctx * (1.0 + csc1) + csh1
        if l % 2 == 0:
            e = l // 2
            y_ctx, y_lat = gdn_retention_mixer(in_ctx, in_lat, ev_w_in[e], ev_w_out[e], ev_conv[e], ev_a_log[e],
                                               ev_dt_bias[e], ev_gdn_norm[e], ev_ret_norm[e], not last)
        else:
            o = l // 2
            y_ctx, y_lat = mlstm_s5_mixer(in_ctx, in_lat, od_w_in[o], od_w_out[o], od_conv[o], od_gate_bias[o],
                                          od_mlstm_norm[o], od_lam_re[o], od_lam_im[o], od_log_dt[o],
                                          od_b_re[o], od_b_im[o], od_c_re[o], od_c_im[o], od_d_skip[o],
                                          od_w_glu[o], od_b_glu[o], not last)
        h_lat = layer_norm(ALPHA * h_lat + g1 * y_lat, ln1_g[l], ln1_b[l])
        f_lat = expert_choice_ffn(h_lat * (1.0 + sc2) + sh2, w_router[l], w_gate[l], w_up[l], w_down[l])
        h_lat = layer_norm(ALPHA * h_lat + g2 * f_lat, ln2_g[l], ln2_b[l])
        if not last:
            h_ctx = layer_norm(ALPHA * h_ctx + cg1 * y_ctx, ln1_g[l], ln1_b[l])
            f_ctx = expert_choice_ffn(h_ctx * (1.0 + csc2) + csh2, w_router[l], w_gate[l], w_up[l], w_down[l])
            h_ctx = layer_norm(ALPHA * h_ctx + cg2 * f_ctx, ln2_g[l], ln2_b[l])
    return h_lat
```

```python
import functools
import math

import jax
import jax.numpy as jnp
import numpy as np
from jax import lax
from jax.experimental import pallas as pl
from jax.experimental.pallas import tpu as pltpu

D_MODEL = 1024
BATCH = 4
SEQ = 4096
DEPTH = 4
GRID_W = 64
CTX_LEN = 256
CHUNK = 64
CONV_K = 3
HEAD_DIM = D_MODEL // 8
GDN_HEADS = 4
RET_HEADS = 4
MLSTM_HEADS = 4
GDN_W = GDN_HEADS * HEAD_DIM
RET_W = RET_HEADS * HEAD_DIM
MLSTM_W = MLSTM_HEADS * HEAD_DIM
S5_CH = D_MODEL // 2
S5_GROUP = 16
S5_GROUPS = S5_CH // S5_GROUP
S5_STATE = 64
N_EXPERTS = 16
EXPERT_FF = 2 * D_MODEL
CAPACITY_FACTOR = 2
ALPHA = (2 * DEPTH) ** 0.25
EPS = 1e-5
EVEN_COLS = (GDN_W, GDN_W, GDN_W, GDN_W, GDN_HEADS, GDN_HEADS, GDN_HEADS, GDN_HEADS,
             RET_W, RET_W, RET_W, RET_W)
ODD_COLS = (MLSTM_W, MLSTM_W, MLSTM_W, MLSTM_W, MLSTM_HEADS, MLSTM_HEADS, MLSTM_HEADS, MLSTM_HEADS, S5_CH)
F32 = jnp.float32
BF16 = jnp.bfloat16

LANE = 128


def _mm_kernel(a_ref, b_ref, o_ref):
    o_ref[...] = jnp.dot(a_ref[...].astype(BF16), b_ref[...].astype(BF16), preferred_element_type=F32)


def pmatmul(a, b, tm=512, tn=512):
    m, k = a.shape
    n = b.shape[1]
    n_pad = -n % LANE
    if n_pad:
        b = jnp.pad(b, ((0, 0), (0, n_pad)))
    np_ = n + n_pad
    tn = min(tn, np_)
    while np_ % tn:
        tn -= LANE
    tm = min(tm, m)
    assert m % tm == 0
    out = pl.pallas_call(
        _mm_kernel,
        grid=(m // tm, np_ // tn),
        in_specs=[pl.BlockSpec((tm, k), lambda i, j: (i, 0)),
                  pl.BlockSpec((k, tn), lambda i, j: (0, j))],
        out_specs=pl.BlockSpec((tm, tn), lambda i, j: (i, j)),
        out_shape=jax.ShapeDtypeStruct((m, np_), F32),
        name="pmatmul",
    )(a, b)
    return out[:, :n] if n_pad else out


def mm(x, w):
    lead = x.shape[:-1]
    return pmatmul(x.reshape(-1, x.shape[-1]), w).reshape(lead + (w.shape[-1],))


S5_L = 16
S5_NB = S5_CH // LANE
S5_GPB = LANE // S5_GROUP
S5_SW = S5_GPB * S5_STATE
VMEM_LIMIT = 56 * 1024 * 1024


def _s5_weights(lam_re, lam_im, log_dt, b_re, b_im, c_re, c_im):
    L, G = S5_L, S5_GROUPS
    hp = lax.Precision.HIGHEST
    taus = jnp.arange(L + 1, dtype=F32)[:, None, None]
    ks, ws, cas, ds = [], [], [], []
    for d in range(2):
        lr = jnp.minimum(lam_re[d], -1e-4)
        li = lam_im[d]
        dt = jnp.exp(log_dt[d])[:, None]
        mag = jnp.exp(lr * dt)
        ab_re, ab_im = mag * jnp.cos(li * dt), mag * jnp.sin(li * dt)
        xr, xi, den = ab_re - 1.0, ab_im, lr * lr + li * li
        f_re = (xr * lr + xi * li) / den
        f_im = (xi * lr - xr * li) / den
        bb_re = f_re[..., None] * b_re - f_im[..., None] * b_im
        bb_im = f_re[..., None] * b_im + f_im[..., None] * b_re
        pmag = jnp.exp(taus * (lr * dt))
        ar, ai = pmag * jnp.cos(taus * (li * dt)), pmag * jnp.sin(taus * (li * dt))
        wr = ar[..., None] * bb_re - ai[..., None] * bb_im
        wi = ar[..., None] * bb_im + ai[..., None] * bb_re
        k = (jnp.einsum('gop,tgpi->tgio', c_re, wr, precision=hp)
             - jnp.einsum('gop,tgpi->tgio', c_im, wi, precision=hp))
        car = c_re[None] * ar[:, :, None, :] - c_im[None] * ai[:, :, None, :]
        cai = c_re[None] * ai[:, :, None, :] + c_im[None] * ar[:, :, None, :]
        ks.append(k)
        ws.append((wr, wi))
        cas.append((car, cai))
        ds.append((ar[L], ai[L]))

    eye = jnp.eye(S5_GPB, dtype=F32)

    def blockdiag(t, in_axis_first):
        lead = t.shape[:-3]
        x, y = t.shape[-2:]
        t = t.reshape(lead + (S5_NB, S5_GPB, x, y))
        t = jnp.einsum('...jaxy,ab->...jaxby', t, eye)
        return t.reshape(lead + (S5_NB, S5_GPB * x, S5_GPB * y))

    kf, kb = ks
    kc = jnp.concatenate([kb[1:L][::-1], (kf[0] + kb[0])[None], kf[1:L]], axis=0)
    kc = blockdiag(kc, True)
    idx = (jnp.arange(L)[None, :] - jnp.arange(L)[:, None]) + (L - 1)
    tz = kc[idx]
    tz = tz.transpose(2, 0, 3, 1, 4).reshape(S5_NB, L * LANE, L * LANE)

    pbs, cab = [], []
    for d in range(2):
        wr, wi = ws[d]
        order = jnp.arange(L - 1, -1, -1) if d == 0 else jnp.arange(L)
        cols = []
        for w in (wr, wi):
            m = jnp.swapaxes(w[order], -1, -2)
            cols.append(blockdiag(m, True))
        pbs.append(jnp.concatenate(cols, axis=-1))
        car, cai = cas[d]
        order = jnp.arange(1, L + 1) if d == 0 else jnp.arange(L, 0, -1)
        rows = []
        for m in (car[order], -cai[order]):
            m = jnp.swapaxes(m, -1, -2)
            rows.append(blockdiag(m, True))
        cab.append(jnp.concatenate(rows, axis=-2))
    pb = jnp.concatenate(pbs, axis=-1)
    pb = pb.transpose(1, 0, 2, 3).reshape(S5_NB, L * LANE, 4 * S5_SW)
    ca = jnp.concatenate(cab, axis=-2)
    ca = ca.transpose(1, 2, 0, 3).reshape(S5_NB, 4 * S5_SW, L * LANE)
    dr = jnp.stack([ds[0][0], ds[1][0]], 0).reshape(2, S5_NB, 1, S5_SW).transpose(1, 0, 2, 3).reshape(2 * S5_NB, 1, S5_SW)
    di = jnp.stack([ds[0][1], ds[1][1]], 0).reshape(2, S5_NB, 1, S5_SW).transpose(1, 0, 2, 3).reshape(2 * S5_NB, 1, S5_SW)
    return tz.astype(BF16), pb.astype(BF16), ca.astype(BF16), dr, di


def _s5_p_kernel(u_ref, pb_ref, p_ref, *, nb, nc):
    res = jnp.dot(u_ref[...], pb_ref[...], preferred_element_type=F32)
    for b in range(nb):
        p_ref[:, b * 2 * S5_SW:(b + 1) * 2 * S5_SW] = res[b * nc:(b + 1) * nc]


def _s5_scan_kernel(p_ref, dr_ref, di_ref, s_ref, *, n_ctx, n_lat):
    rev = pl.program_id(0) % 2
    dr = dr_ref[...]
    di = di_ref[...]
    nbatch = p_ref.shape[1]

    def phase(base, n, carry):
        def body(step, carry):
            sr, si = carry
            row = base + jnp.where(rev == 0, step, n - 1 - step)
            s_ref[row, :, :S5_SW] = sr
            s_ref[row, :, S5_SW:] = si
            p = p_ref[row]
            nr = dr * sr - di * si + p[:, :S5_SW]
            ni = dr * si + di * sr + p[:, S5_SW:]
            return nr, ni
        return lax.fori_loop(0, n, body, carry)

    zero = jnp.zeros((nbatch, S5_SW), F32)
    carry = phase(0, n_ctx, (zero, zero))
    phase(n_ctx, n_lat, carry)


def _s5_y_kernel(u_ref, tz_ref, sf_ref, sb_ref, ca_ref, y_ref):
    y = jnp.dot(u_ref[...], tz_ref[...], preferred_element_type=F32)
    y += jnp.dot(sf_ref[...].astype(BF16), ca_ref[:2 * S5_SW, :], preferred_element_type=F32)
    y += jnp.dot(sb_ref[...].astype(BF16), ca_ref[2 * S5_SW:, :], preferred_element_type=F32)
    y_ref[...] = y


def s5_bidirectional(u_ctx, u_lat, weights):
    tz, pb, ca, dr, di = weights
    L = S5_L
    nb, t_ctx, _ = u_ctx.shape
    t_lat = u_lat.shape[1]
    assert t_ctx % L == 0 and t_lat % L == 0
    n_ctx, n_lat = t_ctx // L, t_lat // L
    nc = n_ctx + n_lat
    kw = L * LANE
    sw2 = 2 * S5_SW
    u = jnp.concatenate([u_ctx, u_lat], axis=1)
    ub = u.reshape(nb * nc, L, S5_NB, LANE).transpose(2, 0, 1, 3).reshape(S5_NB, nb * nc, kw).astype(BF16)

    p = pl.pallas_call(
        functools.partial(_s5_p_kernel, nb=nb, nc=nc),
        grid=(S5_NB, 2),
        in_specs=[pl.BlockSpec((None, nb * nc, kw), lambda j, d: (j, 0, 0)),
                  pl.BlockSpec((None, kw, sw2), lambda j, d: (j, 0, d))],
        out_specs=pl.BlockSpec((nc, nb * sw2), lambda j, d: (0, j * 2 + d)),
        out_shape=jax.ShapeDtypeStruct((nc, S5_NB * 2 * nb * sw2), F32),
        compiler_params=pltpu.CompilerParams(vmem_limit_bytes=VMEM_LIMIT),
        name="s5_chunk_inputs",
    )(ub, pb)

    p4 = p.reshape(nc, S5_NB * 2, nb, sw2)
    s4 = pl.pallas_call(
        functools.partial(_s5_scan_kernel, n_ctx=n_ctx, n_lat=n_lat),
        grid=(S5_NB * 2,),
        in_specs=[pl.BlockSpec((nc, None, nb, sw2), lambda g: (0, g, 0, 0)),
                  pl.BlockSpec((None, 1, S5_SW), lambda g: (g, 0, 0)),
                  pl.BlockSpec((None, 1, S5_SW), lambda g: (g, 0, 0))],
        out_specs=pl.BlockSpec((nc, None, nb, sw2), lambda g: (0, g, 0, 0)),
        out_shape=jax.ShapeDtypeStruct(p4.shape, F32),
        compiler_params=pltpu.CompilerParams(vmem_limit_bytes=VMEM_LIMIT),
        name="s5_state_scan",
    )(p4, dr, di)

    s2 = s4.reshape(nc, S5_NB * 2 * nb * sw2)
    yb = pl.pallas_call(
        _s5_y_kernel,
        grid=(S5_NB, nb),
        in_specs=[pl.BlockSpec((None, nc, kw), lambda j, b: (j, b, 0)),
                  pl.BlockSpec((None, kw, kw), lambda j, b: (j, 0, 0)),
                  pl.BlockSpec((nc, sw2), lambda j, b: (0, (j * 2) * nb + b)),
                  pl.BlockSpec((nc, sw2), lambda j, b: (0, (j * 2 + 1) * nb + b)),
                  pl.BlockSpec((None, 2 * sw2, kw), lambda j, b: (j, 0, 0))],
        out_specs=pl.BlockSpec((None, nc, kw), lambda j, b: (j, b, 0)),
        out_shape=jax.ShapeDtypeStruct((S5_NB, nb * nc, kw), F32),
        compiler_params=pltpu.CompilerParams(vmem_limit_bytes=VMEM_LIMIT),
        name="s5_output",
    )(ub, tz, s2, s2, ca)
    y = yb.reshape(S5_NB, nb, nc, L, LANE).transpose(1, 2, 3, 0, 4).reshape(nb, nc * L, S5_CH)
    return y[:, :t_ctx], y[:, t_ctx:]


def split_cols(t, sizes):
    return jnp.split(t, np.cumsum(sizes)[:-1].tolist(), axis=-1)


def heads(t, n):
    return t.reshape(t.shape[:-1] + (n, t.shape[-1] // n))


def flip_t(t):
    return None if t is None else jnp.flip(t, axis=1)


def l2norm(t):
    return t * lax.rsqrt(jnp.sum(t * t, axis=-1, keepdims=True) + 1e-6)


def layer_norm(t, g, b):
    mu = t.mean(-1, keepdims=True)
    var = jnp.square(t - mu).mean(-1, keepdims=True)
    return ((t - mu) * lax.rsqrt(var + EPS)) * g + b


def rms_norm_heads(o, g):
    y = o * lax.rsqrt(jnp.mean(o * o, axis=-1, keepdims=True) + EPS) * g
    return y.reshape(o.shape[:2] + (-1,))


def group_norm_heads(o, g):
    mu = o.mean(-1, keepdims=True)
    var = jnp.square(o - mu).mean(-1, keepdims=True)
    return ((o - mu) * lax.rsqrt(var + EPS)).reshape(o.shape[:2] + (-1,)) * g


def short_conv(t, w, on_grid):
    ch = t.shape[-1]
    if on_grid:
        b, n = t.shape[:2]
        rows = n // GRID_W
        tg = t.reshape(b, rows, GRID_W, ch)
        y = lax.conv_general_dilated(tg, w[:, :, None, :], (1, 1), 'SAME',
                                     dimension_numbers=('NHWC', 'HWIO', 'NHWC'), feature_group_count=ch)
        return y.reshape(b, n, ch)
    return lax.conv_general_dilated(t, w[CONV_K // 2][:, None, :], (1,), 'SAME',
                                    dimension_numbers=('NWC', 'WIO', 'NWC'), feature_group_count=ch)


def to_chunks(t):
    b, n, h = t.shape[:3]
    t = t.reshape((b, n // CHUNK, CHUNK, h) + t.shape[3:])
    return jnp.moveaxis(jnp.moveaxis(t, 3, 1), 2, 0)


def from_chunks(t):
    t = jnp.moveaxis(jnp.moveaxis(t, 0, 2), 1, 3)
    return t.reshape((t.shape[0], -1) + t.shape[3:])


def linear_scan(q, k, v, g, beta, s0, want_out):
    b, _, h, dk = k.shape
    dv = v.shape[-1]
    kc, vc = to_chunks(k), to_chunks(v)
    gcum = jnp.cumsum(to_chunks(g), axis=-1)
    diff = gcum[..., :, None] - gcum[..., None, :]
    incl = jnp.tril(jnp.ones((CHUNK, CHUNK), bool))
    if beta is None:
        u, w = vc, None
    else:
        bc = to_chunks(beta)[..., None]
        kb = kc * bc
        strict = jnp.tril(jnp.ones((CHUNK, CHUNK), bool), -1)
        a = jnp.where(strict, jnp.einsum('...ik,...jk->...ij', kb, kc) * jnp.exp(jnp.where(strict, diff, 0.0)), 0.0)
        rhs = jnp.concatenate([vc * bc, kb * jnp.exp(gcum)[..., None]], axis=-1)
        sol = lax.linalg.triangular_solve(a + jnp.eye(CHUNK, dtype=a.dtype), rhs, left_side=True,
                                          lower=True, unit_diagonal=True)
        u, w = sol[..., :dv], sol[..., dv:]
    k_end = kc * jnp.exp(gcum[..., -1:] - gcum)[..., None]
    g_end = jnp.exp(gcum[..., -1])[..., None, None]
    if want_out:
        qc = to_chunks(q)
        q_dec = qc * jnp.exp(gcum)[..., None]
        a_qk = jnp.where(incl, jnp.einsum('...ik,...jk->...ij', qc, kc) * jnp.exp(jnp.where(incl, diff, 0.0)), 0.0)
    else:
        q_dec, a_qk = None, None
    if s0 is None:
        s0 = jnp.zeros((b, h, dk, dv), F32)

    def step(s, inp):
        qd, ke, uc, wc, aqk, ge = inp
        vn = uc if wc is None else uc - jnp.einsum('bhck,bhkv->bhcv', wc, s)
        s_new = ge * s + jnp.einsum('bhck,bhcv->bhkv', ke, vn)
        if qd is None:
            return s_new, None
        return s_new, jnp.einsum('bhck,bhkv->bhcv', qd, s) + jnp.einsum('bhij,bhjv->bhiv', aqk, vn)

    s_fin, o = lax.scan(step, s0, (q_dec, k_end, u, w, a_qk, g_end))
    return (from_chunks(o) if want_out else None), s_fin


def mlstm_scan(q, k, v, log_i, log_f, s0, want_out):
    b, _, h, dk = k.shape
    dv = v.shape[-1]
    if s0 is None:
        s0 = (jnp.zeros((b, h, dk, dv), F32), jnp.zeros((b, h, dk), F32), jnp.zeros((b, h), F32))
    incl = jnp.tril(jnp.ones((CHUNK, CHUNK), bool))

    def step(carry, inp):
        c_prev, n_prev, m_prev = carry
        qc, kc, vc, ic, fc = inp
        bcum = jnp.cumsum(fc, axis=-1)
        b_end = bcum[..., -1]
        a = b_end[..., None] - bcum + ic
        m_new = jnp.maximum(b_end + m_prev, a.max(-1))
        w_state = jnp.exp(a - m_new[..., None])
        decay = jnp.exp(b_end + m_prev - m_new)
        c_new = decay[..., None, None] * c_prev + jnp.einsum('bhck,bhcv->bhkv', kc * w_state[..., None], vc)
        n_new = decay[..., None] * n_prev + jnp.einsum('bhck,bhc->bhk', kc, w_state)
        carry_new = (c_new, n_new, m_new)
        if qc is None:
            return carry_new, None
        dlog = jnp.where(incl, bcum[..., :, None] - bcum[..., None, :] + ic[..., None, :], -jnp.inf)
        inter = bcum + m_prev[..., None]
        m_t = jnp.maximum(inter, dlog.max(-1))
        s = jnp.einsum('bhik,bhjk->bhij', qc, kc) * jnp.exp(dlog - m_t[..., None])
        w_inter = jnp.exp(inter - m_t)[..., None]
        num = jnp.einsum('bhij,bhjv->bhiv', s, vc) + w_inter * jnp.einsum('bhik,bhkv->bhiv', qc, c_prev)
        den = s.sum(-1, keepdims=True) + w_inter * jnp.einsum('bhik,bhk->bhi', qc, n_prev)[..., None]
        return carry_new, num / jnp.maximum(jnp.abs(den), jnp.exp(-m_t)[..., None])

    xs = (to_chunks(q) if want_out else None, to_chunks(k), to_chunks(v), to_chunks(log_i), to_chunks(log_f))
    s_fin, hs = lax.scan(step, s0, xs)
    return (from_chunks(hs) if want_out else None), s_fin


def s5_discretize(lam_re, lam_im, log_dt, b_re, b_im):
    lr = jnp.minimum(lam_re, -1e-4)
    li = lam_im
    dt = jnp.exp(log_dt)[:, None]
    mag = jnp.exp(lr * dt)
    ab_re, ab_im = mag * jnp.cos(li * dt), mag * jnp.sin(li * dt)
    xr, xi, den = ab_re - 1.0, ab_im, lr * lr + li * li
    f_re = (xr * lr + xi * li) / den
    f_im = (xi * lr - xr * li) / den
    bb_re = f_re[..., None] * b_re - f_im[..., None] * b_im
    bb_im = f_re[..., None] * b_im + f_im[..., None] * b_re
    return ab_re, ab_im, bb_re, bb_im


def complex_affine_combine(e1, e2):
    a1r, a1i, b1r, b1i = e1
    a2r, a2i, b2r, b2i = e2
    return (a1r * a2r - a1i * a2i, a1r * a2i + a1i * a2r,
            a2r * b1r - a2i * b1i + b2r, a2r * b1i + a2i * b1r + b2i)


def s5_scan(ab_re, ab_im, bb_re, bb_im, c_re, c_im, u, s0, want_out):
    bu_re = jnp.einsum('btgh,gph->btgp', u, bb_re)
    bu_im = jnp.einsum('btgh,gph->btgp', u, bb_im)
    if s0 is not None:
        h0r, h0i = s0
        bu_re = bu_re.at[:, 0].add(ab_re * h0r - ab_im * h0i)
        bu_im = bu_im.at[:, 0].add(ab_re * h0i + ab_im * h0r)
    n = u.shape[1]
    a_re = jnp.broadcast_to(ab_re, (1, n) + ab_re.shape)
    a_im = jnp.broadcast_to(ab_im, (1, n) + ab_im.shape)
    _, _, hr, hi = lax.associative_scan(complex_affine_combine, (a_re, a_im, bu_re, bu_im), axis=1)
    s_fin = (hr[:, -1], hi[:, -1])
    if not want_out:
        return None, s_fin
    y = jnp.einsum('btgp,ghp->btgh', hr, c_re) - jnp.einsum('btgp,ghp->btgh', hi, c_im)
    return y, s_fin


def two_pass(scan_fn, ctx_args, lat_args, ctx_out, reverse):
    if reverse:
        ctx_args = [flip_t(t) for t in ctx_args]
        lat_args = [flip_t(t) for t in lat_args]
    o_ctx, s_ctx = scan_fn(*ctx_args, None, ctx_out)
    o_lat, _ = scan_fn(*lat_args, s_ctx, True)
    if reverse:
        o_ctx, o_lat = flip_t(o_ctx), flip_t(o_lat)
    return o_ctx, o_lat


def retention_log_decay(direction):
    expo = 5.0 + 2.0 * jnp.arange(RET_HEADS, dtype=F32) + direction
    return jnp.log1p(-jnp.exp2(-expo))


def gdn_retention_mixer(h_ctx, h_lat, w_in, w_out, conv_w, a_log, dt_bias, gdn_gain, ret_gain, ctx_out):
    def prep(h, on_grid):
        qa, ka, va, za, af, ab, bf, bb, qr, kr, vr, zr = split_cols(mm(h, w_in), EVEN_COLS)
        qkv = jax.nn.silu(short_conv(jnp.concatenate([qa, ka, va], -1), conv_w, on_grid))
        qa, ka, va = split_cols(qkv, EVEN_COLS[:3])
        q_g = l2norm(heads(qa, GDN_HEADS)) * HEAD_DIM ** -0.5
        k_g = l2norm(heads(ka, GDN_HEADS))
        v_g = heads(va, GDN_HEADS)
        gdn_dirs = []
        for d, (a_pre, b_pre) in enumerate(((af, bf), (ab, bb))):
            g = -jnp.exp(a_log[d]) * jax.nn.softplus(a_pre + dt_bias[d])
            gdn_dirs.append((q_g, k_g, v_g, g, jax.nn.sigmoid(b_pre)))
        q_r = heads(qr, RET_HEADS)
        k_r = heads(kr, RET_HEADS) * HEAD_DIM ** -0.5
        v_r = heads(vr, RET_HEADS)
        ret_dirs = [(q_r, k_r, v_r, jnp.broadcast_to(retention_log_decay(d), v_r.shape[:3]), None) for d in range(2)]
        return gdn_dirs, ret_dirs, za, zr

    gc, rc, za_c, zr_c = prep(h_ctx, False)
    gl, rl, za_l, zr_l = prep(h_lat, True)
    gdn = [two_pass(linear_scan, gc[d], gl[d], ctx_out, d == 1) for d in range(2)]
    ret = [two_pass(linear_scan, rc[d], rl[d], ctx_out, d == 1) for d in range(2)]

    def merge(o_g, o_r, za, zr):
        y_g = rms_norm_heads(o_g, gdn_gain) * jax.nn.silu(za)
        y_r = group_norm_heads(o_r, ret_gain) * jax.nn.silu(zr)
        return mm(jnp.concatenate([y_g, y_r], -1), w_out)

    y_lat = merge(gdn[0][1] + gdn[1][1], ret[0][1] + ret[1][1], za_l, zr_l)
    y_ctx = merge(gdn[0][0] + gdn[1][0], ret[0][0] + ret[1][0], za_c, zr_c) if ctx_out else None
    return y_ctx, y_lat


def mlstm_s5_mixer(h_ctx, h_lat, w_in, w_out, conv_w, gate_bias, mlstm_gain, lam_re, lam_im, log_dt,
                   b_re, b_im, c_re, c_im, d_skip, w_glu, b_glu, ctx_out):
    gb = gate_bias

    def prep(h, on_grid):
        qm, km, vm, om, i_f, i_b, f_f, f_b, u = split_cols(mm(h, w_in), ODD_COLS)
        qk = jax.nn.silu(short_conv(jnp.concatenate([qm, km], -1), conv_w, on_grid))
        qm, km = split_cols(qk, ODD_COLS[:2])
        q = heads(qm, MLSTM_HEADS)
        k = heads(km, MLSTM_HEADS) * HEAD_DIM ** -0.5
        v = heads(vm, MLSTM_HEADS)
        dirs = [(q, k, v, i_pre + gb[d, 0], jax.nn.log_sigmoid(f_pre + gb[d, 1]))
                for d, (i_pre, f_pre) in enumerate(((i_f, f_f), (i_b, f_b)))]
        return dirs, u, om

    mc, u_c, o_c = prep(h_ctx, False)
    ml, u_l, o_l = prep(h_lat, True)
    mls = [two_pass(mlstm_scan, mc[d], ml[d], ctx_out, d == 1) for d in range(2)]
    ys_c, ys_l = s5_bidirectional(u_c, u_l, _s5_weights(lam_re, lam_im, log_dt, b_re, b_im, c_re, c_im))

    def merge(h_m, y_s, u, og):
        y_m = group_norm_heads(h_m, mlstm_gain) * jax.nn.sigmoid(og)
        y = jax.nn.gelu(y_s + d_skip * u)
        y = y * jax.nn.sigmoid(mm(y, w_glu) + b_glu)
        return mm(jnp.concatenate([y_m, y], -1), w_out)

    y_lat = merge(mls[0][1] + mls[1][1], ys_l, u_l, o_l)
    y_ctx = merge(mls[0][0] + mls[1][0], ys_c, u_c, o_c) if ctx_out else None
    return y_ctx, y_lat


def expert_choice_ffn(h, w_router, w_gate, w_up, w_down):
    n, dm = h.shape[1], h.shape[2]
    cap = CAPACITY_FACTOR * n // N_EXPERTS
    aff = jax.nn.softmax(jnp.einsum('bnd,de->bne', h, w_router, precision=lax.Precision.HIGHEST), axis=-1)
    gate, idx = lax.top_k(jnp.swapaxes(aff, 1, 2), cap)
    xs = jax.vmap(lambda hb, ib: hb[ib])(h, idx)
    hid = jax.nn.silu(jnp.einsum('becd,edf->becf', xs, w_gate)) * jnp.einsum('becd,edf->becf', xs, w_up)
    ys = jnp.einsum('becf,efd->becd', hid, w_down) * gate[..., None]
    return jax.vmap(lambda ib, yb: jnp.zeros((n, dm), yb.dtype).at[ib.reshape(-1)].add(yb.reshape(-1, dm)))(idx, ys)


def kernel(x, c, ctx, c_ctx, w_mod, b_mod, ln1_g, ln1_b, ln2_g, ln2_b, w_router, w_gate, w_up, w_down,
           ev_w_in, ev_w_out, ev_conv, ev_a_log, ev_dt_bias, ev_gdn_norm, ev_ret_norm,
           od_w_in, od_w_out, od_conv, od_gate_bias, od_mlstm_norm, od_lam_re, od_lam_im, od_log_dt,
           od_b_re, od_b_im, od_c_re, od_c_im, od_d_skip, od_w_glu, od_b_glu):
    h_lat, h_ctx = x, ctx
    s_lat = jax.nn.silu(c)
    s_ctx = jax.nn.silu(c_ctx)
    for l in range(DEPTH):
        last = l == DEPTH - 1
        sh1, sc1, g1, sh2, sc2, g2 = jnp.split((s_lat @ w_mod[l] + b_mod[l])[:, None, :], 6, axis=-1)
        csh1, csc1, cg1, csh2, csc2, cg2 = jnp.split(s_ctx @ w_mod[l] + b_mod[l], 6, axis=-1)
        in_lat = h_lat * (1.0 + sc1) + sh1
        in_ctx = h_ctx * (1.0 + csc1) + csh1
        if l % 2 == 0:
            e = l // 2
            y_ctx, y_lat = gdn_retention_mixer(in_ctx, in_lat, ev_w_in[e], ev_w_out[e], ev_conv[e], ev_a_log[e],
                                               ev_dt_bias[e], ev_gdn_norm[e], ev_ret_norm[e], not last)
        else:
            o = l // 2
            y_ctx, y_lat = mlstm_s5_mixer(in_ctx, in_lat, od_w_in[o], od_w_out[o], od_conv[o], od_gate_bias[o],
                                          od_mlstm_norm[o], od_lam_re[o], od_lam_im[o], od_log_dt[o],
                                          od_b_re[o], od_b_im[o], od_c_re[o], od_c_im[o], od_d_skip[o],
                                          od_w_glu[o], od_b_glu[o], not last)
        h_lat = layer_norm(ALPHA * h_lat + g1 * y_lat, ln1_g[l], ln1_b[l])
        f_lat = expert_choice_ffn(h_lat * (1.0 + sc2) + sh2, w_router[l], w_gate[l], w_up[l], w_down[l])
        h_lat = layer_norm(ALPHA * h_lat + g2 * f_lat, ln2_g[l], ln2_b[l])
        if not last:
            h_ctx = layer_norm(ALPHA * h_ctx + cg1 * y_ctx, ln1_g[l], ln1_b[l])
            f_ctx = expert_choice_ffn(h_ctx * (1.0 + csc2) + csh2, w_router[l], w_gate[l], w_up[l], w_down[l])
            h_ctx = layer_norm(ALPHA * h_ctx + cg2 * f_ctx, ln2_g[l], ln2_b[l])
    return h_lat
```

```python
import functools
import math

import jax
import jax.numpy as jnp
import numpy as np
from jax import lax
from jax.experimental import pallas as pl
from jax.experimental.pallas import tpu as pltpu

D_MODEL = 1024
BATCH = 4
SEQ = 4096
DEPTH = 4
GRID_W = 64
CTX_LEN = 256
CHUNK = 64
CONV_K = 3
HEAD_DIM = D_MODEL // 8
GDN_HEADS = 4
RET_HEADS = 4
MLSTM_HEADS = 4
GDN_W = GDN_HEADS * HEAD_DIM
RET_W = RET_HEADS * HEAD_DIM
MLSTM_W = MLSTM_HEADS * HEAD_DIM
S5_CH = D_MODEL // 2
S5_GROUP = 16
S5_GROUPS = S5_CH // S5_GROUP
S5_STATE = 64
N_EXPERTS = 16
EXPERT_FF = 2 * D_MODEL
CAPACITY_FACTOR = 2
ALPHA = (2 * DEPTH) ** 0.25
EPS = 1e-5
EVEN_COLS = (GDN_W, GDN_W, GDN_W, GDN_W, GDN_HEADS, GDN_HEADS, GDN_HEADS, GDN_HEADS,
             RET_W, RET_W, RET_W, RET_W)
ODD_COLS = (MLSTM_W, MLSTM_W, MLSTM_W, MLSTM_W, MLSTM_HEADS, MLSTM_HEADS, MLSTM_HEADS, MLSTM_HEADS, S5_CH)
F32 = jnp.float32
BF16 = jnp.bfloat16

LANE = 128


def _mm_kernel(a_ref, b_ref, o_ref):
    o_ref[...] = jnp.dot(a_ref[...].astype(BF16), b_ref[...].astype(BF16), preferred_element_type=F32)


def pmatmul(a, b, tm=512, tn=512):
    m, k = a.shape
    n = b.shape[1]
    n_pad = -n % LANE
    if n_pad:
        b = jnp.pad(b, ((0, 0), (0, n_pad)))
    np_ = n + n_pad
    tn = min(tn, np_)
    while np_ % tn:
        tn -= LANE
    tm = min(tm, m)
    assert m % tm == 0
    out = pl.pallas_call(
        _mm_kernel,
        grid=(m // tm, np_ // tn),
        in_specs=[pl.BlockSpec((tm, k), lambda i, j: (i, 0)),
                  pl.BlockSpec((k, tn), lambda i, j: (0, j))],
        out_specs=pl.BlockSpec((tm, tn), lambda i, j: (i, j)),
        out_shape=jax.ShapeDtypeStruct((m, np_), F32),
        name="pmatmul",
    )(a, b)
    return out[:, :n] if n_pad else out


def mm(x, w):
    lead = x.shape[:-1]
    return pmatmul(x.reshape(-1, x.shape[-1]), w).reshape(lead + (w.shape[-1],))


S5_L = 16
S5_NB = S5_CH // LANE
S5_GPB = LANE // S5_GROUP
S5_SW = S5_GPB * S5_STATE
VMEM_LIMIT = 56 * 1024 * 1024


def _s5_weights(lam_re, lam_im, log_dt, b_re, b_im, c_re, c_im):
    L, G = S5_L, S5_GROUPS
    hp = lax.Precision.HIGHEST
    taus = jnp.arange(L + 1, dtype=F32)[:, None, None]
    ks, ws, cas, ds = [], [], [], []
    for d in range(2):
        lr = jnp.minimum(lam_re[d], -1e-4)
        li = lam_im[d]
        dt = jnp.exp(log_dt[d])[:, None]
        mag = jnp.exp(lr * dt)
        ab_re, ab_im = mag * jnp.cos(li * dt), mag * jnp.sin(li * dt)
        xr, xi, den = ab_re - 1.0, ab_im, lr * lr + li * li
        f_re = (xr * lr + xi * li) / den
        f_im = (xi * lr - xr * li) / den
        bb_re = f_re[..., None] * b_re - f_im[..., None] * b_im
        bb_im = f_re[..., None] * b_im + f_im[..., None] * b_re
        pmag = jnp.exp(taus * (lr * dt))
        ar, ai = pmag * jnp.cos(taus * (li * dt)), pmag * jnp.sin(taus * (li * dt))
        wr = ar[..., None] * bb_re - ai[..., None] * bb_im
        wi = ar[..., None] * bb_im + ai[..., None] * bb_re
        k = (jnp.einsum('gop,tgpi->tgio', c_re, wr, precision=hp)
             - jnp.einsum('gop,tgpi->tgio', c_im, wi, precision=hp))
        car = c_re[None] * ar[:, :, None, :] - c_im[None] * ai[:, :, None, :]
        cai = c_re[None] * ai[:, :, None, :] + c_im[None] * ar[:, :, None, :]
        ks.append(k)
        ws.append((wr, wi))
        cas.append((car, cai))
        ds.append((ar[L], ai[L]))

    eye = jnp.eye(S5_GPB, dtype=F32)

    def blockdiag(t, in_axis_first):
        lead = t.shape[:-3]
        x, y = t.shape[-2:]
        t = t.reshape(lead + (S5_NB, S5_GPB, x, y))
        t = jnp.einsum('...jaxy,ab->...jaxby', t, eye)
        return t.reshape(lead + (S5_NB, S5_GPB * x, S5_GPB * y))

    kf, kb = ks
    kc = jnp.concatenate([kb[1:L][::-1], (kf[0] + kb[0])[None], kf[1:L]], axis=0)
    kc = blockdiag(kc, True)
    idx = (jnp.arange(L)[None, :] - jnp.arange(L)[:, None]) + (L - 1)
    tz = kc[idx]
    tz = tz.transpose(2, 0, 3, 1, 4).reshape(S5_NB, L * LANE, L * LANE)

    pbs, cab = [], []
    for d in range(2):
        wr, wi = ws[d]
        order = jnp.arange(L - 1, -1, -1) if d == 0 else jnp.arange(L)
        cols = []
        for w in (wr, wi):
            m = jnp.swapaxes(w[order], -1, -2)
            cols.append(blockdiag(m, True))
        pbs.append(jnp.concatenate(cols, axis=-1))
        car, cai = cas[d]
        order = jnp.arange(1, L + 1) if d == 0 else jnp.arange(L, 0, -1)
        rows = []
        for m in (car[order], -cai[order]):
            m = jnp.swapaxes(m, -1, -2)
            rows.append(blockdiag(m, True))
        cab.append(jnp.concatenate(rows, axis=-2))
    pb = jnp.concatenate(pbs, axis=-1)
    pb = pb.transpose(1, 0, 2, 3).reshape(S5_NB, L * LANE, 4 * S5_SW)
    ca = jnp.concatenate(cab, axis=-2)
    ca = ca.transpose(1, 2, 0, 3).reshape(S5_NB, 4 * S5_SW, L * LANE)
    dr = jnp.stack([ds[0][0], ds[1][0]], 0).reshape(2, S5_NB, 1, S5_SW).transpose(1, 0, 2, 3).reshape(2 * S5_NB, 1, S5_SW)
    di = jnp.stack([ds[0][1], ds[1][1]], 0).reshape(2, S5_NB, 1, S5_SW).transpose(1, 0, 2, 3).reshape(2 * S5_NB, 1, S5_SW)
    return tz.astype(BF16), pb.astype(BF16), ca.astype(BF16), dr, di


def _s5_p_kernel(u_ref, pb_ref, p_ref, *, nb, nc):
    res = jnp.dot(u_ref[...], pb_ref[...], preferred_element_type=F32)
    for b in range(nb):
        p_ref[:, b * 2 * S5_SW:(b + 1) * 2 * S5_SW] = res[b * nc:(b + 1) * nc]


def _s5_scan_kernel(p_ref, dr_ref, di_ref, s_ref, *, n_ctx, n_lat):
    rev = pl.program_id(0) % 2
    dr = dr_ref[...]
    di = di_ref[...]
    nbatch = p_ref.shape[1]

    def phase(base, n, carry):
        def body(step, carry):
            sr, si = carry
            row = base + jnp.where(rev == 0, step, n - 1 - step)
            s_ref[row, :, :S5_SW] = sr
            s_ref[row, :, S5_SW:] = si
            p = p_ref[row]
            nr = dr * sr - di * si + p[:, :S5_SW]
            ni = dr * si + di * sr + p[:, S5_SW:]
            return nr, ni
        return lax.fori_loop(0, n, body, carry)

    zero = jnp.zeros((nbatch, S5_SW), F32)
    carry = phase(0, n_ctx, (zero, zero))
    phase(n_ctx, n_lat, carry)


def _s5_y_kernel(u_ref, tz_ref, sf_ref, sb_ref, ca_ref, y_ref):
    y = jnp.dot(u_ref[...], tz_ref[...], preferred_element_type=F32)
    y += jnp.dot(sf_ref[...].astype(BF16), ca_ref[:2 * S5_SW, :], preferred_element_type=F32)
    y += jnp.dot(sb_ref[...].astype(BF16), ca_ref[2 * S5_SW:, :], preferred_element_type=F32)
    y_ref[...] = y


def s5_bidirectional(u_ctx, u_lat, weights):
    tz, pb, ca, dr, di = weights
    L = S5_L
    nb, t_ctx, _ = u_ctx.shape
    t_lat = u_lat.shape[1]
    assert t_ctx % L == 0 and t_lat % L == 0
    n_ctx, n_lat = t_ctx // L, t_lat // L
    nc = n_ctx + n_lat
    kw = L * LANE
    sw2 = 2 * S5_SW
    u = jnp.concatenate([u_ctx, u_lat], axis=1)
    ub = u.reshape(nb * nc, L, S5_NB, LANE).transpose(2, 0, 1, 3).reshape(S5_NB, nb * nc, kw).astype(BF16)

    p = pl.pallas_call(
        functools.partial(_s5_p_kernel, nb=nb, nc=nc),
        grid=(S5_NB, 2),
        in_specs=[pl.BlockSpec((None, nb * nc, kw), lambda j, d: (j, 0, 0)),
                  pl.BlockSpec((None, kw, sw2), lambda j, d: (j, 0, d))],
        out_specs=pl.BlockSpec((nc, nb * sw2), lambda j, d: (0, j * 2 + d)),
        out_shape=jax.ShapeDtypeStruct((nc, S5_NB * 2 * nb * sw2), F32),
        compiler_params=pltpu.CompilerParams(vmem_limit_bytes=VMEM_LIMIT),
        name="s5_chunk_inputs",
    )(ub, pb)

    p4 = p.reshape(nc, S5_NB * 2, nb, sw2)
    s4 = pl.pallas_call(
        functools.partial(_s5_scan_kernel, n_ctx=n_ctx, n_lat=n_lat),
        grid=(S5_NB * 2,),
        in_specs=[pl.BlockSpec((nc, None, nb, sw2), lambda g: (0, g, 0, 0)),
                  pl.BlockSpec((None, 1, S5_SW), lambda g: (g, 0, 0)),
                  pl.BlockSpec((None, 1, S5_SW), lambda g: (g, 0, 0))],
        out_specs=pl.BlockSpec((nc, None, nb, sw2), lambda g: (0, g, 0, 0)),
        out_shape=jax.ShapeDtypeStruct(p4.shape, F32),
        compiler_params=pltpu.CompilerParams(vmem_limit_bytes=VMEM_LIMIT),
        name="s5_state_scan",
    )(p4, dr, di)

    s2 = s4.reshape(nc, S5_NB * 2 * nb * sw2)
    yb = pl.pallas_call(
        _s5_y_kernel,
        grid=(S5_NB, nb),
        in_specs=[pl.BlockSpec((None, nc, kw), lambda j, b: (j, b, 0)),
                  pl.BlockSpec((None, kw, kw), lambda j, b: (j, 0, 0)),
                  pl.BlockSpec((nc, sw2), lambda j, b: (0, (j * 2) * nb + b)),
                  pl.BlockSpec((nc, sw2), lambda j, b: (0, (j * 2 + 1) * nb + b)),
                  pl.BlockSpec((None, 2 * sw2, kw), lambda j, b: (j, 0, 0))],
        out_specs=pl.BlockSpec((None, nc, kw), lambda j, b: (j, b, 0)),
        out_shape=jax.ShapeDtypeStruct((S5_NB, nb * nc, kw), F32),
        compiler_params=pltpu.CompilerParams(vmem_limit_bytes=VMEM_LIMIT),
        name="s5_output",
    )(ub, tz, s2, s2, ca)
    y = yb.reshape(S5_NB, nb, nc, L, LANE).transpose(1, 2, 3, 0, 4).reshape(nb, nc * L, S5_CH)
    return y[:, :t_ctx], y[:, t_ctx:]


ROUTE_TN = 512
MOE_TN = 1024
SEL_ROWS = 128
FF_SPLIT = 2
AFF_PARTS = 3


def _route_kernel(h_ref, sc_ref, sh_ref, wr_ref, hm_ref, afft_ref, asp_ref):
    hm = h_ref[...] * (1.0 + sc_ref[...]) + sh_ref[...]
    hm_ref[...] = hm.astype(BF16)
    logits = jnp.dot(hm, wr_ref[...], precision=lax.Precision.HIGHEST, preferred_element_type=F32)
    lane = lax.broadcasted_iota(jnp.int32, logits.shape, 1)
    logits = jnp.where(lane < N_EXPERTS, logits, -jnp.inf)
    ex = jnp.exp(logits - jnp.max(logits, axis=1, keepdims=True))
    aff = ex / jnp.sum(ex, axis=1, keepdims=True)
    afft_ref[...] = aff.T[:N_EXPERTS, :]
    hi = aff.astype(BF16).astype(F32)
    mid = (aff - hi).astype(BF16).astype(F32)
    lo = (aff - hi - mid).astype(BF16).astype(F32)
    asp = hi + pltpu.roll(mid, N_EXPERTS, 1) + pltpu.roll(lo, 2 * N_EXPERTS, 1)
    asp_ref[...] = asp.astype(BF16)


def _select_kernel(aff_ref, pos_ref, *, nblk_log2, cap):
    a = aff_ref[...]
    r = a.shape[0]
    bits = pltpu.bitcast(a, jnp.int32)
    ri = lax.broadcasted_iota(jnp.int32, (r, r), 0)
    rj = lax.broadcasted_iota(jnp.int32, (r, r), 1)
    same = (ri >> nblk_log2) == (rj >> nblk_log2)
    gm = jnp.where(same, 1.0, 0.0).astype(BF16)
    lm = jnp.where(same & (rj < ri), 1.0, 0.0).astype(BF16)
    li = lax.broadcasted_iota(jnp.int32, (LANE, LANE), 0)
    lj = lax.broadcasted_iota(jnp.int32, (LANE, LANE), 1)
    um = jnp.where(li <= lj, 1.0, 0.0).astype(BF16)

    def group_count(mask):
        rc = jnp.sum(jnp.where(mask, 1.0, 0.0), axis=1, keepdims=True)
        gc = jnp.dot(gm, jnp.broadcast_to(rc, (r, LANE)).astype(BF16), preferred_element_type=F32)
        return gc[:, :1]

    def bisect(i, thr):
        cand = thr | jnp.left_shift(1, 29 - i)
        return jnp.where(group_count(bits >= cand) >= cap, cand, thr)

    thr = lax.fori_loop(0, 30, bisect, jnp.zeros((r, 1), jnp.int32))

    def prefix(mask):
        x = jnp.where(mask, 1.0, 0.0)
        inc = jnp.dot(x.astype(BF16), um, preferred_element_type=F32)
        tot = jnp.broadcast_to(inc[:, LANE - 1:LANE], (r, LANE)).astype(BF16)
        return inc - x + jnp.dot(lm, tot, preferred_element_type=F32)

    gt = bits > thr
    eq = bits == thr
    need = cap - group_count(gt)
    sel = gt | (eq & (prefix(eq) < need))
    pos_ref[...] = jnp.where(sel, prefix(sel), -1.0).astype(jnp.int32)


def _onehot(pos_ref, cap):
    rows = lax.broadcasted_iota(jnp.int32, (cap, LANE), 0)
    blocks = [jnp.where(pos_ref[k:k + 1, :] == rows, 1.0, 0.0).astype(BF16) for k in range(pos_ref.shape[0])]
    return jnp.concatenate(blocks, axis=1)


def _ffn_kernel(pos_ref, hb_ref, asp_ref, wg_ref, wu_ref, wd_ref, ys_ref, xs_acc, g_acc, *, cap):
    e = pl.program_id(0)
    kt = pl.program_id(2)

    @pl.when(kt == 0)
    def _():
        xs_acc[...] = jnp.zeros_like(xs_acc)
        g_acc[...] = jnp.zeros_like(g_acc)

    oh = _onehot(pos_ref, cap)
    xs_acc[...] += jnp.dot(oh, hb_ref[...], preferred_element_type=F32)
    g_acc[...] += jnp.dot(oh, asp_ref[...], preferred_element_type=F32)

    @pl.when(kt == pl.num_programs(2) - 1)
    def _():
        xs = xs_acc[...].astype(BF16)
        g = g_acc[...]
        lane = lax.broadcasted_iota(jnp.int32, g.shape, 1)
        gate = jnp.sum(jnp.where((lane & (N_EXPERTS - 1)) == e, g, 0.0), axis=1, keepdims=True)
        fw = EXPERT_FF // FF_SPLIT
        y = jnp.zeros((cap, D_MODEL), F32)
        for f in range(FF_SPLIT):
            hg = jnp.dot(xs, wg_ref[:, f * fw:(f + 1) * fw], preferred_element_type=F32)
            hu = jnp.dot(xs, wu_ref[:, f * fw:(f + 1) * fw], preferred_element_type=F32)
            hid = (hg * jax.nn.sigmoid(hg)) * hu
            y += jnp.dot(hid.astype(BF16), wd_ref[f * fw:(f + 1) * fw, :], preferred_element_type=F32)
        ys_ref[...] = (y * gate).astype(BF16)


def _ln_rows(z, g, b):
    mu = jnp.mean(z, axis=-1, keepdims=True)
    zc = z - mu
    var = jnp.mean(zc * zc, axis=-1, keepdims=True)
    return zc * lax.rsqrt(var + EPS) * g + b


def _combine_kernel(pos_ref, ys_ref, h_ref, g2_ref, lng_ref, lnb_ref, o_ref, acc, *, cap):
    e = pl.program_id(2)

    @pl.when(e == 0)
    def _():
        acc[...] = jnp.zeros_like(acc)

    oh = _onehot(pos_ref, cap)
    acc[...] += lax.dot_general(oh, ys_ref[...], (((0,), (0,)), ((), ())), preferred_element_type=F32)

    @pl.when(e == pl.num_programs(2) - 1)
    def _():
        z = ALPHA * h_ref[...] + g2_ref[...] * acc[...]
        o_ref[...] = _ln_rows(z, lng_ref[...], lnb_ref[...])


def moe_block(h, sc, sh, g2, ln_g, ln_b, w_router, wg, wu, wd):
    nb, n, dm = h.shape
    cap = CAPACITY_FACTOR * n // N_EXPERTS
    nblk = n // LANE
    assert n % LANE == 0 and nblk & (nblk - 1) == 0
    tn_r = min(ROUTE_TN, n)
    tn = min(MOE_TN, n)
    wr = jnp.pad(w_router, ((0, 0), (0, LANE - N_EXPERTS)))
    row = lambda v: v.reshape(1, dm)

    hm, afft, asp = pl.pallas_call(
        _route_kernel,
        grid=(nb, n // tn_r),
        in_specs=[pl.BlockSpec((None, tn_r, dm), lambda b, t: (b, t, 0)),
                  pl.BlockSpec((None, 1, dm), lambda b, t: (b, 0, 0)),
                  pl.BlockSpec((None, 1, dm), lambda b, t: (b, 0, 0)),
                  pl.BlockSpec((dm, LANE), lambda b, t: (0, 0))],
        out_specs=[pl.BlockSpec((None, tn_r, dm), lambda b, t: (b, t, 0)),
                   pl.BlockSpec((None, N_EXPERTS, tn_r), lambda b, t: (b, 0, t)),
                   pl.BlockSpec((None, tn_r, LANE), lambda b, t: (b, t, 0))],
        out_shape=[jax.ShapeDtypeStruct((nb, n, dm), BF16),
                   jax.ShapeDtypeStruct((nb, N_EXPERTS, n), F32),
                   jax.ShapeDtypeStruct((nb, n, LANE), BF16)],
        name="moe_route",
    )(h, sc, sh, wr)

    rows_total = nb * N_EXPERTS * nblk
    rb = max(SEL_ROWS, N_EXPERTS * nblk)
    assert rows_total % rb == 0
    pos = pl.pallas_call(
        functools.partial(_select_kernel, nblk_log2=nblk.bit_length() - 1, cap=cap),
        grid=(rows_total // rb,),
        in_specs=[pl.BlockSpec((rb, LANE), lambda i: (i, 0))],
        out_specs=pl.BlockSpec((rb, LANE), lambda i: (i, 0)),
        out_shape=jax.ShapeDtypeStruct((rows_total, LANE), jnp.int32),
        name="moe_select",
    )(afft.reshape(rows_total, LANE))
    pos = pos.reshape(nb, N_EXPERTS, nblk, LANE)

    tb = tn // LANE
    ys = pl.pallas_call(
        functools.partial(_ffn_kernel, cap=cap),
        grid=(N_EXPERTS, nb, n // tn),
        in_specs=[pl.BlockSpec((None, None, tb, LANE), lambda e, b, k: (b, e, k, 0)),
                  pl.BlockSpec((None, tn, dm), lambda e, b, k: (b, k, 0)),
                  pl.BlockSpec((None, tn, LANE), lambda e, b, k: (b, k, 0)),
                  pl.BlockSpec((None, dm, EXPERT_FF), lambda e, b, k: (e, 0, 0)),
                  pl.BlockSpec((None, dm, EXPERT_FF), lambda e, b, k: (e, 0, 0)),
                  pl.BlockSpec((None, EXPERT_FF, dm), lambda e, b, k: (e, 0, 0))],
        out_specs=pl.BlockSpec((None, None, cap, dm), lambda e, b, k: (b, e, 0, 0)),
        out_shape=jax.ShapeDtypeStruct((nb, N_EXPERTS, cap, dm), BF16),
        scratch_shapes=[pltpu.VMEM((cap, dm), F32), pltpu.VMEM((cap, LANE), F32)],
        compiler_params=pltpu.CompilerParams(vmem_limit_bytes=VMEM_LIMIT),
        name="moe_ffn",
    )(pos, hm, asp, wg, wu, wd)

    return pl.pallas_call(
        functools.partial(_combine_kernel, cap=cap),
        grid=(nb, n // tn, N_EXPERTS),
        in_specs=[pl.BlockSpec((None, None, tb, LANE), lambda b, t, e: (b, e, t, 0)),
                  pl.BlockSpec((None, None, cap, dm), lambda b, t, e: (b, e, 0, 0)),
                  pl.BlockSpec((None, tn, dm), lambda b, t, e: (b, t, 0)),
                  pl.BlockSpec((None, 1, dm), lambda b, t, e: (b, 0, 0)),
                  pl.BlockSpec((1, dm), lambda b, t, e: (0, 0)),
                  pl.BlockSpec((1, dm), lambda b, t, e: (0, 0))],
        out_specs=pl.BlockSpec((None, tn, dm), lambda b, t, e: (b, t, 0)),
        out_shape=jax.ShapeDtypeStruct((nb, n, dm), F32),
        scratch_shapes=[pltpu.VMEM((tn, dm), F32)],
        compiler_params=pltpu.CompilerParams(vmem_limit_bytes=VMEM_LIMIT),
        name="moe_combine",
    )(pos, ys, h, g2, row(ln_g), row(ln_b))


def split_cols(t, sizes):
    return jnp.split(t, np.cumsum(sizes)[:-1].tolist(), axis=-1)


def heads(t, n):
    return t.reshape(t.shape[:-1] + (n, t.shape[-1] // n))


def flip_t(t):
    return None if t is None else jnp.flip(t, axis=1)


def l2norm(t):
    return t * lax.rsqrt(jnp.sum(t * t, axis=-1, keepdims=True) + 1e-6)


def layer_norm(t, g, b):
    mu = t.mean(-1, keepdims=True)
    var = jnp.square(t - mu).mean(-1, keepdims=True)
    return ((t - mu) * lax.rsqrt(var + EPS)) * g + b


def rms_norm_heads(o, g):
    y = o * lax.rsqrt(jnp.mean(o * o, axis=-1, keepdims=True) + EPS) * g
    return y.reshape(o.shape[:2] + (-1,))


def group_norm_heads(o, g):
    mu = o.mean(-1, keepdims=True)
    var = jnp.square(o - mu).mean(-1, keepdims=True)
    return ((o - mu) * lax.rsqrt(var + EPS)).reshape(o.shape[:2] + (-1,)) * g


def short_conv(t, w, on_grid):
    ch = t.shape[-1]
    if on_grid:
        b, n = t.shape[:2]
        rows = n // GRID_W
        tg = t.reshape(b, rows, GRID_W, ch)
        y = lax.conv_general_dilated(tg, w[:, :, None, :], (1, 1), 'SAME',
                                     dimension_numbers=('NHWC', 'HWIO', 'NHWC'), feature_group_count=ch)
        return y.reshape(b, n, ch)
    return lax.conv_general_dilated(t, w[CONV_K // 2][:, None, :], (1,), 'SAME',
                                    dimension_numbers=('NWC', 'WIO', 'NWC'), feature_group_count=ch)


def to_chunks(t):
    b, n, h = t.shape[:3]
    t = t.reshape((b, n // CHUNK, CHUNK, h) + t.shape[3:])
    return jnp.moveaxis(jnp.moveaxis(t, 3, 1), 2, 0)


def from_chunks(t):
    t = jnp.moveaxis(jnp.moveaxis(t, 0, 2), 1, 3)
    return t.reshape((t.shape[0], -1) + t.shape[3:])


def linear_scan(q, k, v, g, beta, s0, want_out):
    b, _, h, dk = k.shape
    dv = v.shape[-1]
    kc, vc = to_chunks(k), to_chunks(v)
    gcum = jnp.cumsum(to_chunks(g), axis=-1)
    diff = gcum[..., :, None] - gcum[..., None, :]
    incl = jnp.tril(jnp.ones((CHUNK, CHUNK), bool))
    if beta is None:
        u, w = vc, None
    else:
        bc = to_chunks(beta)[..., None]
        kb = kc * bc
        strict = jnp.tril(jnp.ones((CHUNK, CHUNK), bool), -1)
        a = jnp.where(strict, jnp.einsum('...ik,...jk->...ij', kb, kc) * jnp.exp(jnp.where(strict, diff, 0.0)), 0.0)
        rhs = jnp.concatenate([vc * bc, kb * jnp.exp(gcum)[..., None]], axis=-1)
        sol = lax.linalg.triangular_solve(a + jnp.eye(CHUNK, dtype=a.dtype), rhs, left_side=True,
                                          lower=True, unit_diagonal=True)
        u, w = sol[..., :dv], sol[..., dv:]
    k_end = kc * jnp.exp(gcum[..., -1:] - gcum)[..., None]
    g_end = jnp.exp(gcum[..., -1])[..., None, None]
    if want_out:
        qc = to_chunks(q)
        q_dec = qc * jnp.exp(gcum)[..., None]
        a_qk = jnp.where(incl, jnp.einsum('...ik,...jk->...ij', qc, kc) * jnp.exp(jnp.where(incl, diff, 0.0)), 0.0)
    else:
        q_dec, a_qk = None, None
    if s0 is None:
        s0 = jnp.zeros((b, h, dk, dv), F32)

    def step(s, inp):
        qd, ke, uc, wc, aqk, ge = inp
        vn = uc if wc is None else uc - jnp.einsum('bhck,bhkv->bhcv', wc, s)
        s_new = ge * s + jnp.einsum('bhck,bhcv->bhkv', ke, vn)
        if qd is None:
            return s_new, None
        return s_new, jnp.einsum('bhck,bhkv->bhcv', qd, s) + jnp.einsum('bhij,bhjv->bhiv', aqk, vn)

    s_fin, o = lax.scan(step, s0, (q_dec, k_end, u, w, a_qk, g_end))
    return (from_chunks(o) if want_out else None), s_fin


def mlstm_scan(q, k, v, log_i, log_f, s0, want_out):
    b, _, h, dk = k.shape
    dv = v.shape[-1]
    if s0 is None:
        s0 = (jnp.zeros((b, h, dk, dv), F32), jnp.zeros((b, h, dk), F32), jnp.zeros((b, h), F32))
    incl = jnp.tril(jnp.ones((CHUNK, CHUNK), bool))

    def step(carry, inp):
        c_prev, n_prev, m_prev = carry
        qc, kc, vc, ic, fc = inp
        bcum = jnp.cumsum(fc, axis=-1)
        b_end = bcum[..., -1]
        a = b_end[..., None] - bcum + ic
        m_new = jnp.maximum(b_end + m_prev, a.max(-1))
        w_state = jnp.exp(a - m_new[..., None])
        decay = jnp.exp(b_end + m_prev - m_new)
        c_new = decay[..., None, None] * c_prev + jnp.einsum('bhck,bhcv->bhkv', kc * w_state[..., None], vc)
        n_new = decay[..., None] * n_prev + jnp.einsum('bhck,bhc->bhk', kc, w_state)
        carry_new = (c_new, n_new, m_new)
        if qc is None:
            return carry_new, None
        dlog = jnp.where(incl, bcum[..., :, None] - bcum[..., None, :] + ic[..., None, :], -jnp.inf)
        inter = bcum + m_prev[..., None]
        m_t = jnp.maximum(inter, dlog.max(-1))
        s = jnp.einsum('bhik,bhjk->bhij', qc, kc) * jnp.exp(dlog - m_t[..., None])
        w_inter = jnp.exp(inter - m_t)[..., None]
        num = jnp.einsum('bhij,bhjv->bhiv', s, vc) + w_inter * jnp.einsum('bhik,bhkv->bhiv', qc, c_prev)
        den = s.sum(-1, keepdims=True) + w_inter * jnp.einsum('bhik,bhk->bhi', qc, n_prev)[..., None]
        return carry_new, num / jnp.maximum(jnp.abs(den), jnp.exp(-m_t)[..., None])

    xs = (to_chunks(q) if want_out else None, to_chunks(k), to_chunks(v), to_chunks(log_i), to_chunks(log_f))
    s_fin, hs = lax.scan(step, s0, xs)
    return (from_chunks(hs) if want_out else None), s_fin


def two_pass(scan_fn, ctx_args, lat_args, ctx_out, reverse):
    if reverse:
        ctx_args = [flip_t(t) for t in ctx_args]
        lat_args = [flip_t(t) for t in lat_args]
    o_ctx, s_ctx = scan_fn(*ctx_args, None, ctx_out)
    o_lat, _ = scan_fn(*lat_args, s_ctx, True)
    if reverse:
        o_ctx, o_lat = flip_t(o_ctx), flip_t(o_lat)
    return o_ctx, o_lat


def retention_log_decay(direction):
    expo = 5.0 + 2.0 * jnp.arange(RET_HEADS, dtype=F32) + direction
    return jnp.log1p(-jnp.exp2(-expo))


def gdn_retention_mixer(h_ctx, h_lat, w_in, w_out, conv_w, a_log, dt_bias, gdn_gain, ret_gain, ctx_out):
    def prep(h, on_grid):
        qa, ka, va, za, af, ab, bf, bb, qr, kr, vr, zr = split_cols(mm(h, w_in), EVEN_COLS)
        qkv = jax.nn.silu(short_conv(jnp.concatenate([qa, ka, va], -1), conv_w, on_grid))
        qa, ka, va = split_cols(qkv, EVEN_COLS[:3])
        q_g = l2norm(heads(qa, GDN_HEADS)) * HEAD_DIM ** -0.5
        k_g = l2norm(heads(ka, GDN_HEADS))
        v_g = heads(va, GDN_HEADS)
        gdn_dirs = []
        for d, (a_pre, b_pre) in enumerate(((af, bf), (ab, bb))):
            g = -jnp.exp(a_log[d]) * jax.nn.softplus(a_pre + dt_bias[d])
            gdn_dirs.append((q_g, k_g, v_g, g, jax.nn.sigmoid(b_pre)))
        q_r = heads(qr, RET_HEADS)
        k_r = heads(kr, RET_HEADS) * HEAD_DIM ** -0.5
        v_r = heads(vr, RET_HEADS)
        ret_dirs = [(q_r, k_r, v_r, jnp.broadcast_to(retention_log_decay(d), v_r.shape[:3]), None) for d in range(2)]
        return gdn_dirs, ret_dirs, za, zr

    gc, rc, za_c, zr_c = prep(h_ctx, False)
    gl, rl, za_l, zr_l = prep(h_lat, True)
    gdn = [two_pass(linear_scan, gc[d], gl[d], ctx_out, d == 1) for d in range(2)]
    ret = [two_pass(linear_scan, rc[d], rl[d], ctx_out, d == 1) for d in range(2)]

    def merge(o_g, o_r, za, zr):
        y_g = rms_norm_heads(o_g, gdn_gain) * jax.nn.silu(za)
        y_r = group_norm_heads(o_r, ret_gain) * jax.nn.silu(zr)
        return mm(jnp.concatenate([y_g, y_r], -1), w_out)

    y_lat = merge(gdn[0][1] + gdn[1][1], ret[0][1] + ret[1][1], za_l, zr_l)
    y_ctx = merge(gdn[0][0] + gdn[1][0], ret[0][0] + ret[1][0], za_c, zr_c) if ctx_out else None
    return y_ctx, y_lat


def mlstm_s5_mixer(h_ctx, h_lat, w_in, w_out, conv_w, gate_bias, mlstm_gain, lam_re, lam_im, log_dt,
                   b_re, b_im, c_re, c_im, d_skip, w_glu, b_glu, ctx_out):
    gb = gate_bias

    def prep(h, on_grid):
        qm, km, vm, om, i_f, i_b, f_f, f_b, u = split_cols(mm(h, w_in), ODD_COLS)
        qk = jax.nn.silu(short_conv(jnp.concatenate([qm, km], -1), conv_w, on_grid))
        qm, km = split_cols(qk, ODD_COLS[:2])
        q = heads(qm, MLSTM_HEADS)
        k = heads(km, MLSTM_HEADS) * HEAD_DIM ** -0.5
        v = heads(vm, MLSTM_HEADS)
        dirs = [(q, k, v, i_pre + gb[d, 0], jax.nn.log_sigmoid(f_pre + gb[d, 1]))
                for d, (i_pre, f_pre) in enumerate(((i_f, f_f), (i_b, f_b)))]
        return dirs, u, om

    mc, u_c, o_c = prep(h_ctx, False)
    ml, u_l, o_l = prep(h_lat, True)
    mls = [two_pass(mlstm_scan, mc[d], ml[d], ctx_out, d == 1) for d in range(2)]
    ys_c, ys_l = s5_bidirectional(u_c, u_l, _s5_weights(lam_re, lam_im, log_dt, b_re, b_im, c_re, c_im))

    def merge(h_m, y_s, u, og):
        y_m = group_norm_heads(h_m, mlstm_gain) * jax.nn.sigmoid(og)
        y = jax.nn.gelu(y_s + d_skip * u)
        y = y * jax.nn.sigmoid(mm(y, w_glu) + b_glu)
        return mm(jnp.concatenate([y_m, y], -1), w_out)

    y_lat = merge(mls[0][1] + mls[1][1], ys_l, u_l, o_l)
    y_ctx = merge(mls[0][0] + mls[1][0], ys_c, u_c, o_c) if ctx_out else None
    return y_ctx, y_lat


def kernel(x, c, ctx, c_ctx, w_mod, b_mod, ln1_g, ln1_b, ln2_g, ln2_b, w_router, w_gate, w_up, w_down,
           ev_w_in, ev_w_out, ev_conv, ev_a_log, ev_dt_bias, ev_gdn_norm, ev_ret_norm,
           od_w_in, od_w_out, od_conv, od_gate_bias, od_mlstm_norm, od_lam_re, od_lam_im, od_log_dt,
           od_b_re, od_b_im, od_c_re, od_c_im, od_d_skip, od_w_glu, od_b_glu):
    h_lat, h_ctx = x, ctx
    s_lat = jax.nn.silu(c)
    s_ctx = jax.nn.silu(c_ctx)
    for l in range(DEPTH):
        last = l == DEPTH - 1
        sh1, sc1, g1, sh2, sc2, g2 = jnp.split((s_lat @ w_mod[l] + b_mod[l])[:, None, :], 6, axis=-1)
        csh1, csc1, cg1, csh2, csc2, cg2 = jnp.split(s_ctx @ w_mod[l] + b_mod[l], 6, axis=-1)
        in_lat = h_lat * (1.0 + sc1) + sh1
        in_ctx = h_ctx * (1.0 + csc1) + csh1
        if l % 2 == 0:
            e = l // 2
            y_ctx, y_lat = gdn_retention_mixer(in_ctx, in_lat, ev_w_in[e], ev_w_out[e], ev_conv[e], ev_a_log[e],
                                               ev_dt_bias[e], ev_gdn_norm[e], ev_ret_norm[e], not last)
        else:
            o = l // 2
            y_ctx, y_lat = mlstm_s5_mixer(in_ctx, in_lat, od_w_in[o], od_w_out[o], od_conv[o], od_gate_bias[o],
                                          od_mlstm_norm[o], od_lam_re[o], od_lam_im[o], od_log_dt[o],
                                          od_b_re[o], od_b_im[o], od_c_re[o], od_c_im[o], od_d_skip[o],
                                          od_w_glu[o], od_b_glu[o], not last)
        h_lat = layer_norm(ALPHA * h_lat + g1 * y_lat, ln1_g[l], ln1_b[l])
        experts = (w_router[l], w_gate[l].astype(BF16), w_up[l].astype(BF16), w_down[l].astype(BF16))
        h_lat = moe_block(h_lat, sc2, sh2, g2, ln2_g[l], ln2_b[l], *experts)
        if not last:
            h_ctx = layer_norm(ALPHA * h_ctx + cg1 * y_ctx, ln1_g[l], ln1_b[l])
            bc = lambda v: jnp.broadcast_to(v, (BATCH, 1, D_MODEL))
            h_ctx = moe_block(h_ctx, bc(csc2), bc(csh2), bc(cg2), ln2_g[l], ln2_b[l], *experts)
    return h_lat
```

```python
import functools
import math

import jax
import jax.numpy as jnp
import numpy as np
from jax import lax
from jax.experimental import pallas as pl
from jax.experimental.pallas import tpu as pltpu

D_MODEL = 1024
BATCH = 4
SEQ = 4096
DEPTH = 4
GRID_W = 64
CTX_LEN = 256
CHUNK = 64
CONV_K = 3
HEAD_DIM = D_MODEL // 8
GDN_HEADS = 4
RET_HEADS = 4
MLSTM_HEADS = 4
GDN_W = GDN_HEADS * HEAD_DIM
RET_W = RET_HEADS * HEAD_DIM
MLSTM_W = MLSTM_HEADS * HEAD_DIM
S5_CH = D_MODEL // 2
S5_GROUP = 16
S5_GROUPS = S5_CH // S5_GROUP
S5_STATE = 64
N_EXPERTS = 16
EXPERT_FF = 2 * D_MODEL
CAPACITY_FACTOR = 2
ALPHA = (2 * DEPTH) ** 0.25
EPS = 1e-5
EVEN_COLS = (GDN_W, GDN_W, GDN_W, GDN_W, GDN_HEADS, GDN_HEADS, GDN_HEADS, GDN_HEADS,
             RET_W, RET_W, RET_W, RET_W)
ODD_COLS = (MLSTM_W, MLSTM_W, MLSTM_W, MLSTM_W, MLSTM_HEADS, MLSTM_HEADS, MLSTM_HEADS, MLSTM_HEADS, S5_CH)
F32 = jnp.float32
BF16 = jnp.bfloat16

LANE = 128


def _mm_kernel(a_ref, b_ref, o_ref):
    o_ref[...] = jnp.dot(a_ref[...].astype(BF16), b_ref[...].astype(BF16), preferred_element_type=F32)


def pmatmul(a, b, tm=512, tn=512):
    m, k = a.shape
    n = b.shape[1]
    n_pad = -n % LANE
    if n_pad:
        b = jnp.pad(b, ((0, 0), (0, n_pad)))
    np_ = n + n_pad
    tn = min(tn, np_)
    while np_ % tn:
        tn -= LANE
    tm = min(tm, m)
    assert m % tm == 0
    out = pl.pallas_call(
        _mm_kernel,
        grid=(m // tm, np_ // tn),
        in_specs=[pl.BlockSpec((tm, k), lambda i, j: (i, 0)),
                  pl.BlockSpec((k, tn), lambda i, j: (0, j))],
        out_specs=pl.BlockSpec((tm, tn), lambda i, j: (i, j)),
        out_shape=jax.ShapeDtypeStruct((m, np_), F32),
        name="pmatmul",
    )(a, b)
    return out[:, :n] if n_pad else out


def mm(x, w):
    lead = x.shape[:-1]
    return pmatmul(x.reshape(-1, x.shape[-1]), w).reshape(lead + (w.shape[-1],))


S5_L = 16
S5_NB = S5_CH // LANE
S5_GPB = LANE // S5_GROUP
S5_SW = S5_GPB * S5_STATE
VMEM_LIMIT = 56 * 1024 * 1024


def _s5_weights(lam_re, lam_im, log_dt, b_re, b_im, c_re, c_im):
    L, G = S5_L, S5_GROUPS
    hp = lax.Precision.HIGHEST
    taus = jnp.arange(L + 1, dtype=F32)[:, None, None]
    ks, ws, cas, ds = [], [], [], []
    for d in range(2):
        lr = jnp.minimum(lam_re[d], -1e-4)
        li = lam_im[d]
        dt = jnp.exp(log_dt[d])[:, None]
        mag = jnp.exp(lr * dt)
        ab_re, ab_im = mag * jnp.cos(li * dt), mag * jnp.sin(li * dt)
        xr, xi, den = ab_re - 1.0, ab_im, lr * lr + li * li
        f_re = (xr * lr + xi * li) / den
        f_im = (xi * lr - xr * li) / den
        bb_re = f_re[..., None] * b_re - f_im[..., None] * b_im
        bb_im = f_re[..., None] * b_im + f_im[..., None] * b_re
        pmag = jnp.exp(taus * (lr * dt))
        ar, ai = pmag * jnp.cos(taus * (li * dt)), pmag * jnp.sin(taus * (li * dt))
        wr = ar[..., None] * bb_re - ai[..., None] * bb_im
        wi = ar[..., None] * bb_im + ai[..., None] * bb_re
        k = (jnp.einsum('gop,tgpi->tgio', c_re, wr, precision=hp)
             - jnp.einsum('gop,tgpi->tgio', c_im, wi, precision=hp))
        car = c_re[None] * ar[:, :, None, :] - c_im[None] * ai[:, :, None, :]
        cai = c_re[None] * ai[:, :, None, :] + c_im[None] * ar[:, :, None, :]
        ks.append(k)
        ws.append((wr, wi))
        cas.append((car, cai))
        ds.append((ar[L], ai[L]))

    eye = jnp.eye(S5_GPB, dtype=F32)

    def blockdiag(t, in_axis_first):
        lead = t.shape[:-3]
        x, y = t.shape[-2:]
        t = t.reshape(lead + (S5_NB, S5_GPB, x, y))
        t = jnp.einsum('...jaxy,ab->...jaxby', t, eye)
        return t.reshape(lead + (S5_NB, S5_GPB * x, S5_GPB * y))

    kf, kb = ks
    kc = jnp.concatenate([kb[1:L][::-1], (kf[0] + kb[0])[None], kf[1:L]], axis=0)
    kc = blockdiag(kc, True)
    idx = (jnp.arange(L)[None, :] - jnp.arange(L)[:, None]) + (L - 1)
    tz = kc[idx]
    tz = tz.transpose(2, 0, 3, 1, 4).reshape(S5_NB, L * LANE, L * LANE)

    pbs, cab = [], []
    for d in range(2):
        wr, wi = ws[d]
        order = jnp.arange(L - 1, -1, -1) if d == 0 else jnp.arange(L)
        cols = []
        for w in (wr, wi):
            m = jnp.swapaxes(w[order], -1, -2)
            cols.append(blockdiag(m, True))
        pbs.append(jnp.concatenate(cols, axis=-1))
        car, cai = cas[d]
        order = jnp.arange(1, L + 1) if d == 0 else jnp.arange(L, 0, -1)
        rows = []
        for m in (car[order], -cai[order]):
            m = jnp.swapaxes(m, -1, -2)
            rows.append(blockdiag(m, True))
        cab.append(jnp.concatenate(rows, axis=-2))
    pb = jnp.concatenate(pbs, axis=-1)
    pb = pb.transpose(1, 0, 2, 3).reshape(S5_NB, L * LANE, 4 * S5_SW)
    ca = jnp.concatenate(cab, axis=-2)
    ca = ca.transpose(1, 2, 0, 3).reshape(S5_NB, 4 * S5_SW, L * LANE)
    dr = jnp.stack([ds[0][0], ds[1][0]], 0).reshape(2, S5_NB, 1, S5_SW).transpose(1, 0, 2, 3).reshape(2 * S5_NB, 1, S5_SW)
    di = jnp.stack([ds[0][1], ds[1][1]], 0).reshape(2, S5_NB, 1, S5_SW).transpose(1, 0, 2, 3).reshape(2 * S5_NB, 1, S5_SW)
    return tz.astype(BF16), pb.astype(BF16), ca.astype(BF16), dr, di


def _s5_p_kernel(u_ref, pb_ref, p_ref, *, nb, nc):
    res = jnp.dot(u_ref[...], pb_ref[...], preferred_element_type=F32)
    for b in range(nb):
        p_ref[:, b * 2 * S5_SW:(b + 1) * 2 * S5_SW] = res[b * nc:(b + 1) * nc]


def _s5_scan_kernel(p_ref, dr_ref, di_ref, s_ref, *, n_ctx, n_lat):
    rev = pl.program_id(0) % 2
    dr = dr_ref[...]
    di = di_ref[...]
    nbatch = p_ref.shape[1]

    def phase(base, n, carry):
        def body(step, carry):
            sr, si = carry
            row = base + jnp.where(rev == 0, step, n - 1 - step)
            s_ref[row, :, :S5_SW] = sr
            s_ref[row, :, S5_SW:] = si
            p = p_ref[row]
            nr = dr * sr - di * si + p[:, :S5_SW]
            ni = dr * si + di * sr + p[:, S5_SW:]
            return nr, ni
        return lax.fori_loop(0, n, body, carry)

    zero = jnp.zeros((nbatch, S5_SW), F32)
    carry = phase(0, n_ctx, (zero, zero))
    phase(n_ctx, n_lat, carry)


def _s5_y_kernel(u_ref, tz_ref, sf_ref, sb_ref, ca_ref, y_ref):
    y = jnp.dot(u_ref[...], tz_ref[...], preferred_element_type=F32)
    y += jnp.dot(sf_ref[...].astype(BF16), ca_ref[:2 * S5_SW, :], preferred_element_type=F32)
    y += jnp.dot(sb_ref[...].astype(BF16), ca_ref[2 * S5_SW:, :], preferred_element_type=F32)
    y_ref[...] = y


def s5_bidirectional(u_ctx, u_lat, weights):
    tz, pb, ca, dr, di = weights
    L = S5_L
    nb, t_ctx, _ = u_ctx.shape
    t_lat = u_lat.shape[1]
    assert t_ctx % L == 0 and t_lat % L == 0
    n_ctx, n_lat = t_ctx // L, t_lat // L
    nc = n_ctx + n_lat
    kw = L * LANE
    sw2 = 2 * S5_SW
    u = jnp.concatenate([u_ctx, u_lat], axis=1)
    ub = u.reshape(nb * nc, L, S5_NB, LANE).transpose(2, 0, 1, 3).reshape(S5_NB, nb * nc, kw).astype(BF16)

    p = pl.pallas_call(
        functools.partial(_s5_p_kernel, nb=nb, nc=nc),
        grid=(S5_NB, 2),
        in_specs=[pl.BlockSpec((None, nb * nc, kw), lambda j, d: (j, 0, 0)),
                  pl.BlockSpec((None, kw, sw2), lambda j, d: (j, 0, d))],
        out_specs=pl.BlockSpec((nc, nb * sw2), lambda j, d: (0, j * 2 + d)),
        out_shape=jax.ShapeDtypeStruct((nc, S5_NB * 2 * nb * sw2), F32),
        compiler_params=pltpu.CompilerParams(vmem_limit_bytes=VMEM_LIMIT),
        name="s5_chunk_inputs",
    )(ub, pb)

    p4 = p.reshape(nc, S5_NB * 2, nb, sw2)
    s4 = pl.pallas_call(
        functools.partial(_s5_scan_kernel, n_ctx=n_ctx, n_lat=n_lat),
        grid=(S5_NB * 2,),
        in_specs=[pl.BlockSpec((nc, None, nb, sw2), lambda g: (0, g, 0, 0)),
                  pl.BlockSpec((None, 1, S5_SW), lambda g: (g, 0, 0)),
                  pl.BlockSpec((None, 1, S5_SW), lambda g: (g, 0, 0))],
        out_specs=pl.BlockSpec((nc, None, nb, sw2), lambda g: (0, g, 0, 0)),
        out_shape=jax.ShapeDtypeStruct(p4.shape, F32),
        compiler_params=pltpu.CompilerParams(vmem_limit_bytes=VMEM_LIMIT),
        name="s5_state_scan",
    )(p4, dr, di)

    s2 = s4.reshape(nc, S5_NB * 2 * nb * sw2)
    yb = pl.pallas_call(
        _s5_y_kernel,
        grid=(S5_NB, nb),
        in_specs=[pl.BlockSpec((None, nc, kw), lambda j, b: (j, b, 0)),
                  pl.BlockSpec((None, kw, kw), lambda j, b: (j, 0, 0)),
                  pl.BlockSpec((nc, sw2), lambda j, b: (0, (j * 2) * nb + b)),
                  pl.BlockSpec((nc, sw2), lambda j, b: (0, (j * 2 + 1) * nb + b)),
                  pl.BlockSpec((None, 2 * sw2, kw), lambda j, b: (j, 0, 0))],
        out_specs=pl.BlockSpec((None, nc, kw), lambda j, b: (j, b, 0)),
        out_shape=jax.ShapeDtypeStruct((S5_NB, nb * nc, kw), F32),
        compiler_params=pltpu.CompilerParams(vmem_limit_bytes=VMEM_LIMIT),
        name="s5_output",
    )(ub, tz, s2, s2, ca)
    y = yb.reshape(S5_NB, nb, nc, L, LANE).transpose(1, 2, 3, 0, 4).reshape(nb, nc * L, S5_CH)
    return y[:, :t_ctx], y[:, t_ctx:]


ROUTE_TN = 512
MOE_TN = 1024
SEL_ROWS = 128
FF_SPLIT = 2
AFF_PARTS = 3


def _route_kernel(h_ref, sc_ref, sh_ref, wr_ref, hm_ref, afft_ref, asp_ref):
    hm = h_ref[...] * (1.0 + sc_ref[...]) + sh_ref[...]
    hm_ref[...] = hm.astype(BF16)
    logits = jnp.dot(hm, wr_ref[...], precision=lax.Precision.HIGHEST, preferred_element_type=F32)
    lane = lax.broadcasted_iota(jnp.int32, logits.shape, 1)
    logits = jnp.where(lane < N_EXPERTS, logits, -jnp.inf)
    ex = jnp.exp(logits - jnp.max(logits, axis=1, keepdims=True))
    aff = ex / jnp.sum(ex, axis=1, keepdims=True)
    afft_ref[...] = aff.T[:N_EXPERTS, :]
    hi = aff.astype(BF16).astype(F32)
    mid = (aff - hi).astype(BF16).astype(F32)
    lo = (aff - hi - mid).astype(BF16).astype(F32)
    asp = hi + pltpu.roll(mid, N_EXPERTS, 1) + pltpu.roll(lo, 2 * N_EXPERTS, 1)
    asp_ref[...] = asp.astype(BF16)


def _select_kernel(aff_ref, pos_ref, *, nblk_log2, cap):
    a = aff_ref[...]
    r = a.shape[0]
    bits = pltpu.bitcast(a, jnp.int32)
    ri = lax.broadcasted_iota(jnp.int32, (r, r), 0)
    rj = lax.broadcasted_iota(jnp.int32, (r, r), 1)
    same = (ri >> nblk_log2) == (rj >> nblk_log2)
    gm = jnp.where(same, 1.0, 0.0).astype(BF16)
    lm = jnp.where(same & (rj < ri), 1.0, 0.0).astype(BF16)
    li = lax.broadcasted_iota(jnp.int32, (LANE, LANE), 0)
    lj = lax.broadcasted_iota(jnp.int32, (LANE, LANE), 1)
    um = jnp.where(li <= lj, 1.0, 0.0).astype(BF16)

    def group_count(mask):
        rc = jnp.sum(jnp.where(mask, 1.0, 0.0), axis=1, keepdims=True)
        gc = jnp.dot(gm, jnp.broadcast_to(rc, (r, LANE)).astype(BF16), preferred_element_type=F32)
        return gc[:, :1]

    def bisect(i, thr):
        cand = thr | jnp.left_shift(1, 29 - i)
        return jnp.where(group_count(bits >= cand) >= cap, cand, thr)

    thr = lax.fori_loop(0, 30, bisect, jnp.zeros((r, 1), jnp.int32))

    def prefix(mask):
        x = jnp.where(mask, 1.0, 0.0)
        inc = jnp.dot(x.astype(BF16), um, preferred_element_type=F32)
        tot = jnp.broadcast_to(inc[:, LANE - 1:LANE], (r, LANE)).astype(BF16)
        return inc - x + jnp.dot(lm, tot, preferred_element_type=F32)

    gt = bits > thr
    eq = bits == thr
    need = cap - group_count(gt)
    sel = gt | (eq & (prefix(eq) < need))
    pos_ref[...] = jnp.where(sel, prefix(sel), -1.0).astype(jnp.int32)


def _onehot(pos_ref, cap):
    rows = lax.broadcasted_iota(jnp.int32, (cap, LANE), 0)
    blocks = [jnp.where(pos_ref[k:k + 1, :] == rows, 1.0, 0.0).astype(BF16) for k in range(pos_ref.shape[0])]
    return jnp.concatenate(blocks, axis=1)


def _ffn_kernel(pos_ref, hb_ref, asp_ref, wg_ref, wu_ref, wd_ref, ys_ref, xs_acc, g_acc, *, cap):
    e = pl.program_id(0)
    kt = pl.program_id(2)

    @pl.when(kt == 0)
    def _():
        xs_acc[...] = jnp.zeros_like(xs_acc)
        g_acc[...] = jnp.zeros_like(g_acc)

    oh = _onehot(pos_ref, cap)
    xs_acc[...] += jnp.dot(oh, hb_ref[...], preferred_element_type=F32)
    g_acc[...] += jnp.dot(oh, asp_ref[...], preferred_element_type=F32)

    @pl.when(kt == pl.num_programs(2) - 1)
    def _():
        xs = xs_acc[...].astype(BF16)
        g = g_acc[...]
        lane = lax.broadcasted_iota(jnp.int32, g.shape, 1)
        gate = jnp.sum(jnp.where((lane & (N_EXPERTS - 1)) == e, g, 0.0), axis=1, keepdims=True)
        fw = EXPERT_FF // FF_SPLIT
        y = jnp.zeros((cap, D_MODEL), F32)
        for f in range(FF_SPLIT):
            hg = jnp.dot(xs, wg_ref[:, f * fw:(f + 1) * fw], preferred_element_type=F32)
            hu = jnp.dot(xs, wu_ref[:, f * fw:(f + 1) * fw], preferred_element_type=F32)
            hid = (hg * jax.nn.sigmoid(hg)) * hu
            y += jnp.dot(hid.astype(BF16), wd_ref[f * fw:(f + 1) * fw, :], preferred_element_type=F32)
        ys_ref[...] = (y * gate).astype(BF16)


def _ln_rows(z, g, b):
    mu = jnp.mean(z, axis=-1, keepdims=True)
    zc = z - mu
    var = jnp.mean(zc * zc, axis=-1, keepdims=True)
    return zc * lax.rsqrt(var + EPS) * g + b


def _combine_kernel(pos_ref, ys_ref, h_ref, g2_ref, lng_ref, lnb_ref, o_ref, acc, *, cap):
    e = pl.program_id(2)

    @pl.when(e == 0)
    def _():
        acc[...] = jnp.zeros_like(acc)

    oh = _onehot(pos_ref, cap)
    acc[...] += lax.dot_general(oh, ys_ref[...], (((0,), (0,)), ((), ())), preferred_element_type=F32)

    @pl.when(e == pl.num_programs(2) - 1)
    def _():
        z = ALPHA * h_ref[...] + g2_ref[...] * acc[...]
        o_ref[...] = _ln_rows(z, lng_ref[...], lnb_ref[...])


def moe_block(h, sc, sh, g2, ln_g, ln_b, w_router, wg, wu, wd):
    nb, n, dm = h.shape
    cap = CAPACITY_FACTOR * n // N_EXPERTS
    nblk = n // LANE
    assert n % LANE == 0 and nblk & (nblk - 1) == 0
    tn_r = min(ROUTE_TN, n)
    tn = min(MOE_TN, n)
    wr = jnp.pad(w_router, ((0, 0), (0, LANE - N_EXPERTS)))
    row = lambda v: v.reshape(1, dm)

    hm, afft, asp = pl.pallas_call(
        _route_kernel,
        grid=(nb, n // tn_r),
        in_specs=[pl.BlockSpec((None, tn_r, dm), lambda b, t: (b, t, 0)),
                  pl.BlockSpec((None, 1, dm), lambda b, t: (b, 0, 0)),
                  pl.BlockSpec((None, 1, dm), lambda b, t: (b, 0, 0)),
                  pl.BlockSpec((dm, LANE), lambda b, t: (0, 0))],
        out_specs=[pl.BlockSpec((None, tn_r, dm), lambda b, t: (b, t, 0)),
                   pl.BlockSpec((None, N_EXPERTS, tn_r), lambda b, t: (b, 0, t)),
                   pl.BlockSpec((None, tn_r, LANE), lambda b, t: (b, t, 0))],
        out_shape=[jax.ShapeDtypeStruct((nb, n, dm), BF16),
                   jax.ShapeDtypeStruct((nb, N_EXPERTS, n), F32),
                   jax.ShapeDtypeStruct((nb, n, LANE), BF16)],
        name="moe_route",
    )(h, sc, sh, wr)

    rows_total = nb * N_EXPERTS * nblk
    rb = max(SEL_ROWS, N_EXPERTS * nblk)
    assert rows_total % rb == 0
    pos = pl.pallas_call(
        functools.partial(_select_kernel, nblk_log2=nblk.bit_length() - 1, cap=cap),
        grid=(rows_total // rb,),
        in_specs=[pl.BlockSpec((rb, LANE), lambda i: (i, 0))],
        out_specs=pl.BlockSpec((rb, LANE), lambda i: (i, 0)),
        out_shape=jax.ShapeDtypeStruct((rows_total, LANE), jnp.int32),
        name="moe_select",
    )(afft.reshape(rows_total, LANE))
    pos = pos.reshape(nb, N_EXPERTS, nblk, LANE)

    tb = tn // LANE
    ys = pl.pallas_call(
        functools.partial(_ffn_kernel, cap=cap),
        grid=(N_EXPERTS, nb, n // tn),
        in_specs=[pl.BlockSpec((None, None, tb, LANE), lambda e, b, k: (b, e, k, 0)),
                  pl.BlockSpec((None, tn, dm), lambda e, b, k: (b, k, 0)),
                  pl.BlockSpec((None, tn, LANE), lambda e, b, k: (b, k, 0)),
                  pl.BlockSpec((None, dm, EXPERT_FF), lambda e, b, k: (e, 0, 0)),
                  pl.BlockSpec((None, dm, EXPERT_FF), lambda e, b, k: (e, 0, 0)),
                  pl.BlockSpec((None, EXPERT_FF, dm), lambda e, b, k: (e, 0, 0))],
        out_specs=pl.BlockSpec((None, None, cap, dm), lambda e, b, k: (b, e, 0, 0)),
        out_shape=jax.ShapeDtypeStruct((nb, N_EXPERTS, cap, dm), BF16),
        scratch_shapes=[pltpu.VMEM((cap, dm), F32), pltpu.VMEM((cap, LANE), F32)],
        compiler_params=pltpu.CompilerParams(vmem_limit_bytes=VMEM_LIMIT),
        name="moe_ffn",
    )(pos, hm, asp, wg, wu, wd)

    return pl.pallas_call(
        functools.partial(_combine_kernel, cap=cap),
        grid=(nb, n // tn, N_EXPERTS),
        in_specs=[pl.BlockSpec((None, None, tb, LANE), lambda b, t, e: (b, e, t, 0)),
                  pl.BlockSpec((None, None, cap, dm), lambda b, t, e: (b, e, 0, 0)),
                  pl.BlockSpec((None, tn, dm), lambda b, t, e: (b, t, 0)),
                  pl.BlockSpec((None, 1, dm), lambda b, t, e: (b, 0, 0)),
                  pl.BlockSpec((1, dm), lambda b, t, e: (0, 0)),
                  pl.BlockSpec((1, dm), lambda b, t, e: (0, 0))],
        out_specs=pl.BlockSpec((None, tn, dm), lambda b, t, e: (b, t, 0)),
        out_shape=jax.ShapeDtypeStruct((nb, n, dm), F32),
        scratch_shapes=[pltpu.VMEM((tn, dm), F32)],
        compiler_params=pltpu.CompilerParams(vmem_limit_bytes=VMEM_LIMIT),
        name="moe_combine",
    )(pos, ys, h, g2, row(ln_g), row(ln_b))


N_HEADS = 4
STACK = N_HEADS * CHUNK
CHUNK_LOG2 = CHUNK.bit_length() - 1
SOLVE_LEVELS = CHUNK_LOG2


def _stack_heads(x):
    return jnp.concatenate([x[:, h * HEAD_DIM:(h + 1) * HEAD_DIM] for h in range(N_HEADS)], axis=0)


def _stack_cols(cols, width):
    return jnp.concatenate([jnp.broadcast_to(c, (CHUNK, width)) for c in cols], axis=0)


def _mxu(a, b, dims, hp):
    dg = lambda x, y: lax.dot_general(x, y, (dims, ((), ())), preferred_element_type=F32)
    a_hi, b_hi = a.astype(BF16), b.astype(BF16)
    if not hp:
        return dg(a_hi, b_hi)
    a_lo = (a - a_hi.astype(F32)).astype(BF16)
    b_lo = (b - b_hi.astype(F32)).astype(BF16)
    return dg(a_hi, b_hi) + (dg(a_lo, b_hi) + dg(a_hi, b_lo))


def _dot_nt(a, b, hp=False):
    return _mxu(a, b, ((1,), (1,)), hp)


def _dot_tn(a, b, hp=False):
    return _mxu(a, b, ((0,), (0,)), hp)


def _dot(a, b, hp=False):
    return _mxu(a, b, ((1,), (0,)), hp)


def _chunk_masks(rev):
    ri = lax.broadcasted_iota(jnp.int32, (STACK, STACK), 0)
    ci = lax.broadcasted_iota(jnp.int32, (STACK, STACK), 1)
    same = (ri >> CHUNK_LOG2) == (ci >> CHUNK_LOG2)
    ahead = jnp.where(rev, ci - ri, ri - ci)
    return same & (ahead >= 0), same & (ahead > 0)


def _scan_cumsum(gt, rev):
    ii = lax.broadcasted_iota(jnp.int32, (CHUNK, CHUNK), 0)
    jj = lax.broadcasted_iota(jnp.int32, (CHUNK, CHUNK), 1)
    tri = jnp.where(jnp.where(rev, jj - ii, ii - jj) >= 0, 1.0, 0.0)
    return jnp.dot(tri, gt, precision=lax.Precision.HIGHEST, preferred_element_type=F32)


def _unit_triangular_inverse(a):
    ri = lax.broadcasted_iota(jnp.int32, a.shape, 0)
    ci = lax.broadcasted_iota(jnp.int32, a.shape, 1)
    joins = lambda lvl: ((ri >> (lvl + 1)) == (ci >> (lvl + 1))) & ((ri >> lvl) != (ci >> lvl))
    t = jnp.where(ri == ci, 1.0, 0.0) - jnp.where(joins(0), a, 0.0)
    for lvl in range(1, CHUNK_LOG2):
        t = t - _dot(t, _dot(jnp.where(joins(lvl), a, 0.0), t))
    return t


def _linear_scan_kernel(q_ref, k_ref, v_ref, gt_ref, s0_ref, o_ref, sfin_ref, s_scr, *, has_beta):
    rev = pl.program_id(0) == 1
    n = pl.program_id(2)

    @pl.when(n == 0)
    def _():
        s_scr[...] = s0_ref[...]

    for i in range(q_ref.shape[0]):
        gt = gt_ref[i] if len(gt_ref.shape) == 3 else gt_ref[...]
        _linear_chunk(q_ref.at[i], k_ref.at[i], v_ref.at[i], gt, o_ref.at[i], s_scr.at[i], rev, has_beta)

    @pl.when(n == pl.num_programs(2) - 1)
    def _():
        sfin_ref[...] = s_scr[...]


def _linear_chunk(q_ref, k_ref, v_ref, gt, o_ref, s_scr, rev, has_beta):
    gt = jnp.broadcast_to(gt, (CHUNK, LANE))
    cum = _scan_cumsum(gt, rev)
    gcols, bcols, gtots = [], [], []
    for h in range(N_HEADS):
        gc = jnp.where(rev, cum[:, N_HEADS + h:N_HEADS + h + 1], cum[:, h:h + 1])
        gcols.append(gc)
        gtots.append(jnp.where(rev, gc[0:1], gc[CHUNK - 1:CHUNK]))
        bcols.append(jnp.where(rev, gt[:, 3 * N_HEADS + h:3 * N_HEADS + h + 1], gt[:, 2 * N_HEADS + h:2 * N_HEADS + h + 1]))
    cb = _stack_cols(gcols, STACK)
    diff = cb - cb.T
    incl, strict = _chunk_masks(rev)
    dec = jnp.exp(jnp.where(incl, diff, 0.0))
    gcb = cb[:, :HEAD_DIM]
    q_st, k_st, v_st = _stack_heads(q_ref[...]), _stack_heads(k_ref[...]), _stack_heads(v_ref[...])
    if has_beta:
        beta = _stack_cols(bcols, HEAD_DIM)
        kb = k_st * beta
        a = _dot_nt(kb, k_st) * jnp.where(strict, dec, 0.0)
        x = _dot(_unit_triangular_inverse(a), jnp.concatenate([v_st * beta, kb * jnp.exp(gcb)], axis=1))
        u_st, w_st = x[:, :HEAD_DIM], x[:, HEAD_DIM:]
    else:
        u_st, w_st = v_st, None
    gtot = _stack_cols(gtots, HEAD_DIM)
    k_end = k_st * jnp.exp(gtot - gcb)
    q_dec = q_st * jnp.exp(gcb)
    a_qk = _dot_nt(q_st, k_st) * jnp.where(incl, dec, 0.0)
    hs = lambda t, h: t[h * CHUNK:(h + 1) * CHUNK]
    states = [s_scr[h] for h in range(N_HEADS)]
    if has_beta:
        vn = jnp.concatenate([hs(u_st, h) - _dot(hs(w_st, h), states[h]) for h in range(N_HEADS)], axis=0)
    else:
        vn = u_st
    o_st = _dot(a_qk, vn)
    for h in range(N_HEADS):
        o_ref[:, h * HEAD_DIM:(h + 1) * HEAD_DIM] = hs(o_st, h) + _dot(hs(q_dec, h), states[h])
        s_scr[h] = jnp.exp(gtots[h]) * states[h] + _dot_tn(hs(k_end, h), hs(vn, h))


SCAN_NS = 2


def linear_scan_bidir(q, k, v, gt, s0, has_beta):
    nb, t, w = q.shape
    nchunk = t // CHUNK
    assert t % CHUNK == 0 and w == N_HEADS * HEAD_DIM
    ns = SCAN_NS
    assert nb % ns == 0
    cidx = lambda d, n: n + d * (nchunk - 1 - 2 * n)
    tok = pl.BlockSpec((ns, CHUNK, w), lambda d, b, n: (b, cidx(d, n), 0))
    if gt.ndim == 3:
        gspec = pl.BlockSpec((ns, CHUNK, LANE), lambda d, b, n: (b, cidx(d, n), 0))
    else:
        gspec = pl.BlockSpec((1, LANE), lambda d, b, n: (0, 0))
    sspec = pl.BlockSpec((None, ns, N_HEADS, HEAD_DIM, HEAD_DIM), lambda d, b, n: (d, b, 0, 0, 0))
    return pl.pallas_call(
        functools.partial(_linear_scan_kernel, has_beta=has_beta),
        grid=(2, nb // ns, nchunk),
        in_specs=[tok, tok, tok, gspec, sspec],
        out_specs=[pl.BlockSpec((None, ns, CHUNK, w), lambda d, b, n: (d, b, cidx(d, n), 0)), sspec],
        out_shape=[jax.ShapeDtypeStruct((2, nb, t, w), F32),
                   jax.ShapeDtypeStruct((2, nb, N_HEADS, HEAD_DIM, HEAD_DIM), F32)],
        scratch_shapes=[pltpu.VMEM((ns, N_HEADS, HEAD_DIM, HEAD_DIM), F32)],
        name="gdn_scan" if has_beta else "retention_scan",
    )(q, k, v, gt, s0)


def linear_scan_two_pass(ctx_args, lat_args, has_beta):
    nb = ctx_args[0].shape[0]
    zero = jnp.zeros((2, nb, N_HEADS, HEAD_DIM, HEAD_DIM), F32)
    o_ctx, s_ctx = linear_scan_bidir(*ctx_args, zero, has_beta)
    o_lat, _ = linear_scan_bidir(*lat_args, s_ctx, has_beta)
    return o_ctx[0] + o_ctx[1], o_lat[0] + o_lat[1]


MLSTM_HP = True


def _mlstm_scan_kernel(q_ref, k_ref, v_ref, gt_ref, c0_ref, n0_ref, m0_ref, o_ref, cfin_ref, nfin_ref, mfin_ref,
                       c_scr, n_scr, m_scr):
    rev = pl.program_id(0) == 1
    step = pl.program_id(2)

    @pl.when(step == 0)
    def _():
        c_scr[...] = c0_ref[...]
        n_scr[...] = n0_ref[...]
        m_scr[...] = m0_ref[...]

    gt = gt_ref[...]
    cum = _scan_cumsum(gt, rev)
    pick = lambda t, c: jnp.where(rev, t[:, N_HEADS + c:N_HEADS + c + 1], t[:, c:c + 1])
    q_st, k_st, v_st = _stack_heads(q_ref[...]), _stack_heads(k_ref[...]), _stack_heads(v_ref[...])
    hs = lambda t, h: t[h * CHUNK:(h + 1) * CHUNK]
    bcums, srcs, inters, qcs, qns = [], [], [], [], []
    for h in range(N_HEADS):
        ic = pick(gt, h)
        bcum = pick(cum, 2 * N_HEADS + h)
        b_end = jnp.where(rev, bcum[0:1], bcum[CHUNK - 1:CHUNK])
        c_prev, n_prev, m_prev = c_scr[h], n_scr[h], m_scr[h][:, :1]
        a = b_end - bcum + ic
        m_new = jnp.maximum(b_end + m_prev, jnp.max(a, axis=0, keepdims=True))
        w_state = jnp.exp(a - m_new)
        decay = jnp.exp(b_end + m_prev - m_new)
        kw = hs(k_st, h) * w_state
        c_scr[h] = decay * c_prev + _dot_tn(kw, hs(v_st, h), MLSTM_HP)
        n_scr[h] = decay * n_prev + jnp.sum(kw, axis=0, keepdims=True)
        m_scr[h] = jnp.broadcast_to(m_new, (1, HEAD_DIM))
        bcums.append(bcum)
        srcs.append(bcum - ic)
        inters.append(bcum + m_prev)
        qcs.append(_dot(hs(q_st, h), c_prev, MLSTM_HP))
        qns.append(jnp.sum(hs(q_st, h) * n_prev, axis=1, keepdims=True))
    incl, _ = _chunk_masks(rev)
    dlog = _stack_cols(bcums, STACK) - _stack_cols(srcs, STACK).T
    inter = jnp.concatenate(inters, axis=0)
    m_t = jnp.maximum(inter, jnp.max(jnp.where(incl, dlog, -1e30), axis=1, keepdims=True))
    s = _dot_nt(q_st, k_st, MLSTM_HP) * jnp.where(incl, jnp.exp(jnp.where(incl, dlog, 0.0) - m_t), 0.0)
    w_inter = jnp.exp(inter - m_t)
    num = _dot(s, v_st, MLSTM_HP) + w_inter * jnp.concatenate(qcs, axis=0)
    den = jnp.sum(s, axis=1, keepdims=True) + w_inter * jnp.concatenate(qns, axis=0)
    out = num / jnp.maximum(jnp.abs(den), jnp.exp(-m_t))
    for h in range(N_HEADS):
        o_ref[:, h * HEAD_DIM:(h + 1) * HEAD_DIM] = hs(out, h)

    @pl.when(step == pl.num_programs(2) - 1)
    def _():
        cfin_ref[...] = c_scr[...]
        nfin_ref[...] = n_scr[...]
        mfin_ref[...] = m_scr[...]


def mlstm_scan_bidir(q, k, v, gt, state):
    nb, t, w = q.shape
    nchunk = t // CHUNK
    assert t % CHUNK == 0 and w == N_HEADS * HEAD_DIM
    cidx = lambda d, n: n + d * (nchunk - 1 - 2 * n)
    tok = pl.BlockSpec((None, CHUNK, w), lambda d, b, n: (b, cidx(d, n), 0))
    gspec = pl.BlockSpec((None, CHUNK, LANE), lambda d, b, n: (b, cidx(d, n), 0))
    cspec = pl.BlockSpec((None, None, N_HEADS, HEAD_DIM, HEAD_DIM), lambda d, b, n: (d, b, 0, 0, 0))
    vspec = pl.BlockSpec((None, None, N_HEADS, 1, HEAD_DIM), lambda d, b, n: (d, b, 0, 0, 0))
    cshape = jax.ShapeDtypeStruct((2, nb, N_HEADS, HEAD_DIM, HEAD_DIM), F32)
    vshape = jax.ShapeDtypeStruct((2, nb, N_HEADS, 1, HEAD_DIM), F32)
    o, c, n, m = pl.pallas_call(
        _mlstm_scan_kernel,
        grid=(2, nb, nchunk),
        in_specs=[tok, tok, tok, gspec, cspec, vspec, vspec],
        out_specs=[pl.BlockSpec((None, None, CHUNK, w), lambda d, b, n: (d, b, cidx(d, n), 0)), cspec, vspec, vspec],
        out_shape=[jax.ShapeDtypeStruct((2, nb, t, w), F32), cshape, vshape, vshape],
        scratch_shapes=[pltpu.VMEM((N_HEADS, HEAD_DIM, HEAD_DIM), F32), pltpu.VMEM((N_HEADS, 1, HEAD_DIM), F32),
                        pltpu.VMEM((N_HEADS, 1, HEAD_DIM), F32)],
        name="mlstm_scan",
    )(q, k, v, gt, *state)
    return o, (c, n, m)


def mlstm_two_pass(ctx_args, lat_args):
    nb = ctx_args[0].shape[0]
    zero = (jnp.zeros((2, nb, N_HEADS, HEAD_DIM, HEAD_DIM), F32), jnp.zeros((2, nb, N_HEADS, 1, HEAD_DIM), F32),
            jnp.zeros((2, nb, N_HEADS, 1, HEAD_DIM), F32))
    o_ctx, s_ctx = mlstm_scan_bidir(*ctx_args, zero)
    o_lat, _ = mlstm_scan_bidir(*lat_args, s_ctx)
    return o_ctx[0] + o_ctx[1], o_lat[0] + o_lat[1]


def split_cols(t, sizes):
    return jnp.split(t, np.cumsum(sizes)[:-1].tolist(), axis=-1)


def heads(t, n):
    return t.reshape(t.shape[:-1] + (n, t.shape[-1] // n))


def flip_t(t):
    return None if t is None else jnp.flip(t, axis=1)


def l2norm(t):
    return t * lax.rsqrt(jnp.sum(t * t, axis=-1, keepdims=True) + 1e-6)


def layer_norm(t, g, b):
    mu = t.mean(-1, keepdims=True)
    var = jnp.square(t - mu).mean(-1, keepdims=True)
    return ((t - mu) * lax.rsqrt(var + EPS)) * g + b


def rms_norm_heads(o, g):
    y = o * lax.rsqrt(jnp.mean(o * o, axis=-1, keepdims=True) + EPS) * g
    return y.reshape(o.shape[:2] + (-1,))


def group_norm_heads(o, g):
    mu = o.mean(-1, keepdims=True)
    var = jnp.square(o - mu).mean(-1, keepdims=True)
    return ((o - mu) * lax.rsqrt(var + EPS)).reshape(o.shape[:2] + (-1,)) * g


def short_conv(t, w, on_grid):
    ch = t.shape[-1]
    if on_grid:
        b, n = t.shape[:2]
        rows = n // GRID_W
        tg = t.reshape(b, rows, GRID_W, ch)
        y = lax.conv_general_dilated(tg, w[:, :, None, :], (1, 1), 'SAME',
                                     dimension_numbers=('NHWC', 'HWIO', 'NHWC'), feature_group_count=ch)
        return y.reshape(b, n, ch)
    return lax.conv_general_dilated(t, w[CONV_K // 2][:, None, :], (1,), 'SAME',
                                    dimension_numbers=('NWC', 'WIO', 'NWC'), feature_group_count=ch)


def to_chunks(t):
    b, n, h = t.shape[:3]
    t = t.reshape((b, n // CHUNK, CHUNK, h) + t.shape[3:])
    return jnp.moveaxis(jnp.moveaxis(t, 3, 1), 2, 0)


def from_chunks(t):
    t = jnp.moveaxis(jnp.moveaxis(t, 0, 2), 1, 3)
    return t.reshape((t.shape[0], -1) + t.shape[3:])


def linear_scan(q, k, v, g, beta, s0, want_out):
    b, _, h, dk = k.shape
    dv = v.shape[-1]
    kc, vc = to_chunks(k), to_chunks(v)
    gcum = jnp.cumsum(to_chunks(g), axis=-1)
    diff = gcum[..., :, None] - gcum[..., None, :]
    incl = jnp.tril(jnp.ones((CHUNK, CHUNK), bool))
    if beta is None:
        u, w = vc, None
    else:
        bc = to_chunks(beta)[..., None]
        kb = kc * bc
        strict = jnp.tril(jnp.ones((CHUNK, CHUNK), bool), -1)
        a = jnp.where(strict, jnp.einsum('...ik,...jk->...ij', kb, kc) * jnp.exp(jnp.where(strict, diff, 0.0)), 0.0)
        rhs = jnp.concatenate([vc * bc, kb * jnp.exp(gcum)[..., None]], axis=-1)
        sol = lax.linalg.triangular_solve(a + jnp.eye(CHUNK, dtype=a.dtype), rhs, left_side=True,
                                          lower=True, unit_diagonal=True)
        u, w = sol[..., :dv], sol[..., dv:]
    k_end = kc * jnp.exp(gcum[..., -1:] - gcum)[..., None]
    g_end = jnp.exp(gcum[..., -1])[..., None, None]
    if want_out:
        qc = to_chunks(q)
        q_dec = qc * jnp.exp(gcum)[..., None]
        a_qk = jnp.where(incl, jnp.einsum('...ik,...jk->...ij', qc, kc) * jnp.exp(jnp.where(incl, diff, 0.0)), 0.0)
    else:
        q_dec, a_qk = None, None
    if s0 is None:
        s0 = jnp.zeros((b, h, dk, dv), F32)

    def step(s, inp):
        qd, ke, uc, wc, aqk, ge = inp
        vn = uc if wc is None else uc - jnp.einsum('bhck,bhkv->bhcv', wc, s)
        s_new = ge * s + jnp.einsum('bhck,bhcv->bhkv', ke, vn)
        if qd is None:
            return s_new, None
        return s_new, jnp.einsum('bhck,bhkv->bhcv', qd, s) + jnp.einsum('bhij,bhjv->bhiv', aqk, vn)

    s_fin, o = lax.scan(step, s0, (q_dec, k_end, u, w, a_qk, g_end))
    return (from_chunks(o) if want_out else None), s_fin


def mlstm_scan(q, k, v, log_i, log_f, s0, want_out):
    b, _, h, dk = k.shape
    dv = v.shape[-1]
    if s0 is None:
        s0 = (jnp.zeros((b, h, dk, dv), F32), jnp.zeros((b, h, dk), F32), jnp.zeros((b, h), F32))
    incl = jnp.tril(jnp.ones((CHUNK, CHUNK), bool))

    def step(carry, inp):
        c_prev, n_prev, m_prev = carry
        qc, kc, vc, ic, fc = inp
        bcum = jnp.cumsum(fc, axis=-1)
        b_end = bcum[..., -1]
        a = b_end[..., None] - bcum + ic
        m_new = jnp.maximum(b_end + m_prev, a.max(-1))
        w_state = jnp.exp(a - m_new[..., None])
        decay = jnp.exp(b_end + m_prev - m_new)
        c_new = decay[..., None, None] * c_prev + jnp.einsum('bhck,bhcv->bhkv', kc * w_state[..., None], vc)
        n_new = decay[..., None] * n_prev + jnp.einsum('bhck,bhc->bhk', kc, w_state)
        carry_new = (c_new, n_new, m_new)
        if qc is None:
            return carry_new, None
        dlog = jnp.where(incl, bcum[..., :, None] - bcum[..., None, :] + ic[..., None, :], -jnp.inf)
        inter = bcum + m_prev[..., None]
        m_t = jnp.maximum(inter, dlog.max(-1))
        s = jnp.einsum('bhik,bhjk->bhij', qc, kc) * jnp.exp(dlog - m_t[..., None])
        w_inter = jnp.exp(inter - m_t)[..., None]
        num = jnp.einsum('bhij,bhjv->bhiv', s, vc) + w_inter * jnp.einsum('bhik,bhkv->bhiv', qc, c_prev)
        den = s.sum(-1, keepdims=True) + w_inter * jnp.einsum('bhik,bhk->bhi', qc, n_prev)[..., None]
        return carry_new, num / jnp.maximum(jnp.abs(den), jnp.exp(-m_t)[..., None])

    xs = (to_chunks(q) if want_out else None, to_chunks(k), to_chunks(v), to_chunks(log_i), to_chunks(log_f))
    s_fin, hs = lax.scan(step, s0, xs)
    return (from_chunks(hs) if want_out else None), s_fin


def two_pass(scan_fn, ctx_args, lat_args, ctx_out, reverse):
    if reverse:
        ctx_args = [flip_t(t) for t in ctx_args]
        lat_args = [flip_t(t) for t in lat_args]
    o_ctx, s_ctx = scan_fn(*ctx_args, None, ctx_out)
    o_lat, _ = scan_fn(*lat_args, s_ctx, True)
    if reverse:
        o_ctx, o_lat = flip_t(o_ctx), flip_t(o_lat)
    return o_ctx, o_lat


def retention_log_decay(direction):
    expo = 5.0 + 2.0 * jnp.arange(RET_HEADS, dtype=F32) + direction
    return jnp.log1p(-jnp.exp2(-expo))


def gdn_retention_mixer(h_ctx, h_lat, w_in, w_out, conv_w, a_log, dt_bias, gdn_gain, ret_gain, ctx_out):
    def prep(h, on_grid):
        qa, ka, va, za, af, ab, bf, bb, qr, kr, vr, zr = split_cols(mm(h, w_in), EVEN_COLS)
        qkv = jax.nn.silu(short_conv(jnp.concatenate([qa, ka, va], -1), conv_w, on_grid))
        qa, ka, va = split_cols(qkv, EVEN_COLS[:3])
        unheads = lambda t: t.reshape(t.shape[:2] + (-1,))
        q_g = unheads(l2norm(heads(qa, GDN_HEADS)) * HEAD_DIM ** -0.5)
        k_g = unheads(l2norm(heads(ka, GDN_HEADS)))
        g = [-jnp.exp(a_log[d]) * jax.nn.softplus(a_pre + dt_bias[d]) for d, a_pre in enumerate((af, ab))]
        gt = jnp.concatenate(g + [jax.nn.sigmoid(bf), jax.nn.sigmoid(bb)], -1)
        gt = jnp.pad(gt, ((0, 0), (0, 0), (0, LANE - gt.shape[-1])))
        return (q_g, k_g, va, gt), (qr, kr * HEAD_DIM ** -0.5, vr), za, zr

    gc, rc, za_c, zr_c = prep(h_ctx, False)
    gl, rl, za_l, zr_l = prep(h_lat, True)
    ret_decay = jnp.concatenate([retention_log_decay(0), retention_log_decay(1)])
    ret_gt = jnp.pad(ret_decay, (0, LANE - ret_decay.shape[0])).reshape(1, LANE)
    og_c, og_l = linear_scan_two_pass(gc, gl, True)
    or_c, or_l = linear_scan_two_pass(rc + (ret_gt,), rl + (ret_gt,), False)

    def merge(o_g, o_r, za, zr):
        y_g = rms_norm_heads(heads(o_g, GDN_HEADS), gdn_gain) * jax.nn.silu(za)
        y_r = group_norm_heads(heads(o_r, RET_HEADS), ret_gain) * jax.nn.silu(zr)
        return mm(jnp.concatenate([y_g, y_r], -1), w_out)

    y_lat = merge(og_l, or_l, za_l, zr_l)
    y_ctx = merge(og_c, or_c, za_c, zr_c) if ctx_out else None
    return y_ctx, y_lat


def mlstm_s5_mixer(h_ctx, h_lat, w_in, w_out, conv_w, gate_bias, mlstm_gain, lam_re, lam_im, log_dt,
                   b_re, b_im, c_re, c_im, d_skip, w_glu, b_glu, ctx_out):
    gb = gate_bias

    def prep(h, on_grid):
        qm, km, vm, om, i_f, i_b, f_f, f_b, u = split_cols(mm(h, w_in), ODD_COLS)
        qk = jax.nn.silu(short_conv(jnp.concatenate([qm, km], -1), conv_w, on_grid))
        qm, km = split_cols(qk, ODD_COLS[:2])
        gt = jnp.concatenate([i_f + gb[0, 0], i_b + gb[1, 0],
                              jax.nn.log_sigmoid(f_f + gb[0, 1]), jax.nn.log_sigmoid(f_b + gb[1, 1])], -1)
        gt = jnp.pad(gt, ((0, 0), (0, 0), (0, LANE - gt.shape[-1])))
        return (qm, km * HEAD_DIM ** -0.5, vm, gt), u, om

    mc, u_c, o_c = prep(h_ctx, False)
    ml, u_l, o_l = prep(h_lat, True)
    hm_c, hm_l = mlstm_two_pass(mc, ml)
    ys_c, ys_l = s5_bidirectional(u_c, u_l, _s5_weights(lam_re, lam_im, log_dt, b_re, b_im, c_re, c_im))

    def merge(h_m, y_s, u, og):
        y_m = group_norm_heads(heads(h_m, MLSTM_HEADS), mlstm_gain) * jax.nn.sigmoid(og)
        y = jax.nn.gelu(y_s + d_skip * u)
        y = y * jax.nn.sigmoid(mm(y, w_glu) + b_glu)
        return mm(jnp.concatenate([y_m, y], -1), w_out)

    y_lat = merge(hm_l, ys_l, u_l, o_l)
    y_ctx = merge(hm_c, ys_c, u_c, o_c) if ctx_out else None
    return y_ctx, y_lat


def kernel(x, c, ctx, c_ctx, w_mod, b_mod, ln1_g, ln1_b, ln2_g, ln2_b, w_router, w_gate, w_up, w_down,
           ev_w_in, ev_w_out, ev_conv, ev_a_log, ev_dt_bias, ev_gdn_norm, ev_ret_norm,
           od_w_in, od_w_out, od_conv, od_gate_bias, od_mlstm_norm, od_lam_re, od_lam_im, od_log_dt,
           od_b_re, od_b_im, od_c_re, od_c_im, od_d_skip, od_w_glu, od_b_glu):
    h_lat, h_ctx = x, ctx
    s_lat = jax.nn.silu(c)
    s_ctx = jax.nn.silu(c_ctx)
    for l in range(DEPTH):
        last = l == DEPTH - 1
        sh1, sc1, g1, sh2, sc2, g2 = jnp.split((s_lat @ w_mod[l] + b_mod[l])[:, None, :], 6, axis=-1)
        csh1, csc1, cg1, csh2, csc2, cg2 = jnp.split(s_ctx @ w_mod[l] + b_mod[l], 6, axis=-1)
        in_lat = h_lat * (1.0 + sc1) + sh1
        in_ctx = h_ctx * (1.0 + csc1) + csh1
        if l % 2 == 0:
            e = l // 2
            y_ctx, y_lat = gdn_retention_mixer(in_ctx, in_lat, ev_w_in[e], ev_w_out[e], ev_conv[e], ev_a_log[e],
                                               ev_dt_bias[e], ev_gdn_norm[e], ev_ret_norm[e], not last)
        else:
            o = l // 2
            y_ctx, y_lat = mlstm_s5_mixer(in_ctx, in_lat, od_w_in[o], od_w_out[o], od_conv[o], od_gate_bias[o],
                                          od_mlstm_norm[o], od_lam_re[o], od_lam_im[o], od_log_dt[o],
                                          od_b_re[o], od_b_im[o], od_c_re[o], od_c_im[o], od_d_skip[o],
                                          od_w_glu[o], od_b_glu[o], not last)
        h_lat = layer_norm(ALPHA * h_lat + g1 * y_lat, ln1_g[l], ln1_b[l])
        experts = (w_router[l], w_gate[l].astype(BF16), w_up[l].astype(BF16), w_down[l].astype(BF16))
        h_lat = moe_block(h_lat, sc2, sh2, g2, ln2_g[l], ln2_b[l], *experts)
        if not last:
            h_ctx = layer_norm(ALPHA * h_ctx + cg1 * y_ctx, ln1_g[l], ln1_b[l])
            bc = lambda v: jnp.broadcast_to(v, (BATCH, 1, D_MODEL))
            h_ctx = moe_block(h_ctx, bc(csc2), bc(csh2), bc(cg2), ln2_g[l], ln2_b[l], *experts)
    return h_lat
```

```python
import functools
import math

import jax
import jax.numpy as jnp
import numpy as np
from jax import lax
from jax.experimental import pallas as pl
from jax.experimental.pallas import tpu as pltpu

D_MODEL = 1024
BATCH = 4
SEQ = 4096
DEPTH = 4
GRID_W = 64
CTX_LEN = 256
CHUNK = 64
CONV_K = 3
HEAD_DIM = D_MODEL // 8
GDN_HEADS = 4
RET_HEADS = 4
MLSTM_HEADS = 4
GDN_W = GDN_HEADS * HEAD_DIM
RET_W = RET_HEADS * HEAD_DIM
MLSTM_W = MLSTM_HEADS * HEAD_DIM
S5_CH = D_MODEL // 2
S5_GROUP = 16
S5_GROUPS = S5_CH // S5_GROUP
S5_STATE = 64
N_EXPERTS = 16
EXPERT_FF = 2 * D_MODEL
CAPACITY_FACTOR = 2
ALPHA = (2 * DEPTH) ** 0.25
EPS = 1e-5
EVEN_COLS = (GDN_W, GDN_W, GDN_W, GDN_W, GDN_HEADS, GDN_HEADS, GDN_HEADS, GDN_HEADS,
             RET_W, RET_W, RET_W, RET_W)
ODD_COLS = (MLSTM_W, MLSTM_W, MLSTM_W, MLSTM_W, MLSTM_HEADS, MLSTM_HEADS, MLSTM_HEADS, MLSTM_HEADS, S5_CH)
F32 = jnp.float32
BF16 = jnp.bfloat16

LANE = 128


def _mm_kernel(a_ref, b_ref, o_ref):
    o_ref[...] = jnp.dot(a_ref[...].astype(BF16), b_ref[...].astype(BF16), preferred_element_type=F32)


def pmatmul(a, b, tm=512, tn=512):
    m, k = a.shape
    n = b.shape[1]
    n_pad = -n % LANE
    if n_pad:
        b = jnp.pad(b, ((0, 0), (0, n_pad)))
    np_ = n + n_pad
    tn = min(tn, np_)
    while np_ % tn:
        tn -= LANE
    tm = min(tm, m)
    assert m % tm == 0
    out = pl.pallas_call(
        _mm_kernel,
        grid=(m // tm, np_ // tn),
        in_specs=[pl.BlockSpec((tm, k), lambda i, j: (i, 0)),
                  pl.BlockSpec((k, tn), lambda i, j: (0, j))],
        out_specs=pl.BlockSpec((tm, tn), lambda i, j: (i, j)),
        out_shape=jax.ShapeDtypeStruct((m, np_), F32),
        name="pmatmul",
    )(a, b)
    return out[:, :n] if n_pad else out


def mm(x, w):
    lead = x.shape[:-1]
    return pmatmul(x.reshape(-1, x.shape[-1]), w).reshape(lead + (w.shape[-1],))


S5_L = 16
S5_NB = S5_CH // LANE
S5_GPB = LANE // S5_GROUP
S5_SW = S5_GPB * S5_STATE
VMEM_LIMIT = 56 * 1024 * 1024


def _s5_weights(lam_re, lam_im, log_dt, b_re, b_im, c_re, c_im):
    L, G = S5_L, S5_GROUPS
    hp = lax.Precision.HIGHEST
    taus = jnp.arange(L + 1, dtype=F32)[:, None, None]
    ks, ws, cas, ds = [], [], [], []
    for d in range(2):
        lr = jnp.minimum(lam_re[d], -1e-4)
        li = lam_im[d]
        dt = jnp.exp(log_dt[d])[:, None]
        mag = jnp.exp(lr * dt)
        ab_re, ab_im = mag * jnp.cos(li * dt), mag * jnp.sin(li * dt)
        xr, xi, den = ab_re - 1.0, ab_im, lr * lr + li * li
        f_re = (xr * lr + xi * li) / den
        f_im = (xi * lr - xr * li) / den
        bb_re = f_re[..., None] * b_re - f_im[..., None] * b_im
        bb_im = f_re[..., None] * b_im + f_im[..., None] * b_re
        pmag = jnp.exp(taus * (lr * dt))
        ar, ai = pmag * jnp.cos(taus * (li * dt)), pmag * jnp.sin(taus * (li * dt))
        wr = ar[..., None] * bb_re - ai[..., None] * bb_im
        wi = ar[..., None] * bb_im + ai[..., None] * bb_re
        k = (jnp.einsum('gop,tgpi->tgio', c_re, wr, precision=hp)
             - jnp.einsum('gop,tgpi->tgio', c_im, wi, precision=hp))
        car = c_re[None] * ar[:, :, None, :] - c_im[None] * ai[:, :, None, :]
        cai = c_re[None] * ai[:, :, None, :] + c_im[None] * ar[:, :, None, :]
        ks.append(k)
        ws.append((wr, wi))
        cas.append((car, cai))
        ds.append((ar[L], ai[L]))

    eye = jnp.eye(S5_GPB, dtype=F32)

    def blockdiag(t, in_axis_first):
        lead = t.shape[:-3]
        x, y = t.shape[-2:]
        t = t.reshape(lead + (S5_NB, S5_GPB, x, y))
        t = jnp.einsum('...jaxy,ab->...jaxby', t, eye)
        return t.reshape(lead + (S5_NB, S5_GPB * x, S5_GPB * y))

    kf, kb = ks
    kc = jnp.concatenate([kb[1:L][::-1], (kf[0] + kb[0])[None], kf[1:L]], axis=0)
    kc = blockdiag(kc, True)
    idx = (jnp.arange(L)[None, :] - jnp.arange(L)[:, None]) + (L - 1)
    tz = kc[idx]
    tz = tz.transpose(2, 0, 3, 1, 4).reshape(S5_NB, L * LANE, L * LANE)

    pbs, cab = [], []
    for d in range(2):
        wr, wi = ws[d]
        order = jnp.arange(L - 1, -1, -1) if d == 0 else jnp.arange(L)
        cols = []
        for w in (wr, wi):
            m = jnp.swapaxes(w[order], -1, -2)
            cols.append(blockdiag(m, True))
        pbs.append(jnp.concatenate(cols, axis=-1))
        car, cai = cas[d]
        order = jnp.arange(1, L + 1) if d == 0 else jnp.arange(L, 0, -1)
        rows = []
        for m in (car[order], -cai[order]):
            m = jnp.swapaxes(m, -1, -2)
            rows.append(blockdiag(m, True))
        cab.append(jnp.concatenate(rows, axis=-2))
    pb = jnp.concatenate(pbs, axis=-1)
    pb = pb.transpose(1, 0, 2, 3).reshape(S5_NB, L * LANE, 4 * S5_SW)
    ca = jnp.concatenate(cab, axis=-2)
    ca = ca.transpose(1, 2, 0, 3).reshape(S5_NB, 4 * S5_SW, L * LANE)
    dr = jnp.stack([ds[0][0], ds[1][0]], 0).reshape(2, S5_NB, 1, S5_SW).transpose(1, 0, 2, 3).reshape(2 * S5_NB, 1, S5_SW)
    di = jnp.stack([ds[0][1], ds[1][1]], 0).reshape(2, S5_NB, 1, S5_SW).transpose(1, 0, 2, 3).reshape(2 * S5_NB, 1, S5_SW)
    return tz.astype(BF16), pb.astype(BF16), ca.astype(BF16), dr, di


def _s5_p_kernel(u_ref, pb_ref, p_ref, *, nb, nc):
    res = jnp.dot(u_ref[...], pb_ref[...], preferred_element_type=F32)
    for b in range(nb):
        p_ref[:, b * 2 * S5_SW:(b + 1) * 2 * S5_SW] = res[b * nc:(b + 1) * nc]


def _s5_scan_kernel(p_ref, dr_ref, di_ref, s_ref, *, n_ctx, n_lat):
    rev = pl.program_id(0) % 2
    dr = dr_ref[...]
    di = di_ref[...]
    nbatch = p_ref.shape[1]

    def phase(base, n, carry):
        def body(step, carry):
            sr, si = carry
            row = base + jnp.where(rev == 0, step, n - 1 - step)
            s_ref[row, :, :S5_SW] = sr
            s_ref[row, :, S5_SW:] = si
            p = p_ref[row]
            nr = dr * sr - di * si + p[:, :S5_SW]
            ni = dr * si + di * sr + p[:, S5_SW:]
            return nr, ni
        return lax.fori_loop(0, n, body, carry)

    zero = jnp.zeros((nbatch, S5_SW), F32)
    carry = phase(0, n_ctx, (zero, zero))
    phase(n_ctx, n_lat, carry)


def _s5_y_kernel(u_ref, tz_ref, sf_ref, sb_ref, ca_ref, y_ref):
    y = jnp.dot(u_ref[...], tz_ref[...], preferred_element_type=F32)
    y += jnp.dot(sf_ref[...].astype(BF16), ca_ref[:2 * S5_SW, :], preferred_element_type=F32)
    y += jnp.dot(sb_ref[...].astype(BF16), ca_ref[2 * S5_SW:, :], preferred_element_type=F32)
    y_ref[...] = y


def s5_bidirectional(u_ctx, u_lat, weights):
    tz, pb, ca, dr, di = weights
    L = S5_L
    nb, t_ctx, _ = u_ctx.shape
    t_lat = u_lat.shape[1]
    assert t_ctx % L == 0 and t_lat % L == 0
    n_ctx, n_lat = t_ctx // L, t_lat // L
    nc = n_ctx + n_lat
    kw = L * LANE
    sw2 = 2 * S5_SW
    u = jnp.concatenate([u_ctx, u_lat], axis=1)
    ub = u.reshape(nb * nc, L, S5_NB, LANE).transpose(2, 0, 1, 3).reshape(S5_NB, nb * nc, kw).astype(BF16)

    p = pl.pallas_call(
        functools.partial(_s5_p_kernel, nb=nb, nc=nc),
        grid=(S5_NB, 2),
        in_specs=[pl.BlockSpec((None, nb * nc, kw), lambda j, d: (j, 0, 0)),
                  pl.BlockSpec((None, kw, sw2), lambda j, d: (j, 0, d))],
        out_specs=pl.BlockSpec((nc, nb * sw2), lambda j, d: (0, j * 2 + d)),
        out_shape=jax.ShapeDtypeStruct((nc, S5_NB * 2 * nb * sw2), F32),
        compiler_params=pltpu.CompilerParams(vmem_limit_bytes=VMEM_LIMIT),
        name="s5_chunk_inputs",
    )(ub, pb)

    p4 = p.reshape(nc, S5_NB * 2, nb, sw2)
    s4 = pl.pallas_call(
        functools.partial(_s5_scan_kernel, n_ctx=n_ctx, n_lat=n_lat),
        grid=(S5_NB * 2,),
        in_specs=[pl.BlockSpec((nc, None, nb, sw2), lambda g: (0, g, 0, 0)),
                  pl.BlockSpec((None, 1, S5_SW), lambda g: (g, 0, 0)),
                  pl.BlockSpec((None, 1, S5_SW), lambda g: (g, 0, 0))],
        out_specs=pl.BlockSpec((nc, None, nb, sw2), lambda g: (0, g, 0, 0)),
        out_shape=jax.ShapeDtypeStruct(p4.shape, F32),
        compiler_params=pltpu.CompilerParams(vmem_limit_bytes=VMEM_LIMIT),
        name="s5_state_scan",
    )(p4, dr, di)

    s2 = s4.reshape(nc, S5_NB * 2 * nb * sw2)
    yb = pl.pallas_call(
        _s5_y_kernel,
        grid=(S5_NB, nb),
        in_specs=[pl.BlockSpec((None, nc, kw), lambda j, b: (j, b, 0)),
                  pl.BlockSpec((None, kw, kw), lambda j, b: (j, 0, 0)),
                  pl.BlockSpec((nc, sw2), lambda j, b: (0, (j * 2) * nb + b)),
                  pl.BlockSpec((nc, sw2), lambda j, b: (0, (j * 2 + 1) * nb + b)),
                  pl.BlockSpec((None, 2 * sw2, kw), lambda j, b: (j, 0, 0))],
        out_specs=pl.BlockSpec((None, nc, kw), lambda j, b: (j, b, 0)),
        out_shape=jax.ShapeDtypeStruct((S5_NB, nb * nc, kw), F32),
        compiler_params=pltpu.CompilerParams(vmem_limit_bytes=VMEM_LIMIT),
        name="s5_output",
    )(ub, tz, s2, s2, ca)
    y = yb.reshape(S5_NB, nb, nc, L, LANE).transpose(1, 2, 3, 0, 4).reshape(nb, nc * L, S5_CH)
    return y[:, :t_ctx], y[:, t_ctx:]


ROUTE_TN = 512
MOE_TN = 1024
SEL_ROWS = 128
FF_SPLIT = 2
AFF_PARTS = 3


def _route_kernel(h_ref, sc_ref, sh_ref, wr_ref, hm_ref, afft_ref, asp_ref):
    hm = h_ref[...] * (1.0 + sc_ref[...]) + sh_ref[...]
    hm_ref[...] = hm.astype(BF16)
    logits = jnp.dot(hm, wr_ref[...], precision=lax.Precision.HIGHEST, preferred_element_type=F32)
    lane = lax.broadcasted_iota(jnp.int32, logits.shape, 1)
    logits = jnp.where(lane < N_EXPERTS, logits, -jnp.inf)
    ex = jnp.exp(logits - jnp.max(logits, axis=1, keepdims=True))
    aff = ex / jnp.sum(ex, axis=1, keepdims=True)
    afft_ref[...] = aff.T[:N_EXPERTS, :]
    hi = aff.astype(BF16).astype(F32)
    mid = (aff - hi).astype(BF16).astype(F32)
    lo = (aff - hi - mid).astype(BF16).astype(F32)
    asp = hi + pltpu.roll(mid, N_EXPERTS, 1) + pltpu.roll(lo, 2 * N_EXPERTS, 1)
    asp_ref[...] = asp.astype(BF16)


def _select_kernel(aff_ref, pos_ref, *, nblk_log2, cap):
    a = aff_ref[...]
    r = a.shape[0]
    bits = pltpu.bitcast(a, jnp.int32)
    ri = lax.broadcasted_iota(jnp.int32, (r, r), 0)
    rj = lax.broadcasted_iota(jnp.int32, (r, r), 1)
    same = (ri >> nblk_log2) == (rj >> nblk_log2)
    gm = jnp.where(same, 1.0, 0.0).astype(BF16)
    lm = jnp.where(same & (rj < ri), 1.0, 0.0).astype(BF16)
    li = lax.broadcasted_iota(jnp.int32, (LANE, LANE), 0)
    lj = lax.broadcasted_iota(jnp.int32, (LANE, LANE), 1)
    um = jnp.where(li <= lj, 1.0, 0.0).astype(BF16)

    def group_count(mask):
        rc = jnp.sum(jnp.where(mask, 1.0, 0.0), axis=1, keepdims=True)
        gc = jnp.dot(gm, jnp.broadcast_to(rc, (r, LANE)).astype(BF16), preferred_element_type=F32)
        return gc[:, :1]

    def bisect(i, thr):
        cand = thr | jnp.left_shift(1, 29 - i)
        return jnp.where(group_count(bits >= cand) >= cap, cand, thr)

    thr = lax.fori_loop(0, 30, bisect, jnp.zeros((r, 1), jnp.int32))

    def prefix(mask):
        x = jnp.where(mask, 1.0, 0.0)
        inc = jnp.dot(x.astype(BF16), um, preferred_element_type=F32)
        tot = jnp.broadcast_to(inc[:, LANE - 1:LANE], (r, LANE)).astype(BF16)
        return inc - x + jnp.dot(lm, tot, preferred_element_type=F32)

    gt = bits > thr
    eq = bits == thr
    need = cap - group_count(gt)
    sel = gt | (eq & (prefix(eq) < need))
    pos_ref[...] = jnp.where(sel, prefix(sel), -1.0).astype(jnp.int32)


def _onehot(pos_ref, cap):
    rows = lax.broadcasted_iota(jnp.int32, (cap, LANE), 0)
    blocks = [jnp.where(pos_ref[k:k + 1, :] == rows, 1.0, 0.0).astype(BF16) for k in range(pos_ref.shape[0])]
    return jnp.concatenate(blocks, axis=1)


def _ffn_kernel(pos_ref, hb_ref, asp_ref, wg_ref, wu_ref, wd_ref, ys_ref, xs_acc, g_acc, *, cap):
    e = pl.program_id(0)
    kt = pl.program_id(2)

    @pl.when(kt == 0)
    def _():
        xs_acc[...] = jnp.zeros_like(xs_acc)
        g_acc[...] = jnp.zeros_like(g_acc)

    oh = _onehot(pos_ref, cap)
    xs_acc[...] += jnp.dot(oh, hb_ref[...], preferred_element_type=F32)
    g_acc[...] += jnp.dot(oh, asp_ref[...], preferred_element_type=F32)

    @pl.when(kt == pl.num_programs(2) - 1)
    def _():
        xs = xs_acc[...].astype(BF16)
        g = g_acc[...]
        lane = lax.broadcasted_iota(jnp.int32, g.shape, 1)
        gate = jnp.sum(jnp.where((lane & (N_EXPERTS - 1)) == e, g, 0.0), axis=1, keepdims=True)
        fw = EXPERT_FF // FF_SPLIT
        y = jnp.zeros((cap, D_MODEL), F32)
        for f in range(FF_SPLIT):
            hg = jnp.dot(xs, wg_ref[:, f * fw:(f + 1) * fw], preferred_element_type=F32)
            hu = jnp.dot(xs, wu_ref[:, f * fw:(f + 1) * fw], preferred_element_type=F32)
            hid = (hg * jax.nn.sigmoid(hg)) * hu
            y += jnp.dot(hid.astype(BF16), wd_ref[f * fw:(f + 1) * fw, :], preferred_element_type=F32)
        ys_ref[...] = (y * gate).astype(BF16)


def _ln_rows(z, g, b):
    mu = jnp.mean(z, axis=-1, keepdims=True)
    zc = z - mu
    var = jnp.mean(zc * zc, axis=-1, keepdims=True)
    return zc * lax.rsqrt(var + EPS) * g + b


def _combine_kernel(pos_ref, ys_ref, h_ref, g2_ref, lng_ref, lnb_ref, o_ref, acc, *, cap):
    e = pl.program_id(2)

    @pl.when(e == 0)
    def _():
        acc[...] = jnp.zeros_like(acc)

    oh = _onehot(pos_ref, cap)
    acc[...] += lax.dot_general(oh, ys_ref[...], (((0,), (0,)), ((), ())), preferred_element_type=F32)

    @pl.when(e == pl.num_programs(2) - 1)
    def _():
        z = ALPHA * h_ref[...] + g2_ref[...] * acc[...]
        o_ref[...] = _ln_rows(z, lng_ref[...], lnb_ref[...])


def moe_block(h, sc, sh, g2, ln_g, ln_b, w_router, wg, wu, wd):
    nb, n, dm = h.shape
    cap = CAPACITY_FACTOR * n // N_EXPERTS
    nblk = n // LANE
    assert n % LANE == 0 and nblk & (nblk - 1) == 0
    tn_r = min(ROUTE_TN, n)
    tn = min(MOE_TN, n)
    wr = jnp.pad(w_router, ((0, 0), (0, LANE - N_EXPERTS)))
    row = lambda v: v.reshape(1, dm)

    hm, afft, asp = pl.pallas_call(
        _route_kernel,
        grid=(nb, n // tn_r),
        in_specs=[pl.BlockSpec((None, tn_r, dm), lambda b, t: (b, t, 0)),
                  pl.BlockSpec((None, 1, dm), lambda b, t: (b, 0, 0)),
                  pl.BlockSpec((None, 1, dm), lambda b, t: (b, 0, 0)),
                  pl.BlockSpec((dm, LANE), lambda b, t: (0, 0))],
        out_specs=[pl.BlockSpec((None, tn_r, dm), lambda b, t: (b, t, 0)),
                   pl.BlockSpec((None, N_EXPERTS, tn_r), lambda b, t: (b, 0, t)),
                   pl.BlockSpec((None, tn_r, LANE), lambda b, t: (b, t, 0))],
        out_shape=[jax.ShapeDtypeStruct((nb, n, dm), BF16),
                   jax.ShapeDtypeStruct((nb, N_EXPERTS, n), F32),
                   jax.ShapeDtypeStruct((nb, n, LANE), BF16)],
        name="moe_route",
    )(h, sc, sh, wr)

    rows_total = nb * N_EXPERTS * nblk
    rb = max(SEL_ROWS, N_EXPERTS * nblk)
    assert rows_total % rb == 0
    pos = pl.pallas_call(
        functools.partial(_select_kernel, nblk_log2=nblk.bit_length() - 1, cap=cap),
        grid=(rows_total // rb,),
        in_specs=[pl.BlockSpec((rb, LANE), lambda i: (i, 0))],
        out_specs=pl.BlockSpec((rb, LANE), lambda i: (i, 0)),
        out_shape=jax.ShapeDtypeStruct((rows_total, LANE), jnp.int32),
        name="moe_select",
    )(afft.reshape(rows_total, LANE))
    pos = pos.reshape(nb, N_EXPERTS, nblk, LANE)

    tb = tn // LANE
    ys = pl.pallas_call(
        functools.partial(_ffn_kernel, cap=cap),
        grid=(N_EXPERTS, nb, n // tn),
        in_specs=[pl.BlockSpec((None, None, tb, LANE), lambda e, b, k: (b, e, k, 0)),
                  pl.BlockSpec((None, tn, dm), lambda e, b, k: (b, k, 0)),
                  pl.BlockSpec((None, tn, LANE), lambda e, b, k: (b, k, 0)),
                  pl.BlockSpec((None, dm, EXPERT_FF), lambda e, b, k: (e, 0, 0)),
                  pl.BlockSpec((None, dm, EXPERT_FF), lambda e, b, k: (e, 0, 0)),
                  pl.BlockSpec((None, EXPERT_FF, dm), lambda e, b, k: (e, 0, 0))],
        out_specs=pl.BlockSpec((None, None, cap, dm), lambda e, b, k: (b, e, 0, 0)),
        out_shape=jax.ShapeDtypeStruct((nb, N_EXPERTS, cap, dm), BF16),
        scratch_shapes=[pltpu.VMEM((cap, dm), F32), pltpu.VMEM((cap, LANE), F32)],
        compiler_params=pltpu.CompilerParams(vmem_limit_bytes=VMEM_LIMIT),
        name="moe_ffn",
    )(pos, hm, asp, wg, wu, wd)

    return pl.pallas_call(
        functools.partial(_combine_kernel, cap=cap),
        grid=(nb, n // tn, N_EXPERTS),
        in_specs=[pl.BlockSpec((None, None, tb, LANE), lambda b, t, e: (b, e, t, 0)),
                  pl.BlockSpec((None, None, cap, dm), lambda b, t, e: (b, e, 0, 0)),
                  pl.BlockSpec((None, tn, dm), lambda b, t, e: (b, t, 0)),
                  pl.BlockSpec((None, 1, dm), lambda b, t, e: (b, 0, 0)),
                  pl.BlockSpec((1, dm), lambda b, t, e: (0, 0)),
                  pl.BlockSpec((1, dm), lambda b, t, e: (0, 0))],
        out_specs=pl.BlockSpec((None, tn, dm), lambda b, t, e: (b, t, 0)),
        out_shape=jax.ShapeDtypeStruct((nb, n, dm), F32),
        scratch_shapes=[pltpu.VMEM((tn, dm), F32)],
        compiler_params=pltpu.CompilerParams(vmem_limit_bytes=VMEM_LIMIT),
        name="moe_combine",
    )(pos, ys, h, g2, row(ln_g), row(ln_b))


N_HEADS = 4
STACK = N_HEADS * CHUNK
CHUNK_LOG2 = CHUNK.bit_length() - 1
SOLVE_LEVELS = CHUNK_LOG2


def _stack_heads(x):
    return jnp.concatenate([x[:, h * HEAD_DIM:(h + 1) * HEAD_DIM] for h in range(N_HEADS)], axis=0)


def _stack_cols(cols, width):
    return jnp.concatenate([jnp.broadcast_to(c, (CHUNK, width)) for c in cols], axis=0)


def _mxu(a, b, dims, hp):
    dg = lambda x, y: lax.dot_general(x, y, (dims, ((), ())), preferred_element_type=F32)
    a_hi, b_hi = a.astype(BF16), b.astype(BF16)
    if not hp:
        return dg(a_hi, b_hi)
    a_lo = (a - a_hi.astype(F32)).astype(BF16)
    b_lo = (b - b_hi.astype(F32)).astype(BF16)
    return dg(a_hi, b_hi) + (dg(a_lo, b_hi) + dg(a_hi, b_lo))


def _dot_nt(a, b, hp=False):
    return _mxu(a, b, ((1,), (1,)), hp)


def _dot_tn(a, b, hp=False):
    return _mxu(a, b, ((0,), (0,)), hp)


def _dot(a, b, hp=False):
    return _mxu(a, b, ((1,), (0,)), hp)


def _chunk_masks(rev):
    ri = lax.broadcasted_iota(jnp.int32, (STACK, STACK), 0)
    ci = lax.broadcasted_iota(jnp.int32, (STACK, STACK), 1)
    same = (ri >> CHUNK_LOG2) == (ci >> CHUNK_LOG2)
    ahead = jnp.where(rev, ci - ri, ri - ci)
    return same & (ahead >= 0), same & (ahead > 0)


def _scan_cumsum(gt, rev):
    ii = lax.broadcasted_iota(jnp.int32, (CHUNK, CHUNK), 0)
    jj = lax.broadcasted_iota(jnp.int32, (CHUNK, CHUNK), 1)
    tri = jnp.where(jnp.where(rev, jj - ii, ii - jj) >= 0, 1.0, 0.0)
    return jnp.dot(tri, gt, precision=lax.Precision.HIGHEST, preferred_element_type=F32)


def _unit_triangular_inverse(a):
    ri = lax.broadcasted_iota(jnp.int32, a.shape, 0)
    ci = lax.broadcasted_iota(jnp.int32, a.shape, 1)
    joins = lambda lvl: ((ri >> (lvl + 1)) == (ci >> (lvl + 1))) & ((ri >> lvl) != (ci >> lvl))
    t = jnp.where(ri == ci, 1.0, 0.0) - jnp.where(joins(0), a, 0.0)
    for lvl in range(1, CHUNK_LOG2):
        t = t - _dot(t, _dot(jnp.where(joins(lvl), a, 0.0), t))
    return t


def _linear_scan_kernel(*refs, has_beta, k_scale):
    if has_beta:
        q_ref, k_ref, v_ref, gt_ref, gp_ref, s0_ref, o_ref, sfin_ref, s_scr = refs
    else:
        q_ref, k_ref, v_ref, gp_ref, s0_ref, o_ref, sfin_ref, s_scr = refs
    rev = pl.program_id(0) == 1
    n = pl.program_id(2)

    @pl.when(n == 0)
    def _():
        s_scr[...] = s0_ref[...]

    for i in range(q_ref.shape[0]):
        if has_beta:
            gt = _gdn_gates(gt_ref[i], gp_ref[...])
        else:
            gt = jnp.broadcast_to(gp_ref[0:1, :], (CHUNK, LANE))
        _linear_chunk(q_ref.at[i], k_ref.at[i], v_ref.at[i], gt, o_ref.at[i], s_scr.at[i], rev, has_beta, k_scale)

    @pl.when(n == pl.num_programs(2) - 1)
    def _():
        sfin_ref[...] = s_scr[...]


def _softplus(x):
    return jnp.maximum(x, 0.0) + jnp.log1p(jnp.exp(-jnp.abs(x)))


def _gdn_gates(raw, gp):
    lane = lax.broadcasted_iota(jnp.int32, raw.shape, 1)
    return jnp.where(lane < 2 * N_HEADS, gp[0:1, :] * _softplus(raw + gp[1:2, :]), jax.nn.sigmoid(raw))


def _mlstm_gates(raw, gp):
    lane = lax.broadcasted_iota(jnp.int32, raw.shape, 1)
    x = raw + gp[0:1, :]
    return jnp.where(lane < 2 * N_HEADS, x, -_softplus(-x))


def _linear_chunk(q_ref, k_ref, v_ref, gt, o_ref, s_scr, rev, has_beta, k_scale):
    cum = _scan_cumsum(gt, rev)
    gcols, bcols, gtots = [], [], []
    for h in range(N_HEADS):
        gc = jnp.where(rev, cum[:, N_HEADS + h:N_HEADS + h + 1], cum[:, h:h + 1])
        gcols.append(gc)
        gtots.append(jnp.where(rev, gc[0:1], gc[CHUNK - 1:CHUNK]))
        bcols.append(jnp.where(rev, gt[:, 3 * N_HEADS + h:3 * N_HEADS + h + 1], gt[:, 2 * N_HEADS + h:2 * N_HEADS + h + 1]))
    cb = _stack_cols(gcols, STACK)
    diff = cb - cb.T
    incl, strict = _chunk_masks(rev)
    dec = jnp.exp(jnp.where(incl, diff, 0.0))
    gcb = cb[:, :HEAD_DIM]
    q_st, k_st, v_st = _stack_heads(q_ref[...]), _stack_heads(k_ref[...]) * k_scale, _stack_heads(v_ref[...])
    if has_beta:
        beta = _stack_cols(bcols, HEAD_DIM)
        kb = k_st * beta
        a = _dot_nt(kb, k_st) * jnp.where(strict, dec, 0.0)
        x = _dot(_unit_triangular_inverse(a), jnp.concatenate([v_st * beta, kb * jnp.exp(gcb)], axis=1))
        u_st, w_st = x[:, :HEAD_DIM], x[:, HEAD_DIM:]
    else:
        u_st, w_st = v_st, None
    gtot = _stack_cols(gtots, HEAD_DIM)
    k_end = k_st * jnp.exp(gtot - gcb)
    q_dec = q_st * jnp.exp(gcb)
    a_qk = _dot_nt(q_st, k_st) * jnp.where(incl, dec, 0.0)
    hs = lambda t, h: t[h * CHUNK:(h + 1) * CHUNK]
    states = [s_scr[h] for h in range(N_HEADS)]
    if has_beta:
        vn = jnp.concatenate([hs(u_st, h) - _dot(hs(w_st, h), states[h]) for h in range(N_HEADS)], axis=0)
    else:
        vn = u_st
    o_st = _dot(a_qk, vn)
    for h in range(N_HEADS):
        o_ref[:, h * HEAD_DIM:(h + 1) * HEAD_DIM] = hs(o_st, h) + _dot(hs(q_dec, h), states[h])
        s_scr[h] = jnp.exp(gtots[h]) * states[h] + _dot_tn(hs(k_end, h), hs(vn, h))


SCAN_NS = 2


HEADS_W = N_HEADS * HEAD_DIM


def _scan_specs(srcs, gates, nb, nchunk):
    ns = SCAN_NS
    assert nb % ns == 0
    cidx = lambda d, n: n + d * (nchunk - 1 - 2 * n)
    specs = [pl.BlockSpec((ns, CHUNK, HEADS_W), lambda d, b, n, c=c: (b, cidx(d, n), c)) for _, c in srcs]
    if gates is not None:
        specs.append(pl.BlockSpec((ns, CHUNK, LANE), lambda d, b, n, c=gates[1]: (b, cidx(d, n), c)))
    specs.append(pl.BlockSpec((2, LANE), lambda d, b, n: (0, 0)))
    ospec = pl.BlockSpec((None, ns, CHUNK, HEADS_W), lambda d, b, n: (d, b, cidx(d, n), 0))
    return specs, ospec


def linear_scan_bidir(q, k, v, gates, gp, s0, has_beta, k_scale=1.0):
    nb, t, _ = q[0].shape
    nchunk = t // CHUNK
    assert t % CHUNK == 0
    ns = SCAN_NS
    specs, ospec = _scan_specs((q, k, v), gates if has_beta else None, nb, nchunk)
    sspec = pl.BlockSpec((None, ns, N_HEADS, HEAD_DIM, HEAD_DIM), lambda d, b, n: (d, b, 0, 0, 0))
    args = [q[0], k[0], v[0]] + ([gates[0]] if has_beta else []) + [gp, s0]
    return pl.pallas_call(
        functools.partial(_linear_scan_kernel, has_beta=has_beta, k_scale=k_scale),
        grid=(2, nb // ns, nchunk),
        in_specs=specs + [sspec],
        out_specs=[ospec, sspec],
        out_shape=[jax.ShapeDtypeStruct((2, nb, t, HEADS_W), F32),
                   jax.ShapeDtypeStruct((2, nb, N_HEADS, HEAD_DIM, HEAD_DIM), F32)],
        scratch_shapes=[pltpu.VMEM((ns, N_HEADS, HEAD_DIM, HEAD_DIM), F32)],
        name="gdn_scan" if has_beta else "retention_scan",
    )(*args)


def linear_scan_two_pass(ctx_args, lat_args, gp, has_beta, k_scale=1.0):
    nb = ctx_args[0][0].shape[0]
    zero = jnp.zeros((2, nb, N_HEADS, HEAD_DIM, HEAD_DIM), F32)
    o_ctx, s_ctx = linear_scan_bidir(*ctx_args, gp, zero, has_beta, k_scale)
    o_lat, _ = linear_scan_bidir(*lat_args, gp, s_ctx, has_beta, k_scale)
    return o_ctx, o_lat


MLSTM_HP = True


def _mlstm_scan_kernel(q_ref, k_ref, v_ref, gt_ref, gp_ref, c0_ref, n0_ref, m0_ref, o_ref, cfin_ref, nfin_ref,
                       mfin_ref, c_scr, n_scr, m_scr):
    rev = pl.program_id(0) == 1
    step = pl.program_id(2)

    @pl.when(step == 0)
    def _():
        c_scr[...] = c0_ref[...]
        n_scr[...] = n0_ref[...]
        m_scr[...] = m0_ref[...]

    for i in range(q_ref.shape[0]):
        _mlstm_chunk(q_ref.at[i], k_ref.at[i], v_ref.at[i], _mlstm_gates(gt_ref[i], gp_ref[...]), o_ref.at[i],
                     c_scr.at[i], n_scr.at[i], m_scr.at[i], rev)

    @pl.when(step == pl.num_programs(2) - 1)
    def _():
        cfin_ref[...] = c_scr[...]
        nfin_ref[...] = n_scr[...]
        mfin_ref[...] = m_scr[...]


def _mlstm_chunk(q_ref, k_ref, v_ref, gt, o_ref, c_scr, n_scr, m_scr, rev):
    cum = _scan_cumsum(gt, rev)
    pick = lambda t, c: jnp.where(rev, t[:, N_HEADS + c:N_HEADS + c + 1], t[:, c:c + 1])
    q_st, k_st, v_st = _stack_heads(q_ref[...]), _stack_heads(k_ref[...]), _stack_heads(v_ref[...])
    hs = lambda t, h: t[h * CHUNK:(h + 1) * CHUNK]
    bcums, srcs, inters, qcs, qns = [], [], [], [], []
    for h in range(N_HEADS):
        ic = pick(gt, h)
        bcum = pick(cum, 2 * N_HEADS + h)
        b_end = jnp.where(rev, bcum[0:1], bcum[CHUNK - 1:CHUNK])
        c_prev, n_prev, m_prev = c_scr[h], n_scr[h], m_scr[h][:, :1]
        a = b_end - bcum + ic
        m_new = jnp.maximum(b_end + m_prev, jnp.max(a, axis=0, keepdims=True))
        w_state = jnp.exp(a - m_new)
        decay = jnp.exp(b_end + m_prev - m_new)
        kw = hs(k_st, h) * w_state
        c_scr[h] = decay * c_prev + _dot_tn(kw, hs(v_st, h), MLSTM_HP)
        n_scr[h] = decay * n_prev + jnp.sum(kw, axis=0, keepdims=True)
        m_scr[h] = jnp.broadcast_to(m_new, (1, HEAD_DIM))
        bcums.append(bcum)
        srcs.append(bcum - ic)
        inters.append(bcum + m_prev)
        qcs.append(_dot(hs(q_st, h), c_prev, MLSTM_HP))
        qns.append(jnp.sum(hs(q_st, h) * n_prev, axis=1, keepdims=True))
    incl, _ = _chunk_masks(rev)
    dlog = _stack_cols(bcums, STACK) - _stack_cols(srcs, STACK).T
    inter = jnp.concatenate(inters, axis=0)
    m_t = jnp.maximum(inter, jnp.max(jnp.where(incl, dlog, -1e30), axis=1, keepdims=True))
    s = _dot_nt(q_st, k_st, MLSTM_HP) * jnp.where(incl, jnp.exp(jnp.where(incl, dlog, 0.0) - m_t), 0.0)
    w_inter = jnp.exp(inter - m_t)
    num = _dot(s, v_st, MLSTM_HP) + w_inter * jnp.concatenate(qcs, axis=0)
    den = jnp.sum(s, axis=1, keepdims=True) + w_inter * jnp.concatenate(qns, axis=0)
    out = num / jnp.maximum(jnp.abs(den), jnp.exp(-m_t))
    for h in range(N_HEADS):
        o_ref[:, h * HEAD_DIM:(h + 1) * HEAD_DIM] = hs(out, h)


def mlstm_scan_bidir(q, k, v, gates, gp, state):
    nb, t, _ = q[0].shape
    nchunk = t // CHUNK
    assert t % CHUNK == 0
    ns = SCAN_NS
    specs, ospec = _scan_specs((q, k, v), gates, nb, nchunk)
    cspec = pl.BlockSpec((None, ns, N_HEADS, HEAD_DIM, HEAD_DIM), lambda d, b, n: (d, b, 0, 0, 0))
    vspec = pl.BlockSpec((None, ns, N_HEADS, 1, HEAD_DIM), lambda d, b, n: (d, b, 0, 0, 0))
    cshape = jax.ShapeDtypeStruct((2, nb, N_HEADS, HEAD_DIM, HEAD_DIM), F32)
    vshape = jax.ShapeDtypeStruct((2, nb, N_HEADS, 1, HEAD_DIM), F32)
    o, c, n, m = pl.pallas_call(
        _mlstm_scan_kernel,
        grid=(2, nb // ns, nchunk),
        in_specs=specs + [cspec, vspec, vspec],
        out_specs=[ospec, cspec, vspec, vspec],
        out_shape=[jax.ShapeDtypeStruct((2, nb, t, HEADS_W), F32), cshape, vshape, vshape],
        scratch_shapes=[pltpu.VMEM((ns, N_HEADS, HEAD_DIM, HEAD_DIM), F32), pltpu.VMEM((ns, N_HEADS, 1, HEAD_DIM), F32),
                        pltpu.VMEM((ns, N_HEADS, 1, HEAD_DIM), F32)],
        name="mlstm_scan",
    )(q[0], k[0], v[0], gates[0], gp, *state)
    return o, (c, n, m)


def mlstm_two_pass(ctx_args, lat_args, gp):
    nb = ctx_args[0][0].shape[0]
    zero = (jnp.zeros((2, nb, N_HEADS, HEAD_DIM, HEAD_DIM), F32), jnp.zeros((2, nb, N_HEADS, 1, HEAD_DIM), F32),
            jnp.zeros((2, nb, N_HEADS, 1, HEAD_DIM), F32))
    o_ctx, s_ctx = mlstm_scan_bidir(*ctx_args, gp, zero)
    o_lat, _ = mlstm_scan_bidir(*lat_args, gp, s_ctx)
    return o_ctx, o_lat


PROJ_TM = 512
PROJ_TN_MAX = 2304


def _inproj_kernel(h_ref, sc_ref, sh_ref, w_ref, o_ref):
    hm = (h_ref[...] * (1.0 + sc_ref[...]) + sh_ref[...]).astype(BF16)
    o_ref[...] = jnp.dot(hm, w_ref[...], preferred_element_type=F32)


def in_projection(h, sc, sh, w):
    nb, n, dm = h.shape
    ncol = w.shape[1]
    tm = min(PROJ_TM, n)
    tn = max(c for c in range(LANE, min(ncol, PROJ_TN_MAX) + 1, LANE) if ncol % c == 0)
    return pl.pallas_call(
        _inproj_kernel,
        grid=(nb, n // tm, ncol // tn),
        in_specs=[pl.BlockSpec((None, tm, dm), lambda b, t, j: (b, t, 0)),
                  pl.BlockSpec((None, 1, dm), lambda b, t, j: (b, 0, 0)),
                  pl.BlockSpec((None, 1, dm), lambda b, t, j: (b, 0, 0)),
                  pl.BlockSpec((dm, tn), lambda b, t, j: (0, j))],
        out_specs=pl.BlockSpec((None, tm, tn), lambda b, t, j: (b, t, j)),
        out_shape=jax.ShapeDtypeStruct((nb, n, ncol), F32),
        name="in_projection",
    )(h, sc, sh, w)


def _conv_kernel(x_ref, w_ref, o_ref, *, grid_w, l2_blocks, q_scale, k_scale):
    x = x_ref[...]
    t = x.shape[0]
    w = w_ref[...]
    col = lax.broadcasted_iota(jnp.int32, x.shape, 0) & (grid_w - 1)
    left = jnp.where(col == 0, 0.0, pltpu.roll(x, 1, 0))
    right = jnp.where(col == grid_w - 1, 0.0, pltpu.roll(x, t - 1, 0))
    row = lambda kh: w[3 * kh:3 * kh + 1] * left + w[3 * kh + 1:3 * kh + 2] * x + w[3 * kh + 2:3 * kh + 3] * right
    acc = row(1)
    if grid_w < t:
        zero = jnp.zeros((grid_w, LANE), F32)
        acc = acc + jnp.concatenate([zero, row(0)[:t - grid_w]], axis=0) + jnp.concatenate([row(2)[grid_w:], zero], axis=0)
    y = acc * jax.nn.sigmoid(acc)
    c = pl.program_id(1)
    normed = y * lax.rsqrt(jnp.sum(y * y, axis=1, keepdims=True) + 1e-6)
    y = jnp.where(c < l2_blocks, normed, y)
    o_ref[...] = y * jnp.where(c < N_HEADS, q_scale, jnp.where(c < 2 * N_HEADS, k_scale, 1.0))


def conv_prep(proj, conv_w, nblk, grid_w, l2_blocks, q_scale, k_scale):
    nb, t, _ = proj.shape
    assert grid_w & (grid_w - 1) == 0 and t % grid_w == 0
    return pl.pallas_call(
        functools.partial(_conv_kernel, grid_w=grid_w, l2_blocks=l2_blocks, q_scale=q_scale, k_scale=k_scale),
        grid=(nb, nblk),
        in_specs=[pl.BlockSpec((None, t, LANE), lambda b, c: (b, 0, c)),
                  pl.BlockSpec((CONV_K * CONV_K, LANE), lambda b, c: (0, c))],
        out_specs=pl.BlockSpec((None, t, LANE), lambda b, c: (b, 0, c)),
        out_shape=jax.ShapeDtypeStruct((nb, t, nblk * LANE), F32),
        compiler_params=pltpu.CompilerParams(vmem_limit_bytes=VMEM_LIMIT),
        name="conv_prep",
    )(proj, conv_w.reshape(CONV_K * CONV_K, -1))


def _head_norm(o, center):
    outs = []
    for h in range(N_HEADS):
        x = o[:, h * HEAD_DIM:(h + 1) * HEAD_DIM]
        if center:
            x = x - jnp.mean(x, axis=1, keepdims=True)
        outs.append(x * lax.rsqrt(jnp.mean(x * x, axis=1, keepdims=True) + EPS))
    return jnp.concatenate(outs, axis=1)


def _mix_out(y_a, y_b, wo_ref, h_ref, g1_ref, lng_ref, lnb_ref, o_ref):
    y = (jnp.dot(y_a.astype(BF16), wo_ref[:HEADS_W, :], preferred_element_type=F32)
         + jnp.dot(y_b.astype(BF16), wo_ref[HEADS_W:, :], preferred_element_type=F32))
    o_ref[...] = _ln_rows(ALPHA * h_ref[...] + g1_ref[...] * y, lng_ref[...], lnb_ref[...])


def _merge_even_kernel(og_ref, or_ref, za_ref, zr_ref, gg_ref, rg_ref, wo_ref, h_ref, g1_ref, lng_ref, lnb_ref, o_ref):
    za, zr = za_ref[...], zr_ref[...]
    y_g = _head_norm(og_ref[0] + og_ref[1], False) * gg_ref[...] * (za * jax.nn.sigmoid(za))
    y_r = _head_norm(or_ref[0] + or_ref[1], True) * rg_ref[...] * (zr * jax.nn.sigmoid(zr))
    _mix_out(y_g, y_r, wo_ref, h_ref, g1_ref, lng_ref, lnb_ref, o_ref)


def _merge_odd_kernel(om_ref, ys_ref, og_ref, u_ref, mg_ref, dsk_ref, bglu_ref, wglu_ref, wo_ref, h_ref, g1_ref,
                      lng_ref, lnb_ref, o_ref):
    y_m = _head_norm(om_ref[0] + om_ref[1], True) * mg_ref[...] * jax.nn.sigmoid(og_ref[...])
    y = jax.nn.gelu(ys_ref[...] + dsk_ref[...] * u_ref[...])
    y = y * jax.nn.sigmoid(jnp.dot(y.astype(BF16), wglu_ref[...], preferred_element_type=F32) + bglu_ref[...])
    _mix_out(y_m, y, wo_ref, h_ref, g1_ref, lng_ref, lnb_ref, o_ref)


def _merge_call(kernel_fn, name, scans, toks, rows, mats, h, g1, ln_g, ln_b):
    nb, n, dm = h.shape
    tm = min(PROJ_TM, n)
    full = lambda a: pl.BlockSpec(a.shape, lambda b, t: (0,) * a.ndim)
    rowv = lambda v: v.reshape(1, -1)
    specs, args = [], []
    for a in scans:
        specs.append(pl.BlockSpec((2, None, tm, HEADS_W), lambda b, t: (0, b, t, 0)))
        args.append(a)
    for a, c in toks:
        specs.append(pl.BlockSpec((None, tm, HEADS_W), lambda b, t, c=c: (b, t, c)))
        args.append(a)
    for v in rows:
        args.append(rowv(v))
        specs.append(full(args[-1]))
    for m in mats:
        args.append(m)
        specs.append(full(m))
    args += [h, g1, rowv(ln_g), rowv(ln_b)]
    specs += [pl.BlockSpec((None, tm, dm), lambda b, t: (b, t, 0)), pl.BlockSpec((None, 1, dm), lambda b, t: (b, 0, 0)),
              full(args[-2]), full(args[-1])]
    return pl.pallas_call(
        kernel_fn,
        grid=(nb, n // tm),
        in_specs=specs,
        out_specs=pl.BlockSpec((None, tm, dm), lambda b, t: (b, t, 0)),
        out_shape=jax.ShapeDtypeStruct((nb, n, dm), F32),
        compiler_params=pltpu.CompilerParams(vmem_limit_bytes=VMEM_LIMIT),
        name=name,
    )(*args)


def split_cols(t, sizes):
    return jnp.split(t, np.cumsum(sizes)[:-1].tolist(), axis=-1)


def heads(t, n):
    return t.reshape(t.shape[:-1] + (n, t.shape[-1] // n))


def flip_t(t):
    return None if t is None else jnp.flip(t, axis=1)


def l2norm(t):
    return t * lax.rsqrt(jnp.sum(t * t, axis=-1, keepdims=True) + 1e-6)


def layer_norm(t, g, b):
    mu = t.mean(-1, keepdims=True)
    var = jnp.square(t - mu).mean(-1, keepdims=True)
    return ((t - mu) * lax.rsqrt(var + EPS)) * g + b


def rms_norm_heads(o, g):
    y = o * lax.rsqrt(jnp.mean(o * o, axis=-1, keepdims=True) + EPS) * g
    return y.reshape(o.shape[:2] + (-1,))


def group_norm_heads(o, g):
    mu = o.mean(-1, keepdims=True)
    var = jnp.square(o - mu).mean(-1, keepdims=True)
    return ((o - mu) * lax.rsqrt(var + EPS)).reshape(o.shape[:2] + (-1,)) * g


def short_conv(t, w, on_grid):
    ch = t.shape[-1]
    if on_grid:
        b, n = t.shape[:2]
        rows = n // GRID_W
        tg = t.reshape(b, rows, GRID_W, ch)
        y = lax.conv_general_dilated(tg, w[:, :, None, :], (1, 1), 'SAME',
                                     dimension_numbers=('NHWC', 'HWIO', 'NHWC'), feature_group_count=ch)
        return y.reshape(b, n, ch)
    return lax.conv_general_dilated(t, w[CONV_K // 2][:, None, :], (1,), 'SAME',
                                    dimension_numbers=('NWC', 'WIO', 'NWC'), feature_group_count=ch)


def to_chunks(t):
    b, n, h = t.shape[:3]
    t = t.reshape((b, n // CHUNK, CHUNK, h) + t.shape[3:])
    return jnp.moveaxis(jnp.moveaxis(t, 3, 1), 2, 0)


def from_chunks(t):
    t = jnp.moveaxis(jnp.moveaxis(t, 0, 2), 1, 3)
    return t.reshape((t.shape[0], -1) + t.shape[3:])


def linear_scan(q, k, v, g, beta, s0, want_out):
    b, _, h, dk = k.shape
    dv = v.shape[-1]
    kc, vc = to_chunks(k), to_chunks(v)
    gcum = jnp.cumsum(to_chunks(g), axis=-1)
    diff = gcum[..., :, None] - gcum[..., None, :]
    incl = jnp.tril(jnp.ones((CHUNK, CHUNK), bool))
    if beta is None:
        u, w = vc, None
    else:
        bc = to_chunks(beta)[..., None]
        kb = kc * bc
        strict = jnp.tril(jnp.ones((CHUNK, CHUNK), bool), -1)
        a = jnp.where(strict, jnp.einsum('...ik,...jk->...ij', kb, kc) * jnp.exp(jnp.where(strict, diff, 0.0)), 0.0)
        rhs = jnp.concatenate([vc * bc, kb * jnp.exp(gcum)[..., None]], axis=-1)
        sol = lax.linalg.triangular_solve(a + jnp.eye(CHUNK, dtype=a.dtype), rhs, left_side=True,
                                          lower=True, unit_diagonal=True)
        u, w = sol[..., :dv], sol[..., dv:]
    k_end = kc * jnp.exp(gcum[..., -1:] - gcum)[..., None]
    g_end = jnp.exp(gcum[..., -1])[..., None, None]
    if want_out:
        qc = to_chunks(q)
        q_dec = qc * jnp.exp(gcum)[..., None]
        a_qk = jnp.where(incl, jnp.einsum('...ik,...jk->...ij', qc, kc) * jnp.exp(jnp.where(incl, diff, 0.0)), 0.0)
    else:
        q_dec, a_qk = None, None
    if s0 is None:
        s0 = jnp.zeros((b, h, dk, dv), F32)

    def step(s, inp):
        qd, ke, uc, wc, aqk, ge = inp
        vn = uc if wc is None else uc - jnp.einsum('bhck,bhkv->bhcv', wc, s)
        s_new = ge * s + jnp.einsum('bhck,bhcv->bhkv', ke, vn)
        if qd is None:
            return s_new, None
        return s_new, jnp.einsum('bhck,bhkv->bhcv', qd, s) + jnp.einsum('bhij,bhjv->bhiv', aqk, vn)

    s_fin, o = lax.scan(step, s0, (q_dec, k_end, u, w, a_qk, g_end))
    return (from_chunks(o) if want_out else None), s_fin


def mlstm_scan(q, k, v, log_i, log_f, s0, want_out):
    b, _, h, dk = k.shape
    dv = v.shape[-1]
    if s0 is None:
        s0 = (jnp.zeros((b, h, dk, dv), F32), jnp.zeros((b, h, dk), F32), jnp.zeros((b, h), F32))
    incl = jnp.tril(jnp.ones((CHUNK, CHUNK), bool))

    def step(carry, inp):
        c_prev, n_prev, m_prev = carry
        qc, kc, vc, ic, fc = inp
        bcum = jnp.cumsum(fc, axis=-1)
        b_end = bcum[..., -1]
        a = b_end[..., None] - bcum + ic
        m_new = jnp.maximum(b_end + m_prev, a.max(-1))
        w_state = jnp.exp(a - m_new[..., None])
        decay = jnp.exp(b_end + m_prev - m_new)
        c_new = decay[..., None, None] * c_prev + jnp.einsum('bhck,bhcv->bhkv', kc * w_state[..., None], vc)
        n_new = decay[..., None] * n_prev + jnp.einsum('bhck,bhc->bhk', kc, w_state)
        carry_new = (c_new, n_new, m_new)
        if qc is None:
            return carry_new, None
        dlog = jnp.where(incl, bcum[..., :, None] - bcum[..., None, :] + ic[..., None, :], -jnp.inf)
        inter = bcum + m_prev[..., None]
        m_t = jnp.maximum(inter, dlog.max(-1))
        s = jnp.einsum('bhik,bhjk->bhij', qc, kc) * jnp.exp(dlog - m_t[..., None])
        w_inter = jnp.exp(inter - m_t)[..., None]
        num = jnp.einsum('bhij,bhjv->bhiv', s, vc) + w_inter * jnp.einsum('bhik,bhkv->bhiv', qc, c_prev)
        den = s.sum(-1, keepdims=True) + w_inter * jnp.einsum('bhik,bhk->bhi', qc, n_prev)[..., None]
        return carry_new, num / jnp.maximum(jnp.abs(den), jnp.exp(-m_t)[..., None])

    xs = (to_chunks(q) if want_out else None, to_chunks(k), to_chunks(v), to_chunks(log_i), to_chunks(log_f))
    s_fin, hs = lax.scan(step, s0, xs)
    return (from_chunks(hs) if want_out else None), s_fin


def two_pass(scan_fn, ctx_args, lat_args, ctx_out, reverse):
    if reverse:
        ctx_args = [flip_t(t) for t in ctx_args]
        lat_args = [flip_t(t) for t in lat_args]
    o_ctx, s_ctx = scan_fn(*ctx_args, None, ctx_out)
    o_lat, _ = scan_fn(*lat_args, s_ctx, True)
    if reverse:
        o_ctx, o_lat = flip_t(o_ctx), flip_t(o_lat)
    return o_ctx, o_lat


def retention_log_decay(direction):
    expo = 5.0 + 2.0 * jnp.arange(RET_HEADS, dtype=F32) + direction
    return jnp.log1p(-jnp.exp2(-expo))


GATE_COLS = 4 * N_HEADS
Q_SCALE = HEAD_DIM ** -0.5


def _gate_row(*vals):
    v = jnp.concatenate([jnp.ravel(x) for x in vals])
    return jnp.pad(v, (0, LANE - v.shape[0]))


def _pad_gate_cols(w):
    return jnp.pad(w, ((0, 0), (0, LANE - w.shape[1])))


def gdn_retention_mixer(hs_, mods, w_in, w_out, conv_w, a_log, dt_bias, gdn_gain, ret_gain, ln_g, ln_b, ctx_out):
    w = jnp.concatenate([w_in[:, :4 * GDN_W], w_in[:, 4 * GDN_W + GATE_COLS:],
                         _pad_gate_cols(w_in[:, 4 * GDN_W:4 * GDN_W + GATE_COLS])], axis=1).astype(BF16)
    gate_blk = (4 * GDN_W + 4 * RET_W) // LANE
    gp_gdn = jnp.stack([_gate_row(-jnp.exp(a_log)), _gate_row(dt_bias)])
    gp_ret = jnp.stack([_gate_row(retention_log_decay(0), retention_log_decay(1)), jnp.zeros((LANE,), F32)])
    wo = w_out.astype(BF16)
    projs, convs = [], []
    for (h, (sc, sh, _)), grid_w in zip(zip(hs_, mods), (hs_[0].shape[1], GRID_W)):
        p = in_projection(h, sc, sh, w)
        projs.append(p)
        convs.append(conv_prep(p, conv_w, 3 * GDN_W // LANE, grid_w, 2 * N_HEADS, Q_SCALE, 1.0))
    gdn_args = [((cv, 0), (cv, 1), (cv, 2), (p, gate_blk)) for p, cv in zip(projs, convs)]
    ret_args = [((p, 4), (p, 5), (p, 6), None) for p in projs]
    og = linear_scan_two_pass(gdn_args[0], gdn_args[1], gp_gdn, True)
    orr = linear_scan_two_pass(ret_args[0], ret_args[1], gp_ret, False, Q_SCALE)
    outs = []
    for i in range(2):
        if i == 0 and not ctx_out:
            outs.append(None)
            continue
        outs.append(_merge_call(_merge_even_kernel, "merge_even", [og[i], orr[i]], [(projs[i], 3), (projs[i], 7)],
                                [jnp.tile(gdn_gain, N_HEADS), ret_gain], [wo], hs_[i], mods[i][2], ln_g, ln_b))
    return outs


def mlstm_s5_mixer(hs_, mods, w_in, w_out, conv_w, gate_bias, mlstm_gain, lam_re, lam_im, log_dt,
                   b_re, b_im, c_re, c_im, d_skip, w_glu, b_glu, ln_g, ln_b, ctx_out):
    w = jnp.concatenate([w_in[:, :4 * MLSTM_W], _pad_gate_cols(w_in[:, 4 * MLSTM_W:4 * MLSTM_W + GATE_COLS])],
                        axis=1).astype(BF16)
    w_u = w_in[:, 4 * MLSTM_W + GATE_COLS:].astype(BF16)
    gate_blk = 4 * MLSTM_W // LANE
    gp = jnp.stack([_gate_row(gate_bias[0, 0], gate_bias[1, 0], gate_bias[0, 1], gate_bias[1, 1]),
                    jnp.zeros((LANE,), F32)])
    wo, wglu = w_out.astype(BF16), w_glu.astype(BF16)
    projs, us, args = [], [], []
    for (h, (sc, sh, _)), grid_w in zip(zip(hs_, mods), (hs_[0].shape[1], GRID_W)):
        p = in_projection(h, sc, sh, w)
        cv = conv_prep(p, conv_w, 2 * MLSTM_W // LANE, grid_w, 0, 1.0, Q_SCALE)
        projs.append(p)
        us.append(in_projection(h, sc, sh, w_u))
        args.append(((cv, 0), (cv, 1), (p, 2), (p, gate_blk)))
    om = mlstm_two_pass(args[0], args[1], gp)
    ys = s5_bidirectional(us[0], us[1], _s5_weights(lam_re, lam_im, log_dt, b_re, b_im, c_re, c_im))
    outs = []
    for i in range(2):
        if i == 0 and not ctx_out:
            outs.append(None)
            continue
        outs.append(_merge_call(_merge_odd_kernel, "merge_odd", [om[i]], [(ys[i], 0), (projs[i], 3), (us[i], 0)],
                                [mlstm_gain, d_skip, b_glu], [wglu, wo], hs_[i], mods[i][2], ln_g, ln_b))
    return outs


def kernel(x, c, ctx, c_ctx, w_mod, b_mod, ln1_g, ln1_b, ln2_g, ln2_b, w_router, w_gate, w_up, w_down,
           ev_w_in, ev_w_out, ev_conv, ev_a_log, ev_dt_bias, ev_gdn_norm, ev_ret_norm,
           od_w_in, od_w_out, od_conv, od_gate_bias, od_mlstm_norm, od_lam_re, od_lam_im, od_log_dt,
           od_b_re, od_b_im, od_c_re, od_c_im, od_d_skip, od_w_glu, od_b_glu):
    h_lat, h_ctx = x, ctx
    s_lat = jax.nn.silu(c)
    s_ctx = jax.nn.silu(c_ctx)
    for l in range(DEPTH):
        last = l == DEPTH - 1
        sh1, sc1, g1, sh2, sc2, g2 = jnp.split((s_lat @ w_mod[l] + b_mod[l])[:, None, :], 6, axis=-1)
        bc = lambda v: jnp.broadcast_to(v, (BATCH, 1, D_MODEL))
        csh1, csc1, cg1, csh2, csc2, cg2 = [bc(v) for v in jnp.split(s_ctx @ w_mod[l] + b_mod[l], 6, axis=-1)]
        streams = (h_ctx, h_lat)
        mods = ((csc1, csh1, cg1), (sc1, sh1, g1))
        if l % 2 == 0:
            e = l // 2
            h_ctx, h_lat = gdn_retention_mixer(streams, mods, ev_w_in[e], ev_w_out[e], ev_conv[e], ev_a_log[e],
                                               ev_dt_bias[e], ev_gdn_norm[e], ev_ret_norm[e], ln1_g[l], ln1_b[l],
                                               not last)
        else:
            o = l // 2
            h_ctx, h_lat = mlstm_s5_mixer(streams, mods, od_w_in[o], od_w_out[o], od_conv[o], od_gate_bias[o],
                                          od_mlstm_norm[o], od_lam_re[o], od_lam_im[o], od_log_dt[o],
                                          od_b_re[o], od_b_im[o], od_c_re[o], od_c_im[o], od_d_skip[o],
                                          od_w_glu[o], od_b_glu[o], ln1_g[l], ln1_b[l], not last)
        experts = (w_router[l], w_gate[l].astype(BF16), w_up[l].astype(BF16), w_down[l].astype(BF16))
        h_lat = moe_block(h_lat, sc2, sh2, g2, ln2_g[l], ln2_b[l], *experts)
        if not last:
            h_ctx = moe_block(h_ctx, csc2, csh2, cg2, ln2_g[l], ln2_b[l], *experts)
    return h_lat
```

```python
import functools
import math

import jax
import jax.numpy as jnp
import numpy as np
from jax import lax
from jax.experimental import pallas as pl
from jax.experimental.pallas import tpu as pltpu

D_MODEL = 1024
BATCH = 4
SEQ = 4096
DEPTH = 4
GRID_W = 64
CTX_LEN = 256
CHUNK = 64
CONV_K = 3
HEAD_DIM = D_MODEL // 8
GDN_HEADS = 4
RET_HEADS = 4
MLSTM_HEADS = 4
GDN_W = GDN_HEADS * HEAD_DIM
RET_W = RET_HEADS * HEAD_DIM
MLSTM_W = MLSTM_HEADS * HEAD_DIM
S5_CH = D_MODEL // 2
S5_GROUP = 16
S5_GROUPS = S5_CH // S5_GROUP
S5_STATE = 64
N_EXPERTS = 16
EXPERT_FF = 2 * D_MODEL
CAPACITY_FACTOR = 2
ALPHA = (2 * DEPTH) ** 0.25
EPS = 1e-5
EVEN_COLS = (GDN_W, GDN_W, GDN_W, GDN_W, GDN_HEADS, GDN_HEADS, GDN_HEADS, GDN_HEADS,
             RET_W, RET_W, RET_W, RET_W)
ODD_COLS = (MLSTM_W, MLSTM_W, MLSTM_W, MLSTM_W, MLSTM_HEADS, MLSTM_HEADS, MLSTM_HEADS, MLSTM_HEADS, S5_CH)
F32 = jnp.float32
BF16 = jnp.bfloat16

LANE = 128


def _mm_kernel(a_ref, b_ref, o_ref):
    o_ref[...] = jnp.dot(a_ref[...].astype(BF16), b_ref[...].astype(BF16), preferred_element_type=F32)


def pmatmul(a, b, tm=512, tn=512):
    m, k = a.shape
    n = b.shape[1]
    n_pad = -n % LANE
    if n_pad:
        b = jnp.pad(b, ((0, 0), (0, n_pad)))
    np_ = n + n_pad
    tn = min(tn, np_)
    while np_ % tn:
        tn -= LANE
    tm = min(tm, m)
    assert m % tm == 0
    out = pl.pallas_call(
        _mm_kernel,
        grid=(m // tm, np_ // tn),
        in_specs=[pl.BlockSpec((tm, k), lambda i, j: (i, 0)),
                  pl.BlockSpec((k, tn), lambda i, j: (0, j))],
        out_specs=pl.BlockSpec((tm, tn), lambda i, j: (i, j)),
        out_shape=jax.ShapeDtypeStruct((m, np_), F32),
        name="pmatmul",
    )(a, b)
    return out[:, :n] if n_pad else out


def mm(x, w):
    lead = x.shape[:-1]
    return pmatmul(x.reshape(-1, x.shape[-1]), w).reshape(lead + (w.shape[-1],))


S5_L = 16
S5_NB = S5_CH // LANE
S5_GPB = LANE // S5_GROUP
S5_SW = S5_GPB * S5_STATE
VMEM_LIMIT = 56 * 1024 * 1024


def _s5_weights(lam_re, lam_im, log_dt, b_re, b_im, c_re, c_im):
    L, G = S5_L, S5_GROUPS
    hp = lax.Precision.HIGHEST
    taus = jnp.arange(L + 1, dtype=F32)[:, None, None]
    ks, ws, cas, ds = [], [], [], []
    for d in range(2):
        lr = jnp.minimum(lam_re[d], -1e-4)
        li = lam_im[d]
        dt = jnp.exp(log_dt[d])[:, None]
        mag = jnp.exp(lr * dt)
        ab_re, ab_im = mag * jnp.cos(li * dt), mag * jnp.sin(li * dt)
        xr, xi, den = ab_re - 1.0, ab_im, lr * lr + li * li
        f_re = (xr * lr + xi * li) / den
        f_im = (xi * lr - xr * li) / den
        bb_re = f_re[..., None] * b_re - f_im[..., None] * b_im
        bb_im = f_re[..., None] * b_im + f_im[..., None] * b_re
        pmag = jnp.exp(taus * (lr * dt))
        ar, ai = pmag * jnp.cos(taus * (li * dt)), pmag * jnp.sin(taus * (li * dt))
        wr = ar[..., None] * bb_re - ai[..., None] * bb_im
        wi = ar[..., None] * bb_im + ai[..., None] * bb_re
        k = (jnp.einsum('gop,tgpi->tgio', c_re, wr, precision=hp)
             - jnp.einsum('gop,tgpi->tgio', c_im, wi, precision=hp))
        car = c_re[None] * ar[:, :, None, :] - c_im[None] * ai[:, :, None, :]
        cai = c_re[None] * ai[:, :, None, :] + c_im[None] * ar[:, :, None, :]
        ks.append(k)
        ws.append((wr, wi))
        cas.append((car, cai))
        ds.append((ar[L], ai[L]))

    eye = jnp.eye(S5_GPB, dtype=F32)

    def blockdiag(t, in_axis_first):
        lead = t.shape[:-3]
        x, y = t.shape[-2:]
        t = t.reshape(lead + (S5_NB, S5_GPB, x, y))
        t = jnp.einsum('...jaxy,ab->...jaxby', t, eye)
        return t.reshape(lead + (S5_NB, S5_GPB * x, S5_GPB * y))

    kf, kb = ks
    kc = jnp.concatenate([kb[1:L][::-1], (kf[0] + kb[0])[None], kf[1:L]], axis=0)
    kc = blockdiag(kc, True)
    idx = (jnp.arange(L)[None, :] - jnp.arange(L)[:, None]) + (L - 1)
    tz = kc[idx]
    tz = tz.transpose(2, 0, 3, 1, 4).reshape(S5_NB, L * LANE, L * LANE)

    pbs, cab = [], []
    for d in range(2):
        wr, wi = ws[d]
        order = jnp.arange(L - 1, -1, -1) if d == 0 else jnp.arange(L)
        cols = []
        for w in (wr, wi):
            m = jnp.swapaxes(w[order], -1, -2)
            cols.append(blockdiag(m, True))
        pbs.append(jnp.concatenate(cols, axis=-1))
        car, cai = cas[d]
        order = jnp.arange(1, L + 1) if d == 0 else jnp.arange(L, 0, -1)
        rows = []
        for m in (car[order], -cai[order]):
            m = jnp.swapaxes(m, -1, -2)
            rows.append(blockdiag(m, True))
        cab.append(jnp.concatenate(rows, axis=-2))
    pb = jnp.concatenate(pbs, axis=-1)
    pb = pb.transpose(1, 0, 2, 3).reshape(S5_NB, L * LANE, 4 * S5_SW)
    ca = jnp.concatenate(cab, axis=-2)
    ca = ca.transpose(1, 2, 0, 3).reshape(S5_NB, 4 * S5_SW, L * LANE)
    dr = jnp.stack([ds[0][0], ds[1][0]], 0).reshape(2, S5_NB, 1, S5_SW).transpose(1, 0, 2, 3).reshape(2 * S5_NB, 1, S5_SW)
    di = jnp.stack([ds[0][1], ds[1][1]], 0).reshape(2, S5_NB, 1, S5_SW).transpose(1, 0, 2, 3).reshape(2 * S5_NB, 1, S5_SW)
    return tz.astype(BF16), pb.astype(BF16), ca.astype(BF16), dr, di


def _s5_p_kernel(u_ref, pb_ref, p_ref, *, nb, nc):
    res = jnp.dot(u_ref[...], pb_ref[...], preferred_element_type=F32)
    for b in range(nb):
        p_ref[:, b * 2 * S5_SW:(b + 1) * 2 * S5_SW] = res[b * nc:(b + 1) * nc]


def _s5_scan_kernel(p_ref, dr_ref, di_ref, s_ref, *, n_ctx, n_lat):
    rev = pl.program_id(0) % 2
    dr = dr_ref[...]
    di = di_ref[...]
    nbatch = p_ref.shape[1]

    def phase(base, n, carry):
        def body(step, carry):
            sr, si = carry
            row = base + jnp.where(rev == 0, step, n - 1 - step)
            s_ref[row, :, :S5_SW] = sr
            s_ref[row, :, S5_SW:] = si
            p = p_ref[row]
            nr = dr * sr - di * si + p[:, :S5_SW]
            ni = dr * si + di * sr + p[:, S5_SW:]
            return nr, ni
        return lax.fori_loop(0, n, body, carry)

    zero = jnp.zeros((nbatch, S5_SW), F32)
    carry = phase(0, n_ctx, (zero, zero))
    phase(n_ctx, n_lat, carry)


def _s5_y_kernel(u_ref, tz_ref, sf_ref, sb_ref, ca_ref, y_ref):
    y = jnp.dot(u_ref[...], tz_ref[...], preferred_element_type=F32)
    y += jnp.dot(sf_ref[...].astype(BF16), ca_ref[:2 * S5_SW, :], preferred_element_type=F32)
    y += jnp.dot(sb_ref[...].astype(BF16), ca_ref[2 * S5_SW:, :], preferred_element_type=F32)
    y_ref[...] = y


def s5_bidirectional(u_ctx, u_lat, weights):
    tz, pb, ca, dr, di = weights
    L = S5_L
    nb, t_ctx, _ = u_ctx.shape
    t_lat = u_lat.shape[1]
    assert t_ctx % L == 0 and t_lat % L == 0
    n_ctx, n_lat = t_ctx // L, t_lat // L
    nc = n_ctx + n_lat
    kw = L * LANE
    sw2 = 2 * S5_SW
    u = jnp.concatenate([u_ctx, u_lat], axis=1)
    ub = u.reshape(nb * nc, L, S5_NB, LANE).transpose(2, 0, 1, 3).reshape(S5_NB, nb * nc, kw).astype(BF16)

    p = pl.pallas_call(
        functools.partial(_s5_p_kernel, nb=nb, nc=nc),
        grid=(S5_NB, 2),
        in_specs=[pl.BlockSpec((None, nb * nc, kw), lambda j, d: (j, 0, 0)),
                  pl.BlockSpec((None, kw, sw2), lambda j, d: (j, 0, d))],
        out_specs=pl.BlockSpec((nc, nb * sw2), lambda j, d: (0, j * 2 + d)),
        out_shape=jax.ShapeDtypeStruct((nc, S5_NB * 2 * nb * sw2), F32),
        compiler_params=pltpu.CompilerParams(vmem_limit_bytes=VMEM_LIMIT),
        name="s5_chunk_inputs",
    )(ub, pb)

    p4 = p.reshape(nc, S5_NB * 2, nb, sw2)
    s4 = pl.pallas_call(
        functools.partial(_s5_scan_kernel, n_ctx=n_ctx, n_lat=n_lat),
        grid=(S5_NB * 2,),
        in_specs=[pl.BlockSpec((nc, None, nb, sw2), lambda g: (0, g, 0, 0)),
                  pl.BlockSpec((None, 1, S5_SW), lambda g: (g, 0, 0)),
                  pl.BlockSpec((None, 1, S5_SW), lambda g: (g, 0, 0))],
        out_specs=pl.BlockSpec((nc, None, nb, sw2), lambda g: (0, g, 0, 0)),
        out_shape=jax.ShapeDtypeStruct(p4.shape, F32),
        compiler_params=pltpu.CompilerParams(vmem_limit_bytes=VMEM_LIMIT),
        name="s5_state_scan",
    )(p4, dr, di)

    s2 = s4.reshape(nc, S5_NB * 2 * nb * sw2)
    yb = pl.pallas_call(
        _s5_y_kernel,
        grid=(S5_NB, nb),
        in_specs=[pl.BlockSpec((None, nc, kw), lambda j, b: (j, b, 0)),
                  pl.BlockSpec((None, kw, kw), lambda j, b: (j, 0, 0)),
                  pl.BlockSpec((nc, sw2), lambda j, b: (0, (j * 2) * nb + b)),
                  pl.BlockSpec((nc, sw2), lambda j, b: (0, (j * 2 + 1) * nb + b)),
                  pl.BlockSpec((None, 2 * sw2, kw), lambda j, b: (j, 0, 0))],
        out_specs=pl.BlockSpec((None, nc, kw), lambda j, b: (j, b, 0)),
        out_shape=jax.ShapeDtypeStruct((S5_NB, nb * nc, kw), F32),
        compiler_params=pltpu.CompilerParams(vmem_limit_bytes=VMEM_LIMIT),
        name="s5_output",
    )(ub, tz, s2, s2, ca)
    y = yb.reshape(S5_NB, nb, nc, L, LANE).transpose(1, 2, 3, 0, 4).reshape(nb, nc * L, S5_CH)
    return y[:, :t_ctx], y[:, t_ctx:]


ROUTE_TN = 512
MOE_TN = 1024
SEL_ROWS = 128
FF_SPLIT = 2
AFF_PARTS = 3


def _route_kernel(h_ref, sc_ref, sh_ref, wr_ref, hm_ref, afft_ref, asp_ref):
    hm = h_ref[...] * (1.0 + sc_ref[...]) + sh_ref[...]
    hm_ref[...] = hm.astype(BF16)
    logits = jnp.dot(hm, wr_ref[...], precision=lax.Precision.HIGHEST, preferred_element_type=F32)
    lane = lax.broadcasted_iota(jnp.int32, logits.shape, 1)
    logits = jnp.where(lane < N_EXPERTS, logits, -jnp.inf)
    ex = jnp.exp(logits - jnp.max(logits, axis=1, keepdims=True))
    aff = ex / jnp.sum(ex, axis=1, keepdims=True)
    afft_ref[...] = aff.T[:N_EXPERTS, :]
    hi = aff.astype(BF16).astype(F32)
    mid = (aff - hi).astype(BF16).astype(F32)
    lo = (aff - hi - mid).astype(BF16).astype(F32)
    asp = hi + pltpu.roll(mid, N_EXPERTS, 1) + pltpu.roll(lo, 2 * N_EXPERTS, 1)
    asp_ref[...] = asp.astype(BF16)


def _select_kernel(aff_ref, pos_ref, *, nblk_log2, cap):
    a = aff_ref[...]
    r = a.shape[0]
    bits = pltpu.bitcast(a, jnp.int32)
    ri = lax.broadcasted_iota(jnp.int32, (r, r), 0)
    rj = lax.broadcasted_iota(jnp.int32, (r, r), 1)
    same = (ri >> nblk_log2) == (rj >> nblk_log2)
    gm = jnp.where(same, 1.0, 0.0).astype(BF16)
    lm = jnp.where(same & (rj < ri), 1.0, 0.0).astype(BF16)
    li = lax.broadcasted_iota(jnp.int32, (LANE, LANE), 0)
    lj = lax.broadcasted_iota(jnp.int32, (LANE, LANE), 1)
    um = jnp.where(li <= lj, 1.0, 0.0).astype(BF16)

    def group_count(mask):
        rc = jnp.sum(jnp.where(mask, 1.0, 0.0), axis=1, keepdims=True)
        gc = jnp.dot(gm, jnp.broadcast_to(rc, (r, LANE)).astype(BF16), preferred_element_type=F32)
        return gc[:, :1]

    def bisect(i, thr):
        cand = thr | jnp.left_shift(1, 29 - i)
        return jnp.where(group_count(bits >= cand) >= cap, cand, thr)

    thr = lax.fori_loop(0, 30, bisect, jnp.zeros((r, 1), jnp.int32))

    def prefix(mask):
        x = jnp.where(mask, 1.0, 0.0)
        inc = jnp.dot(x.astype(BF16), um, preferred_element_type=F32)
        tot = jnp.broadcast_to(inc[:, LANE - 1:LANE], (r, LANE)).astype(BF16)
        return inc - x + jnp.dot(lm, tot, preferred_element_type=F32)

    gt = bits > thr
    eq = bits == thr
    need = cap - group_count(gt)
    sel = gt | (eq & (prefix(eq) < need))
    pos_ref[...] = jnp.where(sel, prefix(sel), -1.0).astype(jnp.int32)


def _onehot(pos_ref, cap):
    rows = lax.broadcasted_iota(jnp.int32, (cap, LANE), 0)
    blocks = [jnp.where(pos_ref[k:k + 1, :] == rows, 1.0, 0.0).astype(BF16) for k in range(pos_ref.shape[0])]
    return jnp.concatenate(blocks, axis=1)


def _ffn_kernel(pos_ref, hb_ref, asp_ref, wg_ref, wu_ref, wd_ref, ys_ref, xs_acc, g_acc, *, cap):
    e = pl.program_id(0)
    kt = pl.program_id(2)

    @pl.when(kt == 0)
    def _():
        xs_acc[...] = jnp.zeros_like(xs_acc)
        g_acc[...] = jnp.zeros_like(g_acc)

    oh = _onehot(pos_ref, cap)
    xs_acc[...] += jnp.dot(oh, hb_ref[...], preferred_element_type=F32)
    g_acc[...] += jnp.dot(oh, asp_ref[...], preferred_element_type=F32)

    @pl.when(kt == pl.num_programs(2) - 1)
    def _():
        xs = xs_acc[...].astype(BF16)
        g = g_acc[...]
        lane = lax.broadcasted_iota(jnp.int32, g.shape, 1)
        gate = jnp.sum(jnp.where((lane & (N_EXPERTS - 1)) == e, g, 0.0), axis=1, keepdims=True)
        fw = EXPERT_FF // FF_SPLIT
        y = jnp.zeros((cap, D_MODEL), F32)
        for f in range(FF_SPLIT):
            hg = jnp.dot(xs, wg_ref[:, f * fw:(f + 1) * fw], preferred_element_type=F32)
            hu = jnp.dot(xs, wu_ref[:, f * fw:(f + 1) * fw], preferred_element_type=F32)
            hid = (hg * jax.nn.sigmoid(hg)) * hu
            y += jnp.dot(hid.astype(BF16), wd_ref[f * fw:(f + 1) * fw, :], preferred_element_type=F32)
        ys_ref[...] = (y * gate).astype(BF16)


def _ln_rows(z, g, b):
    mu = jnp.mean(z, axis=-1, keepdims=True)
    zc = z - mu
    var = jnp.mean(zc * zc, axis=-1, keepdims=True)
    return zc * lax.rsqrt(var + EPS) * g + b


def _combine_kernel(pos_ref, ys_ref, h_ref, g2_ref, lng_ref, lnb_ref, o_ref, acc, *, cap):
    e = pl.program_id(2)

    @pl.when(e == 0)
    def _():
        acc[...] = jnp.zeros_like(acc)

    oh = _onehot(pos_ref, cap)
    acc[...] += lax.dot_general(oh, ys_ref[...], (((0,), (0,)), ((), ())), preferred_element_type=F32)

    @pl.when(e == pl.num_programs(2) - 1)
    def _():
        z = ALPHA * h_ref[...] + g2_ref[...] * acc[...]
        o_ref[...] = _ln_rows(z, lng_ref[...], lnb_ref[...])


def moe_block(h, sc, sh, g2, ln_g, ln_b, w_router, wg, wu, wd):
    nb, n, dm = h.shape
    cap = CAPACITY_FACTOR * n // N_EXPERTS
    nblk = n // LANE
    assert n % LANE == 0 and nblk & (nblk - 1) == 0
    tn_r = min(ROUTE_TN, n)
    tn = min(MOE_TN, n)
    wr = jnp.pad(w_router, ((0, 0), (0, LANE - N_EXPERTS)))
    row = lambda v: v.reshape(1, dm)

    hm, afft, asp = pl.pallas_call(
        _route_kernel,
        grid=(nb, n // tn_r),
        in_specs=[pl.BlockSpec((None, tn_r, dm), lambda b, t: (b, t, 0)),
                  pl.BlockSpec((None, 1, dm), lambda b, t: (b, 0, 0)),
                  pl.BlockSpec((None, 1, dm), lambda b, t: (b, 0, 0)),
                  pl.BlockSpec((dm, LANE), lambda b, t: (0, 0))],
        out_specs=[pl.BlockSpec((None, tn_r, dm), lambda b, t: (b, t, 0)),
                   pl.BlockSpec((None, N_EXPERTS, tn_r), lambda b, t: (b, 0, t)),
                   pl.BlockSpec((None, tn_r, LANE), lambda b, t: (b, t, 0))],
        out_shape=[jax.ShapeDtypeStruct((nb, n, dm), BF16),
                   jax.ShapeDtypeStruct((nb, N_EXPERTS, n), F32),
                   jax.ShapeDtypeStruct((nb, n, LANE), BF16)],
        name="moe_route",
    )(h, sc, sh, wr)

    rows_total = nb * N_EXPERTS * nblk
    rb = max(SEL_ROWS, N_EXPERTS * nblk)
    assert rows_total % rb == 0
    pos = pl.pallas_call(
        functools.partial(_select_kernel, nblk_log2=nblk.bit_length() - 1, cap=cap),
        grid=(rows_total // rb,),
        in_specs=[pl.BlockSpec((rb, LANE), lambda i: (i, 0))],
        out_specs=pl.BlockSpec((rb, LANE), lambda i: (i, 0)),
        out_shape=jax.ShapeDtypeStruct((rows_total, LANE), jnp.int32),
        name="moe_select",
    )(afft.reshape(rows_total, LANE))
    pos = pos.reshape(nb, N_EXPERTS, nblk, LANE)

    tb = tn // LANE
    ys = pl.pallas_call(
        functools.partial(_ffn_kernel, cap=cap),
        grid=(N_EXPERTS, nb, n // tn),
        in_specs=[pl.BlockSpec((None, None, tb, LANE), lambda e, b, k: (b, e, k, 0)),
                  pl.BlockSpec((None, tn, dm), lambda e, b, k: (b, k, 0)),
                  pl.BlockSpec((None, tn, LANE), lambda e, b, k: (b, k, 0)),
                  pl.BlockSpec((None, dm, EXPERT_FF), lambda e, b, k: (e, 0, 0)),
                  pl.BlockSpec((None, dm, EXPERT_FF), lambda e, b, k: (e, 0, 0)),
                  pl.BlockSpec((None, EXPERT_FF, dm), lambda e, b, k: (e, 0, 0))],
        out_specs=pl.BlockSpec((None, None, cap, dm), lambda e, b, k: (b, e, 0, 0)),
        out_shape=jax.ShapeDtypeStruct((nb, N_EXPERTS, cap, dm), BF16),
        scratch_shapes=[pltpu.VMEM((cap, dm), F32), pltpu.VMEM((cap, LANE), F32)],
        compiler_params=pltpu.CompilerParams(vmem_limit_bytes=VMEM_LIMIT),
        name="moe_ffn",
    )(pos, hm, asp, wg, wu, wd)

    return pl.pallas_call(
        functools.partial(_combine_kernel, cap=cap),
        grid=(nb, n // tn, N_EXPERTS),
        in_specs=[pl.BlockSpec((None, None, tb, LANE), lambda b, t, e: (b, e, t, 0)),
                  pl.BlockSpec((None, None, cap, dm), lambda b, t, e: (b, e, 0, 0)),
                  pl.BlockSpec((None, tn, dm), lambda b, t, e: (b, t, 0)),
                  pl.BlockSpec((None, 1, dm), lambda b, t, e: (b, 0, 0)),
                  pl.BlockSpec((1, dm), lambda b, t, e: (0, 0)),
                  pl.BlockSpec((1, dm), lambda b, t, e: (0, 0))],
        out_specs=pl.BlockSpec((None, tn, dm), lambda b, t, e: (b, t, 0)),
        out_shape=jax.ShapeDtypeStruct((nb, n, dm), F32),
        scratch_shapes=[pltpu.VMEM((tn, dm), F32)],
        compiler_params=pltpu.CompilerParams(vmem_limit_bytes=VMEM_LIMIT),
        name="moe_combine",
    )(pos, ys, h, g2, row(ln_g), row(ln_b))


N_HEADS = 4
STACK = N_HEADS * CHUNK
CHUNK_LOG2 = CHUNK.bit_length() - 1
SOLVE_LEVELS = CHUNK_LOG2


def _stack_heads(x):
    return jnp.concatenate([x[:, h * HEAD_DIM:(h + 1) * HEAD_DIM] for h in range(N_HEADS)], axis=0)


def _stack_cols(cols, width):
    return jnp.concatenate([jnp.broadcast_to(c, (CHUNK, width)) for c in cols], axis=0)


def _mxu(a, b, dims, hp):
    dg = lambda x, y: lax.dot_general(x, y, (dims, ((), ())), preferred_element_type=F32)
    a_hi, b_hi = a.astype(BF16), b.astype(BF16)
    if not hp:
        return dg(a_hi, b_hi)
    a_lo = (a - a_hi.astype(F32)).astype(BF16)
    b_lo = (b - b_hi.astype(F32)).astype(BF16)
    return dg(a_hi, b_hi) + (dg(a_lo, b_hi) + dg(a_hi, b_lo))


def _dot_nt(a, b, hp=False):
    return _mxu(a, b, ((1,), (1,)), hp)


def _dot_tn(a, b, hp=False):
    return _mxu(a, b, ((0,), (0,)), hp)


def _dot(a, b, hp=False):
    return _mxu(a, b, ((1,), (0,)), hp)


def _chunk_masks(rev):
    ri = lax.broadcasted_iota(jnp.int32, (STACK, STACK), 0)
    ci = lax.broadcasted_iota(jnp.int32, (STACK, STACK), 1)
    same = (ri >> CHUNK_LOG2) == (ci >> CHUNK_LOG2)
    ahead = jnp.where(rev, ci - ri, ri - ci)
    return same & (ahead >= 0), same & (ahead > 0)


def _scan_cumsum(gt, rev):
    ii = lax.broadcasted_iota(jnp.int32, (CHUNK, CHUNK), 0)
    jj = lax.broadcasted_iota(jnp.int32, (CHUNK, CHUNK), 1)
    tri = jnp.where(jnp.where(rev, jj - ii, ii - jj) >= 0, 1.0, 0.0)
    return jnp.dot(tri, gt, precision=lax.Precision.HIGHEST, preferred_element_type=F32)


def _unit_triangular_inverse(a):
    ri = lax.broadcasted_iota(jnp.int32, a.shape, 0)
    ci = lax.broadcasted_iota(jnp.int32, a.shape, 1)
    joins = lambda lvl: ((ri >> (lvl + 1)) == (ci >> (lvl + 1))) & ((ri >> lvl) != (ci >> lvl))
    t = jnp.where(ri == ci, 1.0, 0.0) - jnp.where(joins(0), a, 0.0)
    for lvl in range(1, CHUNK_LOG2):
        m = _dot(jnp.where(joins(lvl), a, 0.0), t)
        yield
        t = t - _dot(t, m)
        yield
    return t


def _interleave(chains):
    for _ in zip(*chains):
        pass


def _linear_scan_kernel(*refs, has_beta, k_scale):
    if has_beta:
        q_ref, k_ref, v_ref, gt_ref, gp_ref, s0_ref, o_ref, sfin_ref, s_scr = refs
    else:
        q_ref, k_ref, v_ref, gp_ref, s0_ref, o_ref, sfin_ref, s_scr = refs
    rev = pl.program_id(0) == 1
    n = pl.program_id(2)

    @pl.when(n == 0)
    def _():
        s_scr[...] = s0_ref[...]

    chains = []
    for i in range(q_ref.shape[0]):
        if has_beta:
            gt = _gdn_gates(gt_ref[i], gp_ref[...])
        else:
            gt = jnp.broadcast_to(gp_ref[0:1, :], (CHUNK, LANE))
        chains.append(_linear_chunk(q_ref.at[i], k_ref.at[i], v_ref.at[i], gt, o_ref.at[i], s_scr.at[i], rev,
                                    has_beta, k_scale))
    _interleave(chains)

    @pl.when(n == pl.num_programs(2) - 1)
    def _():
        sfin_ref[...] = s_scr[...]


def _softplus(x):
    return jnp.maximum(x, 0.0) + jnp.log1p(jnp.exp(-jnp.abs(x)))


def _gdn_gates(raw, gp):
    lane = lax.broadcasted_iota(jnp.int32, raw.shape, 1)
    return jnp.where(lane < 2 * N_HEADS, gp[0:1, :] * _softplus(raw + gp[1:2, :]), jax.nn.sigmoid(raw))


def _mlstm_gates(raw, gp):
    lane = lax.broadcasted_iota(jnp.int32, raw.shape, 1)
    x = raw + gp[0:1, :]
    return jnp.where(lane < 2 * N_HEADS, x, -_softplus(-x))


def _linear_chunk(q_ref, k_ref, v_ref, gt, o_ref, s_scr, rev, has_beta, k_scale):
    cum = _scan_cumsum(gt, rev)
    gcols, bcols, gtots = [], [], []
    for h in range(N_HEADS):
        gc = jnp.where(rev, cum[:, N_HEADS + h:N_HEADS + h + 1], cum[:, h:h + 1])
        gcols.append(gc)
        gtots.append(jnp.where(rev, gc[0:1], gc[CHUNK - 1:CHUNK]))
        bcols.append(jnp.where(rev, gt[:, 3 * N_HEADS + h:3 * N_HEADS + h + 1], gt[:, 2 * N_HEADS + h:2 * N_HEADS + h + 1]))
    cb = _stack_cols(gcols, STACK)
    diff = cb - cb.T
    incl, strict = _chunk_masks(rev)
    dec = jnp.exp(jnp.where(incl, diff, 0.0))
    gcb = cb[:, :HEAD_DIM]
    q_st, k_st, v_st = _stack_heads(q_ref[...]), _stack_heads(k_ref[...]) * k_scale, _stack_heads(v_ref[...])
    a_qk = _dot_nt(q_st, k_st) * jnp.where(incl, dec, 0.0)
    if has_beta:
        beta = _stack_cols(bcols, HEAD_DIM)
        kb = k_st * beta
        a = _dot_nt(kb, k_st) * jnp.where(strict, dec, 0.0)
        yield
        t_inv = yield from _unit_triangular_inverse(a)
        x = _dot(t_inv, jnp.concatenate([v_st * beta, kb * jnp.exp(gcb)], axis=1))
        u_st, w_st = x[:, :HEAD_DIM], x[:, HEAD_DIM:]
    else:
        u_st, w_st = v_st, None
    gtot = _stack_cols(gtots, HEAD_DIM)
    k_end = k_st * jnp.exp(gtot - gcb)
    q_dec = q_st * jnp.exp(gcb)
    hs = lambda t, h: t[h * CHUNK:(h + 1) * CHUNK]
    states = [s_scr[h] for h in range(N_HEADS)]
    q_s = [_dot(hs(q_dec, h), states[h]) for h in range(N_HEADS)]
    yield
    if has_beta:
        vn = jnp.concatenate([hs(u_st, h) - _dot(hs(w_st, h), states[h]) for h in range(N_HEADS)], axis=0)
    else:
        vn = u_st
    yield
    o_st = _dot(a_qk, vn)
    for h in range(N_HEADS):
        s_scr[h] = jnp.exp(gtots[h]) * states[h] + _dot_tn(hs(k_end, h), hs(vn, h))
    yield
    for h in range(N_HEADS):
        o_ref[:, h * HEAD_DIM:(h + 1) * HEAD_DIM] = hs(o_st, h) + q_s[h]
    yield


SCAN_NS = 4


HEADS_W = N_HEADS * HEAD_DIM


def _scan_specs(srcs, gates, nb, nchunk):
    ns = SCAN_NS
    assert nb % ns == 0
    cidx = lambda d, n: n + d * (nchunk - 1 - 2 * n)
    specs = [pl.BlockSpec((ns, CHUNK, HEADS_W), lambda d, b, n, c=c: (b, cidx(d, n), c)) for _, c in srcs]
    if gates is not None:
        specs.append(pl.BlockSpec((ns, CHUNK, LANE), lambda d, b, n, c=gates[1]: (b, cidx(d, n), c)))
    specs.append(pl.BlockSpec((2, LANE), lambda d, b, n: (0, 0)))
    ospec = pl.BlockSpec((None, ns, CHUNK, HEADS_W), lambda d, b, n: (d, b, cidx(d, n), 0))
    return specs, ospec


def linear_scan_bidir(q, k, v, gates, gp, s0, has_beta, k_scale=1.0):
    nb, t, _ = q[0].shape
    nchunk = t // CHUNK
    assert t % CHUNK == 0
    ns = SCAN_NS
    specs, ospec = _scan_specs((q, k, v), gates if has_beta else None, nb, nchunk)
    sspec = pl.BlockSpec((None, ns, N_HEADS, HEAD_DIM, HEAD_DIM), lambda d, b, n: (d, b, 0, 0, 0))
    args = [q[0], k[0], v[0]] + ([gates[0]] if has_beta else []) + [gp, s0]
    return pl.pallas_call(
        functools.partial(_linear_scan_kernel, has_beta=has_beta, k_scale=k_scale),
        grid=(2, nb // ns, nchunk),
        in_specs=specs + [sspec],
        out_specs=[ospec, sspec],
        out_shape=[jax.ShapeDtypeStruct((2, nb, t, HEADS_W), F32),
                   jax.ShapeDtypeStruct((2, nb, N_HEADS, HEAD_DIM, HEAD_DIM), F32)],
        scratch_shapes=[pltpu.VMEM((ns, N_HEADS, HEAD_DIM, HEAD_DIM), F32)],
        name="gdn_scan" if has_beta else "retention_scan",
    )(*args)


def linear_scan_two_pass(ctx_args, lat_args, gp, has_beta, k_scale=1.0):
    nb = ctx_args[0][0].shape[0]
    zero = jnp.zeros((2, nb, N_HEADS, HEAD_DIM, HEAD_DIM), F32)
    o_ctx, s_ctx = linear_scan_bidir(*ctx_args, gp, zero, has_beta, k_scale)
    o_lat, _ = linear_scan_bidir(*lat_args, gp, s_ctx, has_beta, k_scale)
    return o_ctx, o_lat


MLSTM_HP = True


def _mlstm_scan_kernel(q_ref, k_ref, v_ref, gt_ref, gp_ref, c0_ref, n0_ref, m0_ref, o_ref, cfin_ref, nfin_ref,
                       mfin_ref, c_scr, n_scr, m_scr):
    rev = pl.program_id(0) == 1
    step = pl.program_id(2)

    @pl.when(step == 0)
    def _():
        c_scr[...] = c0_ref[...]
        n_scr[...] = n0_ref[...]
        m_scr[...] = m0_ref[...]

    _interleave([_mlstm_chunk(q_ref.at[i], k_ref.at[i], v_ref.at[i], _mlstm_gates(gt_ref[i], gp_ref[...]),
                              o_ref.at[i], c_scr.at[i], n_scr.at[i], m_scr.at[i], rev)
                 for i in range(q_ref.shape[0])])

    @pl.when(step == pl.num_programs(2) - 1)
    def _():
        cfin_ref[...] = c_scr[...]
        nfin_ref[...] = n_scr[...]
        mfin_ref[...] = m_scr[...]


def _mlstm_chunk(q_ref, k_ref, v_ref, gt, o_ref, c_scr, n_scr, m_scr, rev):
    cum = _scan_cumsum(gt, rev)
    pick = lambda t, c: jnp.where(rev, t[:, N_HEADS + c:N_HEADS + c + 1], t[:, c:c + 1])
    q_st, k_st, v_st = _stack_heads(q_ref[...]), _stack_heads(k_ref[...]), _stack_heads(v_ref[...])
    hs = lambda t, h: t[h * CHUNK:(h + 1) * CHUNK]
    qk = _dot_nt(q_st, k_st, MLSTM_HP)
    yield
    bcums, srcs, inters, qcs, qns = [], [], [], [], []
    for h in range(N_HEADS):
        ic = pick(gt, h)
        bcum = pick(cum, 2 * N_HEADS + h)
        b_end = jnp.where(rev, bcum[0:1], bcum[CHUNK - 1:CHUNK])
        c_prev, n_prev, m_prev = c_scr[h], n_scr[h], m_scr[h][:, :1]
        a = b_end - bcum + ic
        m_new = jnp.maximum(b_end + m_prev, jnp.max(a, axis=0, keepdims=True))
        w_state = jnp.exp(a - m_new)
        decay = jnp.exp(b_end + m_prev - m_new)
        kw = hs(k_st, h) * w_state
        c_scr[h] = decay * c_prev + _dot_tn(kw, hs(v_st, h), MLSTM_HP)
        n_scr[h] = decay * n_prev + jnp.sum(kw, axis=0, keepdims=True)
        m_scr[h] = jnp.broadcast_to(m_new, (1, HEAD_DIM))
        bcums.append(bcum)
        srcs.append(bcum - ic)
        inters.append(bcum + m_prev)
        qcs.append(_dot(hs(q_st, h), c_prev, MLSTM_HP))
        qns.append(jnp.sum(hs(q_st, h) * n_prev, axis=1, keepdims=True))
    yield
    incl, _ = _chunk_masks(rev)
    dlog = _stack_cols(bcums, STACK) - _stack_cols(srcs, STACK).T
    inter = jnp.concatenate(inters, axis=0)
    m_t = jnp.maximum(inter, jnp.max(jnp.where(incl, dlog, -1e30), axis=1, keepdims=True))
    s = qk * jnp.where(incl, jnp.exp(jnp.where(incl, dlog, 0.0) - m_t), 0.0)
    w_inter = jnp.exp(inter - m_t)
    yield
    num = _dot(s, v_st, MLSTM_HP) + w_inter * jnp.concatenate(qcs, axis=0)
    den = jnp.sum(s, axis=1, keepdims=True) + w_inter * jnp.concatenate(qns, axis=0)
    yield
    out = num / jnp.maximum(jnp.abs(den), jnp.exp(-m_t))
    for h in range(N_HEADS):
        o_ref[:, h * HEAD_DIM:(h + 1) * HEAD_DIM] = hs(out, h)
    yield


def mlstm_scan_bidir(q, k, v, gates, gp, state):
    nb, t, _ = q[0].shape
    nchunk = t // CHUNK
    assert t % CHUNK == 0
    ns = SCAN_NS
    specs, ospec = _scan_specs((q, k, v), gates, nb, nchunk)
    cspec = pl.BlockSpec((None, ns, N_HEADS, HEAD_DIM, HEAD_DIM), lambda d, b, n: (d, b, 0, 0, 0))
    vspec = pl.BlockSpec((None, ns, N_HEADS, 1, HEAD_DIM), lambda d, b, n: (d, b, 0, 0, 0))
    cshape = jax.ShapeDtypeStruct((2, nb, N_HEADS, HEAD_DIM, HEAD_DIM), F32)
    vshape = jax.ShapeDtypeStruct((2, nb, N_HEADS, 1, HEAD_DIM), F32)
    o, c, n, m = pl.pallas_call(
        _mlstm_scan_kernel,
        grid=(2, nb // ns, nchunk),
        in_specs=specs + [cspec, vspec, vspec],
        out_specs=[ospec, cspec, vspec, vspec],
        out_shape=[jax.ShapeDtypeStruct((2, nb, t, HEADS_W), F32), cshape, vshape, vshape],
        scratch_shapes=[pltpu.VMEM((ns, N_HEADS, HEAD_DIM, HEAD_DIM), F32), pltpu.VMEM((ns, N_HEADS, 1, HEAD_DIM), F32),
                        pltpu.VMEM((ns, N_HEADS, 1, HEAD_DIM), F32)],
        name="mlstm_scan",
    )(q[0], k[0], v[0], gates[0], gp, *state)
    return o, (c, n, m)


def mlstm_two_pass(ctx_args, lat_args, gp):
    nb = ctx_args[0][0].shape[0]
    zero = (jnp.zeros((2, nb, N_HEADS, HEAD_DIM, HEAD_DIM), F32), jnp.zeros((2, nb, N_HEADS, 1, HEAD_DIM), F32),
            jnp.zeros((2, nb, N_HEADS, 1, HEAD_DIM), F32))
    o_ctx, s_ctx = mlstm_scan_bidir(*ctx_args, gp, zero)
    o_lat, _ = mlstm_scan_bidir(*lat_args, gp, s_ctx)
    return o_ctx, o_lat


PROJ_TM = 512
PROJ_TN_MAX = 2304


def _inproj_kernel(h_ref, sc_ref, sh_ref, w_ref, o_ref):
    hm = (h_ref[...] * (1.0 + sc_ref[...]) + sh_ref[...]).astype(BF16)
    o_ref[...] = jnp.dot(hm, w_ref[...], preferred_element_type=F32)


def in_projection(h, sc, sh, w):
    nb, n, dm = h.shape
    ncol = w.shape[1]
    tm = min(PROJ_TM, n)
    tn = max(c for c in range(LANE, min(ncol, PROJ_TN_MAX) + 1, LANE) if ncol % c == 0)
    return pl.pallas_call(
        _inproj_kernel,
        grid=(nb, n // tm, ncol // tn),
        in_specs=[pl.BlockSpec((None, tm, dm), lambda b, t, j: (b, t, 0)),
                  pl.BlockSpec((None, 1, dm), lambda b, t, j: (b, 0, 0)),
                  pl.BlockSpec((None, 1, dm), lambda b, t, j: (b, 0, 0)),
                  pl.BlockSpec((dm, tn), lambda b, t, j: (0, j))],
        out_specs=pl.BlockSpec((None, tm, tn), lambda b, t, j: (b, t, j)),
        out_shape=jax.ShapeDtypeStruct((nb, n, ncol), F32),
        name="in_projection",
    )(h, sc, sh, w)


def _conv_kernel(x_ref, w_ref, o_ref, *, grid_w, l2_blocks, q_scale, k_scale):
    x = x_ref[...]
    t = x.shape[0]
    w = w_ref[...]
    col = lax.broadcasted_iota(jnp.int32, x.shape, 0) & (grid_w - 1)
    left = jnp.where(col == 0, 0.0, pltpu.roll(x, 1, 0))
    right = jnp.where(col == grid_w - 1, 0.0, pltpu.roll(x, t - 1, 0))
    row = lambda kh: w[3 * kh:3 * kh + 1] * left + w[3 * kh + 1:3 * kh + 2] * x + w[3 * kh + 2:3 * kh + 3] * right
    acc = row(1)
    if grid_w < t:
        zero = jnp.zeros((grid_w, LANE), F32)
        acc = acc + jnp.concatenate([zero, row(0)[:t - grid_w]], axis=0) + jnp.concatenate([row(2)[grid_w:], zero], axis=0)
    y = acc * jax.nn.sigmoid(acc)
    c = pl.program_id(1)
    normed = y * lax.rsqrt(jnp.sum(y * y, axis=1, keepdims=True) + 1e-6)
    y = jnp.where(c < l2_blocks, normed, y)
    o_ref[...] = y * jnp.where(c < N_HEADS, q_scale, jnp.where(c < 2 * N_HEADS, k_scale, 1.0))


def conv_prep(proj, conv_w, nblk, grid_w, l2_blocks, q_scale, k_scale):
    nb, t, _ = proj.shape
    assert grid_w & (grid_w - 1) == 0 and t % grid_w == 0
    return pl.pallas_call(
        functools.partial(_conv_kernel, grid_w=grid_w, l2_blocks=l2_blocks, q_scale=q_scale, k_scale=k_scale),
        grid=(nb, nblk),
        in_specs=[pl.BlockSpec((None, t, LANE), lambda b, c: (b, 0, c)),
                  pl.BlockSpec((CONV_K * CONV_K, LANE), lambda b, c: (0, c))],
        out_specs=pl.BlockSpec((None, t, LANE), lambda b, c: (b, 0, c)),
        out_shape=jax.ShapeDtypeStruct((nb, t, nblk * LANE), F32),
        compiler_params=pltpu.CompilerParams(vmem_limit_bytes=VMEM_LIMIT),
        name="conv_prep",
    )(proj, conv_w.reshape(CONV_K * CONV_K, -1))


def _head_norm(o, center):
    outs = []
    for h in range(N_HEADS):
        x = o[:, h * HEAD_DIM:(h + 1) * HEAD_DIM]
        if center:
            x = x - jnp.mean(x, axis=1, keepdims=True)
        outs.append(x * lax.rsqrt(jnp.mean(x * x, axis=1, keepdims=True) + EPS))
    return jnp.concatenate(outs, axis=1)


def _mix_out(y_a, y_b, wo_ref, h_ref, g1_ref, lng_ref, lnb_ref, o_ref):
    y = (jnp.dot(y_a.astype(BF16), wo_ref[:HEADS_W, :], preferred_element_type=F32)
         + jnp.dot(y_b.astype(BF16), wo_ref[HEADS_W:, :], preferred_element_type=F32))
    o_ref[...] = _ln_rows(ALPHA * h_ref[...] + g1_ref[...] * y, lng_ref[...], lnb_ref[...])


def _merge_even_kernel(og_ref, or_ref, za_ref, zr_ref, gg_ref, rg_ref, wo_ref, h_ref, g1_ref, lng_ref, lnb_ref, o_ref):
    za, zr = za_ref[...], zr_ref[...]
    y_g = _head_norm(og_ref[0] + og_ref[1], False) * gg_ref[...] * (za * jax.nn.sigmoid(za))
    y_r = _head_norm(or_ref[0] + or_ref[1], True) * rg_ref[...] * (zr * jax.nn.sigmoid(zr))
    _mix_out(y_g, y_r, wo_ref, h_ref, g1_ref, lng_ref, lnb_ref, o_ref)


def _merge_odd_kernel(om_ref, ys_ref, og_ref, u_ref, mg_ref, dsk_ref, bglu_ref, wglu_ref, wo_ref, h_ref, g1_ref,
                      lng_ref, lnb_ref, o_ref):
    y_m = _head_norm(om_ref[0] + om_ref[1], True) * mg_ref[...] * jax.nn.sigmoid(og_ref[...])
    y = jax.nn.gelu(ys_ref[...] + dsk_ref[...] * u_ref[...])
    y = y * jax.nn.sigmoid(jnp.dot(y.astype(BF16), wglu_ref[...], preferred_element_type=F32) + bglu_ref[...])
    _mix_out(y_m, y, wo_ref, h_ref, g1_ref, lng_ref, lnb_ref, o_ref)


def _merge_call(kernel_fn, name, scans, toks, rows, mats, h, g1, ln_g, ln_b):
    nb, n, dm = h.shape
    tm = min(PROJ_TM, n)
    full = lambda a: pl.BlockSpec(a.shape, lambda b, t: (0,) * a.ndim)
    rowv = lambda v: v.reshape(1, -1)
    specs, args = [], []
    for a in scans:
        specs.append(pl.BlockSpec((2, None, tm, HEADS_W), lambda b, t: (0, b, t, 0)))
        args.append(a)
    for a, c in toks:
        specs.append(pl.BlockSpec((None, tm, HEADS_W), lambda b, t, c=c: (b, t, c)))
        args.append(a)
    for v in rows:
        args.append(rowv(v))
        specs.append(full(args[-1]))
    for m in mats:
        args.append(m)
        specs.append(full(m))
    args += [h, g1, rowv(ln_g), rowv(ln_b)]
    specs += [pl.BlockSpec((None, tm, dm), lambda b, t: (b, t, 0)), pl.BlockSpec((None, 1, dm), lambda b, t: (b, 0, 0)),
              full(args[-2]), full(args[-1])]
    return pl.pallas_call(
        kernel_fn,
        grid=(nb, n // tm),
        in_specs=specs,
        out_specs=pl.BlockSpec((None, tm, dm), lambda b, t: (b, t, 0)),
        out_shape=jax.ShapeDtypeStruct((nb, n, dm), F32),
        compiler_params=pltpu.CompilerParams(vmem_limit_bytes=VMEM_LIMIT),
        name=name,
    )(*args)


def split_cols(t, sizes):
    return jnp.split(t, np.cumsum(sizes)[:-1].tolist(), axis=-1)


def heads(t, n):
    return t.reshape(t.shape[:-1] + (n, t.shape[-1] // n))


def flip_t(t):
    return None if t is None else jnp.flip(t, axis=1)


def l2norm(t):
    return t * lax.rsqrt(jnp.sum(t * t, axis=-1, keepdims=True) + 1e-6)


def layer_norm(t, g, b):
    mu = t.mean(-1, keepdims=True)
    var = jnp.square(t - mu).mean(-1, keepdims=True)
    return ((t - mu) * lax.rsqrt(var + EPS)) * g + b


def rms_norm_heads(o, g):
    y = o * lax.rsqrt(jnp.mean(o * o, axis=-1, keepdims=True) + EPS) * g
    return y.reshape(o.shape[:2] + (-1,))


def group_norm_heads(o, g):
    mu = o.mean(-1, keepdims=True)
    var = jnp.square(o - mu).mean(-1, keepdims=True)
    return ((o - mu) * lax.rsqrt(var + EPS)).reshape(o.shape[:2] + (-1,)) * g


def short_conv(t, w, on_grid):
    ch = t.shape[-1]
    if on_grid:
        b, n = t.shape[:2]
        rows = n // GRID_W
        tg = t.reshape(b, rows, GRID_W, ch)
        y = lax.conv_general_dilated(tg, w[:, :, None, :], (1, 1), 'SAME',
                                     dimension_numbers=('NHWC', 'HWIO', 'NHWC'), feature_group_count=ch)
        return y.reshape(b, n, ch)
    return lax.conv_general_dilated(t, w[CONV_K // 2][:, None, :], (1,), 'SAME',
                                    dimension_numbers=('NWC', 'WIO', 'NWC'), feature_group_count=ch)


def to_chunks(t):
    b, n, h = t.shape[:3]
    t = t.reshape((b, n // CHUNK, CHUNK, h) + t.shape[3:])
    return jnp.moveaxis(jnp.moveaxis(t, 3, 1), 2, 0)


def from_chunks(t):
    t = jnp.moveaxis(jnp.moveaxis(t, 0, 2), 1, 3)
    return t.reshape((t.shape[0], -1) + t.shape[3:])


def linear_scan(q, k, v, g, beta, s0, want_out):
    b, _, h, dk = k.shape
    dv = v.shape[-1]
    kc, vc = to_chunks(k), to_chunks(v)
    gcum = jnp.cumsum(to_chunks(g), axis=-1)
    diff = gcum[..., :, None] - gcum[..., None, :]
    incl = jnp.tril(jnp.ones((CHUNK, CHUNK), bool))
    if beta is None:
        u, w = vc, None
    else:
        bc = to_chunks(beta)[..., None]
        kb = kc * bc
        strict = jnp.tril(jnp.ones((CHUNK, CHUNK), bool), -1)
        a = jnp.where(strict, jnp.einsum('...ik,...jk->...ij', kb, kc) * jnp.exp(jnp.where(strict, diff, 0.0)), 0.0)
        rhs = jnp.concatenate([vc * bc, kb * jnp.exp(gcum)[..., None]], axis=-1)
        sol = lax.linalg.triangular_solve(a + jnp.eye(CHUNK, dtype=a.dtype), rhs, left_side=True,
                                          lower=True, unit_diagonal=True)
        u, w = sol[..., :dv], sol[..., dv:]
    k_end = kc * jnp.exp(gcum[..., -1:] - gcum)[..., None]
    g_end = jnp.exp(gcum[..., -1])[..., None, None]
    if want_out:
        qc = to_chunks(q)
        q_dec = qc * jnp.exp(gcum)[..., None]
        a_qk = jnp.where(incl, jnp.einsum('...ik,...jk->...ij', qc, kc) * jnp.exp(jnp.where(incl, diff, 0.0)), 0.0)
    else:
        q_dec, a_qk = None, None
    if s0 is None:
        s0 = jnp.zeros((b, h, dk, dv), F32)

    def step(s, inp):
        qd, ke, uc, wc, aqk, ge = inp
        vn = uc if wc is None else uc - jnp.einsum('bhck,bhkv->bhcv', wc, s)
        s_new = ge * s + jnp.einsum('bhck,bhcv->bhkv', ke, vn)
        if qd is None:
            return s_new, None
        return s_new, jnp.einsum('bhck,bhkv->bhcv', qd, s) + jnp.einsum('bhij,bhjv->bhiv', aqk, vn)

    s_fin, o = lax.scan(step, s0, (q_dec, k_end, u, w, a_qk, g_end))
    return (from_chunks(o) if want_out else None), s_fin


def mlstm_scan(q, k, v, log_i, log_f, s0, want_out):
    b, _, h, dk = k.shape
    dv = v.shape[-1]
    if s0 is None:
        s0 = (jnp.zeros((b, h, dk, dv), F32), jnp.zeros((b, h, dk), F32), jnp.zeros((b, h), F32))
    incl = jnp.tril(jnp.ones((CHUNK, CHUNK), bool))

    def step(carry, inp):
        c_prev, n_prev, m_prev = carry
        qc, kc, vc, ic, fc = inp
        bcum = jnp.cumsum(fc, axis=-1)
        b_end = bcum[..., -1]
        a = b_end[..., None] - bcum + ic
        m_new = jnp.maximum(b_end + m_prev, a.max(-1))
        w_state = jnp.exp(a - m_new[..., None])
        decay = jnp.exp(b_end + m_prev - m_new)
        c_new = decay[..., None, None] * c_prev + jnp.einsum('bhck,bhcv->bhkv', kc * w_state[..., None], vc)
        n_new = decay[..., None] * n_prev + jnp.einsum('bhck,bhc->bhk', kc, w_state)
        carry_new = (c_new, n_new, m_new)
        if qc is None:
            return carry_new, None
        dlog = jnp.where(incl, bcum[..., :, None] - bcum[..., None, :] + ic[..., None, :], -jnp.inf)
        inter = bcum + m_prev[..., None]
        m_t = jnp.maximum(inter, dlog.max(-1))
        s = jnp.einsum('bhik,bhjk->bhij', qc, kc) * jnp.exp(dlog - m_t[..., None])
        w_inter = jnp.exp(inter - m_t)[..., None]
        num = jnp.einsum('bhij,bhjv->bhiv', s, vc) + w_inter * jnp.einsum('bhik,bhkv->bhiv', qc, c_prev)
        den = s.sum(-1, keepdims=True) + w_inter * jnp.einsum('bhik,bhk->bhi', qc, n_prev)[..., None]
        return carry_new, num / jnp.maximum(jnp.abs(den), jnp.exp(-m_t)[..., None])

    xs = (to_chunks(q) if want_out else None, to_chunks(k), to_chunks(v), to_chunks(log_i), to_chunks(log_f))
    s_fin, hs = lax.scan(step, s0, xs)
    return (from_chunks(hs) if want_out else None), s_fin


def two_pass(scan_fn, ctx_args, lat_args, ctx_out, reverse):
    if reverse:
        ctx_args = [flip_t(t) for t in ctx_args]
        lat_args = [flip_t(t) for t in lat_args]
    o_ctx, s_ctx = scan_fn(*ctx_args, None, ctx_out)
    o_lat, _ = scan_fn(*lat_args, s_ctx, True)
    if reverse:
        o_ctx, o_lat = flip_t(o_ctx), flip_t(o_lat)
    return o_ctx, o_lat


def retention_log_decay(direction):
    expo = 5.0 + 2.0 * jnp.arange(RET_HEADS, dtype=F32) + direction
    return jnp.log1p(-jnp.exp2(-expo))


GATE_COLS = 4 * N_HEADS
Q_SCALE = HEAD_DIM ** -0.5


def _gate_row(*vals):
    v = jnp.concatenate([jnp.ravel(x) for x in vals])
    return jnp.pad(v, (0, LANE - v.shape[0]))


def _pad_gate_cols(w):
    return jnp.pad(w, ((0, 0), (0, LANE - w.shape[1])))


def gdn_retention_mixer(hs_, mods, w_in, w_out, conv_w, a_log, dt_bias, gdn_gain, ret_gain, ln_g, ln_b, ctx_out):
    w = jnp.concatenate([w_in[:, :4 * GDN_W], w_in[:, 4 * GDN_W + GATE_COLS:],
                         _pad_gate_cols(w_in[:, 4 * GDN_W:4 * GDN_W + GATE_COLS])], axis=1).astype(BF16)
    gate_blk = (4 * GDN_W + 4 * RET_W) // LANE
    gp_gdn = jnp.stack([_gate_row(-jnp.exp(a_log)), _gate_row(dt_bias)])
    gp_ret = jnp.stack([_gate_row(retention_log_decay(0), retention_log_decay(1)), jnp.zeros((LANE,), F32)])
    wo = w_out.astype(BF16)
    projs, convs = [], []
    for (h, (sc, sh, _)), grid_w in zip(zip(hs_, mods), (hs_[0].shape[1], GRID_W)):
        p = in_projection(h, sc, sh, w)
        projs.append(p)
        convs.append(conv_prep(p, conv_w, 3 * GDN_W // LANE, grid_w, 2 * N_HEADS, Q_SCALE, 1.0))
    gdn_args = [((cv, 0), (cv, 1), (cv, 2), (p, gate_blk)) for p, cv in zip(projs, convs)]
    ret_args = [((p, 4), (p, 5), (p, 6), None) for p in projs]
    og = linear_scan_two_pass(gdn_args[0], gdn_args[1], gp_gdn, True)
    orr = linear_scan_two_pass(ret_args[0], ret_args[1], gp_ret, False, Q_SCALE)
    outs = []
    for i in range(2):
        if i == 0 and not ctx_out:
            outs.append(None)
            continue
        outs.append(_merge_call(_merge_even_kernel, "merge_even", [og[i], orr[i]], [(projs[i], 3), (projs[i], 7)],
                                [jnp.tile(gdn_gain, N_HEADS), ret_gain], [wo], hs_[i], mods[i][2], ln_g, ln_b))
    return outs


def mlstm_s5_mixer(hs_, mods, w_in, w_out, conv_w, gate_bias, mlstm_gain, lam_re, lam_im, log_dt,
                   b_re, b_im, c_re, c_im, d_skip, w_glu, b_glu, ln_g, ln_b, ctx_out):
    w = jnp.concatenate([w_in[:, :4 * MLSTM_W], _pad_gate_cols(w_in[:, 4 * MLSTM_W:4 * MLSTM_W + GATE_COLS])],
                        axis=1).astype(BF16)
    w_u = w_in[:, 4 * MLSTM_W + GATE_COLS:].astype(BF16)
    gate_blk = 4 * MLSTM_W // LANE
    gp = jnp.stack([_gate_row(gate_bias[0, 0], gate_bias[1, 0], gate_bias[0, 1], gate_bias[1, 1]),
                    jnp.zeros((LANE,), F32)])
    wo, wglu = w_out.astype(BF16), w_glu.astype(BF16)
    projs, us, args = [], [], []
    for (h, (sc, sh, _)), grid_w in zip(zip(hs_, mods), (hs_[0].shape[1], GRID_W)):
        p = in_projection(h, sc, sh, w)
        cv = conv_prep(p, conv_w, 2 * MLSTM_W // LANE, grid_w, 0, 1.0, Q_SCALE)
        projs.append(p)
        us.append(in_projection(h, sc, sh, w_u))
        args.append(((cv, 0), (cv, 1), (p, 2), (p, gate_blk)))
    om = mlstm_two_pass(args[0], args[1], gp)
    ys = s5_bidirectional(us[0], us[1], _s5_weights(lam_re, lam_im, log_dt, b_re, b_im, c_re, c_im))
    outs = []
    for i in range(2):
        if i == 0 and not ctx_out:
            outs.append(None)
            continue
        outs.append(_merge_call(_merge_odd_kernel, "merge_odd", [om[i]], [(ys[i], 0), (projs[i], 3), (us[i], 0)],
                                [mlstm_gain, d_skip, b_glu], [wglu, wo], hs_[i], mods[i][2], ln_g, ln_b))
    return outs


def kernel(x, c, ctx, c_ctx, w_mod, b_mod, ln1_g, ln1_b, ln2_g, ln2_b, w_router, w_gate, w_up, w_down,
           ev_w_in, ev_w_out, ev_conv, ev_a_log, ev_dt_bias, ev_gdn_norm, ev_ret_norm,
           od_w_in, od_w_out, od_conv, od_gate_bias, od_mlstm_norm, od_lam_re, od_lam_im, od_log_dt,
           od_b_re, od_b_im, od_c_re, od_c_im, od_d_skip, od_w_glu, od_b_glu):
    h_lat, h_ctx = x, ctx
    s_lat = jax.nn.silu(c)
    s_ctx = jax.nn.silu(c_ctx)
    for l in range(DEPTH):
        last = l == DEPTH - 1
        sh1, sc1, g1, sh2, sc2, g2 = jnp.split((s_lat @ w_mod[l] + b_mod[l])[:, None, :], 6, axis=-1)
        bc = lambda v: jnp.broadcast_to(v, (BATCH, 1, D_MODEL))
        csh1, csc1, cg1, csh2, csc2, cg2 = [bc(v) for v in jnp.split(s_ctx @ w_mod[l] + b_mod[l], 6, axis=-1)]
        streams = (h_ctx, h_lat)
        mods = ((csc1, csh1, cg1), (sc1, sh1, g1))
        if l % 2 == 0:
            e = l // 2
            h_ctx, h_lat = gdn_retention_mixer(streams, mods, ev_w_in[e], ev_w_out[e], ev_conv[e], ev_a_log[e],
                                               ev_dt_bias[e], ev_gdn_norm[e], ev_ret_norm[e], ln1_g[l], ln1_b[l],
                                               not last)
        else:
            o = l // 2
            h_ctx, h_lat = mlstm_s5_mixer(streams, mods, od_w_in[o], od_w_out[o], od_conv[o], od_gate_bias[o],
                                          od_mlstm_norm[o], od_lam_re[o], od_lam_im[o], od_log_dt[o],
                                          od_b_re[o], od_b_im[o], od_c_re[o], od_c_im[o], od_d_skip[o],
                                          od_w_glu[o], od_b_glu[o], ln1_g[l], ln1_b[l], not last)
        experts = (w_router[l], w_gate[l].astype(BF16), w_up[l].astype(BF16), w_down[l].astype(BF16))
        h_lat = moe_block(h_lat, sc2, sh2, g2, ln2_g[l], ln2_b[l], *experts)
        if not last:
            h_ctx = moe_block(h_ctx, csc2, csh2, cg2, ln2_g[l], ln2_b[l], *experts)
    return h_lat
```

```python
import functools
import math

import jax
import jax.numpy as jnp
import numpy as np
from jax import lax
from jax.experimental import pallas as pl
from jax.experimental.pallas import tpu as pltpu

D_MODEL = 1024
BATCH = 4
SEQ = 4096
DEPTH = 4
GRID_W = 64
CTX_LEN = 256
CHUNK = 64
CONV_K = 3
HEAD_DIM = D_MODEL // 8
GDN_HEADS = 4
RET_HEADS = 4
MLSTM_HEADS = 4
GDN_W = GDN_HEADS * HEAD_DIM
RET_W = RET_HEADS * HEAD_DIM
MLSTM_W = MLSTM_HEADS * HEAD_DIM
S5_CH = D_MODEL // 2
S5_GROUP = 16
S5_GROUPS = S5_CH // S5_GROUP
S5_STATE = 64
N_EXPERTS = 16
EXPERT_FF = 2 * D_MODEL
CAPACITY_FACTOR = 2
ALPHA = (2 * DEPTH) ** 0.25
EPS = 1e-5
EVEN_COLS = (GDN_W, GDN_W, GDN_W, GDN_W, GDN_HEADS, GDN_HEADS, GDN_HEADS, GDN_HEADS,
             RET_W, RET_W, RET_W, RET_W)
ODD_COLS = (MLSTM_W, MLSTM_W, MLSTM_W, MLSTM_W, MLSTM_HEADS, MLSTM_HEADS, MLSTM_HEADS, MLSTM_HEADS, S5_CH)
F32 = jnp.float32
BF16 = jnp.bfloat16

LANE = 128


def _mm_kernel(a_ref, b_ref, o_ref):
    o_ref[...] = jnp.dot(a_ref[...].astype(BF16), b_ref[...].astype(BF16), preferred_element_type=F32)


def pmatmul(a, b, tm=512, tn=512):
    m, k = a.shape
    n = b.shape[1]
    n_pad = -n % LANE
    if n_pad:
        b = jnp.pad(b, ((0, 0), (0, n_pad)))
    np_ = n + n_pad
    tn = min(tn, np_)
    while np_ % tn:
        tn -= LANE
    tm = min(tm, m)
    assert m % tm == 0
    out = pl.pallas_call(
        _mm_kernel,
        grid=(m // tm, np_ // tn),
        in_specs=[pl.BlockSpec((tm, k), lambda i, j: (i, 0)),
                  pl.BlockSpec((k, tn), lambda i, j: (0, j))],
        out_specs=pl.BlockSpec((tm, tn), lambda i, j: (i, j)),
        out_shape=jax.ShapeDtypeStruct((m, np_), F32),
        name="pmatmul",
    )(a, b)
    return out[:, :n] if n_pad else out


def mm(x, w):
    lead = x.shape[:-1]
    return pmatmul(x.reshape(-1, x.shape[-1]), w).reshape(lead + (w.shape[-1],))


S5_L = 16
S5_NB = S5_CH // LANE
S5_GPB = LANE // S5_GROUP
S5_SW = S5_GPB * S5_STATE
VMEM_LIMIT = 56 * 1024 * 1024


def _s5_weights(lam_re, lam_im, log_dt, b_re, b_im, c_re, c_im):
    L, G = S5_L, S5_GROUPS
    hp = lax.Precision.HIGHEST
    taus = jnp.arange(L + 1, dtype=F32)[:, None, None]
    ks, ws, cas, ds = [], [], [], []
    for d in range(2):
        lr = jnp.minimum(lam_re[d], -1e-4)
        li = lam_im[d]
        dt = jnp.exp(log_dt[d])[:, None]
        mag = jnp.exp(lr * dt)
        ab_re, ab_im = mag * jnp.cos(li * dt), mag * jnp.sin(li * dt)
        xr, xi, den = ab_re - 1.0, ab_im, lr * lr + li * li
        f_re = (xr * lr + xi * li) / den
        f_im = (xi * lr - xr * li) / den
        bb_re = f_re[..., None] * b_re - f_im[..., None] * b_im
        bb_im = f_re[..., None] * b_im + f_im[..., None] * b_re
        pmag = jnp.exp(taus * (lr * dt))
        ar, ai = pmag * jnp.cos(taus * (li * dt)), pmag * jnp.sin(taus * (li * dt))
        wr = ar[..., None] * bb_re - ai[..., None] * bb_im
        wi = ar[..., None] * bb_im + ai[..., None] * bb_re
        k = (jnp.einsum('gop,tgpi->tgio', c_re, wr, precision=hp)
             - jnp.einsum('gop,tgpi->tgio', c_im, wi, precision=hp))
        car = c_re[None] * ar[:, :, None, :] - c_im[None] * ai[:, :, None, :]
        cai = c_re[None] * ai[:, :, None, :] + c_im[None] * ar[:, :, None, :]
        ks.append(k)
        ws.append((wr, wi))
        cas.append((car, cai))
        ds.append((ar[L], ai[L]))

    eye = jnp.eye(S5_GPB, dtype=BF16)
    split_g = lambda t, g_axis: t.reshape(t.shape[:g_axis] + (S5_NB, S5_GPB) + t.shape[g_axis + 1:]).astype(BF16)

    kf, kb = ks
    kc = jnp.concatenate([kb[1:L][::-1], (kf[0] + kb[0])[None], kf[1:L]], axis=0)
    idx = (jnp.arange(L)[None, :] - jnp.arange(L)[:, None]) + (L - 1)
    tz = jnp.einsum('xyjaio,ab->jxaiybo', split_g(kc[idx], 2), eye).reshape(S5_NB, L * LANE, L * LANE)

    pb_c, ca_c = [], []
    for d in range(2):
        wr, wi = ws[d]
        order = jnp.arange(L - 1, -1, -1) if d == 0 else jnp.arange(L)
        pb_c.append(jnp.stack([jnp.swapaxes(w[order], -1, -2) for w in (wr, wi)]))
        car, cai = cas[d]
        order = jnp.arange(1, L + 1) if d == 0 else jnp.arange(L, 0, -1)
        ca_c.append(jnp.stack([jnp.swapaxes(m, -1, -2) for m in (car[order], -cai[order])]))
    pb = jnp.einsum('drljaip,ab->jlaidrbp', split_g(jnp.stack(pb_c), 3), eye).reshape(S5_NB, L * LANE, 4 * S5_SW)
    ca = jnp.einsum('drljapo,ab->jdraplbo', split_g(jnp.stack(ca_c), 3), eye).reshape(S5_NB, 4 * S5_SW, L * LANE)
    dr = jnp.stack([ds[0][0], ds[1][0]], 0).reshape(2, S5_NB, 1, S5_SW).transpose(1, 0, 2, 3).reshape(2 * S5_NB, 1, S5_SW)
    di = jnp.stack([ds[0][1], ds[1][1]], 0).reshape(2, S5_NB, 1, S5_SW).transpose(1, 0, 2, 3).reshape(2 * S5_NB, 1, S5_SW)
    return tz, pb, ca, dr, di


def _s5_p_kernel(u_ref, pb_ref, p_ref, *, nb, nc):
    res = jnp.dot(u_ref[...], pb_ref[...], preferred_element_type=F32)
    for b in range(nb):
        p_ref[:, b * 2 * S5_SW:(b + 1) * 2 * S5_SW] = res[b * nc:(b + 1) * nc]


def _s5_scan_kernel(p_ref, dr_ref, di_ref, s_ref, *, n_ctx, n_lat):
    rev = pl.program_id(0) % 2
    dr = dr_ref[...]
    di = di_ref[...]
    nbatch = p_ref.shape[1]

    def phase(base, n, carry):
        def body(step, carry):
            sr, si = carry
            row = base + jnp.where(rev == 0, step, n - 1 - step)
            s_ref[row, :, :S5_SW] = sr
            s_ref[row, :, S5_SW:] = si
            p = p_ref[row]
            nr = dr * sr - di * si + p[:, :S5_SW]
            ni = dr * si + di * sr + p[:, S5_SW:]
            return nr, ni
        return lax.fori_loop(0, n, body, carry)

    zero = jnp.zeros((nbatch, S5_SW), F32)
    carry = phase(0, n_ctx, (zero, zero))
    phase(n_ctx, n_lat, carry)


def _s5_y_kernel(u_ref, tz_ref, sf_ref, sb_ref, ca_ref, y_ref):
    y = jnp.dot(u_ref[...], tz_ref[...], preferred_element_type=F32)
    y += jnp.dot(sf_ref[...].astype(BF16), ca_ref[:2 * S5_SW, :], preferred_element_type=F32)
    y += jnp.dot(sb_ref[...].astype(BF16), ca_ref[2 * S5_SW:, :], preferred_element_type=F32)
    y_ref[...] = y


def s5_bidirectional(u_ctx, u_lat, weights):
    tz, pb, ca, dr, di = weights
    L = S5_L
    nb, t_ctx, _ = u_ctx.shape
    t_lat = u_lat.shape[1]
    assert t_ctx % L == 0 and t_lat % L == 0
    n_ctx, n_lat = t_ctx // L, t_lat // L
    nc = n_ctx + n_lat
    kw = L * LANE
    sw2 = 2 * S5_SW
    u = jnp.concatenate([u_ctx, u_lat], axis=1)
    ub = u.reshape(nb * nc, L, S5_NB, LANE).transpose(2, 0, 1, 3).reshape(S5_NB, nb * nc, kw).astype(BF16)

    p = pl.pallas_call(
        functools.partial(_s5_p_kernel, nb=nb, nc=nc),
        grid=(S5_NB, 2),
        in_specs=[pl.BlockSpec((None, nb * nc, kw), lambda j, d: (j, 0, 0)),
                  pl.BlockSpec((None, kw, sw2), lambda j, d: (j, 0, d))],
        out_specs=pl.BlockSpec((nc, nb * sw2), lambda j, d: (0, j * 2 + d)),
        out_shape=jax.ShapeDtypeStruct((nc, S5_NB * 2 * nb * sw2), F32),
        compiler_params=pltpu.CompilerParams(vmem_limit_bytes=VMEM_LIMIT),
        name="s5_chunk_inputs",
    )(ub, pb)

    p4 = p.reshape(nc, S5_NB * 2, nb, sw2)
    s4 = pl.pallas_call(
        functools.partial(_s5_scan_kernel, n_ctx=n_ctx, n_lat=n_lat),
        grid=(S5_NB * 2,),
        in_specs=[pl.BlockSpec((nc, None, nb, sw2), lambda g: (0, g, 0, 0)),
                  pl.BlockSpec((None, 1, S5_SW), lambda g: (g, 0, 0)),
                  pl.BlockSpec((None, 1, S5_SW), lambda g: (g, 0, 0))],
        out_specs=pl.BlockSpec((nc, None, nb, sw2), lambda g: (0, g, 0, 0)),
        out_shape=jax.ShapeDtypeStruct(p4.shape, F32),
        compiler_params=pltpu.CompilerParams(vmem_limit_bytes=VMEM_LIMIT),
        name="s5_state_scan",
    )(p4, dr, di)

    s2 = s4.reshape(nc, S5_NB * 2 * nb * sw2)
    yb = pl.pallas_call(
        _s5_y_kernel,
        grid=(S5_NB, nb),
        in_specs=[pl.BlockSpec((None, nc, kw), lambda j, b: (j, b, 0)),
                  pl.BlockSpec((None, kw, kw), lambda j, b: (j, 0, 0)),
                  pl.BlockSpec((nc, sw2), lambda j, b: (0, (j * 2) * nb + b)),
                  pl.BlockSpec((nc, sw2), lambda j, b: (0, (j * 2 + 1) * nb + b)),
                  pl.BlockSpec((None, 2 * sw2, kw), lambda j, b: (j, 0, 0))],
        out_specs=pl.BlockSpec((None, nc, kw), lambda j, b: (j, b, 0)),
        out_shape=jax.ShapeDtypeStruct((S5_NB, nb * nc, kw), F32),
        compiler_params=pltpu.CompilerParams(vmem_limit_bytes=VMEM_LIMIT),
        name="s5_output",
    )(ub, tz, s2, s2, ca)
    y = yb.reshape(S5_NB, nb, nc, L, LANE).transpose(1, 2, 3, 0, 4).reshape(nb, nc * L, S5_CH)
    return y[:, :t_ctx], y[:, t_ctx:]


ROUTE_TN = 512
MOE_TN = 1024
SEL_ROWS = 128
FF_SPLIT = 2
AFF_PARTS = 3
SLOT_BLOCK = 128


def _route_kernel(h_ref, sc_ref, sh_ref, wr_ref, hm_ref, afft_ref, asp_ref):
    hm = h_ref[...] * (1.0 + sc_ref[...]) + sh_ref[...]
    hm_ref[...] = hm.astype(BF16)
    logits = jnp.dot(hm, wr_ref[...], precision=lax.Precision.HIGHEST, preferred_element_type=F32)
    lane = lax.broadcasted_iota(jnp.int32, logits.shape, 1)
    logits = jnp.where(lane < N_EXPERTS, logits, -jnp.inf)
    ex = jnp.exp(logits - jnp.max(logits, axis=1, keepdims=True))
    aff = ex / jnp.sum(ex, axis=1, keepdims=True)
    afft_ref[...] = aff.T[:N_EXPERTS, :]
    hi = aff.astype(BF16).astype(F32)
    mid = (aff - hi).astype(BF16).astype(F32)
    lo = (aff - hi - mid).astype(BF16).astype(F32)
    asp = hi + pltpu.roll(mid, N_EXPERTS, 1) + pltpu.roll(lo, 2 * N_EXPERTS, 1)
    asp_ref[...] = asp.astype(BF16)


def _select_kernel(aff_ref, pos_ref, lo_ref, hi_ref, *, nblk_log2, cap):
    a = aff_ref[...]
    r = a.shape[0]
    bits = pltpu.bitcast(a, jnp.int32)
    ri = lax.broadcasted_iota(jnp.int32, (r, r), 0)
    rj = lax.broadcasted_iota(jnp.int32, (r, r), 1)
    same = (ri >> nblk_log2) == (rj >> nblk_log2)
    gm = jnp.where(same, 1.0, 0.0).astype(BF16)
    lm = jnp.where(same & (rj < ri), 1.0, 0.0).astype(BF16)
    li = lax.broadcasted_iota(jnp.int32, (LANE, LANE), 0)
    lj = lax.broadcasted_iota(jnp.int32, (LANE, LANE), 1)
    um = jnp.where(li <= lj, 1.0, 0.0).astype(BF16)

    def group_count(mask):
        rc = jnp.sum(jnp.where(mask, 1.0, 0.0), axis=1, keepdims=True)
        gc = jnp.dot(gm, jnp.broadcast_to(rc, (r, LANE)).astype(BF16), preferred_element_type=F32)
        return gc[:, :1]

    def bisect(i, thr):
        cand = thr | jnp.left_shift(1, 29 - i)
        return jnp.where(group_count(bits >= cand) >= cap, cand, thr)

    thr = lax.fori_loop(0, 30, bisect, jnp.zeros((r, 1), jnp.int32))

    def prefix(mask):
        x = jnp.where(mask, 1.0, 0.0)
        inc = jnp.dot(x.astype(BF16), um, preferred_element_type=F32)
        tot = jnp.broadcast_to(inc[:, LANE - 1:LANE], (r, LANE)).astype(BF16)
        return inc - x + jnp.dot(lm, tot, preferred_element_type=F32)

    gt = bits > thr
    eq = bits == thr
    need = cap - group_count(gt)
    sel = gt | (eq & (prefix(eq) < need))
    pos = jnp.where(sel, prefix(sel), -1.0).astype(jnp.int32)
    pos_ref[...] = pos
    lo_ref[...] = jnp.broadcast_to(jnp.min(jnp.where(sel, pos, cap), axis=1, keepdims=True), pos.shape)
    hi_ref[...] = jnp.broadcast_to(jnp.max(pos, axis=1, keepdims=True), pos.shape)


def _onehot(pos_ref, first_slot, n_slots):
    rows = lax.broadcasted_iota(jnp.int32, (n_slots, LANE), 0) + first_slot
    blocks = [jnp.where(pos_ref[k:k + 1, :] == rows, 1.0, 0.0).astype(BF16) for k in range(pos_ref.shape[0])]
    return jnp.concatenate(blocks, axis=1)


def _slot_blocks(cap):
    size = min(cap, SLOT_BLOCK)
    return cap // size, size


def _ffn_kernel(hit_ref, pos_ref, hb_ref, asp_ref, wg_ref, wu_ref, wd_ref, ys_ref, xs_acc, g_acc, *, cap):
    e, b, kt = pl.program_id(0), pl.program_id(1), pl.program_id(2)

    @pl.when(kt == 0)
    def _():
        xs_acc[...] = jnp.zeros_like(xs_acc)
        g_acc[...] = jnp.zeros_like(g_acc)

    nsb, sb = _slot_blocks(cap)
    base = ((b * N_EXPERTS + e) * pl.num_programs(2) + kt) * nsb
    for s in range(nsb):
        @pl.when(hit_ref[base + s] != 0)
        def _(s=s):
            oh = _onehot(pos_ref, s * sb, sb)
            xs_acc[s * sb:(s + 1) * sb, :] += jnp.dot(oh, hb_ref[...], preferred_element_type=F32)
            g_acc[s * sb:(s + 1) * sb, :] += jnp.dot(oh, asp_ref[...], preferred_element_type=F32)

    @pl.when(kt == pl.num_programs(2) - 1)
    def _():
        xs = xs_acc[...].astype(BF16)
        g = g_acc[...]
        lane = lax.broadcasted_iota(jnp.int32, g.shape, 1)
        gate = jnp.sum(jnp.where((lane & (N_EXPERTS - 1)) == e, g, 0.0), axis=1, keepdims=True)
        fw = EXPERT_FF // FF_SPLIT
        y = jnp.zeros((cap, D_MODEL), F32)
        for f in range(FF_SPLIT):
            hg = jnp.dot(xs, wg_ref[:, f * fw:(f + 1) * fw], preferred_element_type=F32)
            hu = jnp.dot(xs, wu_ref[:, f * fw:(f + 1) * fw], preferred_element_type=F32)
            hid = (hg * jax.nn.sigmoid(hg)) * hu
            y += jnp.dot(hid.astype(BF16), wd_ref[f * fw:(f + 1) * fw, :], preferred_element_type=F32)
        ys_ref[...] = (y * gate).astype(BF16)


def _ln_rows(z, g, b):
    mu = jnp.mean(z, axis=-1, keepdims=True)
    zc = z - mu
    var = jnp.mean(zc * zc, axis=-1, keepdims=True)
    return zc * lax.rsqrt(var + EPS) * g + b


def _combine_kernel(hit_ref, pos_ref, ys_ref, h_ref, g2_ref, lng_ref, lnb_ref, o_ref, acc, *, cap):
    b, t, e = pl.program_id(0), pl.program_id(1), pl.program_id(2)

    @pl.when(e == 0)
    def _():
        acc[...] = jnp.zeros_like(acc)

    nsb, sb = _slot_blocks(cap)
    base = ((b * N_EXPERTS + e) * pl.num_programs(1) + t) * nsb
    for s in range(nsb):
        @pl.when(hit_ref[base + s] != 0)
        def _(s=s):
            oh = _onehot(pos_ref, s * sb, sb)
            acc[...] += lax.dot_general(oh, ys_ref[s * sb:(s + 1) * sb, :], (((0,), (0,)), ((), ())),
                                        preferred_element_type=F32)

    @pl.when(e == pl.num_programs(2) - 1)
    def _():
        z = ALPHA * h_ref[...] + g2_ref[...] * acc[...]
        o_ref[...] = _ln_rows(z, lng_ref[...], lnb_ref[...])


def moe_block(h, sc, sh, g2, ln_g, ln_b, w_router, wg, wu, wd):
    nb, n, dm = h.shape
    cap = CAPACITY_FACTOR * n // N_EXPERTS
    nblk = n // LANE
    assert n % LANE == 0 and nblk & (nblk - 1) == 0
    tn_r = min(ROUTE_TN, n)
    tn = min(MOE_TN, n)
    wr = jnp.pad(w_router, ((0, 0), (0, LANE - N_EXPERTS)))
    row = lambda v: v.reshape(1, dm)

    hm, afft, asp = pl.pallas_call(
        _route_kernel,
        grid=(nb, n // tn_r),
        in_specs=[pl.BlockSpec((None, tn_r, dm), lambda b, t: (b, t, 0)),
                  pl.BlockSpec((None, 1, dm), lambda b, t: (b, 0, 0)),
                  pl.BlockSpec((None, 1, dm), lambda b, t: (b, 0, 0)),
                  pl.BlockSpec((dm, LANE), lambda b, t: (0, 0))],
        out_specs=[pl.BlockSpec((None, tn_r, dm), lambda b, t: (b, t, 0)),
                   pl.BlockSpec((None, N_EXPERTS, tn_r), lambda b, t: (b, 0, t)),
                   pl.BlockSpec((None, tn_r, LANE), lambda b, t: (b, t, 0))],
        out_shape=[jax.ShapeDtypeStruct((nb, n, dm), BF16),
                   jax.ShapeDtypeStruct((nb, N_EXPERTS, n), F32),
                   jax.ShapeDtypeStruct((nb, n, LANE), BF16)],
        name="moe_route",
    )(h, sc, sh, wr)

    rows_total = nb * N_EXPERTS * nblk
    rb = max(SEL_ROWS, N_EXPERTS * nblk)
    assert rows_total % rb == 0
    sel_spec = pl.BlockSpec((rb, LANE), lambda i: (i, 0))
    sel_shape = jax.ShapeDtypeStruct((rows_total, LANE), jnp.int32)
    pos, row_lo, row_hi = pl.pallas_call(
        functools.partial(_select_kernel, nblk_log2=nblk.bit_length() - 1, cap=cap),
        grid=(rows_total // rb,),
        in_specs=[sel_spec],
        out_specs=[sel_spec, sel_spec, sel_spec],
        out_shape=[sel_shape, sel_shape, sel_shape],
        name="moe_select",
    )(afft.reshape(rows_total, LANE))
    pos = pos.reshape(nb, N_EXPERTS, nblk, LANE)

    tb = tn // LANE
    nsb, sb = _slot_blocks(cap)
    tile_lo = row_lo[:, 0].reshape(nb, N_EXPERTS, n // tn, tb).min(-1)[..., None]
    tile_hi = row_hi[:, 0].reshape(nb, N_EXPERTS, n // tn, tb).max(-1)[..., None]
    first = sb * jnp.arange(nsb, dtype=jnp.int32)
    hit = ((tile_hi >= first) & (tile_lo < first + sb)).astype(jnp.int32).reshape(-1)

    ys = pl.pallas_call(
        functools.partial(_ffn_kernel, cap=cap),
        grid_spec=pltpu.PrefetchScalarGridSpec(
            num_scalar_prefetch=1,
            grid=(N_EXPERTS, nb, n // tn),
            in_specs=[pl.BlockSpec((None, None, tb, LANE), lambda e, b, k, hit: (b, e, k, 0)),
                      pl.BlockSpec((None, tn, dm), lambda e, b, k, hit: (b, k, 0)),
                      pl.BlockSpec((None, tn, LANE), lambda e, b, k, hit: (b, k, 0)),
                      pl.BlockSpec((None, dm, EXPERT_FF), lambda e, b, k, hit: (e, 0, 0)),
                      pl.BlockSpec((None, dm, EXPERT_FF), lambda e, b, k, hit: (e, 0, 0)),
                      pl.BlockSpec((None, EXPERT_FF, dm), lambda e, b, k, hit: (e, 0, 0))],
            out_specs=pl.BlockSpec((None, None, cap, dm), lambda e, b, k, hit: (b, e, 0, 0)),
            scratch_shapes=[pltpu.VMEM((cap, dm), F32), pltpu.VMEM((cap, LANE), F32)]),
        out_shape=jax.ShapeDtypeStruct((nb, N_EXPERTS, cap, dm), BF16),
        compiler_params=pltpu.CompilerParams(vmem_limit_bytes=VMEM_LIMIT),
        name="moe_ffn",
    )(hit, pos, hm, asp, wg, wu, wd)

    return pl.pallas_call(
        functools.partial(_combine_kernel, cap=cap),
        grid_spec=pltpu.PrefetchScalarGridSpec(
            num_scalar_prefetch=1,
            grid=(nb, n // tn, N_EXPERTS),
            in_specs=[pl.BlockSpec((None, None, tb, LANE), lambda b, t, e, hit: (b, e, t, 0)),
                      pl.BlockSpec((None, None, cap, dm), lambda b, t, e, hit: (b, e, 0, 0)),
                      pl.BlockSpec((None, tn, dm), lambda b, t, e, hit: (b, t, 0)),
                      pl.BlockSpec((None, 1, dm), lambda b, t, e, hit: (b, 0, 0)),
                      pl.BlockSpec((1, dm), lambda b, t, e, hit: (0, 0)),
                      pl.BlockSpec((1, dm), lambda b, t, e, hit: (0, 0))],
            out_specs=pl.BlockSpec((None, tn, dm), lambda b, t, e, hit: (b, t, 0)),
            scratch_shapes=[pltpu.VMEM((tn, dm), F32)]),
        out_shape=jax.ShapeDtypeStruct((nb, n, dm), F32),
        compiler_params=pltpu.CompilerParams(vmem_limit_bytes=VMEM_LIMIT),
        name="moe_combine",
    )(hit, pos, ys, h, g2, row(ln_g), row(ln_b))


N_HEADS = 4
STACK = N_HEADS * CHUNK
CHUNK_LOG2 = CHUNK.bit_length() - 1
SOLVE_LEVELS = CHUNK_LOG2


def _stack_heads(x):
    return jnp.concatenate([x[:, h * HEAD_DIM:(h + 1) * HEAD_DIM] for h in range(N_HEADS)], axis=0)


def _stack_cols(cols, width):
    return jnp.concatenate([jnp.broadcast_to(c, (CHUNK, width)) for c in cols], axis=0)


def _mxu(a, b, dims, hp):
    dg = lambda x, y: lax.dot_general(x, y, (dims, ((), ())), preferred_element_type=F32)
    a_hi, b_hi = a.astype(BF16), b.astype(BF16)
    if not hp:
        return dg(a_hi, b_hi)
    a_lo = (a - a_hi.astype(F32)).astype(BF16)
    b_lo = (b - b_hi.astype(F32)).astype(BF16)
    return dg(a_hi, b_hi) + (dg(a_lo, b_hi) + dg(a_hi, b_lo))


def _dot_nt(a, b, hp=False):
    return _mxu(a, b, ((1,), (1,)), hp)


def _dot_tn(a, b, hp=False):
    return _mxu(a, b, ((0,), (0,)), hp)


def _dot(a, b, hp=False):
    return _mxu(a, b, ((1,), (0,)), hp)


def _chunk_masks(rev):
    ri = lax.broadcasted_iota(jnp.int32, (STACK, STACK), 0)
    ci = lax.broadcasted_iota(jnp.int32, (STACK, STACK), 1)
    same = (ri >> CHUNK_LOG2) == (ci >> CHUNK_LOG2)
    ahead = jnp.where(rev, ci - ri, ri - ci)
    return same & (ahead >= 0), same & (ahead > 0)


def _scan_cumsum(gt, rev):
    ii = lax.broadcasted_iota(jnp.int32, (CHUNK, CHUNK), 0)
    jj = lax.broadcasted_iota(jnp.int32, (CHUNK, CHUNK), 1)
    tri = jnp.where(jnp.where(rev, jj - ii, ii - jj) >= 0, 1.0, 0.0)
    return jnp.dot(tri, gt, precision=lax.Precision.HIGHEST, preferred_element_type=F32)


def _unit_triangular_inverse(a):
    ri = lax.broadcasted_iota(jnp.int32, a.shape, 0)
    ci = lax.broadcasted_iota(jnp.int32, a.shape, 1)
    joins = lambda lvl: ((ri >> (lvl + 1)) == (ci >> (lvl + 1))) & ((ri >> lvl) != (ci >> lvl))
    t = jnp.where(ri == ci, 1.0, 0.0) - jnp.where(joins(0), a, 0.0)
    for lvl in range(1, CHUNK_LOG2):
        m = _dot(jnp.where(joins(lvl), a, 0.0), t)
        yield
        t = t - _dot(t, m)
        yield
    return t


def _interleave(chains):
    for _ in zip(*chains):
        pass


def _linear_scan_kernel(*refs, has_beta, k_scale):
    if has_beta:
        q_ref, k_ref, v_ref, gt_ref, gp_ref, s0_ref, o_ref, sfin_ref, s_scr = refs
    else:
        q_ref, k_ref, v_ref, gp_ref, s0_ref, o_ref, sfin_ref, s_scr = refs
    rev = pl.program_id(0) == 1
    n = pl.program_id(2)

    @pl.when(n == 0)
    def _():
        s_scr[...] = s0_ref[...]

    chains = []
    for i in range(q_ref.shape[0]):
        if has_beta:
            gt = _gdn_gates(gt_ref[i], gp_ref[...])
        else:
            gt = jnp.broadcast_to(gp_ref[0:1, :], (CHUNK, LANE))
        chains.append(_linear_chunk(q_ref.at[i], k_ref.at[i], v_ref.at[i], gt, o_ref.at[i], s_scr.at[i], rev,
                                    has_beta, k_scale))
    _interleave(chains)

    @pl.when(n == pl.num_programs(2) - 1)
    def _():
        sfin_ref[...] = s_scr[...]


def _softplus(x):
    return jnp.maximum(x, 0.0) + jnp.log1p(jnp.exp(-jnp.abs(x)))


def _gdn_gates(raw, gp):
    lane = lax.broadcasted_iota(jnp.int32, raw.shape, 1)
    return jnp.where(lane < 2 * N_HEADS, gp[0:1, :] * _softplus(raw + gp[1:2, :]), jax.nn.sigmoid(raw))


def _mlstm_gates(raw, gp):
    lane = lax.broadcasted_iota(jnp.int32, raw.shape, 1)
    x = raw + gp[0:1, :]
    return jnp.where(lane < 2 * N_HEADS, x, -_softplus(-x))


def _linear_chunk(q_ref, k_ref, v_ref, gt, o_ref, s_scr, rev, has_beta, k_scale):
    cum = _scan_cumsum(gt, rev)
    gcols, bcols, gtots = [], [], []
    for h in range(N_HEADS):
        gc = jnp.where(rev, cum[:, N_HEADS + h:N_HEADS + h + 1], cum[:, h:h + 1])
        gcols.append(gc)
        gtots.append(jnp.where(rev, gc[0:1], gc[CHUNK - 1:CHUNK]))
        bcols.append(jnp.where(rev, gt[:, 3 * N_HEADS + h:3 * N_HEADS + h + 1], gt[:, 2 * N_HEADS + h:2 * N_HEADS + h + 1]))
    cb = _stack_cols(gcols, STACK)
    diff = cb - cb.T
    incl, strict = _chunk_masks(rev)
    dec = jnp.exp(jnp.where(incl, diff, 0.0))
    gcb = cb[:, :HEAD_DIM]
    q_st, k_st, v_st = _stack_heads(q_ref[...]), _stack_heads(k_ref[...]) * k_scale, _stack_heads(v_ref[...])
    a_qk = _dot_nt(q_st, k_st) * jnp.where(incl, dec, 0.0)
    if has_beta:
        beta = _stack_cols(bcols, HEAD_DIM)
        kb = k_st * beta
        a = _dot_nt(kb, k_st) * jnp.where(strict, dec, 0.0)
        yield
        t_inv = yield from _unit_triangular_inverse(a)
        x = _dot(t_inv, jnp.concatenate([v_st * beta, kb * jnp.exp(gcb)], axis=1))
        u_st, w_st = x[:, :HEAD_DIM], x[:, HEAD_DIM:]
    else:
        u_st, w_st = v_st, None
    gtot = _stack_cols(gtots, HEAD_DIM)
    k_end = k_st * jnp.exp(gtot - gcb)
    q_dec = q_st * jnp.exp(gcb)
    hs = lambda t, h: t[h * CHUNK:(h + 1) * CHUNK]
    states = [s_scr[h] for h in range(N_HEADS)]
    q_s = [_dot(hs(q_dec, h), states[h]) for h in range(N_HEADS)]
    yield
    if has_beta:
        vn = jnp.concatenate([hs(u_st, h) - _dot(hs(w_st, h), states[h]) for h in range(N_HEADS)], axis=0)
    else:
        vn = u_st
    yield
    o_st = _dot(a_qk, vn)
    for h in range(N_HEADS):
        s_scr[h] = jnp.exp(gtots[h]) * states[h] + _dot_tn(hs(k_end, h), hs(vn, h))
    yield
    for h in range(N_HEADS):
        o_ref[:, h * HEAD_DIM:(h + 1) * HEAD_DIM] = hs(o_st, h) + q_s[h]
    yield


SCAN_NS = 4


HEADS_W = N_HEADS * HEAD_DIM


def _scan_specs(srcs, gates, nb, nchunk):
    ns = SCAN_NS
    assert nb % ns == 0
    cidx = lambda d, n: n + d * (nchunk - 1 - 2 * n)
    specs = [pl.BlockSpec((ns, CHUNK, HEADS_W), lambda d, b, n, c=c: (b, cidx(d, n), c)) for _, c in srcs]
    if gates is not None:
        specs.append(pl.BlockSpec((ns, CHUNK, LANE), lambda d, b, n, c=gates[1]: (b, cidx(d, n), c)))
    specs.append(pl.BlockSpec((2, LANE), lambda d, b, n: (0, 0)))
    ospec = pl.BlockSpec((None, ns, CHUNK, HEADS_W), lambda d, b, n: (d, b, cidx(d, n), 0))
    return specs, ospec


def linear_scan_bidir(q, k, v, gates, gp, s0, has_beta, k_scale=1.0):
    nb, t, _ = q[0].shape
    nchunk = t // CHUNK
    assert t % CHUNK == 0
    ns = SCAN_NS
    specs, ospec = _scan_specs((q, k, v), gates if has_beta else None, nb, nchunk)
    sspec = pl.BlockSpec((None, ns, N_HEADS, HEAD_DIM, HEAD_DIM), lambda d, b, n: (d, b, 0, 0, 0))
    args = [q[0], k[0], v[0]] + ([gates[0]] if has_beta else []) + [gp, s0]
    return pl.pallas_call(
        functools.partial(_linear_scan_kernel, has_beta=has_beta, k_scale=k_scale),
        grid=(2, nb // ns, nchunk),
        in_specs=specs + [sspec],
        out_specs=[ospec, sspec],
        out_shape=[jax.ShapeDtypeStruct((2, nb, t, HEADS_W), F32),
                   jax.ShapeDtypeStruct((2, nb, N_HEADS, HEAD_DIM, HEAD_DIM), F32)],
        scratch_shapes=[pltpu.VMEM((ns, N_HEADS, HEAD_DIM, HEAD_DIM), F32)],
        name="gdn_scan" if has_beta else "retention_scan",
    )(*args)


def linear_scan_two_pass(ctx_args, lat_args, gp, has_beta, k_scale=1.0):
    nb = ctx_args[0][0].shape[0]
    zero = jnp.zeros((2, nb, N_HEADS, HEAD_DIM, HEAD_DIM), F32)
    o_ctx, s_ctx = linear_scan_bidir(*ctx_args, gp, zero, has_beta, k_scale)
    o_lat, _ = linear_scan_bidir(*lat_args, gp, s_ctx, has_beta, k_scale)
    return o_ctx, o_lat


MLSTM_HP = True


def _mlstm_scan_kernel(q_ref, k_ref, v_ref, gt_ref, gp_ref, c0_ref, n0_ref, m0_ref, o_ref, cfin_ref, nfin_ref,
                       mfin_ref, c_scr, n_scr, m_scr):
    rev = pl.program_id(0) == 1
    step = pl.program_id(2)

    @pl.when(step == 0)
    def _():
        c_scr[...] = c0_ref[...]
        n_scr[...] = n0_ref[...]
        m_scr[...] = m0_ref[...]

    _interleave([_mlstm_chunk(q_ref.at[i], k_ref.at[i], v_ref.at[i], _mlstm_gates(gt_ref[i], gp_ref[...]),
                              o_ref.at[i], c_scr.at[i], n_scr.at[i], m_scr.at[i], rev)
                 for i in range(q_ref.shape[0])])

    @pl.when(step == pl.num_programs(2) - 1)
    def _():
        cfin_ref[...] = c_scr[...]
        nfin_ref[...] = n_scr[...]
        mfin_ref[...] = m_scr[...]


def _mlstm_chunk(q_ref, k_ref, v_ref, gt, o_ref, c_scr, n_scr, m_scr, rev):
    cum = _scan_cumsum(gt, rev)
    pick = lambda t, c: jnp.where(rev, t[:, N_HEADS + c:N_HEADS + c + 1], t[:, c:c + 1])
    q_st, k_st, v_st = _stack_heads(q_ref[...]), _stack_heads(k_ref[...]), _stack_heads(v_ref[...])
    hs = lambda t, h: t[h * CHUNK:(h + 1) * CHUNK]
    qk = _dot_nt(q_st, k_st, MLSTM_HP)
    yield
    bcums, srcs, inters, qcs, qns = [], [], [], [], []
    for h in range(N_HEADS):
        ic = pick(gt, h)
        bcum = pick(cum, 2 * N_HEADS + h)
        b_end = jnp.where(rev, bcum[0:1], bcum[CHUNK - 1:CHUNK])
        c_prev, n_prev, m_prev = c_scr[h], n_scr[h], m_scr[h][:, :1]
        a = b_end - bcum + ic
        m_new = jnp.maximum(b_end + m_prev, jnp.max(a, axis=0, keepdims=True))
        w_state = jnp.exp(a - m_new)
        decay = jnp.exp(b_end + m_prev - m_new)
        kw = hs(k_st, h) * w_state
        c_scr[h] = decay * c_prev + _dot_tn(kw, hs(v_st, h), MLSTM_HP)
        n_scr[h] = decay * n_prev + jnp.sum(kw, axis=0, keepdims=True)
        m_scr[h] = jnp.broadcast_to(m_new, (1, HEAD_DIM))
        bcums.append(bcum)
        srcs.append(bcum - ic)
        inters.append(bcum + m_prev)
        qcs.append(_dot(hs(q_st, h), c_prev, MLSTM_HP))
        qns.append(jnp.sum(hs(q_st, h) * n_prev, axis=1, keepdims=True))
    yield
    incl, _ = _chunk_masks(rev)
    dlog = _stack_cols(bcums, STACK) - _stack_cols(srcs, STACK).T
    inter = jnp.concatenate(inters, axis=0)
    m_t = jnp.maximum(inter, jnp.max(jnp.where(incl, dlog, -1e30), axis=1, keepdims=True))
    s = qk * jnp.where(incl, jnp.exp(jnp.where(incl, dlog, 0.0) - m_t), 0.0)
    w_inter = jnp.exp(inter - m_t)
    yield
    num = _dot(s, v_st, MLSTM_HP) + w_inter * jnp.concatenate(qcs, axis=0)
    den = jnp.sum(s, axis=1, keepdims=True) + w_inter * jnp.concatenate(qns, axis=0)
    yield
    out = num / jnp.maximum(jnp.abs(den), jnp.exp(-m_t))
    for h in range(N_HEADS):
        o_ref[:, h * HEAD_DIM:(h + 1) * HEAD_DIM] = hs(out, h)
    yield


def mlstm_scan_bidir(q, k, v, gates, gp, state):
    nb, t, _ = q[0].shape
    nchunk = t // CHUNK
    assert t % CHUNK == 0
    ns = SCAN_NS
    specs, ospec = _scan_specs((q, k, v), gates, nb, nchunk)
    cspec = pl.BlockSpec((None, ns, N_HEADS, HEAD_DIM, HEAD_DIM), lambda d, b, n: (d, b, 0, 0, 0))
    vspec = pl.BlockSpec((None, ns, N_HEADS, 1, HEAD_DIM), lambda d, b, n: (d, b, 0, 0, 0))
    cshape = jax.ShapeDtypeStruct((2, nb, N_HEADS, HEAD_DIM, HEAD_DIM), F32)
    vshape = jax.ShapeDtypeStruct((2, nb, N_HEADS, 1, HEAD_DIM), F32)
    o, c, n, m = pl.pallas_call(
        _mlstm_scan_kernel,
        grid=(2, nb // ns, nchunk),
        in_specs=specs + [cspec, vspec, vspec],
        out_specs=[ospec, cspec, vspec, vspec],
        out_shape=[jax.ShapeDtypeStruct((2, nb, t, HEADS_W), F32), cshape, vshape, vshape],
        scratch_shapes=[pltpu.VMEM((ns, N_HEADS, HEAD_DIM, HEAD_DIM), F32), pltpu.VMEM((ns, N_HEADS, 1, HEAD_DIM), F32),
                        pltpu.VMEM((ns, N_HEADS, 1, HEAD_DIM), F32)],
        name="mlstm_scan",
    )(q[0], k[0], v[0], gates[0], gp, *state)
    return o, (c, n, m)


def mlstm_two_pass(ctx_args, lat_args, gp):
    nb = ctx_args[0][0].shape[0]
    zero = (jnp.zeros((2, nb, N_HEADS, HEAD_DIM, HEAD_DIM), F32), jnp.zeros((2, nb, N_HEADS, 1, HEAD_DIM), F32),
            jnp.zeros((2, nb, N_HEADS, 1, HEAD_DIM), F32))
    o_ctx, s_ctx = mlstm_scan_bidir(*ctx_args, gp, zero)
    o_lat, _ = mlstm_scan_bidir(*lat_args, gp, s_ctx)
    return o_ctx, o_lat


PROJ_TM = 512
PROJ_TN_MAX = 2304


def _inproj_kernel(h_ref, sc_ref, sh_ref, w_ref, o_ref):
    hm = (h_ref[...] * (1.0 + sc_ref[...]) + sh_ref[...]).astype(BF16)
    o_ref[...] = jnp.dot(hm, w_ref[...], preferred_element_type=F32)


def in_projection(h, sc, sh, w):
    nb, n, dm = h.shape
    ncol = w.shape[1]
    tm = min(PROJ_TM, n)
    tn = max(c for c in range(LANE, min(ncol, PROJ_TN_MAX) + 1, LANE) if ncol % c == 0)
    return pl.pallas_call(
        _inproj_kernel,
        grid=(nb, n // tm, ncol // tn),
        in_specs=[pl.BlockSpec((None, tm, dm), lambda b, t, j: (b, t, 0)),
                  pl.BlockSpec((None, 1, dm), lambda b, t, j: (b, 0, 0)),
                  pl.BlockSpec((None, 1, dm), lambda b, t, j: (b, 0, 0)),
                  pl.BlockSpec((dm, tn), lambda b, t, j: (0, j))],
        out_specs=pl.BlockSpec((None, tm, tn), lambda b, t, j: (b, t, j)),
        out_shape=jax.ShapeDtypeStruct((nb, n, ncol), F32),
        name="in_projection",
    )(h, sc, sh, w)


def _conv_kernel(x_ref, w_ref, o_ref, *, grid_w, l2_blocks, q_scale, k_scale):
    x = x_ref[...]
    t = x.shape[0]
    w = w_ref[...]
    col = lax.broadcasted_iota(jnp.int32, x.shape, 0) & (grid_w - 1)
    left = jnp.where(col == 0, 0.0, pltpu.roll(x, 1, 0))
    right = jnp.where(col == grid_w - 1, 0.0, pltpu.roll(x, t - 1, 0))
    row = lambda kh: w[3 * kh:3 * kh + 1] * left + w[3 * kh + 1:3 * kh + 2] * x + w[3 * kh + 2:3 * kh + 3] * right
    acc = row(1)
    if grid_w < t:
        zero = jnp.zeros((grid_w, LANE), F32)
        acc = acc + jnp.concatenate([zero, row(0)[:t - grid_w]], axis=0) + jnp.concatenate([row(2)[grid_w:], zero], axis=0)
    y = acc * jax.nn.sigmoid(acc)
    c = pl.program_id(1)
    normed = y * lax.rsqrt(jnp.sum(y * y, axis=1, keepdims=True) + 1e-6)
    y = jnp.where(c < l2_blocks, normed, y)
    o_ref[...] = y * jnp.where(c < N_HEADS, q_scale, jnp.where(c < 2 * N_HEADS, k_scale, 1.0))


def conv_prep(proj, conv_w, nblk, grid_w, l2_blocks, q_scale, k_scale):
    nb, t, _ = proj.shape
    assert grid_w & (grid_w - 1) == 0 and t % grid_w == 0
    return pl.pallas_call(
        functools.partial(_conv_kernel, grid_w=grid_w, l2_blocks=l2_blocks, q_scale=q_scale, k_scale=k_scale),
        grid=(nb, nblk),
        in_specs=[pl.BlockSpec((None, t, LANE), lambda b, c: (b, 0, c)),
                  pl.BlockSpec((CONV_K * CONV_K, LANE), lambda b, c: (0, c))],
        out_specs=pl.BlockSpec((None, t, LANE), lambda b, c: (b, 0, c)),
        out_shape=jax.ShapeDtypeStruct((nb, t, nblk * LANE), F32),
        compiler_params=pltpu.CompilerParams(vmem_limit_bytes=VMEM_LIMIT),
        name="conv_prep",
    )(proj, conv_w.reshape(CONV_K * CONV_K, -1))


def _head_norm(o, center):
    outs = []
    for h in range(N_HEADS):
        x = o[:, h * HEAD_DIM:(h + 1) * HEAD_DIM]
        if center:
            x = x - jnp.mean(x, axis=1, keepdims=True)
        outs.append(x * lax.rsqrt(jnp.mean(x * x, axis=1, keepdims=True) + EPS))
    return jnp.concatenate(outs, axis=1)


def _mix_out(y_a, y_b, wo_ref, h_ref, g1_ref, lng_ref, lnb_ref, o_ref):
    y = (jnp.dot(y_a.astype(BF16), wo_ref[:HEADS_W, :], preferred_element_type=F32)
         + jnp.dot(y_b.astype(BF16), wo_ref[HEADS_W:, :], preferred_element_type=F32))
    o_ref[...] = _ln_rows(ALPHA * h_ref[...] + g1_ref[...] * y, lng_ref[...], lnb_ref[...])


def _merge_even_kernel(og_ref, or_ref, za_ref, zr_ref, gg_ref, rg_ref, wo_ref, h_ref, g1_ref, lng_ref, lnb_ref, o_ref):
    za, zr = za_ref[...], zr_ref[...]
    y_g = _head_norm(og_ref[0] + og_ref[1], False) * gg_ref[...] * (za * jax.nn.sigmoid(za))
    y_r = _head_norm(or_ref[0] + or_ref[1], True) * rg_ref[...] * (zr * jax.nn.sigmoid(zr))
    _mix_out(y_g, y_r, wo_ref, h_ref, g1_ref, lng_ref, lnb_ref, o_ref)


def _merge_odd_kernel(om_ref, ys_ref, og_ref, u_ref, mg_ref, dsk_ref, bglu_ref, wglu_ref, wo_ref, h_ref, g1_ref,
                      lng_ref, lnb_ref, o_ref):
    y_m = _head_norm(om_ref[0] + om_ref[1], True) * mg_ref[...] * jax.nn.sigmoid(og_ref[...])
    y = jax.nn.gelu(ys_ref[...] + dsk_ref[...] * u_ref[...])
    y = y * jax.nn.sigmoid(jnp.dot(y.astype(BF16), wglu_ref[...], preferred_element_type=F32) + bglu_ref[...])
    _mix_out(y_m, y, wo_ref, h_ref, g1_ref, lng_ref, lnb_ref, o_ref)


def _merge_call(kernel_fn, name, scans, toks, rows, mats, h, g1, ln_g, ln_b):
    nb, n, dm = h.shape
    tm = min(PROJ_TM, n)
    full = lambda a: pl.BlockSpec(a.shape, lambda b, t: (0,) * a.ndim)
    rowv = lambda v: v.reshape(1, -1)
    specs, args = [], []
    for a in scans:
        specs.append(pl.BlockSpec((2, None, tm, HEADS_W), lambda b, t: (0, b, t, 0)))
        args.append(a)
    for a, c in toks:
        specs.append(pl.BlockSpec((None, tm, HEADS_W), lambda b, t, c=c: (b, t, c)))
        args.append(a)
    for v in rows:
        args.append(rowv(v))
        specs.append(full(args[-1]))
    for m in mats:
        args.append(m)
        specs.append(full(m))
    args += [h, g1, rowv(ln_g), rowv(ln_b)]
    specs += [pl.BlockSpec((None, tm, dm), lambda b, t: (b, t, 0)), pl.BlockSpec((None, 1, dm), lambda b, t: (b, 0, 0)),
              full(args[-2]), full(args[-1])]
    return pl.pallas_call(
        kernel_fn,
        grid=(nb, n // tm),
        in_specs=specs,
        out_specs=pl.BlockSpec((None, tm, dm), lambda b, t: (b, t, 0)),
        out_shape=jax.ShapeDtypeStruct((nb, n, dm), F32),
        compiler_params=pltpu.CompilerParams(vmem_limit_bytes=VMEM_LIMIT),
        name=name,
    )(*args)


def split_cols(t, sizes):
    return jnp.split(t, np.cumsum(sizes)[:-1].tolist(), axis=-1)


def heads(t, n):
    return t.reshape(t.shape[:-1] + (n, t.shape[-1] // n))


def flip_t(t):
    return None if t is None else jnp.flip(t, axis=1)


def l2norm(t):
    return t * lax.rsqrt(jnp.sum(t * t, axis=-1, keepdims=True) + 1e-6)


def layer_norm(t, g, b):
    mu = t.mean(-1, keepdims=True)
    var = jnp.square(t - mu).mean(-1, keepdims=True)
    return ((t - mu) * lax.rsqrt(var + EPS)) * g + b


def rms_norm_heads(o, g):
    y = o * lax.rsqrt(jnp.mean(o * o, axis=-1, keepdims=True) + EPS) * g
    return y.reshape(o.shape[:2] + (-1,))


def group_norm_heads(o, g):
    mu = o.mean(-1, keepdims=True)
    var = jnp.square(o - mu).mean(-1, keepdims=True)
    return ((o - mu) * lax.rsqrt(var + EPS)).reshape(o.shape[:2] + (-1,)) * g


def short_conv(t, w, on_grid):
    ch = t.shape[-1]
    if on_grid:
        b, n = t.shape[:2]
        rows = n // GRID_W
        tg = t.reshape(b, rows, GRID_W, ch)
        y = lax.conv_general_dilated(tg, w[:, :, None, :], (1, 1), 'SAME',
                                     dimension_numbers=('NHWC', 'HWIO', 'NHWC'), feature_group_count=ch)
        return y.reshape(b, n, ch)
    return lax.conv_general_dilated(t, w[CONV_K // 2][:, None, :], (1,), 'SAME',
                                    dimension_numbers=('NWC', 'WIO', 'NWC'), feature_group_count=ch)


def to_chunks(t):
    b, n, h = t.shape[:3]
    t = t.reshape((b, n // CHUNK, CHUNK, h) + t.shape[3:])
    return jnp.moveaxis(jnp.moveaxis(t, 3, 1), 2, 0)


def from_chunks(t):
    t = jnp.moveaxis(jnp.moveaxis(t, 0, 2), 1, 3)
    return t.reshape((t.shape[0], -1) + t.shape[3:])


def linear_scan(q, k, v, g, beta, s0, want_out):
    b, _, h, dk = k.shape
    dv = v.shape[-1]
    kc, vc = to_chunks(k), to_chunks(v)
    gcum = jnp.cumsum(to_chunks(g), axis=-1)
    diff = gcum[..., :, None] - gcum[..., None, :]
    incl = jnp.tril(jnp.ones((CHUNK, CHUNK), bool))
    if beta is None:
        u, w = vc, None
    else:
        bc = to_chunks(beta)[..., None]
        kb = kc * bc
        strict = jnp.tril(jnp.ones((CHUNK, CHUNK), bool), -1)
        a = jnp.where(strict, jnp.einsum('...ik,...jk->...ij', kb, kc) * jnp.exp(jnp.where(strict, diff, 0.0)), 0.0)
        rhs = jnp.concatenate([vc * bc, kb * jnp.exp(gcum)[..., None]], axis=-1)
        sol = lax.linalg.triangular_solve(a + jnp.eye(CHUNK, dtype=a.dtype), rhs, left_side=True,
                                          lower=True, unit_diagonal=True)
        u, w = sol[..., :dv], sol[..., dv:]
    k_end = kc * jnp.exp(gcum[..., -1:] - gcum)[..., None]
    g_end = jnp.exp(gcum[..., -1])[..., None, None]
    if want_out:
        qc = to_chunks(q)
        q_dec = qc * jnp.exp(gcum)[..., None]
        a_qk = jnp.where(incl, jnp.einsum('...ik,...jk->...ij', qc, kc) * jnp.exp(jnp.where(incl, diff, 0.0)), 0.0)
    else:
        q_dec, a_qk = None, None
    if s0 is None:
        s0 = jnp.zeros((b, h, dk, dv), F32)

    def step(s, inp):
        qd, ke, uc, wc, aqk, ge = inp
        vn = uc if wc is None else uc - jnp.einsum('bhck,bhkv->bhcv', wc, s)
        s_new = ge * s + jnp.einsum('bhck,bhcv->bhkv', ke, vn)
        if qd is None:
            return s_new, None
        return s_new, jnp.einsum('bhck,bhkv->bhcv', qd, s) + jnp.einsum('bhij,bhjv->bhiv', aqk, vn)

    s_fin, o = lax.scan(step, s0, (q_dec, k_end, u, w, a_qk, g_end))
    return (from_chunks(o) if want_out else None), s_fin


def mlstm_scan(q, k, v, log_i, log_f, s0, want_out):
    b, _, h, dk = k.shape
    dv = v.shape[-1]
    if s0 is None:
        s0 = (jnp.zeros((b, h, dk, dv), F32), jnp.zeros((b, h, dk), F32), jnp.zeros((b, h), F32))
    incl = jnp.tril(jnp.ones((CHUNK, CHUNK), bool))

    def step(carry, inp):
        c_prev, n_prev, m_prev = carry
        qc, kc, vc, ic, fc = inp
        bcum = jnp.cumsum(fc, axis=-1)
        b_end = bcum[..., -1]
        a = b_end[..., None] - bcum + ic
        m_new = jnp.maximum(b_end + m_prev, a.max(-1))
        w_state = jnp.exp(a - m_new[..., None])
        decay = jnp.exp(b_end + m_prev - m_new)
        c_new = decay[..., None, None] * c_prev + jnp.einsum('bhck,bhcv->bhkv', kc * w_state[..., None], vc)
        n_new = decay[..., None] * n_prev + jnp.einsum('bhck,bhc->bhk', kc, w_state)
        carry_new = (c_new, n_new, m_new)
        if qc is None:
            return carry_new, None
        dlog = jnp.where(incl, bcum[..., :, None] - bcum[..., None, :] + ic[..., None, :], -jnp.inf)
        inter = bcum + m_prev[..., None]
        m_t = jnp.maximum(inter, dlog.max(-1))
        s = jnp.einsum('bhik,bhjk->bhij', qc, kc) * jnp.exp(dlog - m_t[..., None])
        w_inter = jnp.exp(inter - m_t)[..., None]
        num = jnp.einsum('bhij,bhjv->bhiv', s, vc) + w_inter * jnp.einsum('bhik,bhkv->bhiv', qc, c_prev)
        den = s.sum(-1, keepdims=True) + w_inter * jnp.einsum('bhik,bhk->bhi', qc, n_prev)[..., None]
        return carry_new, num / jnp.maximum(jnp.abs(den), jnp.exp(-m_t)[..., None])

    xs = (to_chunks(q) if want_out else None, to_chunks(k), to_chunks(v), to_chunks(log_i), to_chunks(log_f))
    s_fin, hs = lax.scan(step, s0, xs)
    return (from_chunks(hs) if want_out else None), s_fin


def two_pass(scan_fn, ctx_args, lat_args, ctx_out, reverse):
    if reverse:
        ctx_args = [flip_t(t) for t in ctx_args]
        lat_args = [flip_t(t) for t in lat_args]
    o_ctx, s_ctx = scan_fn(*ctx_args, None, ctx_out)
    o_lat, _ = scan_fn(*lat_args, s_ctx, True)
    if reverse:
        o_ctx, o_lat = flip_t(o_ctx), flip_t(o_lat)
    return o_ctx, o_lat


def retention_log_decay(direction):
    expo = 5.0 + 2.0 * jnp.arange(RET_HEADS, dtype=F32) + direction
    return jnp.log1p(-jnp.exp2(-expo))


GATE_COLS = 4 * N_HEADS
Q_SCALE = HEAD_DIM ** -0.5


def _gate_row(*vals):
    v = jnp.concatenate([jnp.ravel(x) for x in vals])
    return jnp.pad(v, (0, LANE - v.shape[0]))


def _pad_gate_cols(w):
    return jnp.pad(w, ((0, 0), (0, LANE - w.shape[1])))


def gdn_retention_mixer(hs_, mods, w_in, w_out, conv_w, a_log, dt_bias, gdn_gain, ret_gain, ln_g, ln_b, ctx_out):
    w = jnp.concatenate([w_in[:, :4 * GDN_W], w_in[:, 4 * GDN_W + GATE_COLS:],
                         _pad_gate_cols(w_in[:, 4 * GDN_W:4 * GDN_W + GATE_COLS])], axis=1).astype(BF16)
    gate_blk = (4 * GDN_W + 4 * RET_W) // LANE
    gp_gdn = jnp.stack([_gate_row(-jnp.exp(a_log)), _gate_row(dt_bias)])
    gp_ret = jnp.stack([_gate_row(retention_log_decay(0), retention_log_decay(1)), jnp.zeros((LANE,), F32)])
    wo = w_out.astype(BF16)
    projs, convs = [], []
    for (h, (sc, sh, _)), grid_w in zip(zip(hs_, mods), (hs_[0].shape[1], GRID_W)):
        p = in_projection(h, sc, sh, w)
        projs.append(p)
        convs.append(conv_prep(p, conv_w, 3 * GDN_W // LANE, grid_w, 2 * N_HEADS, Q_SCALE, 1.0))
    gdn_args = [((cv, 0), (cv, 1), (cv, 2), (p, gate_blk)) for p, cv in zip(projs, convs)]
    ret_args = [((p, 4), (p, 5), (p, 6), None) for p in projs]
    og = linear_scan_two_pass(gdn_args[0], gdn_args[1], gp_gdn, True)
    orr = linear_scan_two_pass(ret_args[0], ret_args[1], gp_ret, False, Q_SCALE)
    outs = []
    for i in range(2):
        if i == 0 and not ctx_out:
            outs.append(None)
            continue
        outs.append(_merge_call(_merge_even_kernel, "merge_even", [og[i], orr[i]], [(projs[i], 3), (projs[i], 7)],
                                [jnp.tile(gdn_gain, N_HEADS), ret_gain], [wo], hs_[i], mods[i][2], ln_g, ln_b))
    return outs


def mlstm_s5_mixer(hs_, mods, w_in, w_out, conv_w, gate_bias, mlstm_gain, lam_re, lam_im, log_dt,
                   b_re, b_im, c_re, c_im, d_skip, w_glu, b_glu, ln_g, ln_b, ctx_out):
    w = jnp.concatenate([w_in[:, :4 * MLSTM_W], _pad_gate_cols(w_in[:, 4 * MLSTM_W:4 * MLSTM_W + GATE_COLS])],
                        axis=1).astype(BF16)
    w_u = w_in[:, 4 * MLSTM_W + GATE_COLS:].astype(BF16)
    gate_blk = 4 * MLSTM_W // LANE
    gp = jnp.stack([_gate_row(gate_bias[0, 0], gate_bias[1, 0], gate_bias[0, 1], gate_bias[1, 1]),
                    jnp.zeros((LANE,), F32)])
    wo, wglu = w_out.astype(BF16), w_glu.astype(BF16)
    projs, us, args = [], [], []
    for (h, (sc, sh, _)), grid_w in zip(zip(hs_, mods), (hs_[0].shape[1], GRID_W)):
        p = in_projection(h, sc, sh, w)
        cv = conv_prep(p, conv_w, 2 * MLSTM_W // LANE, grid_w, 0, 1.0, Q_SCALE)
        projs.append(p)
        us.append(in_projection(h, sc, sh, w_u))
        args.append(((cv, 0), (cv, 1), (p, 2), (p, gate_blk)))
    om = mlstm_two_pass(args[0], args[1], gp)
    ys = s5_bidirectional(us[0], us[1], _s5_weights(lam_re, lam_im, log_dt, b_re, b_im, c_re, c_im))
    outs = []
    for i in range(2):
        if i == 0 and not ctx_out:
            outs.append(None)
            continue
        outs.append(_merge_call(_merge_odd_kernel, "merge_odd", [om[i]], [(ys[i], 0), (projs[i], 3), (us[i], 0)],
                                [mlstm_gain, d_skip, b_glu], [wglu, wo], hs_[i], mods[i][2], ln_g, ln_b))
    return outs


def kernel(x, c, ctx, c_ctx, w_mod, b_mod, ln1_g, ln1_b, ln2_g, ln2_b, w_router, w_gate, w_up, w_down,
           ev_w_in, ev_w_out, ev_conv, ev_a_log, ev_dt_bias, ev_gdn_norm, ev_ret_norm,
           od_w_in, od_w_out, od_conv, od_gate_bias, od_mlstm_norm, od_lam_re, od_lam_im, od_log_dt,
           od_b_re, od_b_im, od_c_re, od_c_im, od_d_skip, od_w_glu, od_b_glu):
    h_lat, h_ctx = x, ctx
    s_lat = jax.nn.silu(c)
    s_ctx = jax.nn.silu(c_ctx)
    for l in range(DEPTH):
        last = l == DEPTH - 1
        sh1, sc1, g1, sh2, sc2, g2 = jnp.split((s_lat @ w_mod[l] + b_mod[l])[:, None, :], 6, axis=-1)
        bc = lambda v: jnp.broadcast_to(v, (BATCH, 1, D_MODEL))
        csh1, csc1, cg1, csh2, csc2, cg2 = [bc(v) for v in jnp.split(s_ctx @ w_mod[l] + b_mod[l], 6, axis=-1)]
        streams = (h_ctx, h_lat)
        mods = ((csc1, csh1, cg1), (sc1, sh1, g1))
        if l % 2 == 0:
            e = l // 2
            h_ctx, h_lat = gdn_retention_mixer(streams, mods, ev_w_in[e], ev_w_out[e], ev_conv[e], ev_a_log[e],
                                               ev_dt_bias[e], ev_gdn_norm[e], ev_ret_norm[e], ln1_g[l], ln1_b[l],
                                               not last)
        else:
            o = l // 2
            h_ctx, h_lat = mlstm_s5_mixer(streams, mods, od_w_in[o], od_w_out[o], od_conv[o], od_gate_bias[o],
                                          od_mlstm_norm[o], od_lam_re[o], od_lam_im[o], od_log_dt[o],
                                          od_b_re[o], od_b_im[o], od_c_re[o], od_c_im[o], od_d_skip[o],
                                          od_w_glu[o], od_b_glu[o], ln1_g[l], ln1_b[l], not last)
        experts = (w_router[l], w_gate[l].astype(BF16), w_up[l].astype(BF16), w_down[l].astype(BF16))
        h_lat = moe_block(h_lat, sc2, sh2, g2, ln2_g[l], ln2_b[l], *experts)
        if not last:
            h_ctx = moe_block(h_ctx, csc2, csh2, cg2, ln2_g[l], ln2_b[l], *experts)
    return h_lat
```

```python
import functools
import math

import jax
import jax.numpy as jnp
import numpy as np
from jax import lax
from jax.experimental import pallas as pl
from jax.experimental.pallas import tpu as pltpu

D_MODEL = 1024
BATCH = 4
SEQ = 4096
DEPTH = 4
GRID_W = 64
CTX_LEN = 256
CHUNK = 64
CONV_K = 3
HEAD_DIM = D_MODEL // 8
GDN_HEADS = 4
RET_HEADS = 4
MLSTM_HEADS = 4
GDN_W = GDN_HEADS * HEAD_DIM
RET_W = RET_HEADS * HEAD_DIM
MLSTM_W = MLSTM_HEADS * HEAD_DIM
S5_CH = D_MODEL // 2
S5_GROUP = 16
S5_GROUPS = S5_CH // S5_GROUP
S5_STATE = 64
N_EXPERTS = 16
EXPERT_FF = 2 * D_MODEL
CAPACITY_FACTOR = 2
ALPHA = (2 * DEPTH) ** 0.25
EPS = 1e-5
EVEN_COLS = (GDN_W, GDN_W, GDN_W, GDN_W, GDN_HEADS, GDN_HEADS, GDN_HEADS, GDN_HEADS,
             RET_W, RET_W, RET_W, RET_W)
ODD_COLS = (MLSTM_W, MLSTM_W, MLSTM_W, MLSTM_W, MLSTM_HEADS, MLSTM_HEADS, MLSTM_HEADS, MLSTM_HEADS, S5_CH)
F32 = jnp.float32
BF16 = jnp.bfloat16

LANE = 128


def _mm_kernel(a_ref, b_ref, o_ref):
    o_ref[...] = jnp.dot(a_ref[...].astype(BF16), b_ref[...].astype(BF16), preferred_element_type=F32)


def pmatmul(a, b, tm=512, tn=512):
    m, k = a.shape
    n = b.shape[1]
    n_pad = -n % LANE
    if n_pad:
        b = jnp.pad(b, ((0, 0), (0, n_pad)))
    np_ = n + n_pad
    tn = min(tn, np_)
    while np_ % tn:
        tn -= LANE
    tm = min(tm, m)
    assert m % tm == 0
    out = pl.pallas_call(
        _mm_kernel,
        grid=(m // tm, np_ // tn),
        in_specs=[pl.BlockSpec((tm, k), lambda i, j: (i, 0)),
                  pl.BlockSpec((k, tn), lambda i, j: (0, j))],
        out_specs=pl.BlockSpec((tm, tn), lambda i, j: (i, j)),
        out_shape=jax.ShapeDtypeStruct((m, np_), F32),
        name="pmatmul",
    )(a, b)
    return out[:, :n] if n_pad else out


def mm(x, w):
    lead = x.shape[:-1]
    return pmatmul(x.reshape(-1, x.shape[-1]), w).reshape(lead + (w.shape[-1],))


S5_L = 16
S5_NB = S5_CH // LANE
S5_GPB = LANE // S5_GROUP
S5_SW = S5_GPB * S5_STATE
VMEM_LIMIT = 56 * 1024 * 1024


def _s5_weights(lam_re, lam_im, log_dt, b_re, b_im, c_re, c_im):
    L, G = S5_L, S5_GROUPS
    hp = lax.Precision.HIGHEST
    taus = jnp.arange(L + 1, dtype=F32)[:, None, None]
    ks, ws, cas, ds = [], [], [], []
    for d in range(2):
        lr = jnp.minimum(lam_re[d], -1e-4)
        li = lam_im[d]
        dt = jnp.exp(log_dt[d])[:, None]
        mag = jnp.exp(lr * dt)
        ab_re, ab_im = mag * jnp.cos(li * dt), mag * jnp.sin(li * dt)
        xr, xi, den = ab_re - 1.0, ab_im, lr * lr + li * li
        f_re = (xr * lr + xi * li) / den
        f_im = (xi * lr - xr * li) / den
        bb_re = f_re[..., None] * b_re - f_im[..., None] * b_im
        bb_im = f_re[..., None] * b_im + f_im[..., None] * b_re
        pmag = jnp.exp(taus * (lr * dt))
        ar, ai = pmag * jnp.cos(taus * (li * dt)), pmag * jnp.sin(taus * (li * dt))
        wr = ar[..., None] * bb_re - ai[..., None] * bb_im
        wi = ar[..., None] * bb_im + ai[..., None] * bb_re
        k = (jnp.einsum('gop,tgpi->tgio', c_re, wr, precision=hp)
             - jnp.einsum('gop,tgpi->tgio', c_im, wi, precision=hp))
        car = c_re[None] * ar[:, :, None, :] - c_im[None] * ai[:, :, None, :]
        cai = c_re[None] * ai[:, :, None, :] + c_im[None] * ar[:, :, None, :]
        ks.append(k)
        ws.append((wr, wi))
        cas.append((car, cai))
        ds.append((ar[L], ai[L]))

    split_g = lambda t, g_axis: t.reshape(t.shape[:g_axis] + (S5_NB, S5_GPB) + t.shape[g_axis + 1:]).astype(BF16)
    gid = jnp.arange(S5_GPB)

    def expand(t, a_axis):
        shape_a = [1] * (t.ndim + 1)
        shape_a[a_axis] = S5_GPB
        shape_b = [1] * (t.ndim + 1)
        shape_b[-2] = S5_GPB
        same = gid.reshape(shape_a) == gid.reshape(shape_b)
        return jnp.where(same, t[..., None, :], jnp.zeros((), BF16))

    kf, kb = ks
    kc = jnp.concatenate([kb[1:L][::-1], (kf[0] + kb[0])[None], kf[1:L]], axis=0)
    idx = (jnp.arange(L)[None, :] - jnp.arange(L)[:, None]) + (L - 1)
    tz = expand(split_g(kc[idx], 2).transpose(2, 0, 3, 4, 1, 5), 2).reshape(S5_NB, L * LANE, L * LANE)

    pb_c, ca_c = [], []
    for d in range(2):
        wr, wi = ws[d]
        order = jnp.arange(L - 1, -1, -1) if d == 0 else jnp.arange(L)
        pb_c.append(jnp.stack([jnp.swapaxes(w[order], -1, -2) for w in (wr, wi)]))
        car, cai = cas[d]
        order = jnp.arange(1, L + 1) if d == 0 else jnp.arange(L, 0, -1)
        ca_c.append(jnp.stack([jnp.swapaxes(m, -1, -2) for m in (car[order], -cai[order])]))
    pb = expand(split_g(jnp.stack(pb_c), 3).transpose(3, 2, 4, 5, 0, 1, 6), 2).reshape(S5_NB, L * LANE, 4 * S5_SW)
    ca = expand(split_g(jnp.stack(ca_c), 3).transpose(3, 0, 1, 4, 5, 2, 6), 3).reshape(S5_NB, 4 * S5_SW, L * LANE)
    dr = jnp.stack([ds[0][0], ds[1][0]], 0).reshape(2, S5_NB, 1, S5_SW).transpose(1, 0, 2, 3).reshape(2 * S5_NB, 1, S5_SW)
    di = jnp.stack([ds[0][1], ds[1][1]], 0).reshape(2, S5_NB, 1, S5_SW).transpose(1, 0, 2, 3).reshape(2 * S5_NB, 1, S5_SW)
    return tz, pb, ca, dr, di


def _s5_p_kernel(u_ref, pb_ref, p_ref, *, nb, nc):
    res = jnp.dot(u_ref[...], pb_ref[...], preferred_element_type=F32)
    for b in range(nb):
        p_ref[:, b * 2 * S5_SW:(b + 1) * 2 * S5_SW] = res[b * nc:(b + 1) * nc]


def _s5_scan_kernel(p_ref, dr_ref, di_ref, s_ref, *, n_ctx, n_lat):
    rev = pl.program_id(0) % 2
    dr = dr_ref[...]
    di = di_ref[...]
    nbatch = p_ref.shape[1]

    def phase(base, n, carry):
        def body(step, carry):
            sr, si = carry
            row = base + jnp.where(rev == 0, step, n - 1 - step)
            s_ref[row, :, :S5_SW] = sr
            s_ref[row, :, S5_SW:] = si
            p = p_ref[row]
            nr = dr * sr - di * si + p[:, :S5_SW]
            ni = dr * si + di * sr + p[:, S5_SW:]
            return nr, ni
        return lax.fori_loop(0, n, body, carry)

    zero = jnp.zeros((nbatch, S5_SW), F32)
    carry = phase(0, n_ctx, (zero, zero))
    phase(n_ctx, n_lat, carry)


def _s5_y_kernel(u_ref, tz_ref, sf_ref, sb_ref, ca_ref, y_ref):
    y = jnp.dot(u_ref[...], tz_ref[...], preferred_element_type=F32)
    y += jnp.dot(sf_ref[...].astype(BF16), ca_ref[:2 * S5_SW, :], preferred_element_type=F32)
    y += jnp.dot(sb_ref[...].astype(BF16), ca_ref[2 * S5_SW:, :], preferred_element_type=F32)
    y_ref[...] = y


def s5_bidirectional(u_ctx, u_lat, weights):
    tz, pb, ca, dr, di = weights
    L = S5_L
    nb, t_ctx, _ = u_ctx.shape
    t_lat = u_lat.shape[1]
    assert t_ctx % L == 0 and t_lat % L == 0
    n_ctx, n_lat = t_ctx // L, t_lat // L
    nc = n_ctx + n_lat
    kw = L * LANE
    sw2 = 2 * S5_SW
    u = jnp.concatenate([u_ctx, u_lat], axis=1)
    ub = u.reshape(nb * nc, L, S5_NB, LANE).transpose(2, 0, 1, 3).reshape(S5_NB, nb * nc, kw).astype(BF16)

    p = pl.pallas_call(
        functools.partial(_s5_p_kernel, nb=nb, nc=nc),
        grid=(S5_NB, 2),
        in_specs=[pl.BlockSpec((None, nb * nc, kw), lambda j, d: (j, 0, 0)),
                  pl.BlockSpec((None, kw, sw2), lambda j, d: (j, 0, d))],
        out_specs=pl.BlockSpec((nc, nb * sw2), lambda j, d: (0, j * 2 + d)),
        out_shape=jax.ShapeDtypeStruct((nc, S5_NB * 2 * nb * sw2), F32),
        compiler_params=pltpu.CompilerParams(vmem_limit_bytes=VMEM_LIMIT),
        name="s5_chunk_inputs",
    )(ub, pb)

    p4 = p.reshape(nc, S5_NB * 2, nb, sw2)
    s4 = pl.pallas_call(
        functools.partial(_s5_scan_kernel, n_ctx=n_ctx, n_lat=n_lat),
        grid=(S5_NB * 2,),
        in_specs=[pl.BlockSpec((nc, None, nb, sw2), lambda g: (0, g, 0, 0)),
                  pl.BlockSpec((None, 1, S5_SW), lambda g: (g, 0, 0)),
                  pl.BlockSpec((None, 1, S5_SW), lambda g: (g, 0, 0))],
        out_specs=pl.BlockSpec((nc, None, nb, sw2), lambda g: (0, g, 0, 0)),
        out_shape=jax.ShapeDtypeStruct(p4.shape, F32),
        compiler_params=pltpu.CompilerParams(vmem_limit_bytes=VMEM_LIMIT),
        name="s5_state_scan",
    )(p4, dr, di)

    s2 = s4.reshape(nc, S5_NB * 2 * nb * sw2)
    yb = pl.pallas_call(
        _s5_y_kernel,
        grid=(S5_NB, nb),
        in_specs=[pl.BlockSpec((None, nc, kw), lambda j, b: (j, b, 0)),
                  pl.BlockSpec((None, kw, kw), lambda j, b: (j, 0, 0)),
                  pl.BlockSpec((nc, sw2), lambda j, b: (0, (j * 2) * nb + b)),
                  pl.BlockSpec((nc, sw2), lambda j, b: (0, (j * 2 + 1) * nb + b)),
                  pl.BlockSpec((None, 2 * sw2, kw), lambda j, b: (j, 0, 0))],
        out_specs=pl.BlockSpec((None, nc, kw), lambda j, b: (j, b, 0)),
        out_shape=jax.ShapeDtypeStruct((S5_NB, nb * nc, kw), F32),
        compiler_params=pltpu.CompilerParams(vmem_limit_bytes=VMEM_LIMIT),
        name="s5_output",
    )(ub, tz, s2, s2, ca)
    y = yb.reshape(S5_NB, nb, nc, L, LANE).transpose(1, 2, 3, 0, 4).reshape(nb, nc * L, S5_CH)
    return y[:, :t_ctx], y[:, t_ctx:]


ROUTE_TN = 512
MOE_TN = 1024
SEL_ROWS = 128
FF_SPLIT = 2
AFF_PARTS = 3


def _route_kernel(h_ref, sc_ref, sh_ref, wr_ref, hm_ref, afft_ref, asp_ref):
    hm = h_ref[...] * (1.0 + sc_ref[...]) + sh_ref[...]
    hm_ref[...] = hm.astype(BF16)
    logits = jnp.dot(hm, wr_ref[...], precision=lax.Precision.HIGHEST, preferred_element_type=F32)
    lane = lax.broadcasted_iota(jnp.int32, logits.shape, 1)
    logits = jnp.where(lane < N_EXPERTS, logits, -jnp.inf)
    ex = jnp.exp(logits - jnp.max(logits, axis=1, keepdims=True))
    aff = ex / jnp.sum(ex, axis=1, keepdims=True)
    afft_ref[...] = aff.T[:N_EXPERTS, :]
    hi = aff.astype(BF16).astype(F32)
    mid = (aff - hi).astype(BF16).astype(F32)
    lo = (aff - hi - mid).astype(BF16).astype(F32)
    asp = hi + pltpu.roll(mid, N_EXPERTS, 1) + pltpu.roll(lo, 2 * N_EXPERTS, 1)
    asp_ref[...] = asp.astype(BF16)


def _select_kernel(aff_ref, pos_ref, *, nblk_log2, cap):
    a = aff_ref[...]
    r = a.shape[0]
    bits = pltpu.bitcast(a, jnp.int32)
    ri = lax.broadcasted_iota(jnp.int32, (r, r), 0)
    rj = lax.broadcasted_iota(jnp.int32, (r, r), 1)
    same = (ri >> nblk_log2) == (rj >> nblk_log2)
    gm = jnp.where(same, 1.0, 0.0).astype(BF16)
    lm = jnp.where(same & (rj < ri), 1.0, 0.0).astype(BF16)
    li = lax.broadcasted_iota(jnp.int32, (LANE, LANE), 0)
    lj = lax.broadcasted_iota(jnp.int32, (LANE, LANE), 1)
    um = jnp.where(li <= lj, 1.0, 0.0).astype(BF16)

    def group_count(mask):
        rc = jnp.sum(jnp.where(mask, 1.0, 0.0), axis=1, keepdims=True)
        gc = jnp.dot(gm, jnp.broadcast_to(rc, (r, LANE)).astype(BF16), preferred_element_type=F32)
        return gc[:, :1]

    def bisect(i, thr):
        cand = thr | jnp.left_shift(1, 29 - i)
        return jnp.where(group_count(bits >= cand) >= cap, cand, thr)

    thr = lax.fori_loop(0, 30, bisect, jnp.zeros((r, 1), jnp.int32))

    def prefix(mask):
        x = jnp.where(mask, 1.0, 0.0)
        inc = jnp.dot(x.astype(BF16), um, preferred_element_type=F32)
        tot = jnp.broadcast_to(inc[:, LANE - 1:LANE], (r, LANE)).astype(BF16)
        return inc - x + jnp.dot(lm, tot, preferred_element_type=F32)

    gt = bits > thr
    eq = bits == thr
    need = cap - group_count(gt)
    sel = gt | (eq & (prefix(eq) < need))
    pos_ref[...] = jnp.where(sel, prefix(sel), -1.0).astype(jnp.int32)


def _onehot(pos_ref, cap):
    rows = lax.broadcasted_iota(jnp.int32, (cap, LANE), 0)
    blocks = [jnp.where(pos_ref[k:k + 1, :] == rows, 1.0, 0.0).astype(BF16) for k in range(pos_ref.shape[0])]
    return jnp.concatenate(blocks, axis=1)


def _ffn_kernel(pos_ref, hb_ref, asp_ref, wg_ref, wu_ref, wd_ref, ys_ref, xs_acc, g_acc, *, cap):
    e = pl.program_id(0)
    kt = pl.program_id(2)

    @pl.when(kt == 0)
    def _():
        xs_acc[...] = jnp.zeros_like(xs_acc)
        g_acc[...] = jnp.zeros_like(g_acc)

    oh = _onehot(pos_ref, cap)
    xs_acc[...] += jnp.dot(oh, hb_ref[...], preferred_element_type=F32)
    g_acc[...] += jnp.dot(oh, asp_ref[...], preferred_element_type=F32)

    @pl.when(kt == pl.num_programs(2) - 1)
    def _():
        xs = xs_acc[...].astype(BF16)
        g = g_acc[...]
        lane = lax.broadcasted_iota(jnp.int32, g.shape, 1)
        gate = jnp.sum(jnp.where((lane & (N_EXPERTS - 1)) == e, g, 0.0), axis=1, keepdims=True)
        fw = EXPERT_FF // FF_SPLIT
        y = jnp.zeros((cap, D_MODEL), F32)
        for f in range(FF_SPLIT):
            hg = jnp.dot(xs, wg_ref[:, f * fw:(f + 1) * fw], preferred_element_type=F32)
            hu = jnp.dot(xs, wu_ref[:, f * fw:(f + 1) * fw], preferred_element_type=F32)
            hid = (hg * jax.nn.sigmoid(hg)) * hu
            y += jnp.dot(hid.astype(BF16), wd_ref[f * fw:(f + 1) * fw, :], preferred_element_type=F32)
        ys_ref[...] = (y * gate).astype(BF16)


def _ln_rows(z, g, b):
    mu = jnp.mean(z, axis=-1, keepdims=True)
    zc = z - mu
    var = jnp.mean(zc * zc, axis=-1, keepdims=True)
    return zc * lax.rsqrt(var + EPS) * g + b


def _combine_kernel(pos_ref, ys_ref, h_ref, g2_ref, lng_ref, lnb_ref, o_ref, acc, *, cap):
    e = pl.program_id(2)

    @pl.when(e == 0)
    def _():
        acc[...] = jnp.zeros_like(acc)

    oh = _onehot(pos_ref, cap)
    acc[...] += lax.dot_general(oh, ys_ref[...], (((0,), (0,)), ((), ())), preferred_element_type=F32)

    @pl.when(e == pl.num_programs(2) - 1)
    def _():
        z = ALPHA * h_ref[...] + g2_ref[...] * acc[...]
        o_ref[...] = _ln_rows(z, lng_ref[...], lnb_ref[...])


def moe_block(h, sc, sh, g2, ln_g, ln_b, w_router, wg, wu, wd, layer):
    nb, n, dm = h.shape
    cap = CAPACITY_FACTOR * n // N_EXPERTS
    nblk = n // LANE
    assert n % LANE == 0 and nblk & (nblk - 1) == 0
    tn_r = min(ROUTE_TN, n)
    tn = min(MOE_TN, n)
    wr = jnp.pad(w_router, ((0, 0), (0, LANE - N_EXPERTS)))
    row = lambda v: v.reshape(1, dm)

    hm, afft, asp = pl.pallas_call(
        _route_kernel,
        grid=(nb, n // tn_r),
        in_specs=[pl.BlockSpec((None, tn_r, dm), lambda b, t: (b, t, 0)),
                  pl.BlockSpec((None, 1, dm), lambda b, t: (b, 0, 0)),
                  pl.BlockSpec((None, 1, dm), lambda b, t: (b, 0, 0)),
                  pl.BlockSpec((dm, LANE), lambda b, t: (0, 0))],
        out_specs=[pl.BlockSpec((None, tn_r, dm), lambda b, t: (b, t, 0)),
                   pl.BlockSpec((None, N_EXPERTS, tn_r), lambda b, t: (b, 0, t)),
                   pl.BlockSpec((None, tn_r, LANE), lambda b, t: (b, t, 0))],
        out_shape=[jax.ShapeDtypeStruct((nb, n, dm), BF16),
                   jax.ShapeDtypeStruct((nb, N_EXPERTS, n), F32),
                   jax.ShapeDtypeStruct((nb, n, LANE), BF16)],
        name="moe_route",
    )(h, sc, sh, wr)

    rows_total = nb * N_EXPERTS * nblk
    rb = max(SEL_ROWS, N_EXPERTS * nblk)
    assert rows_total % rb == 0
    pos = pl.pallas_call(
        functools.partial(_select_kernel, nblk_log2=nblk.bit_length() - 1, cap=cap),
        grid=(rows_total // rb,),
        in_specs=[pl.BlockSpec((rb, LANE), lambda i: (i, 0))],
        out_specs=pl.BlockSpec((rb, LANE), lambda i: (i, 0)),
        out_shape=jax.ShapeDtypeStruct((rows_total, LANE), jnp.int32),
        name="moe_select",
    )(afft.reshape(rows_total, LANE))
    pos = pos.reshape(nb, N_EXPERTS, nblk, LANE)

    tb = tn // LANE
    ys = pl.pallas_call(
        functools.partial(_ffn_kernel, cap=cap),
        grid=(N_EXPERTS, nb, n // tn),
        in_specs=[pl.BlockSpec((None, None, tb, LANE), lambda e, b, k: (b, e, k, 0)),
                  pl.BlockSpec((None, tn, dm), lambda e, b, k: (b, k, 0)),
                  pl.BlockSpec((None, tn, LANE), lambda e, b, k: (b, k, 0)),
                  pl.BlockSpec((None, None, dm, EXPERT_FF), lambda e, b, k: (layer, e, 0, 0)),
                  pl.BlockSpec((None, None, dm, EXPERT_FF), lambda e, b, k: (layer, e, 0, 0)),
                  pl.BlockSpec((None, None, EXPERT_FF, dm), lambda e, b, k: (layer, e, 0, 0))],
        out_specs=pl.BlockSpec((None, None, cap, dm), lambda e, b, k: (b, e, 0, 0)),
        out_shape=jax.ShapeDtypeStruct((nb, N_EXPERTS, cap, dm), BF16),
        scratch_shapes=[pltpu.VMEM((cap, dm), F32), pltpu.VMEM((cap, LANE), F32)],
        compiler_params=pltpu.CompilerParams(vmem_limit_bytes=VMEM_LIMIT),
        name="moe_ffn",
    )(pos, hm, asp, wg, wu, wd)

    return pl.pallas_call(
        functools.partial(_combine_kernel, cap=cap),
        grid=(nb, n // tn, N_EXPERTS),
        in_specs=[pl.BlockSpec((None, None, tb, LANE), lambda b, t, e: (b, e, t, 0)),
                  pl.BlockSpec((None, None, cap, dm), lambda b, t, e: (b, e, 0, 0)),
                  pl.BlockSpec((None, tn, dm), lambda b, t, e: (b, t, 0)),
                  pl.BlockSpec((None, 1, dm), lambda b, t, e: (b, 0, 0)),
                  pl.BlockSpec((1, dm), lambda b, t, e: (0, 0)),
                  pl.BlockSpec((1, dm), lambda b, t, e: (0, 0))],
        out_specs=pl.BlockSpec((None, tn, dm), lambda b, t, e: (b, t, 0)),
        out_shape=jax.ShapeDtypeStruct((nb, n, dm), F32),
        scratch_shapes=[pltpu.VMEM((tn, dm), F32)],
        compiler_params=pltpu.CompilerParams(vmem_limit_bytes=VMEM_LIMIT),
        name="moe_combine",
    )(pos, ys, h, g2, row(ln_g), row(ln_b))


N_HEADS = 4
STACK = N_HEADS * CHUNK
CHUNK_LOG2 = CHUNK.bit_length() - 1
SOLVE_LEVELS = CHUNK_LOG2


def _stack_heads(x):
    return jnp.concatenate([x[:, h * HEAD_DIM:(h + 1) * HEAD_DIM] for h in range(N_HEADS)], axis=0)


def _stack_cols(cols, width):
    return jnp.concatenate([jnp.broadcast_to(c, (CHUNK, width)) for c in cols], axis=0)


def _mxu(a, b, dims, hp):
    dg = lambda x, y: lax.dot_general(x, y, (dims, ((), ())), preferred_element_type=F32)
    a_hi, b_hi = a.astype(BF16), b.astype(BF16)
    if not hp:
        return dg(a_hi, b_hi)
    a_lo = (a - a_hi.astype(F32)).astype(BF16)
    b_lo = (b - b_hi.astype(F32)).astype(BF16)
    return dg(a_hi, b_hi) + (dg(a_lo, b_hi) + dg(a_hi, b_lo))


def _dot_nt(a, b, hp=False):
    return _mxu(a, b, ((1,), (1,)), hp)


def _dot_tn(a, b, hp=False):
    return _mxu(a, b, ((0,), (0,)), hp)


def _dot(a, b, hp=False):
    return _mxu(a, b, ((1,), (0,)), hp)


def _chunk_masks(rev):
    ri = lax.broadcasted_iota(jnp.int32, (STACK, STACK), 0)
    ci = lax.broadcasted_iota(jnp.int32, (STACK, STACK), 1)
    same = (ri >> CHUNK_LOG2) == (ci >> CHUNK_LOG2)
    ahead = jnp.where(rev, ci - ri, ri - ci)
    return same & (ahead >= 0), same & (ahead > 0)


def _scan_cumsum(gt, rev):
    ii = lax.broadcasted_iota(jnp.int32, (CHUNK, CHUNK), 0)
    jj = lax.broadcasted_iota(jnp.int32, (CHUNK, CHUNK), 1)
    tri = jnp.where(jnp.where(rev, jj - ii, ii - jj) >= 0, 1.0, 0.0)
    return jnp.dot(tri, gt, precision=lax.Precision.HIGHEST, preferred_element_type=F32)


def _unit_triangular_inverse(a):
    ri = lax.broadcasted_iota(jnp.int32, a.shape, 0)
    ci = lax.broadcasted_iota(jnp.int32, a.shape, 1)
    joins = lambda lvl: ((ri >> (lvl + 1)) == (ci >> (lvl + 1))) & ((ri >> lvl) != (ci >> lvl))
    t = jnp.where(ri == ci, 1.0, 0.0) - jnp.where(joins(0), a, 0.0)
    for lvl in range(1, CHUNK_LOG2):
        m = _dot(jnp.where(joins(lvl), a, 0.0), t)
        yield
        t = t - _dot(t, m)
        yield
    return t


def _interleave(chains):
    for _ in zip(*chains):
        pass


def _linear_scan_kernel(*refs, has_beta, k_scale):
    if has_beta:
        q_ref, k_ref, v_ref, gt_ref, gp_ref, s0_ref, o_ref, sfin_ref, s_scr = refs
    else:
        q_ref, k_ref, v_ref, gp_ref, s0_ref, o_ref, sfin_ref, s_scr = refs
    rev = pl.program_id(0) == 1
    n = pl.program_id(2)

    @pl.when(n == 0)
    def _():
        s_scr[...] = s0_ref[...]

    chains = []
    for i in range(q_ref.shape[0]):
        if has_beta:
            gt = _gdn_gates(gt_ref[i], gp_ref[...])
        else:
            gt = jnp.broadcast_to(gp_ref[0:1, :], (CHUNK, LANE))
        chains.append(_linear_chunk(q_ref.at[i], k_ref.at[i], v_ref.at[i], gt, o_ref.at[i], s_scr.at[i], rev,
                                    has_beta, k_scale))
    _interleave(chains)

    @pl.when(n == pl.num_programs(2) - 1)
    def _():
        sfin_ref[...] = s_scr[...]


def _softplus(x):
    return jnp.maximum(x, 0.0) + jnp.log1p(jnp.exp(-jnp.abs(x)))


def _gdn_gates(raw, gp):
    lane = lax.broadcasted_iota(jnp.int32, raw.shape, 1)
    return jnp.where(lane < 2 * N_HEADS, gp[0:1, :] * _softplus(raw + gp[1:2, :]), jax.nn.sigmoid(raw))


def _mlstm_gates(raw, gp):
    lane = lax.broadcasted_iota(jnp.int32, raw.shape, 1)
    x = raw + gp[0:1, :]
    return jnp.where(lane < 2 * N_HEADS, x, -_softplus(-x))


def _linear_chunk(q_ref, k_ref, v_ref, gt, o_ref, s_scr, rev, has_beta, k_scale):
    cum = _scan_cumsum(gt, rev)
    gcols, bcols, gtots = [], [], []
    for h in range(N_HEADS):
        gc = jnp.where(rev, cum[:, N_HEADS + h:N_HEADS + h + 1], cum[:, h:h + 1])
        gcols.append(gc)
        gtots.append(jnp.where(rev, gc[0:1], gc[CHUNK - 1:CHUNK]))
        bcols.append(jnp.where(rev, gt[:, 3 * N_HEADS + h:3 * N_HEADS + h + 1], gt[:, 2 * N_HEADS + h:2 * N_HEADS + h + 1]))
    cb = _stack_cols(gcols, STACK)
    diff = cb - cb.T
    incl, strict = _chunk_masks(rev)
    dec = jnp.exp(jnp.where(incl, diff, 0.0))
    gcb = cb[:, :HEAD_DIM]
    q_st, k_st, v_st = _stack_heads(q_ref[...]), _stack_heads(k_ref[...]) * k_scale, _stack_heads(v_ref[...])
    a_qk = _dot_nt(q_st, k_st) * jnp.where(incl, dec, 0.0)
    if has_beta:
        beta = _stack_cols(bcols, HEAD_DIM)
        kb = k_st * beta
        a = _dot_nt(kb, k_st) * jnp.where(strict, dec, 0.0)
        yield
        t_inv = yield from _unit_triangular_inverse(a)
        x = _dot(t_inv, jnp.concatenate([v_st * beta, kb * jnp.exp(gcb)], axis=1))
        u_st, w_st = x[:, :HEAD_DIM], x[:, HEAD_DIM:]
    else:
        u_st, w_st = v_st, None
    gtot = _stack_cols(gtots, HEAD_DIM)
    k_end = k_st * jnp.exp(gtot - gcb)
    q_dec = q_st * jnp.exp(gcb)
    hs = lambda t, h: t[h * CHUNK:(h + 1) * CHUNK]
    states = [s_scr[h] for h in range(N_HEADS)]
    q_s = [_dot(hs(q_dec, h), states[h]) for h in range(N_HEADS)]
    yield
    if has_beta:
        vn = jnp.concatenate([hs(u_st, h) - _dot(hs(w_st, h), states[h]) for h in range(N_HEADS)], axis=0)
    else:
        vn = u_st
    yield
    o_st = _dot(a_qk, vn)
    for h in range(N_HEADS):
        s_scr[h] = jnp.exp(gtots[h]) * states[h] + _dot_tn(hs(k_end, h), hs(vn, h))
    yield
    for h in range(N_HEADS):
        o_ref[:, h * HEAD_DIM:(h + 1) * HEAD_DIM] = hs(o_st, h) + q_s[h]
    yield


SCAN_NS = 4


HEADS_W = N_HEADS * HEAD_DIM


def _scan_specs(srcs, gates, nb, nchunk):
    ns = SCAN_NS
    assert nb % ns == 0
    cidx = lambda d, n: n + d * (nchunk - 1 - 2 * n)
    specs = [pl.BlockSpec((ns, CHUNK, HEADS_W), lambda d, b, n, c=c: (b, cidx(d, n), c)) for _, c in srcs]
    if gates is not None:
        specs.append(pl.BlockSpec((ns, CHUNK, LANE), lambda d, b, n, c=gates[1]: (b, cidx(d, n), c)))
    specs.append(pl.BlockSpec((2, LANE), lambda d, b, n: (0, 0)))
    ospec = pl.BlockSpec((None, ns, CHUNK, HEADS_W), lambda d, b, n: (d, b, cidx(d, n), 0))
    return specs, ospec


def linear_scan_bidir(q, k, v, gates, gp, s0, has_beta, k_scale=1.0):
    nb, t, _ = q[0].shape
    nchunk = t // CHUNK
    assert t % CHUNK == 0
    ns = SCAN_NS
    specs, ospec = _scan_specs((q, k, v), gates if has_beta else None, nb, nchunk)
    sspec = pl.BlockSpec((None, ns, N_HEADS, HEAD_DIM, HEAD_DIM), lambda d, b, n: (d, b, 0, 0, 0))
    args = [q[0], k[0], v[0]] + ([gates[0]] if has_beta else []) + [gp, s0]
    return pl.pallas_call(
        functools.partial(_linear_scan_kernel, has_beta=has_beta, k_scale=k_scale),
        grid=(2, nb // ns, nchunk),
        in_specs=specs + [sspec],
        out_specs=[ospec, sspec],
        out_shape=[jax.ShapeDtypeStruct((2, nb, t, HEADS_W), F32),
                   jax.ShapeDtypeStruct((2, nb, N_HEADS, HEAD_DIM, HEAD_DIM), F32)],
        scratch_shapes=[pltpu.VMEM((ns, N_HEADS, HEAD_DIM, HEAD_DIM), F32)],
        name="gdn_scan" if has_beta else "retention_scan",
    )(*args)


def linear_scan_two_pass(ctx_args, lat_args, gp, has_beta, k_scale=1.0):
    nb = ctx_args[0][0].shape[0]
    zero = jnp.zeros((2, nb, N_HEADS, HEAD_DIM, HEAD_DIM), F32)
    o_ctx, s_ctx = linear_scan_bidir(*ctx_args, gp, zero, has_beta, k_scale)
    o_lat, _ = linear_scan_bidir(*lat_args, gp, s_ctx, has_beta, k_scale)
    return o_ctx, o_lat


MLSTM_HP = True


def _mlstm_scan_kernel(q_ref, k_ref, v_ref, gt_ref, gp_ref, c0_ref, n0_ref, m0_ref, o_ref, cfin_ref, nfin_ref,
                       mfin_ref, c_scr, n_scr, m_scr):
    rev = pl.program_id(0) == 1
    step = pl.program_id(2)

    @pl.when(step == 0)
    def _():
        c_scr[...] = c0_ref[...]
        n_scr[...] = n0_ref[...]
        m_scr[...] = m0_ref[...]

    _interleave([_mlstm_chunk(q_ref.at[i], k_ref.at[i], v_ref.at[i], _mlstm_gates(gt_ref[i], gp_ref[...]),
                              o_ref.at[i], c_scr.at[i], n_scr.at[i], m_scr.at[i], rev)
                 for i in range(q_ref.shape[0])])

    @pl.when(step == pl.num_programs(2) - 1)
    def _():
        cfin_ref[...] = c_scr[...]
        nfin_ref[...] = n_scr[...]
        mfin_ref[...] = m_scr[...]


def _mlstm_chunk(q_ref, k_ref, v_ref, gt, o_ref, c_scr, n_scr, m_scr, rev):
    cum = _scan_cumsum(gt, rev)
    pick = lambda t, c: jnp.where(rev, t[:, N_HEADS + c:N_HEADS + c + 1], t[:, c:c + 1])
    q_st, k_st, v_st = _stack_heads(q_ref[...]), _stack_heads(k_ref[...]), _stack_heads(v_ref[...])
    hs = lambda t, h: t[h * CHUNK:(h + 1) * CHUNK]
    qk = _dot_nt(q_st, k_st, MLSTM_HP)
    yield
    bcums, srcs, inters, qcs, qns = [], [], [], [], []
    for h in range(N_HEADS):
        ic = pick(gt, h)
        bcum = pick(cum, 2 * N_HEADS + h)
        b_end = jnp.where(rev, bcum[0:1], bcum[CHUNK - 1:CHUNK])
        c_prev, n_prev, m_prev = c_scr[h], n_scr[h], m_scr[h][:, :1]
        a = b_end - bcum + ic
        m_new = jnp.maximum(b_end + m_prev, jnp.max(a, axis=0, keepdims=True))
        w_state = jnp.exp(a - m_new)
        decay = jnp.exp(b_end + m_prev - m_new)
        kw = hs(k_st, h) * w_state
        c_scr[h] = decay * c_prev + _dot_tn(kw, hs(v_st, h), MLSTM_HP)
        n_scr[h] = decay * n_prev + jnp.sum(kw, axis=0, keepdims=True)
        m_scr[h] = jnp.broadcast_to(m_new, (1, HEAD_DIM))
        bcums.append(bcum)
        srcs.append(bcum - ic)
        inters.append(bcum + m_prev)
        qcs.append(_dot(hs(q_st, h), c_prev, MLSTM_HP))
        qns.append(jnp.sum(hs(q_st, h) * n_prev, axis=1, keepdims=True))
    yield
    incl, _ = _chunk_masks(rev)
    dlog = _stack_cols(bcums, STACK) - _stack_cols(srcs, STACK).T
    inter = jnp.concatenate(inters, axis=0)
    m_t = jnp.maximum(inter, jnp.max(jnp.where(incl, dlog, -1e30), axis=1, keepdims=True))
    s = qk * jnp.where(incl, jnp.exp(jnp.where(incl, dlog, 0.0) - m_t), 0.0)
    w_inter = jnp.exp(inter - m_t)
    yield
    num = _dot(s, v_st, MLSTM_HP) + w_inter * jnp.concatenate(qcs, axis=0)
    den = jnp.sum(s, axis=1, keepdims=True) + w_inter * jnp.concatenate(qns, axis=0)
    yield
    out = num / jnp.maximum(jnp.abs(den), jnp.exp(-m_t))
    for h in range(N_HEADS):
        o_ref[:, h * HEAD_DIM:(h + 1) * HEAD_DIM] = hs(out, h)
    yield


def mlstm_scan_bidir(q, k, v, gates, gp, state):
    nb, t, _ = q[0].shape
    nchunk = t // CHUNK
    assert t % CHUNK == 0
    ns = SCAN_NS
    specs, ospec = _scan_specs((q, k, v), gates, nb, nchunk)
    cspec = pl.BlockSpec((None, ns, N_HEADS, HEAD_DIM, HEAD_DIM), lambda d, b, n: (d, b, 0, 0, 0))
    vspec = pl.BlockSpec((None, ns, N_HEADS, 1, HEAD_DIM), lambda d, b, n: (d, b, 0, 0, 0))
    cshape = jax.ShapeDtypeStruct((2, nb, N_HEADS, HEAD_DIM, HEAD_DIM), F32)
    vshape = jax.ShapeDtypeStruct((2, nb, N_HEADS, 1, HEAD_DIM), F32)
    o, c, n, m = pl.pallas_call(
        _mlstm_scan_kernel,
        grid=(2, nb // ns, nchunk),
        in_specs=specs + [cspec, vspec, vspec],
        out_specs=[ospec, cspec, vspec, vspec],
        out_shape=[jax.ShapeDtypeStruct((2, nb, t, HEADS_W), F32), cshape, vshape, vshape],
        scratch_shapes=[pltpu.VMEM((ns, N_HEADS, HEAD_DIM, HEAD_DIM), F32), pltpu.VMEM((ns, N_HEADS, 1, HEAD_DIM), F32),
                        pltpu.VMEM((ns, N_HEADS, 1, HEAD_DIM), F32)],
        name="mlstm_scan",
    )(q[0], k[0], v[0], gates[0], gp, *state)
    return o, (c, n, m)


def mlstm_two_pass(ctx_args, lat_args, gp):
    nb = ctx_args[0][0].shape[0]
    zero = (jnp.zeros((2, nb, N_HEADS, HEAD_DIM, HEAD_DIM), F32), jnp.zeros((2, nb, N_HEADS, 1, HEAD_DIM), F32),
            jnp.zeros((2, nb, N_HEADS, 1, HEAD_DIM), F32))
    o_ctx, s_ctx = mlstm_scan_bidir(*ctx_args, gp, zero)
    o_lat, _ = mlstm_scan_bidir(*lat_args, gp, s_ctx)
    return o_ctx, o_lat


PROJ_TM = 512
PROJ_TN_MAX = 2304


def _inproj_kernel(h_ref, sc_ref, sh_ref, w_ref, o_ref):
    hm = (h_ref[...] * (1.0 + sc_ref[...]) + sh_ref[...]).astype(BF16)
    o_ref[...] = jnp.dot(hm, w_ref[...], preferred_element_type=F32)


def in_projection(h, sc, sh, w):
    nb, n, dm = h.shape
    ncol = w.shape[1]
    tm = min(PROJ_TM, n)
    tn = max(c for c in range(LANE, min(ncol, PROJ_TN_MAX) + 1, LANE) if ncol % c == 0)
    return pl.pallas_call(
        _inproj_kernel,
        grid=(nb, n // tm, ncol // tn),
        in_specs=[pl.BlockSpec((None, tm, dm), lambda b, t, j: (b, t, 0)),
                  pl.BlockSpec((None, 1, dm), lambda b, t, j: (b, 0, 0)),
                  pl.BlockSpec((None, 1, dm), lambda b, t, j: (b, 0, 0)),
                  pl.BlockSpec((dm, tn), lambda b, t, j: (0, j))],
        out_specs=pl.BlockSpec((None, tm, tn), lambda b, t, j: (b, t, j)),
        out_shape=jax.ShapeDtypeStruct((nb, n, ncol), F32),
        name="in_projection",
    )(h, sc, sh, w)


def _conv_kernel(x_ref, w_ref, o_ref, *, grid_w, l2_blocks, q_scale, k_scale):
    x = x_ref[...]
    t = x.shape[0]
    w = w_ref[...]
    col = lax.broadcasted_iota(jnp.int32, x.shape, 0) & (grid_w - 1)
    left = jnp.where(col == 0, 0.0, pltpu.roll(x, 1, 0))
    right = jnp.where(col == grid_w - 1, 0.0, pltpu.roll(x, t - 1, 0))
    row = lambda kh: w[3 * kh:3 * kh + 1] * left + w[3 * kh + 1:3 * kh + 2] * x + w[3 * kh + 2:3 * kh + 3] * right
    acc = row(1)
    if grid_w < t:
        zero = jnp.zeros((grid_w, LANE), F32)
        acc = acc + jnp.concatenate([zero, row(0)[:t - grid_w]], axis=0) + jnp.concatenate([row(2)[grid_w:], zero], axis=0)
    y = acc * jax.nn.sigmoid(acc)
    c = pl.program_id(1)
    normed = y * lax.rsqrt(jnp.sum(y * y, axis=1, keepdims=True) + 1e-6)
    y = jnp.where(c < l2_blocks, normed, y)
    o_ref[...] = y * jnp.where(c < N_HEADS, q_scale, jnp.where(c < 2 * N_HEADS, k_scale, 1.0))


def conv_prep(proj, conv_w, nblk, grid_w, l2_blocks, q_scale, k_scale):
    nb, t, _ = proj.shape
    assert grid_w & (grid_w - 1) == 0 and t % grid_w == 0
    return pl.pallas_call(
        functools.partial(_conv_kernel, grid_w=grid_w, l2_blocks=l2_blocks, q_scale=q_scale, k_scale=k_scale),
        grid=(nb, nblk),
        in_specs=[pl.BlockSpec((None, t, LANE), lambda b, c: (b, 0, c)),
                  pl.BlockSpec((CONV_K * CONV_K, LANE), lambda b, c: (0, c))],
        out_specs=pl.BlockSpec((None, t, LANE), lambda b, c: (b, 0, c)),
        out_shape=jax.ShapeDtypeStruct((nb, t, nblk * LANE), F32),
        compiler_params=pltpu.CompilerParams(vmem_limit_bytes=VMEM_LIMIT),
        name="conv_prep",
    )(proj, conv_w.reshape(CONV_K * CONV_K, -1))


def _head_norm(o, center):
    outs = []
    for h in range(N_HEADS):
        x = o[:, h * HEAD_DIM:(h + 1) * HEAD_DIM]
        if center:
            x = x - jnp.mean(x, axis=1, keepdims=True)
        outs.append(x * lax.rsqrt(jnp.mean(x * x, axis=1, keepdims=True) + EPS))
    return jnp.concatenate(outs, axis=1)


def _mix_out(y_a, y_b, wo_ref, h_ref, g1_ref, lng_ref, lnb_ref, o_ref):
    y = (jnp.dot(y_a.astype(BF16), wo_ref[:HEADS_W, :], preferred_element_type=F32)
         + jnp.dot(y_b.astype(BF16), wo_ref[HEADS_W:, :], preferred_element_type=F32))
    o_ref[...] = _ln_rows(ALPHA * h_ref[...] + g1_ref[...] * y, lng_ref[...], lnb_ref[...])


def _merge_even_kernel(og_ref, or_ref, za_ref, zr_ref, gg_ref, rg_ref, wo_ref, h_ref, g1_ref, lng_ref, lnb_ref, o_ref):
    za, zr = za_ref[...], zr_ref[...]
    y_g = _head_norm(og_ref[0] + og_ref[1], False) * gg_ref[...] * (za * jax.nn.sigmoid(za))
    y_r = _head_norm(or_ref[0] + or_ref[1], True) * rg_ref[...] * (zr * jax.nn.sigmoid(zr))
    _mix_out(y_g, y_r, wo_ref, h_ref, g1_ref, lng_ref, lnb_ref, o_ref)


def _merge_odd_kernel(om_ref, ys_ref, og_ref, u_ref, mg_ref, dsk_ref, bglu_ref, wglu_ref, wo_ref, h_ref, g1_ref,
                      lng_ref, lnb_ref, o_ref):
    y_m = _head_norm(om_ref[0] + om_ref[1], True) * mg_ref[...] * jax.nn.sigmoid(og_ref[...])
    y = jax.nn.gelu(ys_ref[...] + dsk_ref[...] * u_ref[...])
    y = y * jax.nn.sigmoid(jnp.dot(y.astype(BF16), wglu_ref[...], preferred_element_type=F32) + bglu_ref[...])
    _mix_out(y_m, y, wo_ref, h_ref, g1_ref, lng_ref, lnb_ref, o_ref)


def _merge_call(kernel_fn, name, scans, toks, rows, mats, h, g1, ln_g, ln_b):
    nb, n, dm = h.shape
    tm = min(PROJ_TM, n)
    full = lambda a: pl.BlockSpec(a.shape, lambda b, t: (0,) * a.ndim)
    rowv = lambda v: v.reshape(1, -1)
    specs, args = [], []
    for a in scans:
        specs.append(pl.BlockSpec((2, None, tm, HEADS_W), lambda b, t: (0, b, t, 0)))
        args.append(a)
    for a, c in toks:
        specs.append(pl.BlockSpec((None, tm, HEADS_W), lambda b, t, c=c: (b, t, c)))
        args.append(a)
    for v in rows:
        args.append(rowv(v))
        specs.append(full(args[-1]))
    for m in mats:
        args.append(m)
        specs.append(full(m))
    args += [h, g1, rowv(ln_g), rowv(ln_b)]
    specs += [pl.BlockSpec((None, tm, dm), lambda b, t: (b, t, 0)), pl.BlockSpec((None, 1, dm), lambda b, t: (b, 0, 0)),
              full(args[-2]), full(args[-1])]
    return pl.pallas_call(
        kernel_fn,
        grid=(nb, n // tm),
        in_specs=specs,
        out_specs=pl.BlockSpec((None, tm, dm), lambda b, t: (b, t, 0)),
        out_shape=jax.ShapeDtypeStruct((nb, n, dm), F32),
        compiler_params=pltpu.CompilerParams(vmem_limit_bytes=VMEM_LIMIT),
        name=name,
    )(*args)


def split_cols(t, sizes):
    return jnp.split(t, np.cumsum(sizes)[:-1].tolist(), axis=-1)


def heads(t, n):
    return t.reshape(t.shape[:-1] + (n, t.shape[-1] // n))


def flip_t(t):
    return None if t is None else jnp.flip(t, axis=1)


def l2norm(t):
    return t * lax.rsqrt(jnp.sum(t * t, axis=-1, keepdims=True) + 1e-6)


def layer_norm(t, g, b):
    mu = t.mean(-1, keepdims=True)
    var = jnp.square(t - mu).mean(-1, keepdims=True)
    return ((t - mu) * lax.rsqrt(var + EPS)) * g + b


def rms_norm_heads(o, g):
    y = o * lax.rsqrt(jnp.mean(o * o, axis=-1, keepdims=True) + EPS) * g
    return y.reshape(o.shape[:2] + (-1,))


def group_norm_heads(o, g):
    mu = o.mean(-1, keepdims=True)
    var = jnp.square(o - mu).mean(-1, keepdims=True)
    return ((o - mu) * lax.rsqrt(var + EPS)).reshape(o.shape[:2] + (-1,)) * g


def short_conv(t, w, on_grid):
    ch = t.shape[-1]
    if on_grid:
        b, n = t.shape[:2]
        rows = n // GRID_W
        tg = t.reshape(b, rows, GRID_W, ch)
        y = lax.conv_general_dilated(tg, w[:, :, None, :], (1, 1), 'SAME',
                                     dimension_numbers=('NHWC', 'HWIO', 'NHWC'), feature_group_count=ch)
        return y.reshape(b, n, ch)
    return lax.conv_general_dilated(t, w[CONV_K // 2][:, None, :], (1,), 'SAME',
                                    dimension_numbers=('NWC', 'WIO', 'NWC'), feature_group_count=ch)


def to_chunks(t):
    b, n, h = t.shape[:3]
    t = t.reshape((b, n // CHUNK, CHUNK, h) + t.shape[3:])
    return jnp.moveaxis(jnp.moveaxis(t, 3, 1), 2, 0)


def from_chunks(t):
    t = jnp.moveaxis(jnp.moveaxis(t, 0, 2), 1, 3)
    return t.reshape((t.shape[0], -1) + t.shape[3:])


def linear_scan(q, k, v, g, beta, s0, want_out):
    b, _, h, dk = k.shape
    dv = v.shape[-1]
    kc, vc = to_chunks(k), to_chunks(v)
    gcum = jnp.cumsum(to_chunks(g), axis=-1)
    diff = gcum[..., :, None] - gcum[..., None, :]
    incl = jnp.tril(jnp.ones((CHUNK, CHUNK), bool))
    if beta is None:
        u, w = vc, None
    else:
        bc = to_chunks(beta)[..., None]
        kb = kc * bc
        strict = jnp.tril(jnp.ones((CHUNK, CHUNK), bool), -1)
        a = jnp.where(strict, jnp.einsum('...ik,...jk->...ij', kb, kc) * jnp.exp(jnp.where(strict, diff, 0.0)), 0.0)
        rhs = jnp.concatenate([vc * bc, kb * jnp.exp(gcum)[..., None]], axis=-1)
        sol = lax.linalg.triangular_solve(a + jnp.eye(CHUNK, dtype=a.dtype), rhs, left_side=True,
                                          lower=True, unit_diagonal=True)
        u, w = sol[..., :dv], sol[..., dv:]
    k_end = kc * jnp.exp(gcum[..., -1:] - gcum)[..., None]
    g_end = jnp.exp(gcum[..., -1])[..., None, None]
    if want_out:
        qc = to_chunks(q)
        q_dec = qc * jnp.exp(gcum)[..., None]
        a_qk = jnp.where(incl, jnp.einsum('...ik,...jk->...ij', qc, kc) * jnp.exp(jnp.where(incl, diff, 0.0)), 0.0)
    else:
        q_dec, a_qk = None, None
    if s0 is None:
        s0 = jnp.zeros((b, h, dk, dv), F32)

    def step(s, inp):
        qd, ke, uc, wc, aqk, ge = inp
        vn = uc if wc is None else uc - jnp.einsum('bhck,bhkv->bhcv', wc, s)
        s_new = ge * s + jnp.einsum('bhck,bhcv->bhkv', ke, vn)
        if qd is None:
            return s_new, None
        return s_new, jnp.einsum('bhck,bhkv->bhcv', qd, s) + jnp.einsum('bhij,bhjv->bhiv', aqk, vn)

    s_fin, o = lax.scan(step, s0, (q_dec, k_end, u, w, a_qk, g_end))
    return (from_chunks(o) if want_out else None), s_fin


def mlstm_scan(q, k, v, log_i, log_f, s0, want_out):
    b, _, h, dk = k.shape
    dv = v.shape[-1]
    if s0 is None:
        s0 = (jnp.zeros((b, h, dk, dv), F32), jnp.zeros((b, h, dk), F32), jnp.zeros((b, h), F32))
    incl = jnp.tril(jnp.ones((CHUNK, CHUNK), bool))

    def step(carry, inp):
        c_prev, n_prev, m_prev = carry
        qc, kc, vc, ic, fc = inp
        bcum = jnp.cumsum(fc, axis=-1)
        b_end = bcum[..., -1]
        a = b_end[..., None] - bcum + ic
        m_new = jnp.maximum(b_end + m_prev, a.max(-1))
        w_state = jnp.exp(a - m_new[..., None])
        decay = jnp.exp(b_end + m_prev - m_new)
        c_new = decay[..., None, None] * c_prev + jnp.einsum('bhck,bhcv->bhkv', kc * w_state[..., None], vc)
        n_new = decay[..., None] * n_prev + jnp.einsum('bhck,bhc->bhk', kc, w_state)
        carry_new = (c_new, n_new, m_new)
        if qc is None:
            return carry_new, None
        dlog = jnp.where(incl, bcum[..., :, None] - bcum[..., None, :] + ic[..., None, :], -jnp.inf)
        inter = bcum + m_prev[..., None]
        m_t = jnp.maximum(inter, dlog.max(-1))
        s = jnp.einsum('bhik,bhjk->bhij', qc, kc) * jnp.exp(dlog - m_t[..., None])
        w_inter = jnp.exp(inter - m_t)[..., None]
        num = jnp.einsum('bhij,bhjv->bhiv', s, vc) + w_inter * jnp.einsum('bhik,bhkv->bhiv', qc, c_prev)
        den = s.sum(-1, keepdims=True) + w_inter * jnp.einsum('bhik,bhk->bhi', qc, n_prev)[..., None]
        return carry_new, num / jnp.maximum(jnp.abs(den), jnp.exp(-m_t)[..., None])

    xs = (to_chunks(q) if want_out else None, to_chunks(k), to_chunks(v), to_chunks(log_i), to_chunks(log_f))
    s_fin, hs = lax.scan(step, s0, xs)
    return (from_chunks(hs) if want_out else None), s_fin


def two_pass(scan_fn, ctx_args, lat_args, ctx_out, reverse):
    if reverse:
        ctx_args = [flip_t(t) for t in ctx_args]
        lat_args = [flip_t(t) for t in lat_args]
    o_ctx, s_ctx = scan_fn(*ctx_args, None, ctx_out)
    o_lat, _ = scan_fn(*lat_args, s_ctx, True)
    if reverse:
        o_ctx, o_lat = flip_t(o_ctx), flip_t(o_lat)
    return o_ctx, o_lat


def retention_log_decay(direction):
    expo = 5.0 + 2.0 * jnp.arange(RET_HEADS, dtype=F32) + direction
    return jnp.log1p(-jnp.exp2(-expo))


GATE_COLS = 4 * N_HEADS
Q_SCALE = HEAD_DIM ** -0.5


def _gate_row(*vals):
    v = jnp.concatenate([jnp.ravel(x) for x in vals])
    return jnp.pad(v, (0, LANE - v.shape[0]))


def _pad_gate_cols(w):
    return jnp.pad(w, ((0, 0), (0, LANE - w.shape[1])))


def gdn_retention_mixer(hs_, mods, w_in, w_out, conv_w, a_log, dt_bias, gdn_gain, ret_gain, ln_g, ln_b, ctx_out):
    w = jnp.concatenate([w_in[:, :4 * GDN_W], w_in[:, 4 * GDN_W + GATE_COLS:],
                         _pad_gate_cols(w_in[:, 4 * GDN_W:4 * GDN_W + GATE_COLS])], axis=1).astype(BF16)
    gate_blk = (4 * GDN_W + 4 * RET_W) // LANE
    gp_gdn = jnp.stack([_gate_row(-jnp.exp(a_log)), _gate_row(dt_bias)])
    gp_ret = jnp.stack([_gate_row(retention_log_decay(0), retention_log_decay(1)), jnp.zeros((LANE,), F32)])
    wo = w_out.astype(BF16)
    projs, convs = [], []
    for (h, (sc, sh, _)), grid_w in zip(zip(hs_, mods), (hs_[0].shape[1], GRID_W)):
        p = in_projection(h, sc, sh, w)
        projs.append(p)
        convs.append(conv_prep(p, conv_w, 3 * GDN_W // LANE, grid_w, 2 * N_HEADS, Q_SCALE, 1.0))
    gdn_args = [((cv, 0), (cv, 1), (cv, 2), (p, gate_blk)) for p, cv in zip(projs, convs)]
    ret_args = [((p, 4), (p, 5), (p, 6), None) for p in projs]
    og = linear_scan_two_pass(gdn_args[0], gdn_args[1], gp_gdn, True)
    orr = linear_scan_two_pass(ret_args[0], ret_args[1], gp_ret, False, Q_SCALE)
    outs = []
    for i in range(2):
        if i == 0 and not ctx_out:
            outs.append(None)
            continue
        outs.append(_merge_call(_merge_even_kernel, "merge_even", [og[i], orr[i]], [(projs[i], 3), (projs[i], 7)],
                                [jnp.tile(gdn_gain, N_HEADS), ret_gain], [wo], hs_[i], mods[i][2], ln_g, ln_b))
    return outs


def mlstm_s5_mixer(hs_, mods, w_in, w_out, conv_w, gate_bias, mlstm_gain, lam_re, lam_im, log_dt,
                   b_re, b_im, c_re, c_im, d_skip, w_glu, b_glu, ln_g, ln_b, ctx_out):
    w = jnp.concatenate([w_in[:, :4 * MLSTM_W], _pad_gate_cols(w_in[:, 4 * MLSTM_W:4 * MLSTM_W + GATE_COLS])],
                        axis=1).astype(BF16)
    w_u = w_in[:, 4 * MLSTM_W + GATE_COLS:].astype(BF16)
    gate_blk = 4 * MLSTM_W // LANE
    gp = jnp.stack([_gate_row(gate_bias[0, 0], gate_bias[1, 0], gate_bias[0, 1], gate_bias[1, 1]),
                    jnp.zeros((LANE,), F32)])
    wo, wglu = w_out.astype(BF16), w_glu.astype(BF16)
    projs, us, args = [], [], []
    for (h, (sc, sh, _)), grid_w in zip(zip(hs_, mods), (hs_[0].shape[1], GRID_W)):
        p = in_projection(h, sc, sh, w)
        cv = conv_prep(p, conv_w, 2 * MLSTM_W // LANE, grid_w, 0, 1.0, Q_SCALE)
        projs.append(p)
        us.append(in_projection(h, sc, sh, w_u))
        args.append(((cv, 0), (cv, 1), (p, 2), (p, gate_blk)))
    om = mlstm_two_pass(args[0], args[1], gp)
    ys = s5_bidirectional(us[0], us[1], _s5_weights(lam_re, lam_im, log_dt, b_re, b_im, c_re, c_im))
    outs = []
    for i in range(2):
        if i == 0 and not ctx_out:
            outs.append(None)
            continue
        outs.append(_merge_call(_merge_odd_kernel, "merge_odd", [om[i]], [(ys[i], 0), (projs[i], 3), (us[i], 0)],
                                [mlstm_gain, d_skip, b_glu], [wglu, wo], hs_[i], mods[i][2], ln_g, ln_b))
    return outs


def kernel(x, c, ctx, c_ctx, w_mod, b_mod, ln1_g, ln1_b, ln2_g, ln2_b, w_router, w_gate, w_up, w_down,
           ev_w_in, ev_w_out, ev_conv, ev_a_log, ev_dt_bias, ev_gdn_norm, ev_ret_norm,
           od_w_in, od_w_out, od_conv, od_gate_bias, od_mlstm_norm, od_lam_re, od_lam_im, od_log_dt,
           od_b_re, od_b_im, od_c_re, od_c_im, od_d_skip, od_w_glu, od_b_glu):
    h_lat, h_ctx = x, ctx
    s_lat = jax.nn.silu(c)
    s_ctx = jax.nn.silu(c_ctx)
    experts = (w_gate.astype(BF16), w_up.astype(BF16), w_down.astype(BF16))
    for l in range(DEPTH):
        last = l == DEPTH - 1
        sh1, sc1, g1, sh2, sc2, g2 = jnp.split((s_lat @ w_mod[l] + b_mod[l])[:, None, :], 6, axis=-1)
        bc = lambda v: jnp.broadcast_to(v, (BATCH, 1, D_MODEL))
        csh1, csc1, cg1, csh2, csc2, cg2 = [bc(v) for v in jnp.split(s_ctx @ w_mod[l] + b_mod[l], 6, axis=-1)]
        streams = (h_ctx, h_lat)
        mods = ((csc1, csh1, cg1), (sc1, sh1, g1))
        if l % 2 == 0:
            e = l // 2
            h_ctx, h_lat = gdn_retention_mixer(streams, mods, ev_w_in[e], ev_w_out[e], ev_conv[e], ev_a_log[e],
                                               ev_dt_bias[e], ev_gdn_norm[e], ev_ret_norm[e], ln1_g[l], ln1_b[l],
                                               not last)
        else:
            o = l // 2
            h_ctx, h_lat = mlstm_s5_mixer(streams, mods, od_w_in[o], od_w_out[o], od_conv[o], od_gate_bias[o],
                                          od_mlstm_norm[o], od_lam_re[o], od_lam_im[o], od_log_dt[o],
                                          od_b_re[o], od_b_im[o], od_c_re[o], od_c_im[o], od_d_skip[o],
                                          od_w_glu[o], od_b_glu[o], ln1_g[l], ln1_b[l], not last)
        h_lat = moe_block(h_lat, sc2, sh2, g2, ln2_g[l], ln2_b[l], w_router[l], *experts, l)
        if not last:
            h_ctx = moe_block(h_ctx, csc2, csh2, cg2, ln2_g[l], ln2_b[l], w_router[l], *experts, l)
    return h_lat
```

```python
import functools
import math

import jax
import jax.numpy as jnp
import numpy as np
from jax import lax
from jax.experimental import pallas as pl
from jax.experimental.pallas import tpu as pltpu

D_MODEL = 1024
BATCH = 4
SEQ = 4096
DEPTH = 4
GRID_W = 64
CTX_LEN = 256
CHUNK = 64
CONV_K = 3
HEAD_DIM = D_MODEL // 8
GDN_HEADS = 4
RET_HEADS = 4
MLSTM_HEADS = 4
GDN_W = GDN_HEADS * HEAD_DIM
RET_W = RET_HEADS * HEAD_DIM
MLSTM_W = MLSTM_HEADS * HEAD_DIM
S5_CH = D_MODEL // 2
S5_GROUP = 16
S5_GROUPS = S5_CH // S5_GROUP
S5_STATE = 64
N_EXPERTS = 16
EXPERT_FF = 2 * D_MODEL
CAPACITY_FACTOR = 2
ALPHA = (2 * DEPTH) ** 0.25
EPS = 1e-5
EVEN_COLS = (GDN_W, GDN_W, GDN_W, GDN_W, GDN_HEADS, GDN_HEADS, GDN_HEADS, GDN_HEADS,
             RET_W, RET_W, RET_W, RET_W)
ODD_COLS = (MLSTM_W, MLSTM_W, MLSTM_W, MLSTM_W, MLSTM_HEADS, MLSTM_HEADS, MLSTM_HEADS, MLSTM_HEADS, S5_CH)
F32 = jnp.float32
BF16 = jnp.bfloat16

LANE = 128


def _mm_kernel(a_ref, b_ref, o_ref):
    o_ref[...] = jnp.dot(a_ref[...].astype(BF16), b_ref[...].astype(BF16), preferred_element_type=F32)


def pmatmul(a, b, tm=512, tn=512):
    m, k = a.shape
    n = b.shape[1]
    n_pad = -n % LANE
    if n_pad:
        b = jnp.pad(b, ((0, 0), (0, n_pad)))
    np_ = n + n_pad
    tn = min(tn, np_)
    while np_ % tn:
        tn -= LANE
    tm = min(tm, m)
    assert m % tm == 0
    out = pl.pallas_call(
        _mm_kernel,
        grid=(m // tm, np_ // tn),
        in_specs=[pl.BlockSpec((tm, k), lambda i, j: (i, 0)),
                  pl.BlockSpec((k, tn), lambda i, j: (0, j))],
        out_specs=pl.BlockSpec((tm, tn), lambda i, j: (i, j)),
        out_shape=jax.ShapeDtypeStruct((m, np_), F32),
        name="pmatmul",
    )(a, b)
    return out[:, :n] if n_pad else out


def mm(x, w):
    lead = x.shape[:-1]
    return pmatmul(x.reshape(-1, x.shape[-1]), w).reshape(lead + (w.shape[-1],))


S5_L = 16
S5_NB = S5_CH // LANE
S5_GPB = LANE // S5_GROUP
S5_SW = S5_GPB * S5_STATE
VMEM_LIMIT = 56 * 1024 * 1024


def _s5_expand_kernel(c_ref, o_ref, *, row_item_log2, col_item_log2):
    c = c_ref[...]
    nk, nc = c.shape[1], o_ref.shape[1]
    gmask = S5_GPB - 1
    k = lax.broadcasted_iota(jnp.int32, (nk, nc), 0)
    col = lax.broadcasted_iota(jnp.int32, (nk, nc), 1)
    src = ((col >> (col_item_log2 + S5_GPB.bit_length() - 1)) << col_item_log2) | (col & ((1 << col_item_log2) - 1))
    rep = jnp.where(k == src, 1.0, 0.0).astype(BF16)
    wide = jnp.dot(c, rep, preferred_element_type=F32)
    r = lax.broadcasted_iota(jnp.int32, wide.shape, 0)
    cc = lax.broadcasted_iota(jnp.int32, wide.shape, 1)
    same = ((r >> row_item_log2) & gmask) == ((cc >> col_item_log2) & gmask)
    o_ref[...] = jnp.where(same, wide, 0.0).astype(BF16)


def _s5_expand(c, row_item_log2, col_item_log2):
    nbk, rows, nk = c.shape
    return pl.pallas_call(
        functools.partial(_s5_expand_kernel, row_item_log2=row_item_log2, col_item_log2=col_item_log2),
        grid=(nbk,),
        in_specs=[pl.BlockSpec((None, rows, nk), lambda j: (j, 0, 0))],
        out_specs=pl.BlockSpec((None, rows, nk * S5_GPB), lambda j: (j, 0, 0)),
        out_shape=jax.ShapeDtypeStruct((nbk, rows, nk * S5_GPB), BF16),
        compiler_params=pltpu.CompilerParams(vmem_limit_bytes=VMEM_LIMIT),
        name="s5_expand",
    )(c)


def _s5_weights(lam_re, lam_im, log_dt, b_re, b_im, c_re, c_im):
    L, G = S5_L, S5_GROUPS
    hp = lax.Precision.HIGHEST
    taus = jnp.arange(L + 1, dtype=F32)[:, None, None]
    ks, ws, cas, ds = [], [], [], []
    for d in range(2):
        lr = jnp.minimum(lam_re[d], -1e-4)
        li = lam_im[d]
        dt = jnp.exp(log_dt[d])[:, None]
        mag = jnp.exp(lr * dt)
        ab_re, ab_im = mag * jnp.cos(li * dt), mag * jnp.sin(li * dt)
        xr, xi, den = ab_re - 1.0, ab_im, lr * lr + li * li
        f_re = (xr * lr + xi * li) / den
        f_im = (xi * lr - xr * li) / den
        bb_re = f_re[..., None] * b_re - f_im[..., None] * b_im
        bb_im = f_re[..., None] * b_im + f_im[..., None] * b_re
        pmag = jnp.exp(taus * (lr * dt))
        ar, ai = pmag * jnp.cos(taus * (li * dt)), pmag * jnp.sin(taus * (li * dt))
        wr = ar[..., None] * bb_re - ai[..., None] * bb_im
        wi = ar[..., None] * bb_im + ai[..., None] * bb_re
        k = (jnp.einsum('gop,tgpi->tgio', c_re, wr, precision=hp)
             - jnp.einsum('gop,tgpi->tgio', c_im, wi, precision=hp))
        car = c_re[None] * ar[:, :, None, :] - c_im[None] * ai[:, :, None, :]
        cai = c_re[None] * ai[:, :, None, :] + c_im[None] * ar[:, :, None, :]
        ks.append(k)
        ws.append((wr, wi))
        cas.append((car, cai))
        ds.append((ar[L], ai[L]))

    split_g = lambda t, g_axis: t.reshape(t.shape[:g_axis] + (S5_NB, S5_GPB) + t.shape[g_axis + 1:]).astype(BF16)
    compact = lambda t: t.reshape(S5_NB, L * LANE, -1)
    h_log2, p_log2 = S5_GROUP.bit_length() - 1, S5_STATE.bit_length() - 1

    kf, kb = ks
    kc = jnp.concatenate([kb[1:L][::-1], (kf[0] + kb[0])[None], kf[1:L]], axis=0)
    idx = (jnp.arange(L)[None, :] - jnp.arange(L)[:, None]) + (L - 1)
    tz = _s5_expand(compact(split_g(kc[idx], 2).transpose(2, 0, 3, 4, 1, 5)), h_log2, h_log2)

    pb_c, ca_c = [], []
    for d in range(2):
        wr, wi = ws[d]
        order = jnp.arange(L - 1, -1, -1) if d == 0 else jnp.arange(L)
        pb_c.append(jnp.stack([jnp.swapaxes(w[order], -1, -2) for w in (wr, wi)]))
        car, cai = cas[d]
        order = jnp.arange(1, L + 1) if d == 0 else jnp.arange(L, 0, -1)
        ca_c.append(jnp.stack([jnp.swapaxes(m, -1, -2) for m in (car[order], -cai[order])]))
    pb = _s5_expand(compact(split_g(jnp.stack(pb_c), 3).transpose(3, 2, 4, 5, 0, 1, 6)), h_log2, p_log2)
    ca = _s5_expand(compact(split_g(jnp.stack(ca_c), 3).transpose(3, 0, 1, 4, 5, 2, 6)), p_log2, h_log2)
    dr = jnp.stack([ds[0][0], ds[1][0]], 0).reshape(2, S5_NB, 1, S5_SW).transpose(1, 0, 2, 3).reshape(2 * S5_NB, 1, S5_SW)
    di = jnp.stack([ds[0][1], ds[1][1]], 0).reshape(2, S5_NB, 1, S5_SW).transpose(1, 0, 2, 3).reshape(2 * S5_NB, 1, S5_SW)
    return tz, pb, ca, dr, di


def _s5_p_kernel(u_ref, pb_ref, p_ref, *, nb, nc):
    res = jnp.dot(u_ref[...], pb_ref[...], preferred_element_type=F32)
    for b in range(nb):
        p_ref[:, b * 2 * S5_SW:(b + 1) * 2 * S5_SW] = res[b * nc:(b + 1) * nc]


def _s5_scan_kernel(p_ref, dr_ref, di_ref, s_ref, *, n_ctx, n_lat):
    rev = pl.program_id(0) % 2
    dr = dr_ref[...]
    di = di_ref[...]
    nbatch = p_ref.shape[1]

    def phase(base, n, carry):
        def body(step, carry):
            sr, si = carry
            row = base + jnp.where(rev == 0, step, n - 1 - step)
            s_ref[row, :, :S5_SW] = sr
            s_ref[row, :, S5_SW:] = si
            p = p_ref[row]
            nr = dr * sr - di * si + p[:, :S5_SW]
            ni = dr * si + di * sr + p[:, S5_SW:]
            return nr, ni
        return lax.fori_loop(0, n, body, carry)

    zero = jnp.zeros((nbatch, S5_SW), F32)
    carry = phase(0, n_ctx, (zero, zero))
    phase(n_ctx, n_lat, carry)


def _s5_y_kernel(u_ref, tz_ref, sf_ref, sb_ref, ca_ref, y_ref):
    y = jnp.dot(u_ref[...], tz_ref[...], preferred_element_type=F32)
    y += jnp.dot(sf_ref[...].astype(BF16), ca_ref[:2 * S5_SW, :], preferred_element_type=F32)
    y += jnp.dot(sb_ref[...].astype(BF16), ca_ref[2 * S5_SW:, :], preferred_element_type=F32)
    y_ref[...] = y


def s5_bidirectional(u_ctx, u_lat, weights):
    tz, pb, ca, dr, di = weights
    L = S5_L
    nb, t_ctx, _ = u_ctx.shape
    t_lat = u_lat.shape[1]
    assert t_ctx % L == 0 and t_lat % L == 0
    n_ctx, n_lat = t_ctx // L, t_lat // L
    nc = n_ctx + n_lat
    kw = L * LANE
    sw2 = 2 * S5_SW
    u = jnp.concatenate([u_ctx, u_lat], axis=1)
    ub = u.reshape(nb * nc, L, S5_NB, LANE).transpose(2, 0, 1, 3).reshape(S5_NB, nb * nc, kw).astype(BF16)

    p = pl.pallas_call(
        functools.partial(_s5_p_kernel, nb=nb, nc=nc),
        grid=(S5_NB, 2),
        in_specs=[pl.BlockSpec((None, nb * nc, kw), lambda j, d: (j, 0, 0)),
                  pl.BlockSpec((None, kw, sw2), lambda j, d: (j, 0, d))],
        out_specs=pl.BlockSpec((nc, nb * sw2), lambda j, d: (0, j * 2 + d)),
        out_shape=jax.ShapeDtypeStruct((nc, S5_NB * 2 * nb * sw2), F32),
        compiler_params=pltpu.CompilerParams(vmem_limit_bytes=VMEM_LIMIT),
        name="s5_chunk_inputs",
    )(ub, pb)

    p4 = p.reshape(nc, S5_NB * 2, nb, sw2)
    s4 = pl.pallas_call(
        functools.partial(_s5_scan_kernel, n_ctx=n_ctx, n_lat=n_lat),
        grid=(S5_NB * 2,),
        in_specs=[pl.BlockSpec((nc, None, nb, sw2), lambda g: (0, g, 0, 0)),
                  pl.BlockSpec((None, 1, S5_SW), lambda g: (g, 0, 0)),
                  pl.BlockSpec((None, 1, S5_SW), lambda g: (g, 0, 0))],
        out_specs=pl.BlockSpec((nc, None, nb, sw2), lambda g: (0, g, 0, 0)),
        out_shape=jax.ShapeDtypeStruct(p4.shape, F32),
        compiler_params=pltpu.CompilerParams(vmem_limit_bytes=VMEM_LIMIT),
        name="s5_state_scan",
    )(p4, dr, di)

    s2 = s4.reshape(nc, S5_NB * 2 * nb * sw2)
    yb = pl.pallas_call(
        _s5_y_kernel,
        grid=(S5_NB, nb),
        in_specs=[pl.BlockSpec((None, nc, kw), lambda j, b: (j, b, 0)),
                  pl.BlockSpec((None, kw, kw), lambda j, b: (j, 0, 0)),
                  pl.BlockSpec((nc, sw2), lambda j, b: (0, (j * 2) * nb + b)),
                  pl.BlockSpec((nc, sw2), lambda j, b: (0, (j * 2 + 1) * nb + b)),
                  pl.BlockSpec((None, 2 * sw2, kw), lambda j, b: (j, 0, 0))],
        out_specs=pl.BlockSpec((None, nc, kw), lambda j, b: (j, b, 0)),
        out_shape=jax.ShapeDtypeStruct((S5_NB, nb * nc, kw), F32),
        compiler_params=pltpu.CompilerParams(vmem_limit_bytes=VMEM_LIMIT),
        name="s5_output",
    )(ub, tz, s2, s2, ca)
    y = yb.reshape(S5_NB, nb, nc, L, LANE).transpose(1, 2, 3, 0, 4).reshape(nb, nc * L, S5_CH)
    return y[:, :t_ctx], y[:, t_ctx:]


ROUTE_TN = 512
MOE_TN = 1024
SEL_ROWS = 128
FF_SPLIT = 2
AFF_PARTS = 3


def _route_kernel(h_ref, sc_ref, sh_ref, wr_ref, hm_ref, afft_ref, asp_ref):
    hm = h_ref[...] * (1.0 + sc_ref[...]) + sh_ref[...]
    hm_ref[...] = hm.astype(BF16)
    logits = jnp.dot(hm, wr_ref[...], precision=lax.Precision.HIGHEST, preferred_element_type=F32)
    lane = lax.broadcasted_iota(jnp.int32, logits.shape, 1)
    logits = jnp.where(lane < N_EXPERTS, logits, -jnp.inf)
    ex = jnp.exp(logits - jnp.max(logits, axis=1, keepdims=True))
    aff = ex / jnp.sum(ex, axis=1, keepdims=True)
    afft_ref[...] = aff.T[:N_EXPERTS, :]
    hi = aff.astype(BF16).astype(F32)
    mid = (aff - hi).astype(BF16).astype(F32)
    lo = (aff - hi - mid).astype(BF16).astype(F32)
    asp = hi + pltpu.roll(mid, N_EXPERTS, 1) + pltpu.roll(lo, 2 * N_EXPERTS, 1)
    asp_ref[...] = asp.astype(BF16)


def _select_kernel(aff_ref, pos_ref, *, nblk_log2, cap):
    a = aff_ref[...]
    r = a.shape[0]
    bits = pltpu.bitcast(a, jnp.int32)
    ri = lax.broadcasted_iota(jnp.int32, (r, r), 0)
    rj = lax.broadcasted_iota(jnp.int32, (r, r), 1)
    same = (ri >> nblk_log2) == (rj >> nblk_log2)
    gm = jnp.where(same, 1.0, 0.0).astype(BF16)
    lm = jnp.where(same & (rj < ri), 1.0, 0.0).astype(BF16)
    li = lax.broadcasted_iota(jnp.int32, (LANE, LANE), 0)
    lj = lax.broadcasted_iota(jnp.int32, (LANE, LANE), 1)
    um = jnp.where(li <= lj, 1.0, 0.0).astype(BF16)

    def group_count(mask):
        rc = jnp.sum(jnp.where(mask, 1.0, 0.0), axis=1, keepdims=True)
        gc = jnp.dot(gm, jnp.broadcast_to(rc, (r, LANE)).astype(BF16), preferred_element_type=F32)
        return gc[:, :1]

    def bisect(i, thr):
        cand = thr | jnp.left_shift(1, 29 - i)
        return jnp.where(group_count(bits >= cand) >= cap, cand, thr)

    thr = lax.fori_loop(0, 30, bisect, jnp.zeros((r, 1), jnp.int32))

    def prefix(mask):
        x = jnp.where(mask, 1.0, 0.0)
        inc = jnp.dot(x.astype(BF16), um, preferred_element_type=F32)
        tot = jnp.broadcast_to(inc[:, LANE - 1:LANE], (r, LANE)).astype(BF16)
        return inc - x + jnp.dot(lm, tot, preferred_element_type=F32)

    gt = bits > thr
    eq = bits == thr
    need = cap - group_count(gt)
    sel = gt | (eq & (prefix(eq) < need))
    pos_ref[...] = jnp.where(sel, prefix(sel), -1.0).astype(jnp.int32)


def _onehot(pos_ref, cap):
    rows = lax.broadcasted_iota(jnp.int32, (cap, LANE), 0)
    blocks = [jnp.where(pos_ref[k:k + 1, :] == rows, 1.0, 0.0).astype(BF16) for k in range(pos_ref.shape[0])]
    return jnp.concatenate(blocks, axis=1)


def _ffn_kernel(pos_ref, hb_ref, asp_ref, wg_ref, wu_ref, wd_ref, ys_ref, xs_acc, g_acc, *, cap):
    e = pl.program_id(0)
    kt = pl.program_id(2)

    @pl.when(kt == 0)
    def _():
        xs_acc[...] = jnp.zeros_like(xs_acc)
        g_acc[...] = jnp.zeros_like(g_acc)

    oh = _onehot(pos_ref, cap)
    xs_acc[...] += jnp.dot(oh, hb_ref[...], preferred_element_type=F32)
    g_acc[...] += jnp.dot(oh, asp_ref[...], preferred_element_type=F32)

    @pl.when(kt == pl.num_programs(2) - 1)
    def _():
        xs = xs_acc[...].astype(BF16)
        g = g_acc[...]
        lane = lax.broadcasted_iota(jnp.int32, g.shape, 1)
        gate = jnp.sum(jnp.where((lane & (N_EXPERTS - 1)) == e, g, 0.0), axis=1, keepdims=True)
        fw = EXPERT_FF // FF_SPLIT
        y = jnp.zeros((cap, D_MODEL), F32)
        for f in range(FF_SPLIT):
            hg = jnp.dot(xs, wg_ref[:, f * fw:(f + 1) * fw], preferred_element_type=F32)
            hu = jnp.dot(xs, wu_ref[:, f * fw:(f + 1) * fw], preferred_element_type=F32)
            hid = (hg * jax.nn.sigmoid(hg)) * hu
            y += jnp.dot(hid.astype(BF16), wd_ref[f * fw:(f + 1) * fw, :], preferred_element_type=F32)
        ys_ref[...] = (y * gate).astype(BF16)


def _ln_rows(z, g, b):
    mu = jnp.mean(z, axis=-1, keepdims=True)
    zc = z - mu
    var = jnp.mean(zc * zc, axis=-1, keepdims=True)
    return zc * lax.rsqrt(var + EPS) * g + b


def _combine_kernel(pos_ref, ys_ref, h_ref, g2_ref, lng_ref, lnb_ref, o_ref, acc, *, cap):
    e = pl.program_id(2)

    @pl.when(e == 0)
    def _():
        acc[...] = jnp.zeros_like(acc)

    oh = _onehot(pos_ref, cap)
    acc[...] += lax.dot_general(oh, ys_ref[...], (((0,), (0,)), ((), ())), preferred_element_type=F32)

    @pl.when(e == pl.num_programs(2) - 1)
    def _():
        z = ALPHA * h_ref[...] + g2_ref[...] * acc[...]
        o_ref[...] = _ln_rows(z, lng_ref[...], lnb_ref[...])


def moe_block(h, sc, sh, g2, ln_g, ln_b, w_router, wg, wu, wd, layer):
    nb, n, dm = h.shape
    cap = CAPACITY_FACTOR * n // N_EXPERTS
    nblk = n // LANE
    assert n % LANE == 0 and nblk & (nblk - 1) == 0
    tn_r = min(ROUTE_TN, n)
    tn = min(MOE_TN, n)
    wr = jnp.pad(w_router, ((0, 0), (0, LANE - N_EXPERTS)))
    row = lambda v: v.reshape(1, dm)

    hm, afft, asp = pl.pallas_call(
        _route_kernel,
        grid=(nb, n // tn_r),
        in_specs=[pl.BlockSpec((None, tn_r, dm), lambda b, t: (b, t, 0)),
                  pl.BlockSpec((None, 1, dm), lambda b, t: (b, 0, 0)),
                  pl.BlockSpec((None, 1, dm), lambda b, t: (b, 0, 0)),
                  pl.BlockSpec((dm, LANE), lambda b, t: (0, 0))],
        out_specs=[pl.BlockSpec((None, tn_r, dm), lambda b, t: (b, t, 0)),
                   pl.BlockSpec((None, N_EXPERTS, tn_r), lambda b, t: (b, 0, t)),
                   pl.BlockSpec((None, tn_r, LANE), lambda b, t: (b, t, 0))],
        out_shape=[jax.ShapeDtypeStruct((nb, n, dm), BF16),
                   jax.ShapeDtypeStruct((nb, N_EXPERTS, n), F32),
                   jax.ShapeDtypeStruct((nb, n, LANE), BF16)],
        name="moe_route",
    )(h, sc, sh, wr)

    rows_total = nb * N_EXPERTS * nblk
    rb = max(SEL_ROWS, N_EXPERTS * nblk)
    assert rows_total % rb == 0
    pos = pl.pallas_call(
        functools.partial(_select_kernel, nblk_log2=nblk.bit_length() - 1, cap=cap),
        grid=(rows_total // rb,),
        in_specs=[pl.BlockSpec((rb, LANE), lambda i: (i, 0))],
        out_specs=pl.BlockSpec((rb, LANE), lambda i: (i, 0)),
        out_shape=jax.ShapeDtypeStruct((rows_total, LANE), jnp.int32),
        name="moe_select",
    )(afft.reshape(rows_total, LANE))
    pos = pos.reshape(nb, N_EXPERTS, nblk, LANE)

    tb = tn // LANE
    ys = pl.pallas_call(
        functools.partial(_ffn_kernel, cap=cap),
        grid=(N_EXPERTS, nb, n // tn),
        in_specs=[pl.BlockSpec((None, None, tb, LANE), lambda e, b, k: (b, e, k, 0)),
                  pl.BlockSpec((None, tn, dm), lambda e, b, k: (b, k, 0)),
                  pl.BlockSpec((None, tn, LANE), lambda e, b, k: (b, k, 0)),
                  pl.BlockSpec((None, None, dm, EXPERT_FF), lambda e, b, k: (layer, e, 0, 0)),
                  pl.BlockSpec((None, None, dm, EXPERT_FF), lambda e, b, k: (layer, e, 0, 0)),
                  pl.BlockSpec((None, None, EXPERT_FF, dm), lambda e, b, k: (layer, e, 0, 0))],
        out_specs=pl.BlockSpec((None, None, cap, dm), lambda e, b, k: (b, e, 0, 0)),
        out_shape=jax.ShapeDtypeStruct((nb, N_EXPERTS, cap, dm), BF16),
        scratch_shapes=[pltpu.VMEM((cap, dm), F32), pltpu.VMEM((cap, LANE), F32)],
        compiler_params=pltpu.CompilerParams(vmem_limit_bytes=VMEM_LIMIT),
        name="moe_ffn",
    )(pos, hm, asp, wg, wu, wd)

    return pl.pallas_call(
        functools.partial(_combine_kernel, cap=cap),
        grid=(nb, n // tn, N_EXPERTS),
        in_specs=[pl.BlockSpec((None, None, tb, LANE), lambda b, t, e: (b, e, t, 0)),
                  pl.BlockSpec((None, None, cap, dm), lambda b, t, e: (b, e, 0, 0)),
                  pl.BlockSpec((None, tn, dm), lambda b, t, e: (b, t, 0)),
                  pl.BlockSpec((None, 1, dm), lambda b, t, e: (b, 0, 0)),
                  pl.BlockSpec((1, dm), lambda b, t, e: (0, 0)),
                  pl.BlockSpec((1, dm), lambda b, t, e: (0, 0))],
        out_specs=pl.BlockSpec((None, tn, dm), lambda b, t, e: (b, t, 0)),
        out_shape=jax.ShapeDtypeStruct((nb, n, dm), F32),
        scratch_shapes=[pltpu.VMEM((tn, dm), F32)],
        compiler_params=pltpu.CompilerParams(vmem_limit_bytes=VMEM_LIMIT),
        name="moe_combine",
    )(pos, ys, h, g2, row(ln_g), row(ln_b))


N_HEADS = 4
STACK = N_HEADS * CHUNK
CHUNK_LOG2 = CHUNK.bit_length() - 1
SOLVE_LEVELS = CHUNK_LOG2


def _stack_heads(x):
    return jnp.concatenate([x[:, h * HEAD_DIM:(h + 1) * HEAD_DIM] for h in range(N_HEADS)], axis=0)


def _stack_cols(cols, width):
    return jnp.concatenate([jnp.broadcast_to(c, (CHUNK, width)) for c in cols], axis=0)


def _mxu(a, b, dims, hp):
    dg = lambda x, y: lax.dot_general(x, y, (dims, ((), ())), preferred_element_type=F32)
    a_hi, b_hi = a.astype(BF16), b.astype(BF16)
    if not hp:
        return dg(a_hi, b_hi)
    a_lo = (a - a_hi.astype(F32)).astype(BF16)
    b_lo = (b - b_hi.astype(F32)).astype(BF16)
    return dg(a_hi, b_hi) + (dg(a_lo, b_hi) + dg(a_hi, b_lo))


def _dot_nt(a, b, hp=False):
    return _mxu(a, b, ((1,), (1,)), hp)


def _dot_tn(a, b, hp=False):
    return _mxu(a, b, ((0,), (0,)), hp)


def _dot(a, b, hp=False):
    return _mxu(a, b, ((1,), (0,)), hp)


def _chunk_masks(rev):
    ri = lax.broadcasted_iota(jnp.int32, (STACK, STACK), 0)
    ci = lax.broadcasted_iota(jnp.int32, (STACK, STACK), 1)
    same = (ri >> CHUNK_LOG2) == (ci >> CHUNK_LOG2)
    ahead = jnp.where(rev, ci - ri, ri - ci)
    return same & (ahead >= 0), same & (ahead > 0)


def _scan_cumsum(gt, rev):
    ii = lax.broadcasted_iota(jnp.int32, (CHUNK, CHUNK), 0)
    jj = lax.broadcasted_iota(jnp.int32, (CHUNK, CHUNK), 1)
    tri = jnp.where(jnp.where(rev, jj - ii, ii - jj) >= 0, 1.0, 0.0)
    return jnp.dot(tri, gt, precision=lax.Precision.HIGHEST, preferred_element_type=F32)


def _unit_triangular_inverse(a):
    ri = lax.broadcasted_iota(jnp.int32, a.shape, 0)
    ci = lax.broadcasted_iota(jnp.int32, a.shape, 1)
    joins = lambda lvl: ((ri >> (lvl + 1)) == (ci >> (lvl + 1))) & ((ri >> lvl) != (ci >> lvl))
    t = jnp.where(ri == ci, 1.0, 0.0) - jnp.where(joins(0), a, 0.0)
    for lvl in range(1, CHUNK_LOG2):
        m = _dot(jnp.where(joins(lvl), a, 0.0), t)
        yield
        t = t - _dot(t, m)
        yield
    return t


def _interleave(chains):
    for _ in zip(*chains):
        pass


def _linear_scan_kernel(*refs, has_beta, k_scale):
    if has_beta:
        q_ref, k_ref, v_ref, gt_ref, gp_ref, s0_ref, o_ref, sfin_ref, s_scr = refs
    else:
        q_ref, k_ref, v_ref, gp_ref, s0_ref, o_ref, sfin_ref, s_scr = refs
    rev = pl.program_id(0) == 1
    n = pl.program_id(2)

    @pl.when(n == 0)
    def _():
        s_scr[...] = s0_ref[...]

    chains = []
    for i in range(q_ref.shape[0]):
        if has_beta:
            gt = _gdn_gates(gt_ref[i], gp_ref[...])
        else:
            gt = jnp.broadcast_to(gp_ref[0:1, :], (CHUNK, LANE))
        chains.append(_linear_chunk(q_ref.at[i], k_ref.at[i], v_ref.at[i], gt, o_ref.at[i], s_scr.at[i], rev,
                                    has_beta, k_scale))
    _interleave(chains)

    @pl.when(n == pl.num_programs(2) - 1)
    def _():
        sfin_ref[...] = s_scr[...]


def _softplus(x):
    return jnp.maximum(x, 0.0) + jnp.log1p(jnp.exp(-jnp.abs(x)))


def _gdn_gates(raw, gp):
    lane = lax.broadcasted_iota(jnp.int32, raw.shape, 1)
    return jnp.where(lane < 2 * N_HEADS, gp[0:1, :] * _softplus(raw + gp[1:2, :]), jax.nn.sigmoid(raw))


def _mlstm_gates(raw, gp):
    lane = lax.broadcasted_iota(jnp.int32, raw.shape, 1)
    x = raw + gp[0:1, :]
    return jnp.where(lane < 2 * N_HEADS, x, -_softplus(-x))


def _linear_chunk(q_ref, k_ref, v_ref, gt, o_ref, s_scr, rev, has_beta, k_scale):
    cum = _scan_cumsum(gt, rev)
    gcols, bcols, gtots = [], [], []
    for h in range(N_HEADS):
        gc = jnp.where(rev, cum[:, N_HEADS + h:N_HEADS + h + 1], cum[:, h:h + 1])
        gcols.append(gc)
        gtots.append(jnp.where(rev, gc[0:1], gc[CHUNK - 1:CHUNK]))
        bcols.append(jnp.where(rev, gt[:, 3 * N_HEADS + h:3 * N_HEADS + h + 1], gt[:, 2 * N_HEADS + h:2 * N_HEADS + h + 1]))
    cb = _stack_cols(gcols, STACK)
    diff = cb - cb.T
    incl, strict = _chunk_masks(rev)
    dec = jnp.exp(jnp.where(incl, diff, 0.0))
    gcb = cb[:, :HEAD_DIM]
    q_st, k_st, v_st = _stack_heads(q_ref[...]), _stack_heads(k_ref[...]) * k_scale, _stack_heads(v_ref[...])
    a_qk = _dot_nt(q_st, k_st) * jnp.where(incl, dec, 0.0)
    if has_beta:
        beta = _stack_cols(bcols, HEAD_DIM)
        kb = k_st * beta
        a = _dot_nt(kb, k_st) * jnp.where(strict, dec, 0.0)
        yield
        t_inv = yield from _unit_triangular_inverse(a)
        x = _dot(t_inv, jnp.concatenate([v_st * beta, kb * jnp.exp(gcb)], axis=1))
        u_st, w_st = x[:, :HEAD_DIM], x[:, HEAD_DIM:]
    else:
        u_st, w_st = v_st, None
    gtot = _stack_cols(gtots, HEAD_DIM)
    k_end = k_st * jnp.exp(gtot - gcb)
    q_dec = q_st * jnp.exp(gcb)
    hs = lambda t, h: t[h * CHUNK:(h + 1) * CHUNK]
    states = [s_scr[h] for h in range(N_HEADS)]
    q_s = [_dot(hs(q_dec, h), states[h]) for h in range(N_HEADS)]
    yield
    if has_beta:
        vn = jnp.concatenate([hs(u_st, h) - _dot(hs(w_st, h), states[h]) for h in range(N_HEADS)], axis=0)
    else:
        vn = u_st
    yield
    o_st = _dot(a_qk, vn)
    for h in range(N_HEADS):
        s_scr[h] = jnp.exp(gtots[h]) * states[h] + _dot_tn(hs(k_end, h), hs(vn, h))
    yield
    for h in range(N_HEADS):
        o_ref[:, h * HEAD_DIM:(h + 1) * HEAD_DIM] = hs(o_st, h) + q_s[h]
    yield


SCAN_NS = 4


HEADS_W = N_HEADS * HEAD_DIM


def _scan_specs(srcs, gates, nb, nchunk):
    ns = SCAN_NS
    assert nb % ns == 0
    cidx = lambda d, n: n + d * (nchunk - 1 - 2 * n)
    specs = [pl.BlockSpec((ns, CHUNK, HEADS_W), lambda d, b, n, c=c: (b, cidx(d, n), c)) for _, c in srcs]
    if gates is not None:
        specs.append(pl.BlockSpec((ns, CHUNK, LANE), lambda d, b, n, c=gates[1]: (b, cidx(d, n), c)))
    specs.append(pl.BlockSpec((2, LANE), lambda d, b, n: (0, 0)))
    ospec = pl.BlockSpec((None, ns, CHUNK, HEADS_W), lambda d, b, n: (d, b, cidx(d, n), 0))
    return specs, ospec


def linear_scan_bidir(q, k, v, gates, gp, s0, has_beta, k_scale=1.0):
    nb, t, _ = q[0].shape
    nchunk = t // CHUNK
    assert t % CHUNK == 0
    ns = SCAN_NS
    specs, ospec = _scan_specs((q, k, v), gates if has_beta else None, nb, nchunk)
    sspec = pl.BlockSpec((None, ns, N_HEADS, HEAD_DIM, HEAD_DIM), lambda d, b, n: (d, b, 0, 0, 0))
    args = [q[0], k[0], v[0]] + ([gates[0]] if has_beta else []) + [gp, s0]
    return pl.pallas_call(
        functools.partial(_linear_scan_kernel, has_beta=has_beta, k_scale=k_scale),
        grid=(2, nb // ns, nchunk),
        in_specs=specs + [sspec],
        out_specs=[ospec, sspec],
        out_shape=[jax.ShapeDtypeStruct((2, nb, t, HEADS_W), F32),
                   jax.ShapeDtypeStruct((2, nb, N_HEADS, HEAD_DIM, HEAD_DIM), F32)],
        scratch_shapes=[pltpu.VMEM((ns, N_HEADS, HEAD_DIM, HEAD_DIM), F32)],
        name="gdn_scan" if has_beta else "retention_scan",
    )(*args)


def linear_scan_two_pass(ctx_args, lat_args, gp, has_beta, k_scale=1.0):
    nb = ctx_args[0][0].shape[0]
    zero = jnp.zeros((2, nb, N_HEADS, HEAD_DIM, HEAD_DIM), F32)
    o_ctx, s_ctx = linear_scan_bidir(*ctx_args, gp, zero, has_beta, k_scale)
    o_lat, _ = linear_scan_bidir(*lat_args, gp, s_ctx, has_beta, k_scale)
    return o_ctx, o_lat


MLSTM_HP = True


def _mlstm_scan_kernel(q_ref, k_ref, v_ref, gt_ref, gp_ref, c0_ref, n0_ref, m0_ref, o_ref, cfin_ref, nfin_ref,
                       mfin_ref, c_scr, n_scr, m_scr):
    rev = pl.program_id(0) == 1
    step = pl.program_id(2)

    @pl.when(step == 0)
    def _():
        c_scr[...] = c0_ref[...]
        n_scr[...] = n0_ref[...]
        m_scr[...] = m0_ref[...]

    _interleave([_mlstm_chunk(q_ref.at[i], k_ref.at[i], v_ref.at[i], _mlstm_gates(gt_ref[i], gp_ref[...]),
                              o_ref.at[i], c_scr.at[i], n_scr.at[i], m_scr.at[i], rev)
                 for i in range(q_ref.shape[0])])

    @pl.when(step == pl.num_programs(2) - 1)
    def _():
        cfin_ref[...] = c_scr[...]
        nfin_ref[...] = n_scr[...]
        mfin_ref[...] = m_scr[...]


def _mlstm_chunk(q_ref, k_ref, v_ref, gt, o_ref, c_scr, n_scr, m_scr, rev):
    cum = _scan_cumsum(gt, rev)
    pick = lambda t, c: jnp.where(rev, t[:, N_HEADS + c:N_HEADS + c + 1], t[:, c:c + 1])
    q_st, k_st, v_st = _stack_heads(q_ref[...]), _stack_heads(k_ref[...]), _stack_heads(v_ref[...])
    hs = lambda t, h: t[h * CHUNK:(h + 1) * CHUNK]
    qk = _dot_nt(q_st, k_st, MLSTM_HP)
    yield
    bcums, srcs, inters, qcs, qns = [], [], [], [], []
    for h in range(N_HEADS):
        ic = pick(gt, h)
        bcum = pick(cum, 2 * N_HEADS + h)
        b_end = jnp.where(rev, bcum[0:1], bcum[CHUNK - 1:CHUNK])
        c_prev, n_prev, m_prev = c_scr[h], n_scr[h], m_scr[h][:, :1]
        a = b_end - bcum + ic
        m_new = jnp.maximum(b_end + m_prev, jnp.max(a, axis=0, keepdims=True))
        w_state = jnp.exp(a - m_new)
        decay = jnp.exp(b_end + m_prev - m_new)
        kw = hs(k_st, h) * w_state
        c_scr[h] = decay * c_prev + _dot_tn(kw, hs(v_st, h), MLSTM_HP)
        n_scr[h] = decay * n_prev + jnp.sum(kw, axis=0, keepdims=True)
        m_scr[h] = jnp.broadcast_to(m_new, (1, HEAD_DIM))
        bcums.append(bcum)
        srcs.append(bcum - ic)
        inters.append(bcum + m_prev)
        qcs.append(_dot(hs(q_st, h), c_prev, MLSTM_HP))
        qns.append(jnp.sum(hs(q_st, h) * n_prev, axis=1, keepdims=True))
    yield
    incl, _ = _chunk_masks(rev)
    dlog = _stack_cols(bcums, STACK) - _stack_cols(srcs, STACK).T
    inter = jnp.concatenate(inters, axis=0)
    m_t = jnp.maximum(inter, jnp.max(jnp.where(incl, dlog, -1e30), axis=1, keepdims=True))
    s = qk * jnp.where(incl, jnp.exp(jnp.where(incl, dlog, 0.0) - m_t), 0.0)
    w_inter = jnp.exp(inter - m_t)
    yield
    num = _dot(s, v_st, MLSTM_HP) + w_inter * jnp.concatenate(qcs, axis=0)
    den = jnp.sum(s, axis=1, keepdims=True) + w_inter * jnp.concatenate(qns, axis=0)
    yield
    out = num / jnp.maximum(jnp.abs(den), jnp.exp(-m_t))
    for h in range(N_HEADS):
        o_ref[:, h * HEAD_DIM:(h + 1) * HEAD_DIM] = hs(out, h)
    yield


def mlstm_scan_bidir(q, k, v, gates, gp, state):
    nb, t, _ = q[0].shape
    nchunk = t // CHUNK
    assert t % CHUNK == 0
    ns = SCAN_NS
    specs, ospec = _scan_specs((q, k, v), gates, nb, nchunk)
    cspec = pl.BlockSpec((None, ns, N_HEADS, HEAD_DIM, HEAD_DIM), lambda d, b, n: (d, b, 0, 0, 0))
    vspec = pl.BlockSpec((None, ns, N_HEADS, 1, HEAD_DIM), lambda d, b, n: (d, b, 0, 0, 0))
    cshape = jax.ShapeDtypeStruct((2, nb, N_HEADS, HEAD_DIM, HEAD_DIM), F32)
    vshape = jax.ShapeDtypeStruct((2, nb, N_HEADS, 1, HEAD_DIM), F32)
    o, c, n, m = pl.pallas_call(
        _mlstm_scan_kernel,
        grid=(2, nb // ns, nchunk),
        in_specs=specs + [cspec, vspec, vspec],
        out_specs=[ospec, cspec, vspec, vspec],
        out_shape=[jax.ShapeDtypeStruct((2, nb, t, HEADS_W), F32), cshape, vshape, vshape],
        scratch_shapes=[pltpu.VMEM((ns, N_HEADS, HEAD_DIM, HEAD_DIM), F32), pltpu.VMEM((ns, N_HEADS, 1, HEAD_DIM), F32),
                        pltpu.VMEM((ns, N_HEADS, 1, HEAD_DIM), F32)],
        name="mlstm_scan",
    )(q[0], k[0], v[0], gates[0], gp, *state)
    return o, (c, n, m)


def mlstm_two_pass(ctx_args, lat_args, gp):
    nb = ctx_args[0][0].shape[0]
    zero = (jnp.zeros((2, nb, N_HEADS, HEAD_DIM, HEAD_DIM), F32), jnp.zeros((2, nb, N_HEADS, 1, HEAD_DIM), F32),
            jnp.zeros((2, nb, N_HEADS, 1, HEAD_DIM), F32))
    o_ctx, s_ctx = mlstm_scan_bidir(*ctx_args, gp, zero)
    o_lat, _ = mlstm_scan_bidir(*lat_args, gp, s_ctx)
    return o_ctx, o_lat


PROJ_TM = 512
PROJ_TN_MAX = 2304


def _inproj_kernel(h_ref, sc_ref, sh_ref, w_ref, o_ref):
    hm = (h_ref[...] * (1.0 + sc_ref[...]) + sh_ref[...]).astype(BF16)
    o_ref[...] = jnp.dot(hm, w_ref[...], preferred_element_type=F32)


def in_projection(h, sc, sh, w):
    nb, n, dm = h.shape
    ncol = w.shape[1]
    tm = min(PROJ_TM, n)
    tn = max(c for c in range(LANE, min(ncol, PROJ_TN_MAX) + 1, LANE) if ncol % c == 0)
    return pl.pallas_call(
        _inproj_kernel,
        grid=(nb, n // tm, ncol // tn),
        in_specs=[pl.BlockSpec((None, tm, dm), lambda b, t, j: (b, t, 0)),
                  pl.BlockSpec((None, 1, dm), lambda b, t, j: (b, 0, 0)),
                  pl.BlockSpec((None, 1, dm), lambda b, t, j: (b, 0, 0)),
                  pl.BlockSpec((dm, tn), lambda b, t, j: (0, j))],
        out_specs=pl.BlockSpec((None, tm, tn), lambda b, t, j: (b, t, j)),
        out_shape=jax.ShapeDtypeStruct((nb, n, ncol), F32),
        name="in_projection",
    )(h, sc, sh, w)


def _conv_kernel(x_ref, w_ref, o_ref, *, grid_w, l2_blocks, q_scale, k_scale):
    x = x_ref[...]
    t = x.shape[0]
    w = w_ref[...]
    col = lax.broadcasted_iota(jnp.int32, x.shape, 0) & (grid_w - 1)
    left = jnp.where(col == 0, 0.0, pltpu.roll(x, 1, 0))
    right = jnp.where(col == grid_w - 1, 0.0, pltpu.roll(x, t - 1, 0))
    row = lambda kh: w[3 * kh:3 * kh + 1] * left + w[3 * kh + 1:3 * kh + 2] * x + w[3 * kh + 2:3 * kh + 3] * right
    acc = row(1)
    if grid_w < t:
        zero = jnp.zeros((grid_w, LANE), F32)
        acc = acc + jnp.concatenate([zero, row(0)[:t - grid_w]], axis=0) + jnp.concatenate([row(2)[grid_w:], zero], axis=0)
    y = acc * jax.nn.sigmoid(acc)
    c = pl.program_id(1)
    normed = y * lax.rsqrt(jnp.sum(y * y, axis=1, keepdims=True) + 1e-6)
    y = jnp.where(c < l2_blocks, normed, y)
    o_ref[...] = y * jnp.where(c < N_HEADS, q_scale, jnp.where(c < 2 * N_HEADS, k_scale, 1.0))


def conv_prep(proj, conv_w, nblk, grid_w, l2_blocks, q_scale, k_scale):
    nb, t, _ = proj.shape
    assert grid_w & (grid_w - 1) == 0 and t % grid_w == 0
    return pl.pallas_call(
        functools.partial(_conv_kernel, grid_w=grid_w, l2_blocks=l2_blocks, q_scale=q_scale, k_scale=k_scale),
        grid=(nb, nblk),
        in_specs=[pl.BlockSpec((None, t, LANE), lambda b, c: (b, 0, c)),
                  pl.BlockSpec((CONV_K * CONV_K, LANE), lambda b, c: (0, c))],
        out_specs=pl.BlockSpec((None, t, LANE), lambda b, c: (b, 0, c)),
        out_shape=jax.ShapeDtypeStruct((nb, t, nblk * LANE), F32),
        compiler_params=pltpu.CompilerParams(vmem_limit_bytes=VMEM_LIMIT),
        name="conv_prep",
    )(proj, conv_w.reshape(CONV_K * CONV_K, -1))


def _head_norm(o, center):
    outs = []
    for h in range(N_HEADS):
        x = o[:, h * HEAD_DIM:(h + 1) * HEAD_DIM]
        if center:
            x = x - jnp.mean(x, axis=1, keepdims=True)
        outs.append(x * lax.rsqrt(jnp.mean(x * x, axis=1, keepdims=True) + EPS))
    return jnp.concatenate(outs, axis=1)


def _mix_out(y_a, y_b, wo_ref, h_ref, g1_ref, lng_ref, lnb_ref, o_ref):
    y = (jnp.dot(y_a.astype(BF16), wo_ref[:HEADS_W, :], preferred_element_type=F32)
         + jnp.dot(y_b.astype(BF16), wo_ref[HEADS_W:, :], preferred_element_type=F32))
    o_ref[...] = _ln_rows(ALPHA * h_ref[...] + g1_ref[...] * y, lng_ref[...], lnb_ref[...])


def _merge_even_kernel(og_ref, or_ref, za_ref, zr_ref, gg_ref, rg_ref, wo_ref, h_ref, g1_ref, lng_ref, lnb_ref, o_ref):
    za, zr = za_ref[...], zr_ref[...]
    y_g = _head_norm(og_ref[0] + og_ref[1], False) * gg_ref[...] * (za * jax.nn.sigmoid(za))
    y_r = _head_norm(or_ref[0] + or_ref[1], True) * rg_ref[...] * (zr * jax.nn.sigmoid(zr))
    _mix_out(y_g, y_r, wo_ref, h_ref, g1_ref, lng_ref, lnb_ref, o_ref)


def _merge_odd_kernel(om_ref, ys_ref, og_ref, u_ref, mg_ref, dsk_ref, bglu_ref, wglu_ref, wo_ref, h_ref, g1_ref,
                      lng_ref, lnb_ref, o_ref):
    y_m = _head_norm(om_ref[0] + om_ref[1], True) * mg_ref[...] * jax.nn.sigmoid(og_ref[...])
    y = jax.nn.gelu(ys_ref[...] + dsk_ref[...] * u_ref[...])
    y = y * jax.nn.sigmoid(jnp.dot(y.astype(BF16), wglu_ref[...], preferred_element_type=F32) + bglu_ref[...])
    _mix_out(y_m, y, wo_ref, h_ref, g1_ref, lng_ref, lnb_ref, o_ref)


def _merge_call(kernel_fn, name, scans, toks, rows, mats, h, g1, ln_g, ln_b):
    nb, n, dm = h.shape
    tm = min(PROJ_TM, n)
    full = lambda a: pl.BlockSpec(a.shape, lambda b, t: (0,) * a.ndim)
    rowv = lambda v: v.reshape(1, -1)
    specs, args = [], []
    for a in scans:
        specs.append(pl.BlockSpec((2, None, tm, HEADS_W), lambda b, t: (0, b, t, 0)))
        args.append(a)
    for a, c in toks:
        specs.append(pl.BlockSpec((None, tm, HEADS_W), lambda b, t, c=c: (b, t, c)))
        args.append(a)
    for v in rows:
        args.append(rowv(v))
        specs.append(full(args[-1]))
    for m in mats:
        args.append(m)
        specs.append(full(m))
    args += [h, g1, rowv(ln_g), rowv(ln_b)]
    specs += [pl.BlockSpec((None, tm, dm), lambda b, t: (b, t, 0)), pl.BlockSpec((None, 1, dm), lambda b, t: (b, 0, 0)),
              full(args[-2]), full(args[-1])]
    return pl.pallas_call(
        kernel_fn,
        grid=(nb, n // tm),
        in_specs=specs,
        out_specs=pl.BlockSpec((None, tm, dm), lambda b, t: (b, t, 0)),
        out_shape=jax.ShapeDtypeStruct((nb, n, dm), F32),
        compiler_params=pltpu.CompilerParams(vmem_limit_bytes=VMEM_LIMIT),
        name=name,
    )(*args)


def split_cols(t, sizes):
    return jnp.split(t, np.cumsum(sizes)[:-1].tolist(), axis=-1)


def heads(t, n):
    return t.reshape(t.shape[:-1] + (n, t.shape[-1] // n))


def flip_t(t):
    return None if t is None else jnp.flip(t, axis=1)


def l2norm(t):
    return t * lax.rsqrt(jnp.sum(t * t, axis=-1, keepdims=True) + 1e-6)


def layer_norm(t, g, b):
    mu = t.mean(-1, keepdims=True)
    var = jnp.square(t - mu).mean(-1, keepdims=True)
    return ((t - mu) * lax.rsqrt(var + EPS)) * g + b


def rms_norm_heads(o, g):
    y = o * lax.rsqrt(jnp.mean(o * o, axis=-1, keepdims=True) + EPS) * g
    return y.reshape(o.shape[:2] + (-1,))


def group_norm_heads(o, g):
    mu = o.mean(-1, keepdims=True)
    var = jnp.square(o - mu).mean(-1, keepdims=True)
    return ((o - mu) * lax.rsqrt(var + EPS)).reshape(o.shape[:2] + (-1,)) * g


def short_conv(t, w, on_grid):
    ch = t.shape[-1]
    if on_grid:
        b, n = t.shape[:2]
        rows = n // GRID_W
        tg = t.reshape(b, rows, GRID_W, ch)
        y = lax.conv_general_dilated(tg, w[:, :, None, :], (1, 1), 'SAME',
                                     dimension_numbers=('NHWC', 'HWIO', 'NHWC'), feature_group_count=ch)
        return y.reshape(b, n, ch)
    return lax.conv_general_dilated(t, w[CONV_K // 2][:, None, :], (1,), 'SAME',
                                    dimension_numbers=('NWC', 'WIO', 'NWC'), feature_group_count=ch)


def to_chunks(t):
    b, n, h = t.shape[:3]
    t = t.reshape((b, n // CHUNK, CHUNK, h) + t.shape[3:])
    return jnp.moveaxis(jnp.moveaxis(t, 3, 1), 2, 0)


def from_chunks(t):
    t = jnp.moveaxis(jnp.moveaxis(t, 0, 2), 1, 3)
    return t.reshape((t.shape[0], -1) + t.shape[3:])


def linear_scan(q, k, v, g, beta, s0, want_out):
    b, _, h, dk = k.shape
    dv = v.shape[-1]
    kc, vc = to_chunks(k), to_chunks(v)
    gcum = jnp.cumsum(to_chunks(g), axis=-1)
    diff = gcum[..., :, None] - gcum[..., None, :]
    incl = jnp.tril(jnp.ones((CHUNK, CHUNK), bool))
    if beta is None:
        u, w = vc, None
    else:
        bc = to_chunks(beta)[..., None]
        kb = kc * bc
        strict = jnp.tril(jnp.ones((CHUNK, CHUNK), bool), -1)
        a = jnp.where(strict, jnp.einsum('...ik,...jk->...ij', kb, kc) * jnp.exp(jnp.where(strict, diff, 0.0)), 0.0)
        rhs = jnp.concatenate([vc * bc, kb * jnp.exp(gcum)[..., None]], axis=-1)
        sol = lax.linalg.triangular_solve(a + jnp.eye(CHUNK, dtype=a.dtype), rhs, left_side=True,
                                          lower=True, unit_diagonal=True)
        u, w = sol[..., :dv], sol[..., dv:]
    k_end = kc * jnp.exp(gcum[..., -1:] - gcum)[..., None]
    g_end = jnp.exp(gcum[..., -1])[..., None, None]
    if want_out:
        qc = to_chunks(q)
        q_dec = qc * jnp.exp(gcum)[..., None]
        a_qk = jnp.where(incl, jnp.einsum('...ik,...jk->...ij', qc, kc) * jnp.exp(jnp.where(incl, diff, 0.0)), 0.0)
    else:
        q_dec, a_qk = None, None
    if s0 is None:
        s0 = jnp.zeros((b, h, dk, dv), F32)

    def step(s, inp):
        qd, ke, uc, wc, aqk, ge = inp
        vn = uc if wc is None else uc - jnp.einsum('bhck,bhkv->bhcv', wc, s)
        s_new = ge * s + jnp.einsum('bhck,bhcv->bhkv', ke, vn)
        if qd is None:
            return s_new, None
        return s_new, jnp.einsum('bhck,bhkv->bhcv', qd, s) + jnp.einsum('bhij,bhjv->bhiv', aqk, vn)

    s_fin, o = lax.scan(step, s0, (q_dec, k_end, u, w, a_qk, g_end))
    return (from_chunks(o) if want_out else None), s_fin


def mlstm_scan(q, k, v, log_i, log_f, s0, want_out):
    b, _, h, dk = k.shape
    dv = v.shape[-1]
    if s0 is None:
        s0 = (jnp.zeros((b, h, dk, dv), F32), jnp.zeros((b, h, dk), F32), jnp.zeros((b, h), F32))
    incl = jnp.tril(jnp.ones((CHUNK, CHUNK), bool))

    def step(carry, inp):
        c_prev, n_prev, m_prev = carry
        qc, kc, vc, ic, fc = inp
        bcum = jnp.cumsum(fc, axis=-1)
        b_end = bcum[..., -1]
        a = b_end[..., None] - bcum + ic
        m_new = jnp.maximum(b_end + m_prev, a.max(-1))
        w_state = jnp.exp(a - m_new[..., None])
        decay = jnp.exp(b_end + m_prev - m_new)
        c_new = decay[..., None, None] * c_prev + jnp.einsum('bhck,bhcv->bhkv', kc * w_state[..., None], vc)
        n_new = decay[..., None] * n_prev + jnp.einsum('bhck,bhc->bhk', kc, w_state)
        carry_new = (c_new, n_new, m_new)
        if qc is None:
            return carry_new, None
        dlog = jnp.where(incl, bcum[..., :, None] - bcum[..., None, :] + ic[..., None, :], -jnp.inf)
        inter = bcum + m_prev[..., None]
        m_t = jnp.maximum(inter, dlog.max(-1))
        s = jnp.einsum('bhik,bhjk->bhij', qc, kc) * jnp.exp(dlog - m_t[..., None])
        w_inter = jnp.exp(inter - m_t)[..., None]
        num = jnp.einsum('bhij,bhjv->bhiv', s, vc) + w_inter * jnp.einsum('bhik,bhkv->bhiv', qc, c_prev)
        den = s.sum(-1, keepdims=True) + w_inter * jnp.einsum('bhik,bhk->bhi', qc, n_prev)[..., None]
        return carry_new, num / jnp.maximum(jnp.abs(den), jnp.exp(-m_t)[..., None])

    xs = (to_chunks(q) if want_out else None, to_chunks(k), to_chunks(v), to_chunks(log_i), to_chunks(log_f))
    s_fin, hs = lax.scan(step, s0, xs)
    return (from_chunks(hs) if want_out else None), s_fin


def two_pass(scan_fn, ctx_args, lat_args, ctx_out, reverse):
    if reverse:
        ctx_args = [flip_t(t) for t in ctx_args]
        lat_args = [flip_t(t) for t in lat_args]
    o_ctx, s_ctx = scan_fn(*ctx_args, None, ctx_out)
    o_lat, _ = scan_fn(*lat_args, s_ctx, True)
    if reverse:
        o_ctx, o_lat = flip_t(o_ctx), flip_t(o_lat)
    return o_ctx, o_lat


def retention_log_decay(direction):
    expo = 5.0 + 2.0 * jnp.arange(RET_HEADS, dtype=F32) + direction
    return jnp.log1p(-jnp.exp2(-expo))


GATE_COLS = 4 * N_HEADS
Q_SCALE = HEAD_DIM ** -0.5


def _gate_row(*vals):
    v = jnp.concatenate([jnp.ravel(x) for x in vals])
    return jnp.pad(v, (0, LANE - v.shape[0]))


def _pad_gate_cols(w):
    return jnp.pad(w, ((0, 0), (0, LANE - w.shape[1])))


def gdn_retention_mixer(hs_, mods, w_in, w_out, conv_w, a_log, dt_bias, gdn_gain, ret_gain, ln_g, ln_b, ctx_out):
    w = jnp.concatenate([w_in[:, :4 * GDN_W], w_in[:, 4 * GDN_W + GATE_COLS:],
                         _pad_gate_cols(w_in[:, 4 * GDN_W:4 * GDN_W + GATE_COLS])], axis=1).astype(BF16)
    gate_blk = (4 * GDN_W + 4 * RET_W) // LANE
    gp_gdn = jnp.stack([_gate_row(-jnp.exp(a_log)), _gate_row(dt_bias)])
    gp_ret = jnp.stack([_gate_row(retention_log_decay(0), retention_log_decay(1)), jnp.zeros((LANE,), F32)])
    wo = w_out.astype(BF16)
    projs, convs = [], []
    for (h, (sc, sh, _)), grid_w in zip(zip(hs_, mods), (hs_[0].shape[1], GRID_W)):
        p = in_projection(h, sc, sh, w)
        projs.append(p)
        convs.append(conv_prep(p, conv_w, 3 * GDN_W // LANE, grid_w, 2 * N_HEADS, Q_SCALE, 1.0))
    gdn_args = [((cv, 0), (cv, 1), (cv, 2), (p, gate_blk)) for p, cv in zip(projs, convs)]
    ret_args = [((p, 4), (p, 5), (p, 6), None) for p in projs]
    og = linear_scan_two_pass(gdn_args[0], gdn_args[1], gp_gdn, True)
    orr = linear_scan_two_pass(ret_args[0], ret_args[1], gp_ret, False, Q_SCALE)
    outs = []
    for i in range(2):
        if i == 0 and not ctx_out:
            outs.append(None)
            continue
        outs.append(_merge_call(_merge_even_kernel, "merge_even", [og[i], orr[i]], [(projs[i], 3), (projs[i], 7)],
                                [jnp.tile(gdn_gain, N_HEADS), ret_gain], [wo], hs_[i], mods[i][2], ln_g, ln_b))
    return outs


def mlstm_s5_mixer(hs_, mods, w_in, w_out, conv_w, gate_bias, mlstm_gain, lam_re, lam_im, log_dt,
                   b_re, b_im, c_re, c_im, d_skip, w_glu, b_glu, ln_g, ln_b, ctx_out):
    w = jnp.concatenate([w_in[:, :4 * MLSTM_W], _pad_gate_cols(w_in[:, 4 * MLSTM_W:4 * MLSTM_W + GATE_COLS])],
                        axis=1).astype(BF16)
    w_u = w_in[:, 4 * MLSTM_W + GATE_COLS:].astype(BF16)
    gate_blk = 4 * MLSTM_W // LANE
    gp = jnp.stack([_gate_row(gate_bias[0, 0], gate_bias[1, 0], gate_bias[0, 1], gate_bias[1, 1]),
                    jnp.zeros((LANE,), F32)])
    wo, wglu = w_out.astype(BF16), w_glu.astype(BF16)
    projs, us, args = [], [], []
    for (h, (sc, sh, _)), grid_w in zip(zip(hs_, mods), (hs_[0].shape[1], GRID_W)):
        p = in_projection(h, sc, sh, w)
        cv = conv_prep(p, conv_w, 2 * MLSTM_W // LANE, grid_w, 0, 1.0, Q_SCALE)
        projs.append(p)
        us.append(in_projection(h, sc, sh, w_u))
        args.append(((cv, 0), (cv, 1), (p, 2), (p, gate_blk)))
    om = mlstm_two_pass(args[0], args[1], gp)
    ys = s5_bidirectional(us[0], us[1], _s5_weights(lam_re, lam_im, log_dt, b_re, b_im, c_re, c_im))
    outs = []
    for i in range(2):
        if i == 0 and not ctx_out:
            outs.append(None)
            continue
        outs.append(_merge_call(_merge_odd_kernel, "merge_odd", [om[i]], [(ys[i], 0), (projs[i], 3), (us[i], 0)],
                                [mlstm_gain, d_skip, b_glu], [wglu, wo], hs_[i], mods[i][2], ln_g, ln_b))
    return outs


def kernel(x, c, ctx, c_ctx, w_mod, b_mod, ln1_g, ln1_b, ln2_g, ln2_b, w_router, w_gate, w_up, w_down,
           ev_w_in, ev_w_out, ev_conv, ev_a_log, ev_dt_bias, ev_gdn_norm, ev_ret_norm,
           od_w_in, od_w_out, od_conv, od_gate_bias, od_mlstm_norm, od_lam_re, od_lam_im, od_log_dt,
           od_b_re, od_b_im, od_c_re, od_c_im, od_d_skip, od_w_glu, od_b_glu):
    h_lat, h_ctx = x, ctx
    s_lat = jax.nn.silu(c)
    s_ctx = jax.nn.silu(c_ctx)
    experts = (w_gate.astype(BF16), w_up.astype(BF16), w_down.astype(BF16))
    for l in range(DEPTH):
        last = l == DEPTH - 1
        sh1, sc1, g1, sh2, sc2, g2 = jnp.split((s_lat @ w_mod[l] + b_mod[l])[:, None, :], 6, axis=-1)
        bc = lambda v: jnp.broadcast_to(v, (BATCH, 1, D_MODEL))
        csh1, csc1, cg1, csh2, csc2, cg2 = [bc(v) for v in jnp.split(s_ctx @ w_mod[l] + b_mod[l], 6, axis=-1)]
        streams = (h_ctx, h_lat)
        mods = ((csc1, csh1, cg1), (sc1, sh1, g1))
        if l % 2 == 0:
            e = l // 2
            h_ctx, h_lat = gdn_retention_mixer(streams, mods, ev_w_in[e], ev_w_out[e], ev_conv[e], ev_a_log[e],
                                               ev_dt_bias[e], ev_gdn_norm[e], ev_ret_norm[e], ln1_g[l], ln1_b[l],
                                               not last)
        else:
            o = l // 2
            h_ctx, h_lat = mlstm_s5_mixer(streams, mods, od_w_in[o], od_w_out[o], od_conv[o], od_gate_bias[o],
                                          od_mlstm_norm[o], od_lam_re[o], od_lam_im[o], od_log_dt[o],
                                          od_b_re[o], od_b_im[o], od_c_re[o], od_c_im[o], od_d_skip[o],
                                          od_w_glu[o], od_b_glu[o], ln1_g[l], ln1_b[l], not last)
        h_lat = moe_block(h_lat, sc2, sh2, g2, ln2_g[l], ln2_b[l], w_router[l], *experts, l)
        if not last:
            h_ctx = moe_block(h_ctx, csc2, csh2, cg2, ln2_g[l], ln2_b[l], w_router[l], *experts, l)
    return h_lat
```

```python
import functools
import math

import jax
import jax.numpy as jnp
import numpy as np
from jax import lax
from jax.experimental import pallas as pl
from jax.experimental.pallas import tpu as pltpu

D_MODEL = 1024
BATCH = 4
SEQ = 4096
DEPTH = 4
GRID_W = 64
CTX_LEN = 256
CHUNK = 64
CONV_K = 3
HEAD_DIM = D_MODEL // 8
GDN_HEADS = 4
RET_HEADS = 4
MLSTM_HEADS = 4
GDN_W = GDN_HEADS * HEAD_DIM
RET_W = RET_HEADS * HEAD_DIM
MLSTM_W = MLSTM_HEADS * HEAD_DIM
S5_CH = D_MODEL // 2
S5_GROUP = 16
S5_GROUPS = S5_CH // S5_GROUP
S5_STATE = 64
N_EXPERTS = 16
EXPERT_FF = 2 * D_MODEL
CAPACITY_FACTOR = 2
ALPHA = (2 * DEPTH) ** 0.25
EPS = 1e-5
EVEN_COLS = (GDN_W, GDN_W, GDN_W, GDN_W, GDN_HEADS, GDN_HEADS, GDN_HEADS, GDN_HEADS,
             RET_W, RET_W, RET_W, RET_W)
ODD_COLS = (MLSTM_W, MLSTM_W, MLSTM_W, MLSTM_W, MLSTM_HEADS, MLSTM_HEADS, MLSTM_HEADS, MLSTM_HEADS, S5_CH)
F32 = jnp.float32
BF16 = jnp.bfloat16

LANE = 128


def _mm_kernel(a_ref, b_ref, o_ref):
    o_ref[...] = jnp.dot(a_ref[...].astype(BF16), b_ref[...].astype(BF16), preferred_element_type=F32)


def pmatmul(a, b, tm=512, tn=512):
    m, k = a.shape
    n = b.shape[1]
    n_pad = -n % LANE
    if n_pad:
        b = jnp.pad(b, ((0, 0), (0, n_pad)))
    np_ = n + n_pad
    tn = min(tn, np_)
    while np_ % tn:
        tn -= LANE
    tm = min(tm, m)
    assert m % tm == 0
    out = pl.pallas_call(
        _mm_kernel,
        grid=(m // tm, np_ // tn),
        in_specs=[pl.BlockSpec((tm, k), lambda i, j: (i, 0)),
                  pl.BlockSpec((k, tn), lambda i, j: (0, j))],
        out_specs=pl.BlockSpec((tm, tn), lambda i, j: (i, j)),
        out_shape=jax.ShapeDtypeStruct((m, np_), F32),
        name="pmatmul",
    )(a, b)
    return out[:, :n] if n_pad else out


def mm(x, w):
    lead = x.shape[:-1]
    return pmatmul(x.reshape(-1, x.shape[-1]), w).reshape(lead + (w.shape[-1],))


S5_L = 16
S5_NB = S5_CH // LANE
S5_GPB = LANE // S5_GROUP
S5_SW = S5_GPB * S5_STATE
VMEM_LIMIT = 56 * 1024 * 1024


def _s5_expand_kernel(c_ref, o_ref, *, row_item_log2, col_item_log2):
    c = c_ref[...]
    nk, nc = c.shape[1], o_ref.shape[1]
    gmask = S5_GPB - 1
    k = lax.broadcasted_iota(jnp.int32, (nk, nc), 0)
    col = lax.broadcasted_iota(jnp.int32, (nk, nc), 1)
    src = ((col >> (col_item_log2 + S5_GPB.bit_length() - 1)) << col_item_log2) | (col & ((1 << col_item_log2) - 1))
    rep = jnp.where(k == src, 1.0, 0.0).astype(BF16)
    wide = jnp.dot(c, rep, preferred_element_type=F32)
    r = lax.broadcasted_iota(jnp.int32, wide.shape, 0)
    cc = lax.broadcasted_iota(jnp.int32, wide.shape, 1)
    same = ((r >> row_item_log2) & gmask) == ((cc >> col_item_log2) & gmask)
    o_ref[...] = jnp.where(same, wide, 0.0).astype(BF16)


def _s5_expand(c, row_item_log2, col_item_log2):
    nbk, rows, nk = c.shape
    return pl.pallas_call(
        functools.partial(_s5_expand_kernel, row_item_log2=row_item_log2, col_item_log2=col_item_log2),
        grid=(nbk,),
        in_specs=[pl.BlockSpec((None, rows, nk), lambda j: (j, 0, 0))],
        out_specs=pl.BlockSpec((None, rows, nk * S5_GPB), lambda j: (j, 0, 0)),
        out_shape=jax.ShapeDtypeStruct((nbk, rows, nk * S5_GPB), BF16),
        compiler_params=pltpu.CompilerParams(vmem_limit_bytes=VMEM_LIMIT),
        name="s5_expand",
    )(c)


def _s5_weights(lam_re, lam_im, log_dt, b_re, b_im, c_re, c_im):
    L, G = S5_L, S5_GROUPS
    hp = lax.Precision.HIGHEST
    taus = jnp.arange(L + 1, dtype=F32)[:, None, None]
    ks, ws, cas, ds = [], [], [], []
    for d in range(2):
        lr = jnp.minimum(lam_re[d], -1e-4)
        li = lam_im[d]
        dt = jnp.exp(log_dt[d])[:, None]
        mag = jnp.exp(lr * dt)
        ab_re, ab_im = mag * jnp.cos(li * dt), mag * jnp.sin(li * dt)
        xr, xi, den = ab_re - 1.0, ab_im, lr * lr + li * li
        f_re = (xr * lr + xi * li) / den
        f_im = (xi * lr - xr * li) / den
        bb_re = f_re[..., None] * b_re - f_im[..., None] * b_im
        bb_im = f_re[..., None] * b_im + f_im[..., None] * b_re
        pmag = jnp.exp(taus * (lr * dt))
        ar, ai = pmag * jnp.cos(taus * (li * dt)), pmag * jnp.sin(taus * (li * dt))
        wr = ar[..., None] * bb_re - ai[..., None] * bb_im
        wi = ar[..., None] * bb_im + ai[..., None] * bb_re
        k = (jnp.einsum('gop,tgpi->tgio', c_re, wr, precision=hp)
             - jnp.einsum('gop,tgpi->tgio', c_im, wi, precision=hp))
        car = c_re[None] * ar[:, :, None, :] - c_im[None] * ai[:, :, None, :]
        cai = c_re[None] * ai[:, :, None, :] + c_im[None] * ar[:, :, None, :]
        ks.append(k)
        ws.append((wr, wi))
        cas.append((car, cai))
        ds.append((ar[L], ai[L]))

    split_g = lambda t, g_axis: t.reshape(t.shape[:g_axis] + (S5_NB, S5_GPB) + t.shape[g_axis + 1:]).astype(BF16)
    compact = lambda t: t.reshape(S5_NB, L * LANE, -1)
    h_log2, p_log2 = S5_GROUP.bit_length() - 1, S5_STATE.bit_length() - 1

    kf, kb = ks
    kc = jnp.concatenate([kb[1:L][::-1], (kf[0] + kb[0])[None], kf[1:L]], axis=0)
    idx = (jnp.arange(L)[None, :] - jnp.arange(L)[:, None]) + (L - 1)
    tz = _s5_expand(compact(split_g(kc[idx], 2).transpose(2, 0, 3, 4, 1, 5)), h_log2, h_log2)

    pb_c, ca_c = [], []
    for d in range(2):
        wr, wi = ws[d]
        order = jnp.arange(L - 1, -1, -1) if d == 0 else jnp.arange(L)
        pb_c.append(jnp.stack([jnp.swapaxes(w[order], -1, -2) for w in (wr, wi)]))
        car, cai = cas[d]
        order = jnp.arange(1, L + 1) if d == 0 else jnp.arange(L, 0, -1)
        ca_c.append(jnp.stack([jnp.swapaxes(m, -1, -2) for m in (car[order], -cai[order])]))
    pb = _s5_expand(compact(split_g(jnp.stack(pb_c), 3).transpose(3, 2, 4, 5, 0, 1, 6)), h_log2, p_log2)
    ca = _s5_expand(compact(split_g(jnp.stack(ca_c), 3).transpose(3, 0, 1, 4, 5, 2, 6)), p_log2, h_log2)
    dr = jnp.stack([ds[0][0], ds[1][0]], 0).reshape(2, S5_NB, 1, S5_SW).transpose(1, 0, 2, 3).reshape(2 * S5_NB, 1, S5_SW)
    di = jnp.stack([ds[0][1], ds[1][1]], 0).reshape(2, S5_NB, 1, S5_SW).transpose(1, 0, 2, 3).reshape(2 * S5_NB, 1, S5_SW)
    return tz, pb, ca, dr, di


def _s5_p_kernel(u_ref, pb_ref, p_ref, *, nb, nc):
    res = jnp.dot(u_ref[...], pb_ref[...], preferred_element_type=F32)
    for b in range(nb):
        p_ref[:, b * 2 * S5_SW:(b + 1) * 2 * S5_SW] = res[b * nc:(b + 1) * nc]


def _s5_scan_kernel(p_ref, dr_ref, di_ref, s_ref, *, n_ctx, n_lat):
    rev = pl.program_id(0) % 2
    dr = dr_ref[...]
    di = di_ref[...]
    nbatch = p_ref.shape[1]

    def phase(base, n, carry):
        def body(step, carry):
            sr, si = carry
            row = base + jnp.where(rev == 0, step, n - 1 - step)
            s_ref[row, :, :S5_SW] = sr
            s_ref[row, :, S5_SW:] = si
            p = p_ref[row]
            nr = dr * sr - di * si + p[:, :S5_SW]
            ni = dr * si + di * sr + p[:, S5_SW:]
            return nr, ni
        return lax.fori_loop(0, n, body, carry)

    zero = jnp.zeros((nbatch, S5_SW), F32)
    carry = phase(0, n_ctx, (zero, zero))
    phase(n_ctx, n_lat, carry)


def _s5_y_kernel(u_ref, tz_ref, sf_ref, sb_ref, ca_ref, y_ref):
    y = jnp.dot(u_ref[...], tz_ref[...], preferred_element_type=F32)
    y += jnp.dot(sf_ref[...].astype(BF16), ca_ref[:2 * S5_SW, :], preferred_element_type=F32)
    y += jnp.dot(sb_ref[...].astype(BF16), ca_ref[2 * S5_SW:, :], preferred_element_type=F32)
    y_ref[...] = y


def s5_bidirectional(u_ctx, u_lat, weights):
    tz, pb, ca, dr, di = weights
    L = S5_L
    nb, t_ctx, _ = u_ctx.shape
    t_lat = u_lat.shape[1]
    assert t_ctx % L == 0 and t_lat % L == 0
    n_ctx, n_lat = t_ctx // L, t_lat // L
    nc = n_ctx + n_lat
    kw = L * LANE
    sw2 = 2 * S5_SW
    u = jnp.concatenate([u_ctx, u_lat], axis=1)
    ub = u.reshape(nb * nc, L, S5_NB, LANE).transpose(2, 0, 1, 3).reshape(S5_NB, nb * nc, kw).astype(BF16)

    p = pl.pallas_call(
        functools.partial(_s5_p_kernel, nb=nb, nc=nc),
        grid=(S5_NB, 2),
        in_specs=[pl.BlockSpec((None, nb * nc, kw), lambda j, d: (j, 0, 0)),
                  pl.BlockSpec((None, kw, sw2), lambda j, d: (j, 0, d))],
        out_specs=pl.BlockSpec((nc, nb * sw2), lambda j, d: (0, j * 2 + d)),
        out_shape=jax.ShapeDtypeStruct((nc, S5_NB * 2 * nb * sw2), F32),
        compiler_params=pltpu.CompilerParams(vmem_limit_bytes=VMEM_LIMIT),
        name="s5_chunk_inputs",
    )(ub, pb)

    p4 = p.reshape(nc, S5_NB * 2, nb, sw2)
    s4 = pl.pallas_call(
        functools.partial(_s5_scan_kernel, n_ctx=n_ctx, n_lat=n_lat),
        grid=(S5_NB * 2,),
        in_specs=[pl.BlockSpec((nc, None, nb, sw2), lambda g: (0, g, 0, 0)),
                  pl.BlockSpec((None, 1, S5_SW), lambda g: (g, 0, 0)),
                  pl.BlockSpec((None, 1, S5_SW), lambda g: (g, 0, 0))],
        out_specs=pl.BlockSpec((nc, None, nb, sw2), lambda g: (0, g, 0, 0)),
        out_shape=jax.ShapeDtypeStruct(p4.shape, F32),
        compiler_params=pltpu.CompilerParams(vmem_limit_bytes=VMEM_LIMIT),
        name="s5_state_scan",
    )(p4, dr, di)

    s2 = s4.reshape(nc, S5_NB * 2 * nb * sw2)
    yb = pl.pallas_call(
        _s5_y_kernel,
        grid=(S5_NB, nb),
        in_specs=[pl.BlockSpec((None, nc, kw), lambda j, b: (j, b, 0)),
                  pl.BlockSpec((None, kw, kw), lambda j, b: (j, 0, 0)),
                  pl.BlockSpec((nc, sw2), lambda j, b: (0, (j * 2) * nb + b)),
                  pl.BlockSpec((nc, sw2), lambda j, b: (0, (j * 2 + 1) * nb + b)),
                  pl.BlockSpec((None, 2 * sw2, kw), lambda j, b: (j, 0, 0))],
        out_specs=pl.BlockSpec((None, nc, kw), lambda j, b: (j, b, 0)),
        out_shape=jax.ShapeDtypeStruct((S5_NB, nb * nc, kw), F32),
        compiler_params=pltpu.CompilerParams(vmem_limit_bytes=VMEM_LIMIT),
        name="s5_output",
    )(ub, tz, s2, s2, ca)
    y = yb.reshape(S5_NB, nb, nc, L, LANE).transpose(1, 2, 3, 0, 4).reshape(nb, nc * L, S5_CH)
    return y[:, :t_ctx], y[:, t_ctx:]


ROUTE_TN = 512
MOE_TN = 1024
SEL_ROWS = 128
FF_SPLIT = 4
FFN_ROWS = 1024
AFF_PARTS = 3


def _route_kernel(h_ref, sc_ref, sh_ref, wr_ref, hm_ref, afft_ref, asp_ref):
    hm = h_ref[...] * (1.0 + sc_ref[...]) + sh_ref[...]
    hm_ref[...] = hm.astype(BF16)
    logits = jnp.dot(hm, wr_ref[...], precision=lax.Precision.HIGHEST, preferred_element_type=F32)
    lane = lax.broadcasted_iota(jnp.int32, logits.shape, 1)
    logits = jnp.where(lane < N_EXPERTS, logits, -jnp.inf)
    ex = jnp.exp(logits - jnp.max(logits, axis=1, keepdims=True))
    aff = ex / jnp.sum(ex, axis=1, keepdims=True)
    afft_ref[...] = aff.T[:N_EXPERTS, :]
    hi = aff.astype(BF16).astype(F32)
    mid = (aff - hi).astype(BF16).astype(F32)
    lo = (aff - hi - mid).astype(BF16).astype(F32)
    asp = hi + pltpu.roll(mid, N_EXPERTS, 1) + pltpu.roll(lo, 2 * N_EXPERTS, 1)
    asp_ref[...] = asp.astype(BF16)


def _select_kernel(aff_ref, pos_ref, *, nblk_log2, cap):
    a = aff_ref[...]
    r = a.shape[0]
    bits = pltpu.bitcast(a, jnp.int32)
    ri = lax.broadcasted_iota(jnp.int32, (r, r), 0)
    rj = lax.broadcasted_iota(jnp.int32, (r, r), 1)
    same = (ri >> nblk_log2) == (rj >> nblk_log2)
    gm = jnp.where(same, 1.0, 0.0).astype(BF16)
    lm = jnp.where(same & (rj < ri), 1.0, 0.0).astype(BF16)
    li = lax.broadcasted_iota(jnp.int32, (LANE, LANE), 0)
    lj = lax.broadcasted_iota(jnp.int32, (LANE, LANE), 1)
    um = jnp.where(li <= lj, 1.0, 0.0).astype(BF16)

    def group_count(mask):
        rc = jnp.sum(jnp.where(mask, 1.0, 0.0), axis=1, keepdims=True)
        gc = jnp.dot(gm, jnp.broadcast_to(rc, (r, LANE)).astype(BF16), preferred_element_type=F32)
        return gc[:, :1]

    def bisect(i, thr):
        cand = thr | jnp.left_shift(1, 29 - i)
        return jnp.where(group_count(bits >= cand) >= cap, cand, thr)

    thr = lax.fori_loop(0, 30, bisect, jnp.zeros((r, 1), jnp.int32))

    def prefix(mask):
        x = jnp.where(mask, 1.0, 0.0)
        inc = jnp.dot(x.astype(BF16), um, preferred_element_type=F32)
        tot = jnp.broadcast_to(inc[:, LANE - 1:LANE], (r, LANE)).astype(BF16)
        return inc - x + jnp.dot(lm, tot, preferred_element_type=F32)

    gt = bits > thr
    eq = bits == thr
    need = cap - group_count(gt)
    sel = gt | (eq & (prefix(eq) < need))
    pos_ref[...] = jnp.where(sel, prefix(sel), -1.0).astype(jnp.int32)


def _onehot(pos_ref, cap):
    rows = lax.broadcasted_iota(jnp.int32, (cap, LANE), 0)
    blocks = [jnp.where(pos_ref[k:k + 1, :] == rows, 1.0, 0.0).astype(BF16) for k in range(pos_ref.shape[0])]
    return jnp.concatenate(blocks, axis=1)


def _ffn_kernel(pos_ref, hb_ref, asp_ref, wg_ref, wu_ref, wd_ref, ys_ref, xs_acc, g_acc, *, cap, nkt):
    e = pl.program_id(0)
    s = pl.program_id(2)

    @pl.when(s == 0)
    def _():
        xs_acc[...] = jnp.zeros_like(xs_acc)
        g_acc[...] = jnp.zeros_like(g_acc)

    rows = pl.ds(pl.multiple_of((s // nkt) * cap, cap), cap)
    oh = _onehot(pos_ref, cap)
    xs_acc[rows, :] += jnp.dot(oh, hb_ref[...], preferred_element_type=F32)
    g_acc[rows, :] += jnp.dot(oh, asp_ref[...], preferred_element_type=F32)

    @pl.when(s == pl.num_programs(2) - 1)
    def _():
        xs = xs_acc[...].astype(BF16)
        g = g_acc[...]
        lane = lax.broadcasted_iota(jnp.int32, g.shape, 1)
        gate = jnp.sum(jnp.where((lane & (N_EXPERTS - 1)) == e, g, 0.0), axis=1, keepdims=True)
        fw = EXPERT_FF // FF_SPLIT
        y = jnp.zeros(xs.shape, F32)
        for f in range(FF_SPLIT):
            hg = jnp.dot(xs, wg_ref[:, f * fw:(f + 1) * fw], preferred_element_type=F32)
            hu = jnp.dot(xs, wu_ref[:, f * fw:(f + 1) * fw], preferred_element_type=F32)
            hid = (hg * jax.nn.sigmoid(hg)) * hu
            y += jnp.dot(hid.astype(BF16), wd_ref[f * fw:(f + 1) * fw, :], preferred_element_type=F32)
        y = (y * gate).astype(BF16)
        for i in range(ys_ref.shape[0]):
            ys_ref[i] = y[i * cap:(i + 1) * cap]


def _ln_rows(z, g, b):
    mu = jnp.mean(z, axis=-1, keepdims=True)
    zc = z - mu
    var = jnp.mean(zc * zc, axis=-1, keepdims=True)
    return zc * lax.rsqrt(var + EPS) * g + b


def _combine_kernel(pos_ref, ys_ref, h_ref, g2_ref, lng_ref, lnb_ref, o_ref, acc, *, cap):
    e = pl.program_id(2)

    @pl.when(e == 0)
    def _():
        acc[...] = jnp.zeros_like(acc)

    oh = _onehot(pos_ref, cap)
    acc[...] += lax.dot_general(oh, ys_ref[...], (((0,), (0,)), ((), ())), preferred_element_type=F32)

    @pl.when(e == pl.num_programs(2) - 1)
    def _():
        z = ALPHA * h_ref[...] + g2_ref[...] * acc[...]
        o_ref[...] = _ln_rows(z, lng_ref[...], lnb_ref[...])


def moe_block(h, sc, sh, g2, ln_g, ln_b, w_router, wg, wu, wd, layer):
    nb, n, dm = h.shape
    cap = CAPACITY_FACTOR * n // N_EXPERTS
    nblk = n // LANE
    assert n % LANE == 0 and nblk & (nblk - 1) == 0
    tn_r = min(ROUTE_TN, n)
    tn = min(MOE_TN, n)
    wr = jnp.pad(w_router, ((0, 0), (0, LANE - N_EXPERTS)))
    row = lambda v: v.reshape(1, dm)

    hm, afft, asp = pl.pallas_call(
        _route_kernel,
        grid=(nb, n // tn_r),
        in_specs=[pl.BlockSpec((None, tn_r, dm), lambda b, t: (b, t, 0)),
                  pl.BlockSpec((None, 1, dm), lambda b, t: (b, 0, 0)),
                  pl.BlockSpec((None, 1, dm), lambda b, t: (b, 0, 0)),
                  pl.BlockSpec((dm, LANE), lambda b, t: (0, 0))],
        out_specs=[pl.BlockSpec((None, tn_r, dm), lambda b, t: (b, t, 0)),
                   pl.BlockSpec((None, N_EXPERTS, tn_r), lambda b, t: (b, 0, t)),
                   pl.BlockSpec((None, tn_r, LANE), lambda b, t: (b, t, 0))],
        out_shape=[jax.ShapeDtypeStruct((nb, n, dm), BF16),
                   jax.ShapeDtypeStruct((nb, N_EXPERTS, n), F32),
                   jax.ShapeDtypeStruct((nb, n, LANE), BF16)],
        name="moe_route",
    )(h, sc, sh, wr)

    rows_total = nb * N_EXPERTS * nblk
    rb = max(SEL_ROWS, N_EXPERTS * nblk)
    assert rows_total % rb == 0
    pos = pl.pallas_call(
        functools.partial(_select_kernel, nblk_log2=nblk.bit_length() - 1, cap=cap),
        grid=(rows_total // rb,),
        in_specs=[pl.BlockSpec((rb, LANE), lambda i: (i, 0))],
        out_specs=pl.BlockSpec((rb, LANE), lambda i: (i, 0)),
        out_shape=jax.ShapeDtypeStruct((rows_total, LANE), jnp.int32),
        name="moe_select",
    )(afft.reshape(rows_total, LANE))
    pos = pos.reshape(nb, N_EXPERTS, nblk, LANE)

    tb = tn // LANE
    nkt = n // tn
    group = min(nb, max(1, FFN_ROWS // cap))
    assert nb % group == 0
    sample = lambda bg, s: bg * group + s // nkt
    ys = pl.pallas_call(
        functools.partial(_ffn_kernel, cap=cap, nkt=nkt),
        grid=(N_EXPERTS, nb // group, group * nkt),
        in_specs=[pl.BlockSpec((None, None, tb, LANE), lambda e, bg, s: (sample(bg, s), e, s % nkt, 0)),
                  pl.BlockSpec((None, tn, dm), lambda e, bg, s: (sample(bg, s), s % nkt, 0)),
                  pl.BlockSpec((None, tn, LANE), lambda e, bg, s: (sample(bg, s), s % nkt, 0)),
                  pl.BlockSpec((None, None, dm, EXPERT_FF), lambda e, bg, s: (layer, e, 0, 0)),
                  pl.BlockSpec((None, None, dm, EXPERT_FF), lambda e, bg, s: (layer, e, 0, 0)),
                  pl.BlockSpec((None, None, EXPERT_FF, dm), lambda e, bg, s: (layer, e, 0, 0))],
        out_specs=pl.BlockSpec((group, None, cap, dm), lambda e, bg, s: (bg, e, 0, 0)),
        out_shape=jax.ShapeDtypeStruct((nb, N_EXPERTS, cap, dm), BF16),
        scratch_shapes=[pltpu.VMEM((group * cap, dm), F32), pltpu.VMEM((group * cap, LANE), F32)],
        compiler_params=pltpu.CompilerParams(vmem_limit_bytes=VMEM_LIMIT),
        name="moe_ffn",
    )(pos, hm, asp, wg, wu, wd)

    return pl.pallas_call(
        functools.partial(_combine_kernel, cap=cap),
        grid=(nb, n // tn, N_EXPERTS),
        in_specs=[pl.BlockSpec((None, None, tb, LANE), lambda b, t, e: (b, e, t, 0)),
                  pl.BlockSpec((None, None, cap, dm), lambda b, t, e: (b, e, 0, 0)),
                  pl.BlockSpec((None, tn, dm), lambda b, t, e: (b, t, 0)),
                  pl.BlockSpec((None, 1, dm), lambda b, t, e: (b, 0, 0)),
                  pl.BlockSpec((1, dm), lambda b, t, e: (0, 0)),
                  pl.BlockSpec((1, dm), lambda b, t, e: (0, 0))],
        out_specs=pl.BlockSpec((None, tn, dm), lambda b, t, e: (b, t, 0)),
        out_shape=jax.ShapeDtypeStruct((nb, n, dm), F32),
        scratch_shapes=[pltpu.VMEM((tn, dm), F32)],
        compiler_params=pltpu.CompilerParams(vmem_limit_bytes=VMEM_LIMIT),
        name="moe_combine",
    )(pos, ys, h, g2, row(ln_g), row(ln_b))


N_HEADS = 4
STACK = N_HEADS * CHUNK
CHUNK_LOG2 = CHUNK.bit_length() - 1
SOLVE_LEVELS = CHUNK_LOG2


def _stack_heads(x):
    return jnp.concatenate([x[:, h * HEAD_DIM:(h + 1) * HEAD_DIM] for h in range(N_HEADS)], axis=0)


def _stack_cols(cols, width):
    return jnp.concatenate([jnp.broadcast_to(c, (CHUNK, width)) for c in cols], axis=0)


def _mxu(a, b, dims, hp):
    dg = lambda x, y: lax.dot_general(x, y, (dims, ((), ())), preferred_element_type=F32)
    a_hi, b_hi = a.astype(BF16), b.astype(BF16)
    if not hp:
        return dg(a_hi, b_hi)
    a_lo = (a - a_hi.astype(F32)).astype(BF16)
    b_lo = (b - b_hi.astype(F32)).astype(BF16)
    return dg(a_hi, b_hi) + (dg(a_lo, b_hi) + dg(a_hi, b_lo))


def _dot_nt(a, b, hp=False):
    return _mxu(a, b, ((1,), (1,)), hp)


def _dot_tn(a, b, hp=False):
    return _mxu(a, b, ((0,), (0,)), hp)


def _dot(a, b, hp=False):
    return _mxu(a, b, ((1,), (0,)), hp)


def _chunk_masks(rev):
    ri = lax.broadcasted_iota(jnp.int32, (STACK, STACK), 0)
    ci = lax.broadcasted_iota(jnp.int32, (STACK, STACK), 1)
    same = (ri >> CHUNK_LOG2) == (ci >> CHUNK_LOG2)
    ahead = jnp.where(rev, ci - ri, ri - ci)
    return same & (ahead >= 0), same & (ahead > 0)


def _scan_cumsum(gt, rev):
    ii = lax.broadcasted_iota(jnp.int32, (CHUNK, CHUNK), 0)
    jj = lax.broadcasted_iota(jnp.int32, (CHUNK, CHUNK), 1)
    tri = jnp.where(jnp.where(rev, jj - ii, ii - jj) >= 0, 1.0, 0.0)
    return jnp.dot(tri, gt, precision=lax.Precision.HIGHEST, preferred_element_type=F32)


def _unit_triangular_inverse(a):
    ri = lax.broadcasted_iota(jnp.int32, a.shape, 0)
    ci = lax.broadcasted_iota(jnp.int32, a.shape, 1)
    joins = lambda lvl: ((ri >> (lvl + 1)) == (ci >> (lvl + 1))) & ((ri >> lvl) != (ci >> lvl))
    t = jnp.where(ri == ci, 1.0, 0.0) - jnp.where(joins(0), a, 0.0)
    for lvl in range(1, CHUNK_LOG2):
        m = _dot(jnp.where(joins(lvl), a, 0.0), t)
        yield
        t = t - _dot(t, m)
        yield
    return t


def _interleave(chains):
    for _ in zip(*chains):
        pass


def _linear_scan_kernel(*refs, has_beta, k_scale):
    if has_beta:
        q_ref, k_ref, v_ref, gt_ref, gp_ref, s0_ref, o_ref, sfin_ref, s_scr = refs
    else:
        q_ref, k_ref, v_ref, gp_ref, s0_ref, o_ref, sfin_ref, s_scr = refs
    rev = pl.program_id(0) == 1
    n = pl.program_id(2)

    @pl.when(n == 0)
    def _():
        s_scr[...] = s0_ref[...]

    chains = []
    for i in range(q_ref.shape[0]):
        if has_beta:
            gt = _gdn_gates(gt_ref[i], gp_ref[...])
        else:
            gt = jnp.broadcast_to(gp_ref[0:1, :], (CHUNK, LANE))
        chains.append(_linear_chunk(q_ref.at[i], k_ref.at[i], v_ref.at[i], gt, o_ref.at[i], s_scr.at[i], rev,
                                    has_beta, k_scale))
    _interleave(chains)

    @pl.when(n == pl.num_programs(2) - 1)
    def _():
        sfin_ref[...] = s_scr[...]


def _softplus(x):
    return jnp.maximum(x, 0.0) + jnp.log1p(jnp.exp(-jnp.abs(x)))


def _gdn_gates(raw, gp):
    lane = lax.broadcasted_iota(jnp.int32, raw.shape, 1)
    return jnp.where(lane < 2 * N_HEADS, gp[0:1, :] * _softplus(raw + gp[1:2, :]), jax.nn.sigmoid(raw))


def _mlstm_gates(raw, gp):
    lane = lax.broadcasted_iota(jnp.int32, raw.shape, 1)
    x = raw + gp[0:1, :]
    return jnp.where(lane < 2 * N_HEADS, x, -_softplus(-x))


def _linear_chunk(q_ref, k_ref, v_ref, gt, o_ref, s_scr, rev, has_beta, k_scale):
    cum = _scan_cumsum(gt, rev)
    gcols, bcols, gtots = [], [], []
    for h in range(N_HEADS):
        gc = jnp.where(rev, cum[:, N_HEADS + h:N_HEADS + h + 1], cum[:, h:h + 1])
        gcols.append(gc)
        gtots.append(jnp.where(rev, gc[0:1], gc[CHUNK - 1:CHUNK]))
        bcols.append(jnp.where(rev, gt[:, 3 * N_HEADS + h:3 * N_HEADS + h + 1], gt[:, 2 * N_HEADS + h:2 * N_HEADS + h + 1]))
    cb = _stack_cols(gcols, STACK)
    diff = cb - cb.T
    incl, strict = _chunk_masks(rev)
    dec = jnp.exp(jnp.where(incl, diff, 0.0))
    gcb = cb[:, :HEAD_DIM]
    q_st, k_st, v_st = _stack_heads(q_ref[...]), _stack_heads(k_ref[...]) * k_scale, _stack_heads(v_ref[...])
    a_qk = _dot_nt(q_st, k_st) * jnp.where(incl, dec, 0.0)
    if has_beta:
        beta = _stack_cols(bcols, HEAD_DIM)
        kb = k_st * beta
        a = _dot_nt(kb, k_st) * jnp.where(strict, dec, 0.0)
        yield
        t_inv = yield from _unit_triangular_inverse(a)
        x = _dot(t_inv, jnp.concatenate([v_st * beta, kb * jnp.exp(gcb)], axis=1))
        u_st, w_st = x[:, :HEAD_DIM], x[:, HEAD_DIM:]
    else:
        u_st, w_st = v_st, None
    gtot = _stack_cols(gtots, HEAD_DIM)
    k_end = k_st * jnp.exp(gtot - gcb)
    q_dec = q_st * jnp.exp(gcb)
    hs = lambda t, h: t[h * CHUNK:(h + 1) * CHUNK]
    states = [s_scr[h] for h in range(N_HEADS)]
    q_s = [_dot(hs(q_dec, h), states[h]) for h in range(N_HEADS)]
    yield
    if has_beta:
        vn = jnp.concatenate([hs(u_st, h) - _dot(hs(w_st, h), states[h]) for h in range(N_HEADS)], axis=0)
    else:
        vn = u_st
    yield
    o_st = _dot(a_qk, vn)
    for h in range(N_HEADS):
        s_scr[h] = jnp.exp(gtots[h]) * states[h] + _dot_tn(hs(k_end, h), hs(vn, h))
    yield
    for h in range(N_HEADS):
        o_ref[:, h * HEAD_DIM:(h + 1) * HEAD_DIM] = hs(o_st, h) + q_s[h]
    yield


SCAN_NS = 4


HEADS_W = N_HEADS * HEAD_DIM


def _scan_specs(srcs, gates, nb, nchunk):
    ns = SCAN_NS
    assert nb % ns == 0
    cidx = lambda d, n: n + d * (nchunk - 1 - 2 * n)
    specs = [pl.BlockSpec((ns, CHUNK, HEADS_W), lambda d, b, n, c=c: (b, cidx(d, n), c)) for _, c in srcs]
    if gates is not None:
        specs.append(pl.BlockSpec((ns, CHUNK, LANE), lambda d, b, n, c=gates[1]: (b, cidx(d, n), c)))
    specs.append(pl.BlockSpec((2, LANE), lambda d, b, n: (0, 0)))
    ospec = pl.BlockSpec((None, ns, CHUNK, HEADS_W), lambda d, b, n: (d, b, cidx(d, n), 0))
    return specs, ospec


def linear_scan_bidir(q, k, v, gates, gp, s0, has_beta, k_scale=1.0):
    nb, t, _ = q[0].shape
    nchunk = t // CHUNK
    assert t % CHUNK == 0
    ns = SCAN_NS
    specs, ospec = _scan_specs((q, k, v), gates if has_beta else None, nb, nchunk)
    sspec = pl.BlockSpec((None, ns, N_HEADS, HEAD_DIM, HEAD_DIM), lambda d, b, n: (d, b, 0, 0, 0))
    args = [q[0], k[0], v[0]] + ([gates[0]] if has_beta else []) + [gp, s0]
    return pl.pallas_call(
        functools.partial(_linear_scan_kernel, has_beta=has_beta, k_scale=k_scale),
        grid=(2, nb // ns, nchunk),
        in_specs=specs + [sspec],
        out_specs=[ospec, sspec],
        out_shape=[jax.ShapeDtypeStruct((2, nb, t, HEADS_W), F32),
                   jax.ShapeDtypeStruct((2, nb, N_HEADS, HEAD_DIM, HEAD_DIM), F32)],
        scratch_shapes=[pltpu.VMEM((ns, N_HEADS, HEAD_DIM, HEAD_DIM), F32)],
        name="gdn_scan" if has_beta else "retention_scan",
    )(*args)


def linear_scan_two_pass(ctx_args, lat_args, gp, has_beta, k_scale=1.0):
    nb = ctx_args[0][0].shape[0]
    zero = jnp.zeros((2, nb, N_HEADS, HEAD_DIM, HEAD_DIM), F32)
    o_ctx, s_ctx = linear_scan_bidir(*ctx_args, gp, zero, has_beta, k_scale)
    o_lat, _ = linear_scan_bidir(*lat_args, gp, s_ctx, has_beta, k_scale)
    return o_ctx, o_lat


MLSTM_HP = True


def _mlstm_scan_kernel(q_ref, k_ref, v_ref, gt_ref, gp_ref, c0_ref, n0_ref, m0_ref, o_ref, cfin_ref, nfin_ref,
                       mfin_ref, c_scr, n_scr, m_scr):
    rev = pl.program_id(0) == 1
    step = pl.program_id(2)

    @pl.when(step == 0)
    def _():
        c_scr[...] = c0_ref[...]
        n_scr[...] = n0_ref[...]
        m_scr[...] = m0_ref[...]

    _interleave([_mlstm_chunk(q_ref.at[i], k_ref.at[i], v_ref.at[i], _mlstm_gates(gt_ref[i], gp_ref[...]),
                              o_ref.at[i], c_scr.at[i], n_scr.at[i], m_scr.at[i], rev)
                 for i in range(q_ref.shape[0])])

    @pl.when(step == pl.num_programs(2) - 1)
    def _():
        cfin_ref[...] = c_scr[...]
        nfin_ref[...] = n_scr[...]
        mfin_ref[...] = m_scr[...]


def _mlstm_chunk(q_ref, k_ref, v_ref, gt, o_ref, c_scr, n_scr, m_scr, rev):
    cum = _scan_cumsum(gt, rev)
    pick = lambda t, c: jnp.where(rev, t[:, N_HEADS + c:N_HEADS + c + 1], t[:, c:c + 1])
    q_st, k_st, v_st = _stack_heads(q_ref[...]), _stack_heads(k_ref[...]), _stack_heads(v_ref[...])
    hs = lambda t, h: t[h * CHUNK:(h + 1) * CHUNK]
    qk = _dot_nt(q_st, k_st, MLSTM_HP)
    yield
    bcums, srcs, inters, qcs, qns = [], [], [], [], []
    for h in range(N_HEADS):
        ic = pick(gt, h)
        bcum = pick(cum, 2 * N_HEADS + h)
        b_end = jnp.where(rev, bcum[0:1], bcum[CHUNK - 1:CHUNK])
        c_prev, n_prev, m_prev = c_scr[h], n_scr[h], m_scr[h][:, :1]
        a = b_end - bcum + ic
        m_new = jnp.maximum(b_end + m_prev, jnp.max(a, axis=0, keepdims=True))
        w_state = jnp.exp(a - m_new)
        decay = jnp.exp(b_end + m_prev - m_new)
        kw = hs(k_st, h) * w_state
        c_scr[h] = decay * c_prev + _dot_tn(kw, hs(v_st, h), MLSTM_HP)
        n_scr[h] = decay * n_prev + jnp.sum(kw, axis=0, keepdims=True)
        m_scr[h] = jnp.broadcast_to(m_new, (1, HEAD_DIM))
        bcums.append(bcum)
        srcs.append(bcum - ic)
        inters.append(bcum + m_prev)
        qcs.append(_dot(hs(q_st, h), c_prev, MLSTM_HP))
        qns.append(jnp.sum(hs(q_st, h) * n_prev, axis=1, keepdims=True))
    yield
    incl, _ = _chunk_masks(rev)
    dlog = _stack_cols(bcums, STACK) - _stack_cols(srcs, STACK).T
    inter = jnp.concatenate(inters, axis=0)
    m_t = jnp.maximum(inter, jnp.max(jnp.where(incl, dlog, -1e30), axis=1, keepdims=True))
    s = qk * jnp.where(incl, jnp.exp(jnp.where(incl, dlog, 0.0) - m_t), 0.0)
    w_inter = jnp.exp(inter - m_t)
    yield
    num = _dot(s, v_st, MLSTM_HP) + w_inter * jnp.concatenate(qcs, axis=0)
    den = jnp.sum(s, axis=1, keepdims=True) + w_inter * jnp.concatenate(qns, axis=0)
    yield
    out = num / jnp.maximum(jnp.abs(den), jnp.exp(-m_t))
    for h in range(N_HEADS):
        o_ref[:, h * HEAD_DIM:(h + 1) * HEAD_DIM] = hs(out, h)
    yield


def mlstm_scan_bidir(q, k, v, gates, gp, state):
    nb, t, _ = q[0].shape
    nchunk = t // CHUNK
    assert t % CHUNK == 0
    ns = SCAN_NS
    specs, ospec = _scan_specs((q, k, v), gates, nb, nchunk)
    cspec = pl.BlockSpec((None, ns, N_HEADS, HEAD_DIM, HEAD_DIM), lambda d, b, n: (d, b, 0, 0, 0))
    vspec = pl.BlockSpec((None, ns, N_HEADS, 1, HEAD_DIM), lambda d, b, n: (d, b, 0, 0, 0))
    cshape = jax.ShapeDtypeStruct((2, nb, N_HEADS, HEAD_DIM, HEAD_DIM), F32)
    vshape = jax.ShapeDtypeStruct((2, nb, N_HEADS, 1, HEAD_DIM), F32)
    o, c, n, m = pl.pallas_call(
        _mlstm_scan_kernel,
        grid=(2, nb // ns, nchunk),
        in_specs=specs + [cspec, vspec, vspec],
        out_specs=[ospec, cspec, vspec, vspec],
        out_shape=[jax.ShapeDtypeStruct((2, nb, t, HEADS_W), F32), cshape, vshape, vshape],
        scratch_shapes=[pltpu.VMEM((ns, N_HEADS, HEAD_DIM, HEAD_DIM), F32), pltpu.VMEM((ns, N_HEADS, 1, HEAD_DIM), F32),
                        pltpu.VMEM((ns, N_HEADS, 1, HEAD_DIM), F32)],
        name="mlstm_scan",
    )(q[0], k[0], v[0], gates[0], gp, *state)
    return o, (c, n, m)


def mlstm_two_pass(ctx_args, lat_args, gp):
    nb = ctx_args[0][0].shape[0]
    zero = (jnp.zeros((2, nb, N_HEADS, HEAD_DIM, HEAD_DIM), F32), jnp.zeros((2, nb, N_HEADS, 1, HEAD_DIM), F32),
            jnp.zeros((2, nb, N_HEADS, 1, HEAD_DIM), F32))
    o_ctx, s_ctx = mlstm_scan_bidir(*ctx_args, gp, zero)
    o_lat, _ = mlstm_scan_bidir(*lat_args, gp, s_ctx)
    return o_ctx, o_lat


PROJ_TM = 512
PROJ_TN_MAX = 2304


def _inproj_kernel(h_ref, sc_ref, sh_ref, w_ref, o_ref):
    hm = (h_ref[...] * (1.0 + sc_ref[...]) + sh_ref[...]).astype(BF16)
    o_ref[...] = jnp.dot(hm, w_ref[...], preferred_element_type=F32)


def in_projection(h, sc, sh, w):
    nb, n, dm = h.shape
    ncol = w.shape[1]
    tm = min(PROJ_TM, n)
    tn = max(c for c in range(LANE, min(ncol, PROJ_TN_MAX) + 1, LANE) if ncol % c == 0)
    return pl.pallas_call(
        _inproj_kernel,
        grid=(nb, n // tm, ncol // tn),
        in_specs=[pl.BlockSpec((None, tm, dm), lambda b, t, j: (b, t, 0)),
                  pl.BlockSpec((None, 1, dm), lambda b, t, j: (b, 0, 0)),
                  pl.BlockSpec((None, 1, dm), lambda b, t, j: (b, 0, 0)),
                  pl.BlockSpec((dm, tn), lambda b, t, j: (0, j))],
        out_specs=pl.BlockSpec((None, tm, tn), lambda b, t, j: (b, t, j)),
        out_shape=jax.ShapeDtypeStruct((nb, n, ncol), F32),
        name="in_projection",
    )(h, sc, sh, w)


def _conv_kernel(x_ref, w_ref, o_ref, *, grid_w, l2_blocks, q_scale, k_scale):
    x = x_ref[...]
    t = x.shape[0]
    w = w_ref[...]
    col = lax.broadcasted_iota(jnp.int32, x.shape, 0) & (grid_w - 1)
    left = jnp.where(col == 0, 0.0, pltpu.roll(x, 1, 0))
    right = jnp.where(col == grid_w - 1, 0.0, pltpu.roll(x, t - 1, 0))
    row = lambda kh: w[3 * kh:3 * kh + 1] * left + w[3 * kh + 1:3 * kh + 2] * x + w[3 * kh + 2:3 * kh + 3] * right
    acc = row(1)
    if grid_w < t:
        zero = jnp.zeros((grid_w, LANE), F32)
        acc = acc + jnp.concatenate([zero, row(0)[:t - grid_w]], axis=0) + jnp.concatenate([row(2)[grid_w:], zero], axis=0)
    y = acc * jax.nn.sigmoid(acc)
    c = pl.program_id(1)
    normed = y * lax.rsqrt(jnp.sum(y * y, axis=1, keepdims=True) + 1e-6)
    y = jnp.where(c < l2_blocks, normed, y)
    o_ref[...] = y * jnp.where(c < N_HEADS, q_scale, jnp.where(c < 2 * N_HEADS, k_scale, 1.0))


def conv_prep(proj, conv_w, nblk, grid_w, l2_blocks, q_scale, k_scale):
    nb, t, _ = proj.shape
    assert grid_w & (grid_w - 1) == 0 and t % grid_w == 0
    return pl.pallas_call(
        functools.partial(_conv_kernel, grid_w=grid_w, l2_blocks=l2_blocks, q_scale=q_scale, k_scale=k_scale),
        grid=(nb, nblk),
        in_specs=[pl.BlockSpec((None, t, LANE), lambda b, c: (b, 0, c)),
                  pl.BlockSpec((CONV_K * CONV_K, LANE), lambda b, c: (0, c))],
        out_specs=pl.BlockSpec((None, t, LANE), lambda b, c: (b, 0, c)),
        out_shape=jax.ShapeDtypeStruct((nb, t, nblk * LANE), F32),
        compiler_params=pltpu.CompilerParams(vmem_limit_bytes=VMEM_LIMIT),
        name="conv_prep",
    )(proj, conv_w.reshape(CONV_K * CONV_K, -1))


def _head_norm(o, center):
    outs = []
    for h in range(N_HEADS):
        x = o[:, h * HEAD_DIM:(h + 1) * HEAD_DIM]
        if center:
            x = x - jnp.mean(x, axis=1, keepdims=True)
        outs.append(x * lax.rsqrt(jnp.mean(x * x, axis=1, keepdims=True) + EPS))
    return jnp.concatenate(outs, axis=1)


def _mix_out(y_a, y_b, wo_ref, h_ref, g1_ref, lng_ref, lnb_ref, o_ref):
    y = (jnp.dot(y_a.astype(BF16), wo_ref[:HEADS_W, :], preferred_element_type=F32)
         + jnp.dot(y_b.astype(BF16), wo_ref[HEADS_W:, :], preferred_element_type=F32))
    o_ref[...] = _ln_rows(ALPHA * h_ref[...] + g1_ref[...] * y, lng_ref[...], lnb_ref[...])


def _merge_even_kernel(og_ref, or_ref, za_ref, zr_ref, gg_ref, rg_ref, wo_ref, h_ref, g1_ref, lng_ref, lnb_ref, o_ref):
    za, zr = za_ref[...], zr_ref[...]
    y_g = _head_norm(og_ref[0] + og_ref[1], False) * gg_ref[...] * (za * jax.nn.sigmoid(za))
    y_r = _head_norm(or_ref[0] + or_ref[1], True) * rg_ref[...] * (zr * jax.nn.sigmoid(zr))
    _mix_out(y_g, y_r, wo_ref, h_ref, g1_ref, lng_ref, lnb_ref, o_ref)


def _merge_odd_kernel(om_ref, ys_ref, og_ref, u_ref, mg_ref, dsk_ref, bglu_ref, wglu_ref, wo_ref, h_ref, g1_ref,
                      lng_ref, lnb_ref, o_ref):
    y_m = _head_norm(om_ref[0] + om_ref[1], True) * mg_ref[...] * jax.nn.sigmoid(og_ref[...])
    y = jax.nn.gelu(ys_ref[...] + dsk_ref[...] * u_ref[...])
    y = y * jax.nn.sigmoid(jnp.dot(y.astype(BF16), wglu_ref[...], preferred_element_type=F32) + bglu_ref[...])
    _mix_out(y_m, y, wo_ref, h_ref, g1_ref, lng_ref, lnb_ref, o_ref)


def _merge_call(kernel_fn, name, scans, toks, rows, mats, h, g1, ln_g, ln_b):
    nb, n, dm = h.shape
    tm = min(PROJ_TM, n)
    full = lambda a: pl.BlockSpec(a.shape, lambda b, t: (0,) * a.ndim)
    rowv = lambda v: v.reshape(1, -1)
    specs, args = [], []
    for a in scans:
        specs.append(pl.BlockSpec((2, None, tm, HEADS_W), lambda b, t: (0, b, t, 0)))
        args.append(a)
    for a, c in toks:
        specs.append(pl.BlockSpec((None, tm, HEADS_W), lambda b, t, c=c: (b, t, c)))
        args.append(a)
    for v in rows:
        args.append(rowv(v))
        specs.append(full(args[-1]))
    for m in mats:
        args.append(m)
        specs.append(full(m))
    args += [h, g1, rowv(ln_g), rowv(ln_b)]
    specs += [pl.BlockSpec((None, tm, dm), lambda b, t: (b, t, 0)), pl.BlockSpec((None, 1, dm), lambda b, t: (b, 0, 0)),
              full(args[-2]), full(args[-1])]
    return pl.pallas_call(
        kernel_fn,
        grid=(nb, n // tm),
        in_specs=specs,
        out_specs=pl.BlockSpec((None, tm, dm), lambda b, t: (b, t, 0)),
        out_shape=jax.ShapeDtypeStruct((nb, n, dm), F32),
        compiler_params=pltpu.CompilerParams(vmem_limit_bytes=VMEM_LIMIT),
        name=name,
    )(*args)


def split_cols(t, sizes):
    return jnp.split(t, np.cumsum(sizes)[:-1].tolist(), axis=-1)


def heads(t, n):
    return t.reshape(t.shape[:-1] + (n, t.shape[-1] // n))


def flip_t(t):
    return None if t is None else jnp.flip(t, axis=1)


def l2norm(t):
    return t * lax.rsqrt(jnp.sum(t * t, axis=-1, keepdims=True) + 1e-6)


def layer_norm(t, g, b):
    mu = t.mean(-1, keepdims=True)
    var = jnp.square(t - mu).mean(-1, keepdims=True)
    return ((t - mu) * lax.rsqrt(var + EPS)) * g + b


def rms_norm_heads(o, g):
    y = o * lax.rsqrt(jnp.mean(o * o, axis=-1, keepdims=True) + EPS) * g
    return y.reshape(o.shape[:2] + (-1,))


def group_norm_heads(o, g):
    mu = o.mean(-1, keepdims=True)
    var = jnp.square(o - mu).mean(-1, keepdims=True)
    return ((o - mu) * lax.rsqrt(var + EPS)).reshape(o.shape[:2] + (-1,)) * g


def short_conv(t, w, on_grid):
    ch = t.shape[-1]
    if on_grid:
        b, n = t.shape[:2]
        rows = n // GRID_W
        tg = t.reshape(b, rows, GRID_W, ch)
        y = lax.conv_general_dilated(tg, w[:, :, None, :], (1, 1), 'SAME',
                                     dimension_numbers=('NHWC', 'HWIO', 'NHWC'), feature_group_count=ch)
        return y.reshape(b, n, ch)
    return lax.conv_general_dilated(t, w[CONV_K // 2][:, None, :], (1,), 'SAME',
                                    dimension_numbers=('NWC', 'WIO', 'NWC'), feature_group_count=ch)


def to_chunks(t):
    b, n, h = t.shape[:3]
    t = t.reshape((b, n // CHUNK, CHUNK, h) + t.shape[3:])
    return jnp.moveaxis(jnp.moveaxis(t, 3, 1), 2, 0)


def from_chunks(t):
    t = jnp.moveaxis(jnp.moveaxis(t, 0, 2), 1, 3)
    return t.reshape((t.shape[0], -1) + t.shape[3:])


def linear_scan(q, k, v, g, beta, s0, want_out):
    b, _, h, dk = k.shape
    dv = v.shape[-1]
    kc, vc = to_chunks(k), to_chunks(v)
    gcum = jnp.cumsum(to_chunks(g), axis=-1)
    diff = gcum[..., :, None] - gcum[..., None, :]
    incl = jnp.tril(jnp.ones((CHUNK, CHUNK), bool))
    if beta is None:
        u, w = vc, None
    else:
        bc = to_chunks(beta)[..., None]
        kb = kc * bc
        strict = jnp.tril(jnp.ones((CHUNK, CHUNK), bool), -1)
        a = jnp.where(strict, jnp.einsum('...ik,...jk->...ij', kb, kc) * jnp.exp(jnp.where(strict, diff, 0.0)), 0.0)
        rhs = jnp.concatenate([vc * bc, kb * jnp.exp(gcum)[..., None]], axis=-1)
        sol = lax.linalg.triangular_solve(a + jnp.eye(CHUNK, dtype=a.dtype), rhs, left_side=True,
                                          lower=True, unit_diagonal=True)
        u, w = sol[..., :dv], sol[..., dv:]
    k_end = kc * jnp.exp(gcum[..., -1:] - gcum)[..., None]
    g_end = jnp.exp(gcum[..., -1])[..., None, None]
    if want_out:
        qc = to_chunks(q)
        q_dec = qc * jnp.exp(gcum)[..., None]
        a_qk = jnp.where(incl, jnp.einsum('...ik,...jk->...ij', qc, kc) * jnp.exp(jnp.where(incl, diff, 0.0)), 0.0)
    else:
        q_dec, a_qk = None, None
    if s0 is None:
        s0 = jnp.zeros((b, h, dk, dv), F32)

    def step(s, inp):
        qd, ke, uc, wc, aqk, ge = inp
        vn = uc if wc is None else uc - jnp.einsum('bhck,bhkv->bhcv', wc, s)
        s_new = ge * s + jnp.einsum('bhck,bhcv->bhkv', ke, vn)
        if qd is None:
            return s_new, None
        return s_new, jnp.einsum('bhck,bhkv->bhcv', qd, s) + jnp.einsum('bhij,bhjv->bhiv', aqk, vn)

    s_fin, o = lax.scan(step, s0, (q_dec, k_end, u, w, a_qk, g_end))
    return (from_chunks(o) if want_out else None), s_fin


def mlstm_scan(q, k, v, log_i, log_f, s0, want_out):
    b, _, h, dk = k.shape
    dv = v.shape[-1]
    if s0 is None:
        s0 = (jnp.zeros((b, h, dk, dv), F32), jnp.zeros((b, h, dk), F32), jnp.zeros((b, h), F32))
    incl = jnp.tril(jnp.ones((CHUNK, CHUNK), bool))

    def step(carry, inp):
        c_prev, n_prev, m_prev = carry
        qc, kc, vc, ic, fc = inp
        bcum = jnp.cumsum(fc, axis=-1)
        b_end = bcum[..., -1]
        a = b_end[..., None] - bcum + ic
        m_new = jnp.maximum(b_end + m_prev, a.max(-1))
        w_state = jnp.exp(a - m_new[..., None])
        decay = jnp.exp(b_end + m_prev - m_new)
        c_new = decay[..., None, None] * c_prev + jnp.einsum('bhck,bhcv->bhkv', kc * w_state[..., None], vc)
        n_new = decay[..., None] * n_prev + jnp.einsum('bhck,bhc->bhk', kc, w_state)
        carry_new = (c_new, n_new, m_new)
        if qc is None:
            return carry_new, None
        dlog = jnp.where(incl, bcum[..., :, None] - bcum[..., None, :] + ic[..., None, :], -jnp.inf)
        inter = bcum + m_prev[..., None]
        m_t = jnp.maximum(inter, dlog.max(-1))
        s = jnp.einsum('bhik,bhjk->bhij', qc, kc) * jnp.exp(dlog - m_t[..., None])
        w_inter = jnp.exp(inter - m_t)[..., None]
        num = jnp.einsum('bhij,bhjv->bhiv', s, vc) + w_inter * jnp.einsum('bhik,bhkv->bhiv', qc, c_prev)
        den = s.sum(-1, keepdims=True) + w_inter * jnp.einsum('bhik,bhk->bhi', qc, n_prev)[..., None]
        return carry_new, num / jnp.maximum(jnp.abs(den), jnp.exp(-m_t)[..., None])

    xs = (to_chunks(q) if want_out else None, to_chunks(k), to_chunks(v), to_chunks(log_i), to_chunks(log_f))
    s_fin, hs = lax.scan(step, s0, xs)
    return (from_chunks(hs) if want_out else None), s_fin


def two_pass(scan_fn, ctx_args, lat_args, ctx_out, reverse):
    if reverse:
        ctx_args = [flip_t(t) for t in ctx_args]
        lat_args = [flip_t(t) for t in lat_args]
    o_ctx, s_ctx = scan_fn(*ctx_args, None, ctx_out)
    o_lat, _ = scan_fn(*lat_args, s_ctx, True)
    if reverse:
        o_ctx, o_lat = flip_t(o_ctx), flip_t(o_lat)
    return o_ctx, o_lat


def retention_log_decay(direction):
    expo = 5.0 + 2.0 * jnp.arange(RET_HEADS, dtype=F32) + direction
    return jnp.log1p(-jnp.exp2(-expo))


GATE_COLS = 4 * N_HEADS
Q_SCALE = HEAD_DIM ** -0.5


def _gate_row(*vals):
    v = jnp.concatenate([jnp.ravel(x) for x in vals])
    return jnp.pad(v, (0, LANE - v.shape[0]))


def _pad_gate_cols(w):
    return jnp.pad(w, ((0, 0), (0, LANE - w.shape[1])))


def gdn_retention_mixer(hs_, mods, w_in, w_out, conv_w, a_log, dt_bias, gdn_gain, ret_gain, ln_g, ln_b, ctx_out):
    w = jnp.concatenate([w_in[:, :4 * GDN_W], w_in[:, 4 * GDN_W + GATE_COLS:],
                         _pad_gate_cols(w_in[:, 4 * GDN_W:4 * GDN_W + GATE_COLS])], axis=1).astype(BF16)
    gate_blk = (4 * GDN_W + 4 * RET_W) // LANE
    gp_gdn = jnp.stack([_gate_row(-jnp.exp(a_log)), _gate_row(dt_bias)])
    gp_ret = jnp.stack([_gate_row(retention_log_decay(0), retention_log_decay(1)), jnp.zeros((LANE,), F32)])
    wo = w_out.astype(BF16)
    projs, convs = [], []
    for (h, (sc, sh, _)), grid_w in zip(zip(hs_, mods), (hs_[0].shape[1], GRID_W)):
        p = in_projection(h, sc, sh, w)
        projs.append(p)
        convs.append(conv_prep(p, conv_w, 3 * GDN_W // LANE, grid_w, 2 * N_HEADS, Q_SCALE, 1.0))
    gdn_args = [((cv, 0), (cv, 1), (cv, 2), (p, gate_blk)) for p, cv in zip(projs, convs)]
    ret_args = [((p, 4), (p, 5), (p, 6), None) for p in projs]
    og = linear_scan_two_pass(gdn_args[0], gdn_args[1], gp_gdn, True)
    orr = linear_scan_two_pass(ret_args[0], ret_args[1], gp_ret, False, Q_SCALE)
    outs = []
    for i in range(2):
        if i == 0 and not ctx_out:
            outs.append(None)
            continue
        outs.append(_merge_call(_merge_even_kernel, "merge_even", [og[i], orr[i]], [(projs[i], 3), (projs[i], 7)],
                                [jnp.tile(gdn_gain, N_HEADS), ret_gain], [wo], hs_[i], mods[i][2], ln_g, ln_b))
    return outs


def mlstm_s5_mixer(hs_, mods, w_in, w_out, conv_w, gate_bias, mlstm_gain, lam_re, lam_im, log_dt,
                   b_re, b_im, c_re, c_im, d_skip, w_glu, b_glu, ln_g, ln_b, ctx_out):
    w = jnp.concatenate([w_in[:, :4 * MLSTM_W], _pad_gate_cols(w_in[:, 4 * MLSTM_W:4 * MLSTM_W + GATE_COLS])],
                        axis=1).astype(BF16)
    w_u = w_in[:, 4 * MLSTM_W + GATE_COLS:].astype(BF16)
    gate_blk = 4 * MLSTM_W // LANE
    gp = jnp.stack([_gate_row(gate_bias[0, 0], gate_bias[1, 0], gate_bias[0, 1], gate_bias[1, 1]),
                    jnp.zeros((LANE,), F32)])
    wo, wglu = w_out.astype(BF16), w_glu.astype(BF16)
    projs, us, args = [], [], []
    for (h, (sc, sh, _)), grid_w in zip(zip(hs_, mods), (hs_[0].shape[1], GRID_W)):
        p = in_projection(h, sc, sh, w)
        cv = conv_prep(p, conv_w, 2 * MLSTM_W // LANE, grid_w, 0, 1.0, Q_SCALE)
        projs.append(p)
        us.append(in_projection(h, sc, sh, w_u))
        args.append(((cv, 0), (cv, 1), (p, 2), (p, gate_blk)))
    om = mlstm_two_pass(args[0], args[1], gp)
    ys = s5_bidirectional(us[0], us[1], _s5_weights(lam_re, lam_im, log_dt, b_re, b_im, c_re, c_im))
    outs = []
    for i in range(2):
        if i == 0 and not ctx_out:
            outs.append(None)
            continue
        outs.append(_merge_call(_merge_odd_kernel, "merge_odd", [om[i]], [(ys[i], 0), (projs[i], 3), (us[i], 0)],
                                [mlstm_gain, d_skip, b_glu], [wglu, wo], hs_[i], mods[i][2], ln_g, ln_b))
    return outs


def kernel(x, c, ctx, c_ctx, w_mod, b_mod, ln1_g, ln1_b, ln2_g, ln2_b, w_router, w_gate, w_up, w_down,
           ev_w_in, ev_w_out, ev_conv, ev_a_log, ev_dt_bias, ev_gdn_norm, ev_ret_norm,
           od_w_in, od_w_out, od_conv, od_gate_bias, od_mlstm_norm, od_lam_re, od_lam_im, od_log_dt,
           od_b_re, od_b_im, od_c_re, od_c_im, od_d_skip, od_w_glu, od_b_glu):
    h_lat, h_ctx = x, ctx
    s_lat = jax.nn.silu(c)
    s_ctx = jax.nn.silu(c_ctx)
    experts = (w_gate.astype(BF16), w_up.astype(BF16), w_down.astype(BF16))
    for l in range(DEPTH):
        last = l == DEPTH - 1
        sh1, sc1, g1, sh2, sc2, g2 = jnp.split((s_lat @ w_mod[l] + b_mod[l])[:, None, :], 6, axis=-1)
        bc = lambda v: jnp.broadcast_to(v, (BATCH, 1, D_MODEL))
        csh1, csc1, cg1, csh2, csc2, cg2 = [bc(v) for v in jnp.split(s_ctx @ w_mod[l] + b_mod[l], 6, axis=-1)]
        streams = (h_ctx, h_lat)
        mods = ((csc1, csh1, cg1), (sc1, sh1, g1))
        if l % 2 == 0:
            e = l // 2
            h_ctx, h_lat = gdn_retention_mixer(streams, mods, ev_w_in[e], ev_w_out[e], ev_conv[e], ev_a_log[e],
                                               ev_dt_bias[e], ev_gdn_norm[e], ev_ret_norm[e], ln1_g[l], ln1_b[l],
                                               not last)
        else:
            o = l // 2
            h_ctx, h_lat = mlstm_s5_mixer(streams, mods, od_w_in[o], od_w_out[o], od_conv[o], od_gate_bias[o],
                                          od_mlstm_norm[o], od_lam_re[o], od_lam_im[o], od_log_dt[o],
                                          od_b_re[o], od_b_im[o], od_c_re[o], od_c_im[o], od_d_skip[o],
                                          od_w_glu[o], od_b_glu[o], ln1_g[l], ln1_b[l], not last)
        h_lat = moe_block(h_lat, sc2, sh2, g2, ln2_g[l], ln2_b[l], w_router[l], *experts, l)
        if not last:
            h_ctx = moe_block(h_ctx, csc2, csh2, cg2, ln2_g[l], ln2_b[l], w_router[l], *experts, l)
    return h_lat
```

```python
import functools

import jax
import jax.numpy as jnp
from jax import lax
from jax.experimental import pallas as pl
from jax.experimental.pallas import tpu as pltpu

D_MODEL = 1024
DEPTH = 4
GRID_W = 64
CHUNK = 64
CONV_K = 3
HEAD_DIM = D_MODEL // 8
GDN_HEADS = 4
RET_HEADS = 4
MLSTM_HEADS = 4
GDN_W = GDN_HEADS * HEAD_DIM
RET_W = RET_HEADS * HEAD_DIM
MLSTM_W = MLSTM_HEADS * HEAD_DIM
S5_CH = D_MODEL // 2
S5_GROUP = 16
S5_GROUPS = S5_CH // S5_GROUP
S5_STATE = 64
N_EXPERTS = 16
EXPERT_FF = 2 * D_MODEL
CAPACITY_FACTOR = 2
ALPHA = (2 * DEPTH) ** 0.25
EPS = 1e-5
L2_EPS = 1e-6
MASKED = -1e30
F32 = jnp.float32
BF16 = jnp.bfloat16

LANE = 128


S5_L = 16
S5_NB = S5_CH // LANE
S5_GPB = LANE // S5_GROUP
S5_SW = S5_GPB * S5_STATE
VMEM_LIMIT = 56 * 1024 * 1024


def _s5_expand_kernel(c_ref, o_ref, *, row_item_log2, col_item_log2):
    c = c_ref[...]
    nk, nc = c.shape[1], o_ref.shape[1]
    gmask = S5_GPB - 1
    k = lax.broadcasted_iota(jnp.int32, (nk, nc), 0)
    col = lax.broadcasted_iota(jnp.int32, (nk, nc), 1)
    src = ((col >> (col_item_log2 + S5_GPB.bit_length() - 1)) << col_item_log2) | (col & ((1 << col_item_log2) - 1))
    rep = jnp.where(k == src, 1.0, 0.0).astype(BF16)
    wide = jnp.dot(c, rep, preferred_element_type=F32)
    r = lax.broadcasted_iota(jnp.int32, wide.shape, 0)
    cc = lax.broadcasted_iota(jnp.int32, wide.shape, 1)
    same = ((r >> row_item_log2) & gmask) == ((cc >> col_item_log2) & gmask)
    o_ref[...] = jnp.where(same, wide, 0.0).astype(BF16)


def _s5_expand(c, row_item_log2, col_item_log2):
    nbk, rows, nk = c.shape
    return pl.pallas_call(
        functools.partial(_s5_expand_kernel, row_item_log2=row_item_log2, col_item_log2=col_item_log2),
        grid=(nbk,),
        in_specs=[pl.BlockSpec((None, rows, nk), lambda j: (j, 0, 0))],
        out_specs=pl.BlockSpec((None, rows, nk * S5_GPB), lambda j: (j, 0, 0)),
        out_shape=jax.ShapeDtypeStruct((nbk, rows, nk * S5_GPB), BF16),
        compiler_params=pltpu.CompilerParams(vmem_limit_bytes=VMEM_LIMIT),
        name="s5_expand",
    )(c)


def _s5_weights(lam_re, lam_im, log_dt, b_re, b_im, c_re, c_im):
    L, G = S5_L, S5_GROUPS
    hp = lax.Precision.HIGHEST
    taus = jnp.arange(L + 1, dtype=F32)[:, None, None]
    ks, ws, cas, ds = [], [], [], []
    for d in range(2):
        lr = jnp.minimum(lam_re[d], -1e-4)
        li = lam_im[d]
        dt = jnp.exp(log_dt[d])[:, None]
        mag = jnp.exp(lr * dt)
        ab_re, ab_im = mag * jnp.cos(li * dt), mag * jnp.sin(li * dt)
        xr, xi, den = ab_re - 1.0, ab_im, lr * lr + li * li
        f_re = (xr * lr + xi * li) / den
        f_im = (xi * lr - xr * li) / den
        bb_re = f_re[..., None] * b_re - f_im[..., None] * b_im
        bb_im = f_re[..., None] * b_im + f_im[..., None] * b_re
        pmag = jnp.exp(taus * (lr * dt))
        ar, ai = pmag * jnp.cos(taus * (li * dt)), pmag * jnp.sin(taus * (li * dt))
        wr = ar[..., None] * bb_re - ai[..., None] * bb_im
        wi = ar[..., None] * bb_im + ai[..., None] * bb_re
        k = (jnp.einsum('gop,tgpi->tgio', c_re, wr, precision=hp)
             - jnp.einsum('gop,tgpi->tgio', c_im, wi, precision=hp))
        car = c_re[None] * ar[:, :, None, :] - c_im[None] * ai[:, :, None, :]
        cai = c_re[None] * ai[:, :, None, :] + c_im[None] * ar[:, :, None, :]
        ks.append(k)
        ws.append((wr, wi))
        cas.append((car, cai))
        ds.append((ar[L], ai[L]))

    split_g = lambda t, g_axis: t.reshape(t.shape[:g_axis] + (S5_NB, S5_GPB) + t.shape[g_axis + 1:]).astype(BF16)
    compact = lambda t: t.reshape(S5_NB, L * LANE, -1)
    h_log2, p_log2 = S5_GROUP.bit_length() - 1, S5_STATE.bit_length() - 1

    kf, kb = ks
    kc = jnp.concatenate([kb[1:L][::-1], (kf[0] + kb[0])[None], kf[1:L]], axis=0)
    idx = (jnp.arange(L)[None, :] - jnp.arange(L)[:, None]) + (L - 1)
    tz = _s5_expand(compact(split_g(kc[idx], 2).transpose(2, 0, 3, 4, 1, 5)), h_log2, h_log2)

    pb_c, ca_c = [], []
    for d in range(2):
        wr, wi = ws[d]
        order = jnp.arange(L - 1, -1, -1) if d == 0 else jnp.arange(L)
        pb_c.append(jnp.stack([jnp.swapaxes(w[order], -1, -2) for w in (wr, wi)]))
        car, cai = cas[d]
        order = jnp.arange(1, L + 1) if d == 0 else jnp.arange(L, 0, -1)
        ca_c.append(jnp.stack([jnp.swapaxes(m, -1, -2) for m in (car[order], -cai[order])]))
    pb = _s5_expand(compact(split_g(jnp.stack(pb_c), 3).transpose(3, 2, 4, 5, 0, 1, 6)), h_log2, p_log2)
    ca = _s5_expand(compact(split_g(jnp.stack(ca_c), 3).transpose(3, 0, 1, 4, 5, 2, 6)), p_log2, h_log2)
    dr = jnp.stack([ds[0][0], ds[1][0]], 0).reshape(2, S5_NB, 1, S5_SW).transpose(1, 0, 2, 3).reshape(2 * S5_NB, 1, S5_SW)
    di = jnp.stack([ds[0][1], ds[1][1]], 0).reshape(2, S5_NB, 1, S5_SW).transpose(1, 0, 2, 3).reshape(2 * S5_NB, 1, S5_SW)
    return tz, pb, ca, dr, di


def _s5_p_kernel(u_ref, pb_ref, p_ref, *, nb, nc):
    res = jnp.dot(u_ref[...], pb_ref[...], preferred_element_type=F32)
    for b in range(nb):
        p_ref[:, b * 2 * S5_SW:(b + 1) * 2 * S5_SW] = res[b * nc:(b + 1) * nc]


def _s5_scan_kernel(p_ref, dr_ref, di_ref, s_ref, *, n_ctx, n_lat):
    rev = pl.program_id(0) % 2
    dr = dr_ref[...]
    di = di_ref[...]
    nbatch = p_ref.shape[1]

    def phase(base, n, carry):
        def body(step, carry):
            sr, si = carry
            row = base + jnp.where(rev == 0, step, n - 1 - step)
            s_ref[row, :, :S5_SW] = sr
            s_ref[row, :, S5_SW:] = si
            p = p_ref[row]
            nr = dr * sr - di * si + p[:, :S5_SW]
            ni = dr * si + di * sr + p[:, S5_SW:]
            return nr, ni
        return lax.fori_loop(0, n, body, carry)

    zero = jnp.zeros((nbatch, S5_SW), F32)
    carry = phase(0, n_ctx, (zero, zero))
    phase(n_ctx, n_lat, carry)


def _s5_y_kernel(u_ref, tz_ref, sf_ref, sb_ref, ca_ref, y_ref):
    y = jnp.dot(u_ref[...], tz_ref[...], preferred_element_type=F32)
    y += jnp.dot(sf_ref[...].astype(BF16), ca_ref[:2 * S5_SW, :], preferred_element_type=F32)
    y += jnp.dot(sb_ref[...].astype(BF16), ca_ref[2 * S5_SW:, :], preferred_element_type=F32)
    y_ref[...] = y


def s5_bidirectional(u_ctx, u_lat, weights):
    tz, pb, ca, dr, di = weights
    L = S5_L
    nb, t_ctx, _ = u_ctx.shape
    t_lat = u_lat.shape[1]
    assert t_ctx % L == 0 and t_lat % L == 0
    n_ctx, n_lat = t_ctx // L, t_lat // L
    nc = n_ctx + n_lat
    kw = L * LANE
    sw2 = 2 * S5_SW
    u = jnp.concatenate([u_ctx, u_lat], axis=1)
    ub = u.reshape(nb * nc, L, S5_NB, LANE).transpose(2, 0, 1, 3).reshape(S5_NB, nb * nc, kw).astype(BF16)

    p = pl.pallas_call(
        functools.partial(_s5_p_kernel, nb=nb, nc=nc),
        grid=(S5_NB, 2),
        in_specs=[pl.BlockSpec((None, nb * nc, kw), lambda j, d: (j, 0, 0)),
                  pl.BlockSpec((None, kw, sw2), lambda j, d: (j, 0, d))],
        out_specs=pl.BlockSpec((nc, nb * sw2), lambda j, d: (0, j * 2 + d)),
        out_shape=jax.ShapeDtypeStruct((nc, S5_NB * 2 * nb * sw2), F32),
        compiler_params=pltpu.CompilerParams(vmem_limit_bytes=VMEM_LIMIT),
        name="s5_chunk_inputs",
    )(ub, pb)

    p4 = p.reshape(nc, S5_NB * 2, nb, sw2)
    s4 = pl.pallas_call(
        functools.partial(_s5_scan_kernel, n_ctx=n_ctx, n_lat=n_lat),
        grid=(S5_NB * 2,),
        in_specs=[pl.BlockSpec((nc, None, nb, sw2), lambda g: (0, g, 0, 0)),
                  pl.BlockSpec((None, 1, S5_SW), lambda g: (g, 0, 0)),
                  pl.BlockSpec((None, 1, S5_SW), lambda g: (g, 0, 0))],
        out_specs=pl.BlockSpec((nc, None, nb, sw2), lambda g: (0, g, 0, 0)),
        out_shape=jax.ShapeDtypeStruct(p4.shape, F32),
        compiler_params=pltpu.CompilerParams(vmem_limit_bytes=VMEM_LIMIT),
        name="s5_state_scan",
    )(p4, dr, di)

    s2 = s4.reshape(nc, S5_NB * 2 * nb * sw2)
    yb = pl.pallas_call(
        _s5_y_kernel,
        grid=(S5_NB, nb),
        in_specs=[pl.BlockSpec((None, nc, kw), lambda j, b: (j, b, 0)),
                  pl.BlockSpec((None, kw, kw), lambda j, b: (j, 0, 0)),
                  pl.BlockSpec((nc, sw2), lambda j, b: (0, (j * 2) * nb + b)),
                  pl.BlockSpec((nc, sw2), lambda j, b: (0, (j * 2 + 1) * nb + b)),
                  pl.BlockSpec((None, 2 * sw2, kw), lambda j, b: (j, 0, 0))],
        out_specs=pl.BlockSpec((None, nc, kw), lambda j, b: (j, b, 0)),
        out_shape=jax.ShapeDtypeStruct((S5_NB, nb * nc, kw), F32),
        compiler_params=pltpu.CompilerParams(vmem_limit_bytes=VMEM_LIMIT),
        name="s5_output",
    )(ub, tz, s2, s2, ca)
    y = yb.reshape(S5_NB, nb, nc, L, LANE).transpose(1, 2, 3, 0, 4).reshape(nb, nc * L, S5_CH)
    return y[:, :t_ctx], y[:, t_ctx:]


ROUTE_TN = 512
MOE_TN = 2048
SEL_ROWS = 128
FF_SPLIT = 4
FFN_ROWS = 1024
AFF_BITS = 30


def _route_kernel(h_ref, sc_ref, sh_ref, wr_ref, hm_ref, afft_ref, asp_ref):
    hm = h_ref[...] * (1.0 + sc_ref[...]) + sh_ref[...]
    hm_ref[...] = hm.astype(BF16)
    logits = jnp.dot(hm, wr_ref[...], precision=lax.Precision.HIGHEST, preferred_element_type=F32)
    lane = lax.broadcasted_iota(jnp.int32, logits.shape, 1)
    logits = jnp.where(lane < N_EXPERTS, logits, -jnp.inf)
    ex = jnp.exp(logits - jnp.max(logits, axis=1, keepdims=True))
    aff = ex / jnp.sum(ex, axis=1, keepdims=True)
    afft_ref[...] = aff.T[:N_EXPERTS, :]
    hi = aff.astype(BF16).astype(F32)
    mid = (aff - hi).astype(BF16).astype(F32)
    lo = (aff - hi - mid).astype(BF16).astype(F32)
    asp = hi + pltpu.roll(mid, N_EXPERTS, 1) + pltpu.roll(lo, 2 * N_EXPERTS, 1)
    asp_ref[...] = asp.astype(BF16)


def _select_kernel(aff_ref, pos_ref, *, nblk_log2, cap):
    a = aff_ref[...]
    r = a.shape[0]
    bits = pltpu.bitcast(a, jnp.int32)
    ri = lax.broadcasted_iota(jnp.int32, (r, r), 0)
    rj = lax.broadcasted_iota(jnp.int32, (r, r), 1)
    same = (ri >> nblk_log2) == (rj >> nblk_log2)
    gm = jnp.where(same, 1.0, 0.0).astype(BF16)
    lm = jnp.where(same & (rj < ri), 1.0, 0.0).astype(BF16)
    li = lax.broadcasted_iota(jnp.int32, (LANE, LANE), 0)
    lj = lax.broadcasted_iota(jnp.int32, (LANE, LANE), 1)
    um = jnp.where(li <= lj, 1.0, 0.0).astype(BF16)

    def group_count(mask):
        rc = jnp.sum(jnp.where(mask, 1.0, 0.0), axis=1, keepdims=True)
        gc = jnp.dot(gm, jnp.broadcast_to(rc, (r, LANE)).astype(BF16), preferred_element_type=F32)
        return gc[:, :1]

    def bisect(i, thr):
        cand = thr | jnp.left_shift(1, AFF_BITS - 1 - i)
        return jnp.where(group_count(bits >= cand) >= cap, cand, thr)

    thr = lax.fori_loop(0, AFF_BITS, bisect, jnp.zeros((r, 1), jnp.int32))

    def prefix(mask):
        x = jnp.where(mask, 1.0, 0.0)
        inc = jnp.dot(x.astype(BF16), um, preferred_element_type=F32)
        tot = jnp.broadcast_to(inc[:, LANE - 1:LANE], (r, LANE)).astype(BF16)
        return inc - x + jnp.dot(lm, tot, preferred_element_type=F32)

    gt = bits > thr
    eq = bits == thr
    need = cap - group_count(gt)
    sel = gt | (eq & (prefix(eq) < need))
    pos_ref[...] = jnp.where(sel, prefix(sel), -1.0).astype(jnp.int32)


def _onehot(pos_ref, cap):
    rows = lax.broadcasted_iota(jnp.int32, (cap, LANE), 0)
    blocks = [jnp.where(pos_ref[k:k + 1, :] == rows, 1.0, 0.0).astype(BF16) for k in range(pos_ref.shape[0])]
    return jnp.concatenate(blocks, axis=1)


def _ffn_kernel(pos_ref, hb_ref, asp_ref, wg_ref, wu_ref, wd_ref, ys_ref, xs_acc, g_acc, *, cap, nkt):
    e = pl.program_id(0)
    s = pl.program_id(2)

    @pl.when(s == 0)
    def _():
        xs_acc[...] = jnp.zeros_like(xs_acc)
        g_acc[...] = jnp.zeros_like(g_acc)

    rows = pl.ds(pl.multiple_of((s // nkt) * cap, cap), cap)
    oh = _onehot(pos_ref, cap)
    xs_acc[rows, :] += jnp.dot(oh, hb_ref[...], preferred_element_type=F32)
    g_acc[rows, :] += jnp.dot(oh, asp_ref[...], preferred_element_type=F32)

    @pl.when(s == pl.num_programs(2) - 1)
    def _():
        xs = xs_acc[...].astype(BF16)
        g = g_acc[...]
        lane = lax.broadcasted_iota(jnp.int32, g.shape, 1)
        gate = jnp.sum(jnp.where((lane & (N_EXPERTS - 1)) == e, g, 0.0), axis=1, keepdims=True)
        fw = EXPERT_FF // FF_SPLIT
        y = jnp.zeros(xs.shape, F32)
        for f in range(FF_SPLIT):
            hg = jnp.dot(xs, wg_ref[:, f * fw:(f + 1) * fw], preferred_element_type=F32)
            hu = jnp.dot(xs, wu_ref[:, f * fw:(f + 1) * fw], preferred_element_type=F32)
            hid = (hg * jax.nn.sigmoid(hg)) * hu
            y += jnp.dot(hid.astype(BF16), wd_ref[f * fw:(f + 1) * fw, :], preferred_element_type=F32)
        y = (y * gate).astype(BF16)
        for i in range(ys_ref.shape[0]):
            ys_ref[i] = y[i * cap:(i + 1) * cap]


def _ln_rows(z, g, b):
    mu = jnp.mean(z, axis=-1, keepdims=True)
    zc = z - mu
    var = jnp.mean(zc * zc, axis=-1, keepdims=True)
    return zc * lax.rsqrt(var + EPS) * g + b


def _combine_kernel(pos_ref, ys_ref, h_ref, g2_ref, lng_ref, lnb_ref, o_ref, acc, *, cap):
    e = pl.program_id(2)

    @pl.when(e == 0)
    def _():
        acc[...] = jnp.zeros_like(acc)

    oh = _onehot(pos_ref, cap)
    acc[...] += lax.dot_general(oh, ys_ref[...], (((0,), (0,)), ((), ())), preferred_element_type=F32)

    @pl.when(e == pl.num_programs(2) - 1)
    def _():
        z = ALPHA * h_ref[...] + g2_ref[...] * acc[...]
        o_ref[...] = _ln_rows(z, lng_ref[...], lnb_ref[...])


def moe_block(h, sc, sh, g2, ln_g, ln_b, w_router, wg, wu, wd, layer):
    nb, n, dm = h.shape
    cap = CAPACITY_FACTOR * n // N_EXPERTS
    nblk = n // LANE
    assert n % LANE == 0 and nblk & (nblk - 1) == 0
    tn_r = min(ROUTE_TN, n)
    tn = min(MOE_TN, n)
    wr = jnp.pad(w_router, ((0, 0), (0, LANE - N_EXPERTS)))
    row = lambda v: v.reshape(1, dm)

    hm, afft, asp = pl.pallas_call(
        _route_kernel,
        grid=(nb, n // tn_r),
        in_specs=[pl.BlockSpec((None, tn_r, dm), lambda b, t: (b, t, 0)),
                  pl.BlockSpec((None, 1, dm), lambda b, t: (b, 0, 0)),
                  pl.BlockSpec((None, 1, dm), lambda b, t: (b, 0, 0)),
                  pl.BlockSpec((dm, LANE), lambda b, t: (0, 0))],
        out_specs=[pl.BlockSpec((None, tn_r, dm), lambda b, t: (b, t, 0)),
                   pl.BlockSpec((None, N_EXPERTS, tn_r), lambda b, t: (b, 0, t)),
                   pl.BlockSpec((None, tn_r, LANE), lambda b, t: (b, t, 0))],
        out_shape=[jax.ShapeDtypeStruct((nb, n, dm), BF16),
                   jax.ShapeDtypeStruct((nb, N_EXPERTS, n), F32),
                   jax.ShapeDtypeStruct((nb, n, LANE), BF16)],
        name="moe_route",
    )(h, sc, sh, wr)

    rows_total = nb * N_EXPERTS * nblk
    rb = max(SEL_ROWS, N_EXPERTS * nblk)
    assert rows_total % rb == 0
    pos = pl.pallas_call(
        functools.partial(_select_kernel, nblk_log2=nblk.bit_length() - 1, cap=cap),
        grid=(rows_total // rb,),
        in_specs=[pl.BlockSpec((rb, LANE), lambda i: (i, 0))],
        out_specs=pl.BlockSpec((rb, LANE), lambda i: (i, 0)),
        out_shape=jax.ShapeDtypeStruct((rows_total, LANE), jnp.int32),
        name="moe_select",
    )(afft.reshape(rows_total, LANE))
    pos = pos.reshape(nb, N_EXPERTS, nblk, LANE)

    tb = tn // LANE
    nkt = n // tn
    group = min(nb, max(1, FFN_ROWS // cap))
    assert nb % group == 0
    sample = lambda bg, s: bg * group + s // nkt
    ys = pl.pallas_call(
        functools.partial(_ffn_kernel, cap=cap, nkt=nkt),
        grid=(N_EXPERTS, nb // group, group * nkt),
        in_specs=[pl.BlockSpec((None, None, tb, LANE), lambda e, bg, s: (sample(bg, s), e, s % nkt, 0)),
                  pl.BlockSpec((None, tn, dm), lambda e, bg, s: (sample(bg, s), s % nkt, 0)),
                  pl.BlockSpec((None, tn, LANE), lambda e, bg, s: (sample(bg, s), s % nkt, 0)),
                  pl.BlockSpec((None, None, dm, EXPERT_FF), lambda e, bg, s: (layer, e, 0, 0)),
                  pl.BlockSpec((None, None, dm, EXPERT_FF), lambda e, bg, s: (layer, e, 0, 0)),
                  pl.BlockSpec((None, None, EXPERT_FF, dm), lambda e, bg, s: (layer, e, 0, 0))],
        out_specs=pl.BlockSpec((group, None, cap, dm), lambda e, bg, s: (bg, e, 0, 0)),
        out_shape=jax.ShapeDtypeStruct((nb, N_EXPERTS, cap, dm), BF16),
        scratch_shapes=[pltpu.VMEM((group * cap, dm), F32), pltpu.VMEM((group * cap, LANE), F32)],
        compiler_params=pltpu.CompilerParams(vmem_limit_bytes=VMEM_LIMIT),
        name="moe_ffn",
    )(pos, hm, asp, wg, wu, wd)

    return pl.pallas_call(
        functools.partial(_combine_kernel, cap=cap),
        grid=(nb, n // tn, N_EXPERTS),
        in_specs=[pl.BlockSpec((None, None, tb, LANE), lambda b, t, e: (b, e, t, 0)),
                  pl.BlockSpec((None, None, cap, dm), lambda b, t, e: (b, e, 0, 0)),
                  pl.BlockSpec((None, tn, dm), lambda b, t, e: (b, t, 0)),
                  pl.BlockSpec((None, 1, dm), lambda b, t, e: (b, 0, 0)),
                  pl.BlockSpec((1, dm), lambda b, t, e: (0, 0)),
                  pl.BlockSpec((1, dm), lambda b, t, e: (0, 0))],
        out_specs=pl.BlockSpec((None, tn, dm), lambda b, t, e: (b, t, 0)),
        out_shape=jax.ShapeDtypeStruct((nb, n, dm), F32),
        scratch_shapes=[pltpu.VMEM((tn, dm), F32)],
        compiler_params=pltpu.CompilerParams(vmem_limit_bytes=VMEM_LIMIT),
        name="moe_combine",
    )(pos, ys, h, g2, row(ln_g), row(ln_b))


N_HEADS = 4
STACK = N_HEADS * CHUNK
CHUNK_LOG2 = CHUNK.bit_length() - 1


def _stack_heads(x):
    return jnp.concatenate([x[:, h * HEAD_DIM:(h + 1) * HEAD_DIM] for h in range(N_HEADS)], axis=0)


def _stack_cols(cols, width):
    return jnp.concatenate([jnp.broadcast_to(c, (CHUNK, width)) for c in cols], axis=0)


def _mxu(a, b, dims, hp):
    dg = lambda x, y: lax.dot_general(x, y, (dims, ((), ())), preferred_element_type=F32)
    a_hi, b_hi = a.astype(BF16), b.astype(BF16)
    if not hp:
        return dg(a_hi, b_hi)
    a_lo = (a - a_hi.astype(F32)).astype(BF16)
    b_lo = (b - b_hi.astype(F32)).astype(BF16)
    return dg(a_hi, b_hi) + (dg(a_lo, b_hi) + dg(a_hi, b_lo))


def _dot_nt(a, b, hp=False):
    return _mxu(a, b, ((1,), (1,)), hp)


def _dot_tn(a, b, hp=False):
    return _mxu(a, b, ((0,), (0,)), hp)


def _dot(a, b, hp=False):
    return _mxu(a, b, ((1,), (0,)), hp)


def _chunk_masks(rev):
    ri = lax.broadcasted_iota(jnp.int32, (STACK, STACK), 0)
    ci = lax.broadcasted_iota(jnp.int32, (STACK, STACK), 1)
    same = (ri >> CHUNK_LOG2) == (ci >> CHUNK_LOG2)
    ahead = jnp.where(rev, ci - ri, ri - ci)
    return same & (ahead >= 0), same & (ahead > 0)


def _scan_cumsum(gt, rev):
    ii = lax.broadcasted_iota(jnp.int32, (CHUNK, CHUNK), 0)
    jj = lax.broadcasted_iota(jnp.int32, (CHUNK, CHUNK), 1)
    tri = jnp.where(jnp.where(rev, jj - ii, ii - jj) >= 0, 1.0, 0.0)
    return jnp.dot(tri, gt, precision=lax.Precision.HIGHEST, preferred_element_type=F32)


def _unit_triangular_inverse(a):
    ri = lax.broadcasted_iota(jnp.int32, a.shape, 0)
    ci = lax.broadcasted_iota(jnp.int32, a.shape, 1)
    joins = lambda lvl: ((ri >> (lvl + 1)) == (ci >> (lvl + 1))) & ((ri >> lvl) != (ci >> lvl))
    t = jnp.where(ri == ci, 1.0, 0.0) - jnp.where(joins(0), a, 0.0)
    for lvl in range(1, CHUNK_LOG2):
        m = _dot(jnp.where(joins(lvl), a, 0.0), t)
        yield
        t = t - _dot(t, m)
        yield
    return t


def _interleave(chains):
    for _ in zip(*chains):
        pass


def _linear_scan_kernel(*refs, has_beta, k_scale):
    if has_beta:
        q_ref, k_ref, v_ref, gt_ref, gp_ref, s0_ref, o_ref, sfin_ref, s_scr = refs
    else:
        q_ref, k_ref, v_ref, gp_ref, s0_ref, o_ref, sfin_ref, s_scr = refs
    rev = pl.program_id(0) == 1
    n = pl.program_id(2)

    @pl.when(n == 0)
    def _():
        s_scr[...] = s0_ref[...]

    chains = []
    for i in range(q_ref.shape[0]):
        if has_beta:
            gt = _gdn_gates(gt_ref[i], gp_ref[...])
        else:
            gt = jnp.broadcast_to(gp_ref[0:1, :], (CHUNK, LANE))
        chains.append(_linear_chunk(q_ref.at[i], k_ref.at[i], v_ref.at[i], gt, o_ref.at[i], s_scr.at[i], rev,
                                    has_beta, k_scale))
    _interleave(chains)

    @pl.when(n == pl.num_programs(2) - 1)
    def _():
        sfin_ref[...] = s_scr[...]


def _softplus(x):
    return jnp.maximum(x, 0.0) + jnp.log1p(jnp.exp(-jnp.abs(x)))


def _gdn_gates(raw, gp):
    lane = lax.broadcasted_iota(jnp.int32, raw.shape, 1)
    return jnp.where(lane < 2 * N_HEADS, gp[0:1, :] * _softplus(raw + gp[1:2, :]), jax.nn.sigmoid(raw))


def _mlstm_gates(raw, gp):
    lane = lax.broadcasted_iota(jnp.int32, raw.shape, 1)
    x = raw + gp[0:1, :]
    return jnp.where(lane < 2 * N_HEADS, x, -_softplus(-x))


def _linear_chunk(q_ref, k_ref, v_ref, gt, o_ref, s_scr, rev, has_beta, k_scale):
    cum = _scan_cumsum(gt, rev)
    gcols, bcols, gtots = [], [], []
    for h in range(N_HEADS):
        gc = jnp.where(rev, cum[:, N_HEADS + h:N_HEADS + h + 1], cum[:, h:h + 1])
        gcols.append(gc)
        gtots.append(jnp.where(rev, gc[0:1], gc[CHUNK - 1:CHUNK]))
        bcols.append(jnp.where(rev, gt[:, 3 * N_HEADS + h:3 * N_HEADS + h + 1], gt[:, 2 * N_HEADS + h:2 * N_HEADS + h + 1]))
    cb = _stack_cols(gcols, STACK)
    diff = cb - cb.T
    incl, strict = _chunk_masks(rev)
    dec = jnp.exp(jnp.where(incl, diff, 0.0))
    gcb = cb[:, :HEAD_DIM]
    q_st, k_st, v_st = _stack_heads(q_ref[...]), _stack_heads(k_ref[...]) * k_scale, _stack_heads(v_ref[...])
    a_qk = _dot_nt(q_st, k_st) * jnp.where(incl, dec, 0.0)
    if has_beta:
        beta = _stack_cols(bcols, HEAD_DIM)
        kb = k_st * beta
        a = _dot_nt(kb, k_st) * jnp.where(strict, dec, 0.0)
        yield
        t_inv = yield from _unit_triangular_inverse(a)
        x = _dot(t_inv, jnp.concatenate([v_st * beta, kb * jnp.exp(gcb)], axis=1))
        u_st, w_st = x[:, :HEAD_DIM], x[:, HEAD_DIM:]
    else:
        u_st, w_st = v_st, None
    gtot = _stack_cols(gtots, HEAD_DIM)
    k_end = k_st * jnp.exp(gtot - gcb)
    q_dec = q_st * jnp.exp(gcb)
    hs = lambda t, h: t[h * CHUNK:(h + 1) * CHUNK]
    states = [s_scr[h] for h in range(N_HEADS)]
    q_s = [_dot(hs(q_dec, h), states[h]) for h in range(N_HEADS)]
    yield
    if has_beta:
        vn = jnp.concatenate([hs(u_st, h) - _dot(hs(w_st, h), states[h]) for h in range(N_HEADS)], axis=0)
    else:
        vn = u_st
    yield
    o_st = _dot(a_qk, vn)
    for h in range(N_HEADS):
        s_scr[h] = jnp.exp(gtots[h]) * states[h] + _dot_tn(hs(k_end, h), hs(vn, h))
    yield
    for h in range(N_HEADS):
        o_ref[:, h * HEAD_DIM:(h + 1) * HEAD_DIM] = hs(o_st, h) + q_s[h]
    yield


SCAN_NS = 4


HEADS_W = N_HEADS * HEAD_DIM


def _scan_specs(srcs, gates, nb, nchunk):
    ns = SCAN_NS
    assert nb % ns == 0
    cidx = lambda d, n: n + d * (nchunk - 1 - 2 * n)
    specs = [pl.BlockSpec((ns, CHUNK, HEADS_W), lambda d, b, n, c=c: (b, cidx(d, n), c)) for _, c in srcs]
    if gates is not None:
        specs.append(pl.BlockSpec((ns, CHUNK, LANE), lambda d, b, n, c=gates[1]: (b, cidx(d, n), c)))
    specs.append(pl.BlockSpec((2, LANE), lambda d, b, n: (0, 0)))
    ospec = pl.BlockSpec((None, ns, CHUNK, HEADS_W), lambda d, b, n: (d, b, cidx(d, n), 0))
    return specs, ospec


def linear_scan_bidir(q, k, v, gates, gp, s0, has_beta, k_scale=1.0):
    nb, t, _ = q[0].shape
    nchunk = t // CHUNK
    assert t % CHUNK == 0
    ns = SCAN_NS
    specs, ospec = _scan_specs((q, k, v), gates if has_beta else None, nb, nchunk)
    sspec = pl.BlockSpec((None, ns, N_HEADS, HEAD_DIM, HEAD_DIM), lambda d, b, n: (d, b, 0, 0, 0))
    args = [q[0], k[0], v[0]] + ([gates[0]] if has_beta else []) + [gp, s0]
    return pl.pallas_call(
        functools.partial(_linear_scan_kernel, has_beta=has_beta, k_scale=k_scale),
        grid=(2, nb // ns, nchunk),
        in_specs=specs + [sspec],
        out_specs=[ospec, sspec],
        out_shape=[jax.ShapeDtypeStruct((2, nb, t, HEADS_W), F32),
                   jax.ShapeDtypeStruct((2, nb, N_HEADS, HEAD_DIM, HEAD_DIM), F32)],
        scratch_shapes=[pltpu.VMEM((ns, N_HEADS, HEAD_DIM, HEAD_DIM), F32)],
        name="gdn_scan" if has_beta else "retention_scan",
    )(*args)


def linear_scan_two_pass(ctx_args, lat_args, gp, has_beta, k_scale=1.0):
    nb = ctx_args[0][0].shape[0]
    zero = jnp.zeros((2, nb, N_HEADS, HEAD_DIM, HEAD_DIM), F32)
    o_ctx, s_ctx = linear_scan_bidir(*ctx_args, gp, zero, has_beta, k_scale)
    o_lat, _ = linear_scan_bidir(*lat_args, gp, s_ctx, has_beta, k_scale)
    return o_ctx, o_lat


MLSTM_HP = True


def _mlstm_scan_kernel(q_ref, k_ref, v_ref, gt_ref, gp_ref, c0_ref, n0_ref, m0_ref, o_ref, cfin_ref, nfin_ref,
                       mfin_ref, c_scr, n_scr, m_scr):
    rev = pl.program_id(0) == 1
    step = pl.program_id(2)

    @pl.when(step == 0)
    def _():
        c_scr[...] = c0_ref[...]
        n_scr[...] = n0_ref[...]
        m_scr[...] = m0_ref[...]

    _interleave([_mlstm_chunk(q_ref.at[i], k_ref.at[i], v_ref.at[i], _mlstm_gates(gt_ref[i], gp_ref[...]),
                              o_ref.at[i], c_scr.at[i], n_scr.at[i], m_scr.at[i], rev)
                 for i in range(q_ref.shape[0])])

    @pl.when(step == pl.num_programs(2) - 1)
    def _():
        cfin_ref[...] = c_scr[...]
        nfin_ref[...] = n_scr[...]
        mfin_ref[...] = m_scr[...]


def _mlstm_chunk(q_ref, k_ref, v_ref, gt, o_ref, c_scr, n_scr, m_scr, rev):
    cum = _scan_cumsum(gt, rev)
    pick = lambda t, c: jnp.where(rev, t[:, N_HEADS + c:N_HEADS + c + 1], t[:, c:c + 1])
    q_st, k_st, v_st = _stack_heads(q_ref[...]), _stack_heads(k_ref[...]), _stack_heads(v_ref[...])
    hs = lambda t, h: t[h * CHUNK:(h + 1) * CHUNK]
    qk = _dot_nt(q_st, k_st, MLSTM_HP)
    yield
    bcums, srcs, inters, qcs, qns = [], [], [], [], []
    for h in range(N_HEADS):
        ic = pick(gt, h)
        bcum = pick(cum, 2 * N_HEADS + h)
        b_end = jnp.where(rev, bcum[0:1], bcum[CHUNK - 1:CHUNK])
        c_prev, n_prev, m_prev = c_scr[h], n_scr[h], m_scr[h][:, :1]
        a = b_end - bcum + ic
        m_new = jnp.maximum(b_end + m_prev, jnp.max(a, axis=0, keepdims=True))
        w_state = jnp.exp(a - m_new)
        decay = jnp.exp(b_end + m_prev - m_new)
        kw = hs(k_st, h) * w_state
        c_scr[h] = decay * c_prev + _dot_tn(kw, hs(v_st, h), MLSTM_HP)
        n_scr[h] = decay * n_prev + jnp.sum(kw, axis=0, keepdims=True)
        m_scr[h] = jnp.broadcast_to(m_new, (1, HEAD_DIM))
        bcums.append(bcum)
        srcs.append(bcum - ic)
        inters.append(bcum + m_prev)
        qcs.append(_dot(hs(q_st, h), c_prev, MLSTM_HP))
        qns.append(jnp.sum(hs(q_st, h) * n_prev, axis=1, keepdims=True))
    yield
    incl, _ = _chunk_masks(rev)
    dlog = _stack_cols(bcums, STACK) - _stack_cols(srcs, STACK).T
    inter = jnp.concatenate(inters, axis=0)
    m_t = jnp.maximum(inter, jnp.max(jnp.where(incl, dlog, MASKED), axis=1, keepdims=True))
    s = qk * jnp.where(incl, jnp.exp(jnp.where(incl, dlog, 0.0) - m_t), 0.0)
    w_inter = jnp.exp(inter - m_t)
    yield
    num = _dot(s, v_st, MLSTM_HP) + w_inter * jnp.concatenate(qcs, axis=0)
    den = jnp.sum(s, axis=1, keepdims=True) + w_inter * jnp.concatenate(qns, axis=0)
    yield
    out = num / jnp.maximum(jnp.abs(den), jnp.exp(-m_t))
    for h in range(N_HEADS):
        o_ref[:, h * HEAD_DIM:(h + 1) * HEAD_DIM] = hs(out, h)
    yield


def mlstm_scan_bidir(q, k, v, gates, gp, state):
    nb, t, _ = q[0].shape
    nchunk = t // CHUNK
    assert t % CHUNK == 0
    ns = SCAN_NS
    specs, ospec = _scan_specs((q, k, v), gates, nb, nchunk)
    cspec = pl.BlockSpec((None, ns, N_HEADS, HEAD_DIM, HEAD_DIM), lambda d, b, n: (d, b, 0, 0, 0))
    vspec = pl.BlockSpec((None, ns, N_HEADS, 1, HEAD_DIM), lambda d, b, n: (d, b, 0, 0, 0))
    cshape = jax.ShapeDtypeStruct((2, nb, N_HEADS, HEAD_DIM, HEAD_DIM), F32)
    vshape = jax.ShapeDtypeStruct((2, nb, N_HEADS, 1, HEAD_DIM), F32)
    o, c, n, m = pl.pallas_call(
        _mlstm_scan_kernel,
        grid=(2, nb // ns, nchunk),
        in_specs=specs + [cspec, vspec, vspec],
        out_specs=[ospec, cspec, vspec, vspec],
        out_shape=[jax.ShapeDtypeStruct((2, nb, t, HEADS_W), F32), cshape, vshape, vshape],
        scratch_shapes=[pltpu.VMEM((ns, N_HEADS, HEAD_DIM, HEAD_DIM), F32), pltpu.VMEM((ns, N_HEADS, 1, HEAD_DIM), F32),
                        pltpu.VMEM((ns, N_HEADS, 1, HEAD_DIM), F32)],
        name="mlstm_scan",
    )(q[0], k[0], v[0], gates[0], gp, *state)
    return o, (c, n, m)


def mlstm_two_pass(ctx_args, lat_args, gp):
    nb = ctx_args[0][0].shape[0]
    zero = (jnp.zeros((2, nb, N_HEADS, HEAD_DIM, HEAD_DIM), F32), jnp.zeros((2, nb, N_HEADS, 1, HEAD_DIM), F32),
            jnp.zeros((2, nb, N_HEADS, 1, HEAD_DIM), F32))
    o_ctx, s_ctx = mlstm_scan_bidir(*ctx_args, gp, zero)
    o_lat, _ = mlstm_scan_bidir(*lat_args, gp, s_ctx)
    return o_ctx, o_lat


PROJ_TM = 512
PROJ_TN_MAX = 2304


def _inproj_kernel(h_ref, sc_ref, sh_ref, w_ref, o_ref):
    hm = (h_ref[...] * (1.0 + sc_ref[...]) + sh_ref[...]).astype(BF16)
    o_ref[...] = jnp.dot(hm, w_ref[...], preferred_element_type=F32)


def in_projection(h, sc, sh, w):
    nb, n, dm = h.shape
    ncol = w.shape[1]
    tm = min(PROJ_TM, n)
    tn = max(c for c in range(LANE, min(ncol, PROJ_TN_MAX) + 1, LANE) if ncol % c == 0)
    return pl.pallas_call(
        _inproj_kernel,
        grid=(nb, n // tm, ncol // tn),
        in_specs=[pl.BlockSpec((None, tm, dm), lambda b, t, j: (b, t, 0)),
                  pl.BlockSpec((None, 1, dm), lambda b, t, j: (b, 0, 0)),
                  pl.BlockSpec((None, 1, dm), lambda b, t, j: (b, 0, 0)),
                  pl.BlockSpec((dm, tn), lambda b, t, j: (0, j))],
        out_specs=pl.BlockSpec((None, tm, tn), lambda b, t, j: (b, t, j)),
        out_shape=jax.ShapeDtypeStruct((nb, n, ncol), F32),
        name="in_projection",
    )(h, sc, sh, w)


def _conv_kernel(x_ref, w_ref, o_ref, *, grid_w, l2_blocks, q_scale, k_scale):
    x = x_ref[...]
    t = x.shape[0]
    w = w_ref[...]
    col = lax.broadcasted_iota(jnp.int32, x.shape, 0) & (grid_w - 1)
    left = jnp.where(col == 0, 0.0, pltpu.roll(x, 1, 0))
    right = jnp.where(col == grid_w - 1, 0.0, pltpu.roll(x, t - 1, 0))
    row = lambda kh: w[3 * kh:3 * kh + 1] * left + w[3 * kh + 1:3 * kh + 2] * x + w[3 * kh + 2:3 * kh + 3] * right
    acc = row(1)
    if grid_w < t:
        zero = jnp.zeros((grid_w, LANE), F32)
        acc = acc + jnp.concatenate([zero, row(0)[:t - grid_w]], axis=0) + jnp.concatenate([row(2)[grid_w:], zero], axis=0)
    y = acc * jax.nn.sigmoid(acc)
    c = pl.program_id(1)
    normed = y * lax.rsqrt(jnp.sum(y * y, axis=1, keepdims=True) + L2_EPS)
    y = jnp.where(c < l2_blocks, normed, y)
    o_ref[...] = y * jnp.where(c < N_HEADS, q_scale, jnp.where(c < 2 * N_HEADS, k_scale, 1.0))


def conv_prep(proj, conv_w, nblk, grid_w, l2_blocks, q_scale, k_scale):
    nb, t, _ = proj.shape
    assert grid_w & (grid_w - 1) == 0 and t % grid_w == 0
    return pl.pallas_call(
        functools.partial(_conv_kernel, grid_w=grid_w, l2_blocks=l2_blocks, q_scale=q_scale, k_scale=k_scale),
        grid=(nb, nblk),
        in_specs=[pl.BlockSpec((None, t, LANE), lambda b, c: (b, 0, c)),
                  pl.BlockSpec((CONV_K * CONV_K, LANE), lambda b, c: (0, c))],
        out_specs=pl.BlockSpec((None, t, LANE), lambda b, c: (b, 0, c)),
        out_shape=jax.ShapeDtypeStruct((nb, t, nblk * LANE), F32),
        compiler_params=pltpu.CompilerParams(vmem_limit_bytes=VMEM_LIMIT),
        name="conv_prep",
    )(proj, conv_w.reshape(CONV_K * CONV_K, -1))


def _head_norm(o, center):
    outs = []
    for h in range(N_HEADS):
        x = o[:, h * HEAD_DIM:(h + 1) * HEAD_DIM]
        if center:
            x = x - jnp.mean(x, axis=1, keepdims=True)
        outs.append(x * lax.rsqrt(jnp.mean(x * x, axis=1, keepdims=True) + EPS))
    return jnp.concatenate(outs, axis=1)


def _mix_out(y_a, y_b, wo_ref, h_ref, g1_ref, lng_ref, lnb_ref, o_ref):
    y = (jnp.dot(y_a.astype(BF16), wo_ref[:HEADS_W, :], preferred_element_type=F32)
         + jnp.dot(y_b.astype(BF16), wo_ref[HEADS_W:, :], preferred_element_type=F32))
    o_ref[...] = _ln_rows(ALPHA * h_ref[...] + g1_ref[...] * y, lng_ref[...], lnb_ref[...])


def _merge_even_kernel(og_ref, or_ref, za_ref, zr_ref, gg_ref, rg_ref, wo_ref, h_ref, g1_ref, lng_ref, lnb_ref, o_ref):
    za, zr = za_ref[...], zr_ref[...]
    y_g = _head_norm(og_ref[0] + og_ref[1], False) * gg_ref[...] * (za * jax.nn.sigmoid(za))
    y_r = _head_norm(or_ref[0] + or_ref[1], True) * rg_ref[...] * (zr * jax.nn.sigmoid(zr))
    _mix_out(y_g, y_r, wo_ref, h_ref, g1_ref, lng_ref, lnb_ref, o_ref)


def _merge_odd_kernel(om_ref, ys_ref, og_ref, u_ref, mg_ref, dsk_ref, bglu_ref, wglu_ref, wo_ref, h_ref, g1_ref,
                      lng_ref, lnb_ref, o_ref):
    y_m = _head_norm(om_ref[0] + om_ref[1], True) * mg_ref[...] * jax.nn.sigmoid(og_ref[...])
    y = jax.nn.gelu(ys_ref[...] + dsk_ref[...] * u_ref[...])
    y = y * jax.nn.sigmoid(jnp.dot(y.astype(BF16), wglu_ref[...], preferred_element_type=F32) + bglu_ref[...])
    _mix_out(y_m, y, wo_ref, h_ref, g1_ref, lng_ref, lnb_ref, o_ref)


def _merge_call(kernel_fn, name, scans, toks, rows, mats, h, g1, ln_g, ln_b):
    nb, n, dm = h.shape
    tm = min(PROJ_TM, n)
    full = lambda a: pl.BlockSpec(a.shape, lambda b, t: (0,) * a.ndim)
    rowv = lambda v: v.reshape(1, -1)
    specs, args = [], []
    for a in scans:
        specs.append(pl.BlockSpec((2, None, tm, HEADS_W), lambda b, t: (0, b, t, 0)))
        args.append(a)
    for a, c in toks:
        specs.append(pl.BlockSpec((None, tm, HEADS_W), lambda b, t, c=c: (b, t, c)))
        args.append(a)
    for v in rows:
        args.append(rowv(v))
        specs.append(full(args[-1]))
    for m in mats:
        args.append(m)
        specs.append(full(m))
    args += [h, g1, rowv(ln_g), rowv(ln_b)]
    specs += [pl.BlockSpec((None, tm, dm), lambda b, t: (b, t, 0)), pl.BlockSpec((None, 1, dm), lambda b, t: (b, 0, 0)),
              full(args[-2]), full(args[-1])]
    return pl.pallas_call(
        kernel_fn,
        grid=(nb, n // tm),
        in_specs=specs,
        out_specs=pl.BlockSpec((None, tm, dm), lambda b, t: (b, t, 0)),
        out_shape=jax.ShapeDtypeStruct((nb, n, dm), F32),
        compiler_params=pltpu.CompilerParams(vmem_limit_bytes=VMEM_LIMIT),
        name=name,
    )(*args)


def retention_log_decay(direction):
    expo = 5.0 + 2.0 * jnp.arange(RET_HEADS, dtype=F32) + direction
    return jnp.log1p(-jnp.exp2(-expo))


GATE_COLS = 4 * N_HEADS
Q_SCALE = HEAD_DIM ** -0.5


def _gate_row(*vals):
    v = jnp.concatenate([jnp.ravel(x) for x in vals])
    return jnp.pad(v, (0, LANE - v.shape[0]))


def _pad_gate_cols(w):
    return jnp.pad(w, ((0, 0), (0, LANE - w.shape[1])))


def gdn_retention_mixer(hs_, mods, w_in, w_out, conv_w, a_log, dt_bias, gdn_gain, ret_gain, ln_g, ln_b, ctx_out):
    w = jnp.concatenate([w_in[:, :4 * GDN_W], w_in[:, 4 * GDN_W + GATE_COLS:],
                         _pad_gate_cols(w_in[:, 4 * GDN_W:4 * GDN_W + GATE_COLS])], axis=1).astype(BF16)
    gate_blk = (4 * GDN_W + 4 * RET_W) // LANE
    gp_gdn = jnp.stack([_gate_row(-jnp.exp(a_log)), _gate_row(dt_bias)])
    gp_ret = jnp.stack([_gate_row(retention_log_decay(0), retention_log_decay(1)), jnp.zeros((LANE,), F32)])
    wo = w_out.astype(BF16)
    projs, convs = [], []
    for (h, (sc, sh, _)), grid_w in zip(zip(hs_, mods), (hs_[0].shape[1], GRID_W)):
        p = in_projection(h, sc, sh, w)
        projs.append(p)
        convs.append(conv_prep(p, conv_w, 3 * GDN_W // LANE, grid_w, 2 * N_HEADS, Q_SCALE, 1.0))
    gdn_args = [((cv, 0), (cv, 1), (cv, 2), (p, gate_blk)) for p, cv in zip(projs, convs)]
    ret_args = [((p, 4), (p, 5), (p, 6), None) for p in projs]
    og = linear_scan_two_pass(gdn_args[0], gdn_args[1], gp_gdn, True)
    orr = linear_scan_two_pass(ret_args[0], ret_args[1], gp_ret, False, Q_SCALE)
    outs = []
    for i in range(2):
        if i == 0 and not ctx_out:
            outs.append(None)
            continue
        outs.append(_merge_call(_merge_even_kernel, "merge_even", [og[i], orr[i]], [(projs[i], 3), (projs[i], 7)],
                                [jnp.tile(gdn_gain, N_HEADS), ret_gain], [wo], hs_[i], mods[i][2], ln_g, ln_b))
    return outs


def mlstm_s5_mixer(hs_, mods, w_in, w_out, conv_w, gate_bias, mlstm_gain, lam_re, lam_im, log_dt,
                   b_re, b_im, c_re, c_im, d_skip, w_glu, b_glu, ln_g, ln_b, ctx_out):
    w = jnp.concatenate([w_in[:, :4 * MLSTM_W], _pad_gate_cols(w_in[:, 4 * MLSTM_W:4 * MLSTM_W + GATE_COLS])],
                        axis=1).astype(BF16)
    w_u = w_in[:, 4 * MLSTM_W + GATE_COLS:].astype(BF16)
    gate_blk = 4 * MLSTM_W // LANE
    gp = jnp.stack([_gate_row(gate_bias[0, 0], gate_bias[1, 0], gate_bias[0, 1], gate_bias[1, 1]),
                    jnp.zeros((LANE,), F32)])
    wo, wglu = w_out.astype(BF16), w_glu.astype(BF16)
    projs, us, args = [], [], []
    for (h, (sc, sh, _)), grid_w in zip(zip(hs_, mods), (hs_[0].shape[1], GRID_W)):
        p = in_projection(h, sc, sh, w)
        cv = conv_prep(p, conv_w, 2 * MLSTM_W // LANE, grid_w, 0, 1.0, Q_SCALE)
        projs.append(p)
        us.append(in_projection(h, sc, sh, w_u))
        args.append(((cv, 0), (cv, 1), (p, 2), (p, gate_blk)))
    om = mlstm_two_pass(args[0], args[1], gp)
    ys = s5_bidirectional(us[0], us[1], _s5_weights(lam_re, lam_im, log_dt, b_re, b_im, c_re, c_im))
    outs = []
    for i in range(2):
        if i == 0 and not ctx_out:
            outs.append(None)
            continue
        outs.append(_merge_call(_merge_odd_kernel, "merge_odd", [om[i]], [(ys[i], 0), (projs[i], 3), (us[i], 0)],
                                [mlstm_gain, d_skip, b_glu], [wglu, wo], hs_[i], mods[i][2], ln_g, ln_b))
    return outs


def kernel(x, c, ctx, c_ctx, w_mod, b_mod, ln1_g, ln1_b, ln2_g, ln2_b, w_router, w_gate, w_up, w_down,
           ev_w_in, ev_w_out, ev_conv, ev_a_log, ev_dt_bias, ev_gdn_norm, ev_ret_norm,
           od_w_in, od_w_out, od_conv, od_gate_bias, od_mlstm_norm, od_lam_re, od_lam_im, od_log_dt,
           od_b_re, od_b_im, od_c_re, od_c_im, od_d_skip, od_w_glu, od_b_glu):
    h_lat, h_ctx = x, ctx
    s_lat = jax.nn.silu(c)
    s_ctx = jax.nn.silu(c_ctx)
    experts = (w_gate.astype(BF16), w_up.astype(BF16), w_down.astype(BF16))
    for l in range(DEPTH):
        last = l == DEPTH - 1
        sh1, sc1, g1, sh2, sc2, g2 = jnp.split((s_lat @ w_mod[l] + b_mod[l])[:, None, :], 6, axis=-1)
        bc = lambda v: jnp.broadcast_to(v, (h_ctx.shape[0], 1, D_MODEL))
        csh1, csc1, cg1, csh2, csc2, cg2 = [bc(v) for v in jnp.split(s_ctx @ w_mod[l] + b_mod[l], 6, axis=-1)]
        streams = (h_ctx, h_lat)
        mods = ((csc1, csh1, cg1), (sc1, sh1, g1))
        if l % 2 == 0:
            e = l // 2
            h_ctx, h_lat = gdn_retention_mixer(streams, mods, ev_w_in[e], ev_w_out[e], ev_conv[e], ev_a_log[e],
                                               ev_dt_bias[e], ev_gdn_norm[e], ev_ret_norm[e], ln1_g[l], ln1_b[l],
                                               not last)
        else:
            o = l // 2
            h_ctx, h_lat = mlstm_s5_mixer(streams, mods, od_w_in[o], od_w_out[o], od_conv[o], od_gate_bias[o],
                                          od_mlstm_norm[o], od_lam_re[o], od_lam_im[o], od_log_dt[o],
                                          od_b_re[o], od_b_im[o], od_c_re[o], od_c_im[o], od_d_skip[o],
                                          od_w_glu[o], od_b_glu[o], ln1_g[l], ln1_b[l], not last)
        h_lat = moe_block(h_lat, sc2, sh2, g2, ln2_g[l], ln2_b[l], w_router[l], *experts, l)
        if not last:
            h_ctx = moe_block(h_ctx, csc2, csh2, cg2, ln2_g[l], ln2_b[l], w_router[l], *experts, l)
    return h_lat
```

```python
import functools

import jax
import jax.numpy as jnp
from jax import lax
from jax.experimental import pallas as pl
from jax.experimental.pallas import tpu as pltpu

D_MODEL = 1024
DEPTH = 4
GRID_W = 64
CHUNK = 64
CONV_K = 3
HEAD_DIM = D_MODEL // 8
GDN_HEADS = 4
RET_HEADS = 4
MLSTM_HEADS = 4
GDN_W = GDN_HEADS * HEAD_DIM
RET_W = RET_HEADS * HEAD_DIM
MLSTM_W = MLSTM_HEADS * HEAD_DIM
S5_CH = D_MODEL // 2
S5_GROUP = 16
S5_GROUPS = S5_CH // S5_GROUP
S5_STATE = 64
N_EXPERTS = 16
EXPERT_FF = 2 * D_MODEL
CAPACITY_FACTOR = 2
ALPHA = (2 * DEPTH) ** 0.25
EPS = 1e-5
L2_EPS = 1e-6
MASKED = -1e30
F32 = jnp.float32
BF16 = jnp.bfloat16

LANE = 128


S5_L = 16
S5_NB = S5_CH // LANE
S5_GPB = LANE // S5_GROUP
S5_SW = S5_GPB * S5_STATE
VMEM_LIMIT = 56 * 1024 * 1024


def _s5_expand_kernel(c_ref, o_ref, *, row_item_log2, col_item_log2):
    c = c_ref[...]
    nk, nc = c.shape[1], o_ref.shape[1]
    gmask = S5_GPB - 1
    k = lax.broadcasted_iota(jnp.int32, (nk, nc), 0)
    col = lax.broadcasted_iota(jnp.int32, (nk, nc), 1)
    src = ((col >> (col_item_log2 + S5_GPB.bit_length() - 1)) << col_item_log2) | (col & ((1 << col_item_log2) - 1))
    rep = jnp.where(k == src, 1.0, 0.0).astype(BF16)
    wide = jnp.dot(c, rep, preferred_element_type=F32)
    r = lax.broadcasted_iota(jnp.int32, wide.shape, 0)
    cc = lax.broadcasted_iota(jnp.int32, wide.shape, 1)
    same = ((r >> row_item_log2) & gmask) == ((cc >> col_item_log2) & gmask)
    o_ref[...] = jnp.where(same, wide, 0.0).astype(BF16)


def _s5_expand(c, row_item_log2, col_item_log2):
    nbk, rows, nk = c.shape
    return pl.pallas_call(
        functools.partial(_s5_expand_kernel, row_item_log2=row_item_log2, col_item_log2=col_item_log2),
        grid=(nbk,),
        in_specs=[pl.BlockSpec((None, rows, nk), lambda j: (j, 0, 0))],
        out_specs=pl.BlockSpec((None, rows, nk * S5_GPB), lambda j: (j, 0, 0)),
        out_shape=jax.ShapeDtypeStruct((nbk, rows, nk * S5_GPB), BF16),
        compiler_params=pltpu.CompilerParams(vmem_limit_bytes=VMEM_LIMIT),
        name="s5_expand",
    )(c)


def _s5_weights(lam_re, lam_im, log_dt, b_re, b_im, c_re, c_im):
    L, G = S5_L, S5_GROUPS
    hp = lax.Precision.HIGHEST
    taus = jnp.arange(L + 1, dtype=F32)[:, None, None]
    ks, ws, cas, ds = [], [], [], []
    for d in range(2):
        lr = jnp.minimum(lam_re[d], -1e-4)
        li = lam_im[d]
        dt = jnp.exp(log_dt[d])[:, None]
        mag = jnp.exp(lr * dt)
        ab_re, ab_im = mag * jnp.cos(li * dt), mag * jnp.sin(li * dt)
        xr, xi, den = ab_re - 1.0, ab_im, lr * lr + li * li
        f_re = (xr * lr + xi * li) / den
        f_im = (xi * lr - xr * li) / den
        bb_re = f_re[..., None] * b_re - f_im[..., None] * b_im
        bb_im = f_re[..., None] * b_im + f_im[..., None] * b_re
        pmag = jnp.exp(taus * (lr * dt))
        ar, ai = pmag * jnp.cos(taus * (li * dt)), pmag * jnp.sin(taus * (li * dt))
        wr = ar[..., None] * bb_re - ai[..., None] * bb_im
        wi = ar[..., None] * bb_im + ai[..., None] * bb_re
        k = (jnp.einsum('gop,tgpi->tgio', c_re, wr, precision=hp)
             - jnp.einsum('gop,tgpi->tgio', c_im, wi, precision=hp))
        car = c_re[None] * ar[:, :, None, :] - c_im[None] * ai[:, :, None, :]
        cai = c_re[None] * ai[:, :, None, :] + c_im[None] * ar[:, :, None, :]
        ks.append(k)
        ws.append((wr, wi))
        cas.append((car, cai))
        ds.append((ar[L], ai[L]))

    split_g = lambda t, g_axis: t.reshape(t.shape[:g_axis] + (S5_NB, S5_GPB) + t.shape[g_axis + 1:]).astype(BF16)
    compact = lambda t: t.reshape(S5_NB, L * LANE, -1)
    h_log2, p_log2 = S5_GROUP.bit_length() - 1, S5_STATE.bit_length() - 1

    kf, kb = ks
    kc = jnp.concatenate([kb[1:L][::-1], (kf[0] + kb[0])[None], kf[1:L]], axis=0)
    idx = (jnp.arange(L)[None, :] - jnp.arange(L)[:, None]) + (L - 1)
    tz = _s5_expand(compact(split_g(kc[idx], 2).transpose(2, 0, 3, 4, 1, 5)), h_log2, h_log2)

    pb_c, ca_c = [], []
    for d in range(2):
        wr, wi = ws[d]
        order = jnp.arange(L - 1, -1, -1) if d == 0 else jnp.arange(L)
        pb_c.append(jnp.stack([jnp.swapaxes(w[order], -1, -2) for w in (wr, wi)]))
        car, cai = cas[d]
        order = jnp.arange(1, L + 1) if d == 0 else jnp.arange(L, 0, -1)
        ca_c.append(jnp.stack([jnp.swapaxes(m, -1, -2) for m in (car[order], -cai[order])]))
    pb = _s5_expand(compact(split_g(jnp.stack(pb_c), 3).transpose(3, 2, 4, 5, 0, 1, 6)), h_log2, p_log2)
    ca = _s5_expand(compact(split_g(jnp.stack(ca_c), 3).transpose(3, 0, 1, 4, 5, 2, 6)), p_log2, h_log2)
    dr = jnp.stack([ds[0][0], ds[1][0]], 0).reshape(2, S5_NB, 1, S5_SW).transpose(1, 0, 2, 3).reshape(2 * S5_NB, 1, S5_SW)
    di = jnp.stack([ds[0][1], ds[1][1]], 0).reshape(2, S5_NB, 1, S5_SW).transpose(1, 0, 2, 3).reshape(2 * S5_NB, 1, S5_SW)
    return tz, pb, ca, dr, di


def _s5_p_kernel(u_ref, pb_ref, p_ref, *, nb, nc):
    res = jnp.dot(u_ref[...], pb_ref[...], preferred_element_type=F32)
    for b in range(nb):
        p_ref[:, b * 2 * S5_SW:(b + 1) * 2 * S5_SW] = res[b * nc:(b + 1) * nc]


def _s5_scan_kernel(p_ref, dr_ref, di_ref, s_ref, *, n_ctx, n_lat):
    rev = pl.program_id(0) % 2
    dr = dr_ref[...]
    di = di_ref[...]
    nbatch = p_ref.shape[1]

    def phase(base, n, carry):
        def body(step, carry):
            sr, si = carry
            row = base + jnp.where(rev == 0, step, n - 1 - step)
            s_ref[row, :, :S5_SW] = sr
            s_ref[row, :, S5_SW:] = si
            p = p_ref[row]
            nr = dr * sr - di * si + p[:, :S5_SW]
            ni = dr * si + di * sr + p[:, S5_SW:]
            return nr, ni
        return lax.fori_loop(0, n, body, carry)

    zero = jnp.zeros((nbatch, S5_SW), F32)
    carry = phase(0, n_ctx, (zero, zero))
    phase(n_ctx, n_lat, carry)


def _s5_y_kernel(u_ref, tz_ref, sf_ref, sb_ref, ca_ref, y_ref):
    y = jnp.dot(u_ref[...], tz_ref[...], preferred_element_type=F32)
    y += jnp.dot(sf_ref[...].astype(BF16), ca_ref[:2 * S5_SW, :], preferred_element_type=F32)
    y += jnp.dot(sb_ref[...].astype(BF16), ca_ref[2 * S5_SW:, :], preferred_element_type=F32)
    y_ref[...] = y


def s5_bidirectional(u_ctx, u_lat, weights):
    tz, pb, ca, dr, di = weights
    L = S5_L
    nb, t_ctx, _ = u_ctx.shape
    t_lat = u_lat.shape[1]
    assert t_ctx % L == 0 and t_lat % L == 0
    n_ctx, n_lat = t_ctx // L, t_lat // L
    nc = n_ctx + n_lat
    kw = L * LANE
    sw2 = 2 * S5_SW
    u = jnp.concatenate([u_ctx, u_lat], axis=1)
    ub = u.reshape(nb * nc, L, S5_NB, LANE).transpose(2, 0, 1, 3).reshape(S5_NB, nb * nc, kw).astype(BF16)

    p = pl.pallas_call(
        functools.partial(_s5_p_kernel, nb=nb, nc=nc),
        grid=(S5_NB, 2),
        in_specs=[pl.BlockSpec((None, nb * nc, kw), lambda j, d: (j, 0, 0)),
                  pl.BlockSpec((None, kw, sw2), lambda j, d: (j, 0, d))],
        out_specs=pl.BlockSpec((nc, nb * sw2), lambda j, d: (0, j * 2 + d)),
        out_shape=jax.ShapeDtypeStruct((nc, S5_NB * 2 * nb * sw2), F32),
        compiler_params=pltpu.CompilerParams(vmem_limit_bytes=VMEM_LIMIT),
        name="s5_chunk_inputs",
    )(ub, pb)

    p4 = p.reshape(nc, S5_NB * 2, nb, sw2)
    s4 = pl.pallas_call(
        functools.partial(_s5_scan_kernel, n_ctx=n_ctx, n_lat=n_lat),
        grid=(S5_NB * 2,),
        in_specs=[pl.BlockSpec((nc, None, nb, sw2), lambda g: (0, g, 0, 0)),
                  pl.BlockSpec((None, 1, S5_SW), lambda g: (g, 0, 0)),
                  pl.BlockSpec((None, 1, S5_SW), lambda g: (g, 0, 0))],
        out_specs=pl.BlockSpec((nc, None, nb, sw2), lambda g: (0, g, 0, 0)),
        out_shape=jax.ShapeDtypeStruct(p4.shape, F32),
        compiler_params=pltpu.CompilerParams(vmem_limit_bytes=VMEM_LIMIT),
        name="s5_state_scan",
    )(p4, dr, di)

    s2 = s4.reshape(nc, S5_NB * 2 * nb * sw2)
    yb = pl.pallas_call(
        _s5_y_kernel,
        grid=(S5_NB, nb),
        in_specs=[pl.BlockSpec((None, nc, kw), lambda j, b: (j, b, 0)),
                  pl.BlockSpec((None, kw, kw), lambda j, b: (j, 0, 0)),
                  pl.BlockSpec((nc, sw2), lambda j, b: (0, (j * 2) * nb + b)),
                  pl.BlockSpec((nc, sw2), lambda j, b: (0, (j * 2 + 1) * nb + b)),
                  pl.BlockSpec((None, 2 * sw2, kw), lambda j, b: (j, 0, 0))],
        out_specs=pl.BlockSpec((None, nc, kw), lambda j, b: (j, b, 0)),
        out_shape=jax.ShapeDtypeStruct((S5_NB, nb * nc, kw), F32),
        compiler_params=pltpu.CompilerParams(vmem_limit_bytes=VMEM_LIMIT),
        name="s5_output",
    )(ub, tz, s2, s2, ca)
    y = yb.reshape(S5_NB, nb, nc, L, LANE).transpose(1, 2, 3, 0, 4).reshape(nb, nc * L, S5_CH)
    return y[:, :t_ctx], y[:, t_ctx:]


ROUTE_TN = 1024
MOE_TN = 2048
SEL_ROWS = 128
FF_SPLIT = 4
FFN_ROWS = 1024
AFF_BITS = 30


def _route_kernel(h_ref, sc_ref, sh_ref, wr_ref, hm_ref, afft_ref, asp_ref):
    hm = h_ref[...] * (1.0 + sc_ref[...]) + sh_ref[...]
    hm_ref[...] = hm.astype(BF16)
    logits = jnp.dot(hm, wr_ref[...], precision=lax.Precision.HIGHEST, preferred_element_type=F32)
    lane = lax.broadcasted_iota(jnp.int32, logits.shape, 1)
    logits = jnp.where(lane < N_EXPERTS, logits, -jnp.inf)
    ex = jnp.exp(logits - jnp.max(logits, axis=1, keepdims=True))
    aff = ex / jnp.sum(ex, axis=1, keepdims=True)
    afft_ref[...] = aff.T[:N_EXPERTS, :]
    hi = aff.astype(BF16).astype(F32)
    mid = (aff - hi).astype(BF16).astype(F32)
    lo = (aff - hi - mid).astype(BF16).astype(F32)
    asp = hi + pltpu.roll(mid, N_EXPERTS, 1) + pltpu.roll(lo, 2 * N_EXPERTS, 1)
    asp_ref[...] = asp.astype(BF16)


def _select_kernel(aff_ref, pos_ref, *, nblk_log2, cap):
    a = aff_ref[...]
    r = a.shape[0]
    bits = pltpu.bitcast(a, jnp.int32)
    ri = lax.broadcasted_iota(jnp.int32, (r, r), 0)
    rj = lax.broadcasted_iota(jnp.int32, (r, r), 1)
    same = (ri >> nblk_log2) == (rj >> nblk_log2)
    gm = jnp.where(same, 1.0, 0.0).astype(BF16)
    lm = jnp.where(same & (rj < ri), 1.0, 0.0).astype(BF16)
    li = lax.broadcasted_iota(jnp.int32, (LANE, LANE), 0)
    lj = lax.broadcasted_iota(jnp.int32, (LANE, LANE), 1)
    um = jnp.where(li <= lj, 1.0, 0.0).astype(BF16)

    def group_count(mask):
        rc = jnp.sum(jnp.where(mask, 1.0, 0.0), axis=1, keepdims=True)
        gc = jnp.dot(gm, jnp.broadcast_to(rc, (r, LANE)).astype(BF16), preferred_element_type=F32)
        return gc[:, :1]

    def bisect(i, thr):
        cand = thr | jnp.left_shift(1, AFF_BITS - 1 - i)
        return jnp.where(group_count(bits >= cand) >= cap, cand, thr)

    thr = lax.fori_loop(0, AFF_BITS, bisect, jnp.zeros((r, 1), jnp.int32))

    def prefix(mask):
        x = jnp.where(mask, 1.0, 0.0)
        inc = jnp.dot(x.astype(BF16), um, preferred_element_type=F32)
        tot = jnp.broadcast_to(inc[:, LANE - 1:LANE], (r, LANE)).astype(BF16)
        return inc - x + jnp.dot(lm, tot, preferred_element_type=F32)

    gt = bits > thr
    eq = bits == thr
    need = cap - group_count(gt)
    sel = gt | (eq & (prefix(eq) < need))
    pos_ref[...] = jnp.where(sel, prefix(sel), -1.0).astype(jnp.int32)


def _onehot(pos_ref, cap):
    rows = lax.broadcasted_iota(jnp.int32, (cap, LANE), 0)
    blocks = [jnp.where(pos_ref[k:k + 1, :] == rows, 1.0, 0.0).astype(BF16) for k in range(pos_ref.shape[0])]
    return jnp.concatenate(blocks, axis=1)


def _ffn_kernel(pos_ref, hb_ref, asp_ref, wg_ref, wu_ref, wd_ref, ys_ref, xs_acc, g_acc, *, cap, nkt):
    e = pl.program_id(0)
    s = pl.program_id(2)

    @pl.when(s == 0)
    def _():
        xs_acc[...] = jnp.zeros_like(xs_acc)
        g_acc[...] = jnp.zeros_like(g_acc)

    rows = pl.ds(pl.multiple_of((s // nkt) * cap, cap), cap)
    oh = _onehot(pos_ref, cap)
    xs_acc[rows, :] += jnp.dot(oh, hb_ref[...], preferred_element_type=F32)
    g_acc[rows, :] += jnp.dot(oh, asp_ref[...], preferred_element_type=F32)

    @pl.when(s == pl.num_programs(2) - 1)
    def _():
        xs = xs_acc[...].astype(BF16)
        g = g_acc[...]
        lane = lax.broadcasted_iota(jnp.int32, g.shape, 1)
        gate = jnp.sum(jnp.where((lane & (N_EXPERTS - 1)) == e, g, 0.0), axis=1, keepdims=True)
        fw = EXPERT_FF // FF_SPLIT
        y = jnp.zeros(xs.shape, F32)
        for f in range(FF_SPLIT):
            hg = jnp.dot(xs, wg_ref[:, f * fw:(f + 1) * fw], preferred_element_type=F32)
            hu = jnp.dot(xs, wu_ref[:, f * fw:(f + 1) * fw], preferred_element_type=F32)
            hid = (hg * jax.nn.sigmoid(hg)) * hu
            y += jnp.dot(hid.astype(BF16), wd_ref[f * fw:(f + 1) * fw, :], preferred_element_type=F32)
        y = (y * gate).astype(BF16)
        for i in range(ys_ref.shape[0]):
            ys_ref[i] = y[i * cap:(i + 1) * cap]


def _ln_rows(z, g, b):
    mu = jnp.mean(z, axis=-1, keepdims=True)
    zc = z - mu
    var = jnp.mean(zc * zc, axis=-1, keepdims=True)
    return zc * lax.rsqrt(var + EPS) * g + b


def _combine_kernel(pos_ref, ys_ref, h_ref, g2_ref, lng_ref, lnb_ref, o_ref, acc, *, cap):
    e = pl.program_id(2)

    @pl.when(e == 0)
    def _():
        acc[...] = jnp.zeros_like(acc)

    oh = _onehot(pos_ref, cap)
    acc[...] += lax.dot_general(oh, ys_ref[...], (((0,), (0,)), ((), ())), preferred_element_type=F32)

    @pl.when(e == pl.num_programs(2) - 1)
    def _():
        z = ALPHA * h_ref[...] + g2_ref[...] * acc[...]
        o_ref[...] = _ln_rows(z, lng_ref[...], lnb_ref[...])


def moe_block(h, sc, sh, g2, ln_g, ln_b, w_router, wg, wu, wd, layer):
    nb, n, dm = h.shape
    cap = CAPACITY_FACTOR * n // N_EXPERTS
    nblk = n // LANE
    assert n % LANE == 0 and nblk & (nblk - 1) == 0
    tn_r = min(ROUTE_TN, n)
    tn = min(MOE_TN, n)
    wr = jnp.pad(w_router, ((0, 0), (0, LANE - N_EXPERTS)))
    row = lambda v: v.reshape(1, dm)

    hm, afft, asp = pl.pallas_call(
        _route_kernel,
        grid=(nb, n // tn_r),
        in_specs=[pl.BlockSpec((None, tn_r, dm), lambda b, t: (b, t, 0)),
                  pl.BlockSpec((None, 1, dm), lambda b, t: (b, 0, 0)),
                  pl.BlockSpec((None, 1, dm), lambda b, t: (b, 0, 0)),
                  pl.BlockSpec((dm, LANE), lambda b, t: (0, 0))],
        out_specs=[pl.BlockSpec((None, tn_r, dm), lambda b, t: (b, t, 0)),
                   pl.BlockSpec((None, N_EXPERTS, tn_r), lambda b, t: (b, 0, t)),
                   pl.BlockSpec((None, tn_r, LANE), lambda b, t: (b, t, 0))],
        out_shape=[jax.ShapeDtypeStruct((nb, n, dm), BF16),
                   jax.ShapeDtypeStruct((nb, N_EXPERTS, n), F32),
                   jax.ShapeDtypeStruct((nb, n, LANE), BF16)],
        name="moe_route",
    )(h, sc, sh, wr)

    rows_total = nb * N_EXPERTS * nblk
    rb = max(SEL_ROWS, N_EXPERTS * nblk)
    assert rows_total % rb == 0
    pos = pl.pallas_call(
        functools.partial(_select_kernel, nblk_log2=nblk.bit_length() - 1, cap=cap),
        grid=(rows_total // rb,),
        in_specs=[pl.BlockSpec((rb, LANE), lambda i: (i, 0))],
        out_specs=pl.BlockSpec((rb, LANE), lambda i: (i, 0)),
        out_shape=jax.ShapeDtypeStruct((rows_total, LANE), jnp.int32),
        name="moe_select",
    )(afft.reshape(rows_total, LANE))
    pos = pos.reshape(nb, N_EXPERTS, nblk, LANE)

    tb = tn // LANE
    nkt = n // tn
    group = min(nb, max(1, FFN_ROWS // cap))
    assert nb % group == 0
    sample = lambda bg, s: bg * group + s // nkt
    ys = pl.pallas_call(
        functools.partial(_ffn_kernel, cap=cap, nkt=nkt),
        grid=(N_EXPERTS, nb // group, group * nkt),
        in_specs=[pl.BlockSpec((None, None, tb, LANE), lambda e, bg, s: (sample(bg, s), e, s % nkt, 0)),
                  pl.BlockSpec((None, tn, dm), lambda e, bg, s: (sample(bg, s), s % nkt, 0)),
                  pl.BlockSpec((None, tn, LANE), lambda e, bg, s: (sample(bg, s), s % nkt, 0)),
                  pl.BlockSpec((None, None, dm, EXPERT_FF), lambda e, bg, s: (layer, e, 0, 0)),
                  pl.BlockSpec((None, None, dm, EXPERT_FF), lambda e, bg, s: (layer, e, 0, 0)),
                  pl.BlockSpec((None, None, EXPERT_FF, dm), lambda e, bg, s: (layer, e, 0, 0))],
        out_specs=pl.BlockSpec((group, None, cap, dm), lambda e, bg, s: (bg, e, 0, 0)),
        out_shape=jax.ShapeDtypeStruct((nb, N_EXPERTS, cap, dm), BF16),
        scratch_shapes=[pltpu.VMEM((group * cap, dm), F32), pltpu.VMEM((group * cap, LANE), F32)],
        compiler_params=pltpu.CompilerParams(vmem_limit_bytes=VMEM_LIMIT),
        name="moe_ffn",
    )(pos, hm, asp, wg, wu, wd)

    return pl.pallas_call(
        functools.partial(_combine_kernel, cap=cap),
        grid=(nb, n // tn, N_EXPERTS),
        in_specs=[pl.BlockSpec((None, None, tb, LANE), lambda b, t, e: (b, e, t, 0)),
                  pl.BlockSpec((None, None, cap, dm), lambda b, t, e: (b, e, 0, 0)),
                  pl.BlockSpec((None, tn, dm), lambda b, t, e: (b, t, 0)),
                  pl.BlockSpec((None, 1, dm), lambda b, t, e: (b, 0, 0)),
                  pl.BlockSpec((1, dm), lambda b, t, e: (0, 0)),
                  pl.BlockSpec((1, dm), lambda b, t, e: (0, 0))],
        out_specs=pl.BlockSpec((None, tn, dm), lambda b, t, e: (b, t, 0)),
        out_shape=jax.ShapeDtypeStruct((nb, n, dm), F32),
        scratch_shapes=[pltpu.VMEM((tn, dm), F32)],
        compiler_params=pltpu.CompilerParams(vmem_limit_bytes=VMEM_LIMIT),
        name="moe_combine",
    )(pos, ys, h, g2, row(ln_g), row(ln_b))


N_HEADS = 4
STACK = N_HEADS * CHUNK
CHUNK_LOG2 = CHUNK.bit_length() - 1


def _stack_heads(x):
    return jnp.concatenate([x[:, h * HEAD_DIM:(h + 1) * HEAD_DIM] for h in range(N_HEADS)], axis=0)


def _stack_cols(cols, width):
    return jnp.concatenate([jnp.broadcast_to(c, (CHUNK, width)) for c in cols], axis=0)


def _mxu(a, b, dims, hp):
    dg = lambda x, y: lax.dot_general(x, y, (dims, ((), ())), preferred_element_type=F32)
    a_hi, b_hi = a.astype(BF16), b.astype(BF16)
    if not hp:
        return dg(a_hi, b_hi)
    a_lo = (a - a_hi.astype(F32)).astype(BF16)
    b_lo = (b - b_hi.astype(F32)).astype(BF16)
    return dg(a_hi, b_hi) + (dg(a_lo, b_hi) + dg(a_hi, b_lo))


def _dot_nt(a, b, hp=False):
    return _mxu(a, b, ((1,), (1,)), hp)


def _dot_tn(a, b, hp=False):
    return _mxu(a, b, ((0,), (0,)), hp)


def _dot(a, b, hp=False):
    return _mxu(a, b, ((1,), (0,)), hp)


def _chunk_masks(rev):
    ri = lax.broadcasted_iota(jnp.int32, (STACK, STACK), 0)
    ci = lax.broadcasted_iota(jnp.int32, (STACK, STACK), 1)
    same = (ri >> CHUNK_LOG2) == (ci >> CHUNK_LOG2)
    ahead = jnp.where(rev, ci - ri, ri - ci)
    return same & (ahead >= 0), same & (ahead > 0)


def _scan_cumsum(gt, rev):
    ii = lax.broadcasted_iota(jnp.int32, (CHUNK, CHUNK), 0)
    jj = lax.broadcasted_iota(jnp.int32, (CHUNK, CHUNK), 1)
    tri = jnp.where(jnp.where(rev, jj - ii, ii - jj) >= 0, 1.0, 0.0)
    return jnp.dot(tri, gt, precision=lax.Precision.HIGHEST, preferred_element_type=F32)


def _unit_triangular_inverse(a):
    ri = lax.broadcasted_iota(jnp.int32, a.shape, 0)
    ci = lax.broadcasted_iota(jnp.int32, a.shape, 1)
    joins = lambda lvl: ((ri >> (lvl + 1)) == (ci >> (lvl + 1))) & ((ri >> lvl) != (ci >> lvl))
    t = jnp.where(ri == ci, 1.0, 0.0) - jnp.where(joins(0), a, 0.0)
    for lvl in range(1, CHUNK_LOG2):
        m = _dot(jnp.where(joins(lvl), a, 0.0), t)
        yield
        t = t - _dot(t, m)
        yield
    return t


def _interleave(chains):
    for _ in zip(*chains):
        pass


def _linear_scan_kernel(*refs, has_beta, k_scale):
    if has_beta:
        q_ref, k_ref, v_ref, gt_ref, gp_ref, s0_ref, o_ref, sfin_ref, s_scr = refs
    else:
        q_ref, k_ref, v_ref, gp_ref, s0_ref, o_ref, sfin_ref, s_scr = refs
    rev = pl.program_id(0) == 1
    n = pl.program_id(2)

    @pl.when(n == 0)
    def _():
        s_scr[...] = s0_ref[...]

    chains = []
    for i in range(q_ref.shape[0]):
        if has_beta:
            gt = _gdn_gates(gt_ref[i], gp_ref[...])
        else:
            gt = jnp.broadcast_to(gp_ref[0:1, :], (CHUNK, LANE))
        chains.append(_linear_chunk(q_ref.at[i], k_ref.at[i], v_ref.at[i], gt, o_ref.at[i], s_scr.at[i], rev,
                                    has_beta, k_scale))
    _interleave(chains)

    @pl.when(n == pl.num_programs(2) - 1)
    def _():
        sfin_ref[...] = s_scr[...]


def _softplus(x):
    return jnp.maximum(x, 0.0) + jnp.log1p(jnp.exp(-jnp.abs(x)))


def _gdn_gates(raw, gp):
    lane = lax.broadcasted_iota(jnp.int32, raw.shape, 1)
    return jnp.where(lane < 2 * N_HEADS, gp[0:1, :] * _softplus(raw + gp[1:2, :]), jax.nn.sigmoid(raw))


def _mlstm_gates(raw, gp):
    lane = lax.broadcasted_iota(jnp.int32, raw.shape, 1)
    x = raw + gp[0:1, :]
    return jnp.where(lane < 2 * N_HEADS, x, -_softplus(-x))


def _linear_chunk(q_ref, k_ref, v_ref, gt, o_ref, s_scr, rev, has_beta, k_scale):
    cum = _scan_cumsum(gt, rev)
    gcols, bcols, gtots = [], [], []
    for h in range(N_HEADS):
        gc = jnp.where(rev, cum[:, N_HEADS + h:N_HEADS + h + 1], cum[:, h:h + 1])
        gcols.append(gc)
        gtots.append(jnp.where(rev, gc[0:1], gc[CHUNK - 1:CHUNK]))
        bcols.append(jnp.where(rev, gt[:, 3 * N_HEADS + h:3 * N_HEADS + h + 1], gt[:, 2 * N_HEADS + h:2 * N_HEADS + h + 1]))
    cb = _stack_cols(gcols, STACK)
    diff = cb - cb.T
    incl, strict = _chunk_masks(rev)
    dec = jnp.exp(jnp.where(incl, diff, 0.0))
    gcb = cb[:, :HEAD_DIM]
    q_st, k_st, v_st = _stack_heads(q_ref[...]), _stack_heads(k_ref[...]) * k_scale, _stack_heads(v_ref[...])
    a_qk = _dot_nt(q_st, k_st) * jnp.where(incl, dec, 0.0)
    if has_beta:
        beta = _stack_cols(bcols, HEAD_DIM)
        kb = k_st * beta
        a = _dot_nt(kb, k_st) * jnp.where(strict, dec, 0.0)
        yield
        t_inv = yield from _unit_triangular_inverse(a)
        x = _dot(t_inv, jnp.concatenate([v_st * beta, kb * jnp.exp(gcb)], axis=1))
        u_st, w_st = x[:, :HEAD_DIM], x[:, HEAD_DIM:]
    else:
        u_st, w_st = v_st, None
    gtot = _stack_cols(gtots, HEAD_DIM)
    k_end = k_st * jnp.exp(gtot - gcb)
    q_dec = q_st * jnp.exp(gcb)
    hs = lambda t, h: t[h * CHUNK:(h + 1) * CHUNK]
    states = [s_scr[h] for h in range(N_HEADS)]
    q_s = [_dot(hs(q_dec, h), states[h]) for h in range(N_HEADS)]
    yield
    if has_beta:
        vn = jnp.concatenate([hs(u_st, h) - _dot(hs(w_st, h), states[h]) for h in range(N_HEADS)], axis=0)
    else:
        vn = u_st
    yield
    o_st = _dot(a_qk, vn)
    for h in range(N_HEADS):
        s_scr[h] = jnp.exp(gtots[h]) * states[h] + _dot_tn(hs(k_end, h), hs(vn, h))
    yield
    for h in range(N_HEADS):
        o_ref[:, h * HEAD_DIM:(h + 1) * HEAD_DIM] = hs(o_st, h) + q_s[h]
    yield


SCAN_NS = 4


HEADS_W = N_HEADS * HEAD_DIM


def _scan_specs(srcs, gates, nb, nchunk):
    ns = SCAN_NS
    assert nb % ns == 0
    cidx = lambda d, n: n + d * (nchunk - 1 - 2 * n)
    specs = [pl.BlockSpec((ns, CHUNK, HEADS_W), lambda d, b, n, c=c: (b, cidx(d, n), c)) for _, c in srcs]
    if gates is not None:
        specs.append(pl.BlockSpec((ns, CHUNK, LANE), lambda d, b, n, c=gates[1]: (b, cidx(d, n), c)))
    specs.append(pl.BlockSpec((2, LANE), lambda d, b, n: (0, 0)))
    ospec = pl.BlockSpec((None, ns, CHUNK, HEADS_W), lambda d, b, n: (d, b, cidx(d, n), 0))
    return specs, ospec


def linear_scan_bidir(q, k, v, gates, gp, s0, has_beta, k_scale=1.0):
    nb, t, _ = q[0].shape
    nchunk = t // CHUNK
    assert t % CHUNK == 0
    ns = SCAN_NS
    specs, ospec = _scan_specs((q, k, v), gates if has_beta else None, nb, nchunk)
    sspec = pl.BlockSpec((None, ns, N_HEADS, HEAD_DIM, HEAD_DIM), lambda d, b, n: (d, b, 0, 0, 0))
    args = [q[0], k[0], v[0]] + ([gates[0]] if has_beta else []) + [gp, s0]
    return pl.pallas_call(
        functools.partial(_linear_scan_kernel, has_beta=has_beta, k_scale=k_scale),
        grid=(2, nb // ns, nchunk),
        in_specs=specs + [sspec],
        out_specs=[ospec, sspec],
        out_shape=[jax.ShapeDtypeStruct((2, nb, t, HEADS_W), F32),
                   jax.ShapeDtypeStruct((2, nb, N_HEADS, HEAD_DIM, HEAD_DIM), F32)],
        scratch_shapes=[pltpu.VMEM((ns, N_HEADS, HEAD_DIM, HEAD_DIM), F32)],
        name="gdn_scan" if has_beta else "retention_scan",
    )(*args)


def linear_scan_two_pass(ctx_args, lat_args, gp, has_beta, k_scale=1.0):
    nb = ctx_args[0][0].shape[0]
    zero = jnp.zeros((2, nb, N_HEADS, HEAD_DIM, HEAD_DIM), F32)
    o_ctx, s_ctx = linear_scan_bidir(*ctx_args, gp, zero, has_beta, k_scale)
    o_lat, _ = linear_scan_bidir(*lat_args, gp, s_ctx, has_beta, k_scale)
    return o_ctx, o_lat


MLSTM_HP = True


def _mlstm_scan_kernel(q_ref, k_ref, v_ref, gt_ref, gp_ref, c0_ref, n0_ref, m0_ref, o_ref, cfin_ref, nfin_ref,
                       mfin_ref, c_scr, n_scr, m_scr):
    rev = pl.program_id(0) == 1
    step = pl.program_id(2)

    @pl.when(step == 0)
    def _():
        c_scr[...] = c0_ref[...]
        n_scr[...] = n0_ref[...]
        m_scr[...] = m0_ref[...]

    _interleave([_mlstm_chunk(q_ref.at[i], k_ref.at[i], v_ref.at[i], _mlstm_gates(gt_ref[i], gp_ref[...]),
                              o_ref.at[i], c_scr.at[i], n_scr.at[i], m_scr.at[i], rev)
                 for i in range(q_ref.shape[0])])

    @pl.when(step == pl.num_programs(2) - 1)
    def _():
        cfin_ref[...] = c_scr[...]
        nfin_ref[...] = n_scr[...]
        mfin_ref[...] = m_scr[...]


def _mlstm_chunk(q_ref, k_ref, v_ref, gt, o_ref, c_scr, n_scr, m_scr, rev):
    cum = _scan_cumsum(gt, rev)
    pick = lambda t, c: jnp.where(rev, t[:, N_HEADS + c:N_HEADS + c + 1], t[:, c:c + 1])
    q_st, k_st, v_st = _stack_heads(q_ref[...]), _stack_heads(k_ref[...]), _stack_heads(v_ref[...])
    hs = lambda t, h: t[h * CHUNK:(h + 1) * CHUNK]
    qk = _dot_nt(q_st, k_st, MLSTM_HP)
    yield
    bcums, srcs, inters, qcs, qns = [], [], [], [], []
    for h in range(N_HEADS):
        ic = pick(gt, h)
        bcum = pick(cum, 2 * N_HEADS + h)
        b_end = jnp.where(rev, bcum[0:1], bcum[CHUNK - 1:CHUNK])
        c_prev, n_prev, m_prev = c_scr[h], n_scr[h], m_scr[h][:, :1]
        a = b_end - bcum + ic
        m_new = jnp.maximum(b_end + m_prev, jnp.max(a, axis=0, keepdims=True))
        w_state = jnp.exp(a - m_new)
        decay = jnp.exp(b_end + m_prev - m_new)
        kw = hs(k_st, h) * w_state
        c_scr[h] = decay * c_prev + _dot_tn(kw, hs(v_st, h), MLSTM_HP)
        n_scr[h] = decay * n_prev + jnp.sum(kw, axis=0, keepdims=True)
        m_scr[h] = jnp.broadcast_to(m_new, (1, HEAD_DIM))
        bcums.append(bcum)
        srcs.append(bcum - ic)
        inters.append(bcum + m_prev)
        qcs.append(_dot(hs(q_st, h), c_prev, MLSTM_HP))
        qns.append(jnp.sum(hs(q_st, h) * n_prev, axis=1, keepdims=True))
    yield
    incl, _ = _chunk_masks(rev)
    dlog = _stack_cols(bcums, STACK) - _stack_cols(srcs, STACK).T
    inter = jnp.concatenate(inters, axis=0)
    m_t = jnp.maximum(inter, jnp.max(jnp.where(incl, dlog, MASKED), axis=1, keepdims=True))
    s = qk * jnp.where(incl, jnp.exp(jnp.where(incl, dlog, 0.0) - m_t), 0.0)
    w_inter = jnp.exp(inter - m_t)
    yield
    num = _dot(s, v_st, MLSTM_HP) + w_inter * jnp.concatenate(qcs, axis=0)
    den = jnp.sum(s, axis=1, keepdims=True) + w_inter * jnp.concatenate(qns, axis=0)
    yield
    out = num / jnp.maximum(jnp.abs(den), jnp.exp(-m_t))
    for h in range(N_HEADS):
        o_ref[:, h * HEAD_DIM:(h + 1) * HEAD_DIM] = hs(out, h)
    yield


def mlstm_scan_bidir(q, k, v, gates, gp, state):
    nb, t, _ = q[0].shape
    nchunk = t // CHUNK
    assert t % CHUNK == 0
    ns = SCAN_NS
    specs, ospec = _scan_specs((q, k, v), gates, nb, nchunk)
    cspec = pl.BlockSpec((None, ns, N_HEADS, HEAD_DIM, HEAD_DIM), lambda d, b, n: (d, b, 0, 0, 0))
    vspec = pl.BlockSpec((None, ns, N_HEADS, 1, HEAD_DIM), lambda d, b, n: (d, b, 0, 0, 0))
    cshape = jax.ShapeDtypeStruct((2, nb, N_HEADS, HEAD_DIM, HEAD_DIM), F32)
    vshape = jax.ShapeDtypeStruct((2, nb, N_HEADS, 1, HEAD_DIM), F32)
    o, c, n, m = pl.pallas_call(
        _mlstm_scan_kernel,
        grid=(2, nb // ns, nchunk),
        in_specs=specs + [cspec, vspec, vspec],
        out_specs=[ospec, cspec, vspec, vspec],
        out_shape=[jax.ShapeDtypeStruct((2, nb, t, HEADS_W), F32), cshape, vshape, vshape],
        scratch_shapes=[pltpu.VMEM((ns, N_HEADS, HEAD_DIM, HEAD_DIM), F32), pltpu.VMEM((ns, N_HEADS, 1, HEAD_DIM), F32),
                        pltpu.VMEM((ns, N_HEADS, 1, HEAD_DIM), F32)],
        name="mlstm_scan",
    )(q[0], k[0], v[0], gates[0], gp, *state)
    return o, (c, n, m)


def mlstm_two_pass(ctx_args, lat_args, gp):
    nb = ctx_args[0][0].shape[0]
    zero = (jnp.zeros((2, nb, N_HEADS, HEAD_DIM, HEAD_DIM), F32), jnp.zeros((2, nb, N_HEADS, 1, HEAD_DIM), F32),
            jnp.zeros((2, nb, N_HEADS, 1, HEAD_DIM), F32))
    o_ctx, s_ctx = mlstm_scan_bidir(*ctx_args, gp, zero)
    o_lat, _ = mlstm_scan_bidir(*lat_args, gp, s_ctx)
    return o_ctx, o_lat


PROJ_TM = 1024
PROJ_TN_MAX = 2304


def _inproj_kernel(h_ref, sc_ref, sh_ref, w_ref, o_ref):
    hm = (h_ref[...] * (1.0 + sc_ref[...]) + sh_ref[...]).astype(BF16)
    o_ref[...] = jnp.dot(hm, w_ref[...], preferred_element_type=F32)


def in_projection(h, sc, sh, w):
    nb, n, dm = h.shape
    ncol = w.shape[1]
    tm = min(PROJ_TM, n)
    tn = max(c for c in range(LANE, min(ncol, PROJ_TN_MAX) + 1, LANE) if ncol % c == 0)
    return pl.pallas_call(
        _inproj_kernel,
        grid=(nb, n // tm, ncol // tn),
        in_specs=[pl.BlockSpec((None, tm, dm), lambda b, t, j: (b, t, 0)),
                  pl.BlockSpec((None, 1, dm), lambda b, t, j: (b, 0, 0)),
                  pl.BlockSpec((None, 1, dm), lambda b, t, j: (b, 0, 0)),
                  pl.BlockSpec((dm, tn), lambda b, t, j: (0, j))],
        out_specs=pl.BlockSpec((None, tm, tn), lambda b, t, j: (b, t, j)),
        out_shape=jax.ShapeDtypeStruct((nb, n, ncol), F32),
        name="in_projection",
    )(h, sc, sh, w)


def _conv_kernel(x_ref, w_ref, o_ref, *, grid_w, l2_blocks, q_scale, k_scale):
    x = x_ref[...]
    t = x.shape[0]
    w = w_ref[...]
    col = lax.broadcasted_iota(jnp.int32, x.shape, 0) & (grid_w - 1)
    left = jnp.where(col == 0, 0.0, pltpu.roll(x, 1, 0))
    right = jnp.where(col == grid_w - 1, 0.0, pltpu.roll(x, t - 1, 0))
    row = lambda kh: w[3 * kh:3 * kh + 1] * left + w[3 * kh + 1:3 * kh + 2] * x + w[3 * kh + 2:3 * kh + 3] * right
    acc = row(1)
    if grid_w < t:
        zero = jnp.zeros((grid_w, LANE), F32)
        acc = acc + jnp.concatenate([zero, row(0)[:t - grid_w]], axis=0) + jnp.concatenate([row(2)[grid_w:], zero], axis=0)
    y = acc * jax.nn.sigmoid(acc)
    c = pl.program_id(1)
    normed = y * lax.rsqrt(jnp.sum(y * y, axis=1, keepdims=True) + L2_EPS)
    y = jnp.where(c < l2_blocks, normed, y)
    o_ref[...] = y * jnp.where(c < N_HEADS, q_scale, jnp.where(c < 2 * N_HEADS, k_scale, 1.0))


def conv_prep(proj, conv_w, nblk, grid_w, l2_blocks, q_scale, k_scale):
    nb, t, _ = proj.shape
    assert grid_w & (grid_w - 1) == 0 and t % grid_w == 0
    return pl.pallas_call(
        functools.partial(_conv_kernel, grid_w=grid_w, l2_blocks=l2_blocks, q_scale=q_scale, k_scale=k_scale),
        grid=(nb, nblk),
        in_specs=[pl.BlockSpec((None, t, LANE), lambda b, c: (b, 0, c)),
                  pl.BlockSpec((CONV_K * CONV_K, LANE), lambda b, c: (0, c))],
        out_specs=pl.BlockSpec((None, t, LANE), lambda b, c: (b, 0, c)),
        out_shape=jax.ShapeDtypeStruct((nb, t, nblk * LANE), F32),
        compiler_params=pltpu.CompilerParams(vmem_limit_bytes=VMEM_LIMIT),
        name="conv_prep",
    )(proj, conv_w.reshape(CONV_K * CONV_K, -1))


def _head_norm(o, center):
    outs = []
    for h in range(N_HEADS):
        x = o[:, h * HEAD_DIM:(h + 1) * HEAD_DIM]
        if center:
            x = x - jnp.mean(x, axis=1, keepdims=True)
        outs.append(x * lax.rsqrt(jnp.mean(x * x, axis=1, keepdims=True) + EPS))
    return jnp.concatenate(outs, axis=1)


def _mix_out(y_a, y_b, wo_ref, h_ref, g1_ref, lng_ref, lnb_ref, o_ref):
    y = (jnp.dot(y_a.astype(BF16), wo_ref[:HEADS_W, :], preferred_element_type=F32)
         + jnp.dot(y_b.astype(BF16), wo_ref[HEADS_W:, :], preferred_element_type=F32))
    o_ref[...] = _ln_rows(ALPHA * h_ref[...] + g1_ref[...] * y, lng_ref[...], lnb_ref[...])


def _merge_even_kernel(og_ref, or_ref, za_ref, zr_ref, gg_ref, rg_ref, wo_ref, h_ref, g1_ref, lng_ref, lnb_ref, o_ref):
    za, zr = za_ref[...], zr_ref[...]
    y_g = _head_norm(og_ref[0] + og_ref[1], False) * gg_ref[...] * (za * jax.nn.sigmoid(za))
    y_r = _head_norm(or_ref[0] + or_ref[1], True) * rg_ref[...] * (zr * jax.nn.sigmoid(zr))
    _mix_out(y_g, y_r, wo_ref, h_ref, g1_ref, lng_ref, lnb_ref, o_ref)


def _merge_odd_kernel(om_ref, ys_ref, og_ref, u_ref, mg_ref, dsk_ref, bglu_ref, wglu_ref, wo_ref, h_ref, g1_ref,
                      lng_ref, lnb_ref, o_ref):
    y_m = _head_norm(om_ref[0] + om_ref[1], True) * mg_ref[...] * jax.nn.sigmoid(og_ref[...])
    y = jax.nn.gelu(ys_ref[...] + dsk_ref[...] * u_ref[...])
    y = y * jax.nn.sigmoid(jnp.dot(y.astype(BF16), wglu_ref[...], preferred_element_type=F32) + bglu_ref[...])
    _mix_out(y_m, y, wo_ref, h_ref, g1_ref, lng_ref, lnb_ref, o_ref)


def _merge_call(kernel_fn, name, scans, toks, rows, mats, h, g1, ln_g, ln_b):
    nb, n, dm = h.shape
    tm = min(PROJ_TM, n)
    full = lambda a: pl.BlockSpec(a.shape, lambda b, t: (0,) * a.ndim)
    rowv = lambda v: v.reshape(1, -1)
    specs, args = [], []
    for a in scans:
        specs.append(pl.BlockSpec((2, None, tm, HEADS_W), lambda b, t: (0, b, t, 0)))
        args.append(a)
    for a, c in toks:
        specs.append(pl.BlockSpec((None, tm, HEADS_W), lambda b, t, c=c: (b, t, c)))
        args.append(a)
    for v in rows:
        args.append(rowv(v))
        specs.append(full(args[-1]))
    for m in mats:
        args.append(m)
        specs.append(full(m))
    args += [h, g1, rowv(ln_g), rowv(ln_b)]
    specs += [pl.BlockSpec((None, tm, dm), lambda b, t: (b, t, 0)), pl.BlockSpec((None, 1, dm), lambda b, t: (b, 0, 0)),
              full(args[-2]), full(args[-1])]
    return pl.pallas_call(
        kernel_fn,
        grid=(nb, n // tm),
        in_specs=specs,
        out_specs=pl.BlockSpec((None, tm, dm), lambda b, t: (b, t, 0)),
        out_shape=jax.ShapeDtypeStruct((nb, n, dm), F32),
        compiler_params=pltpu.CompilerParams(vmem_limit_bytes=VMEM_LIMIT),
        name=name,
    )(*args)


def retention_log_decay(direction):
    expo = 5.0 + 2.0 * jnp.arange(RET_HEADS, dtype=F32) + direction
    return jnp.log1p(-jnp.exp2(-expo))


GATE_COLS = 4 * N_HEADS
Q_SCALE = HEAD_DIM ** -0.5


def _gate_row(*vals):
    v = jnp.concatenate([jnp.ravel(x) for x in vals])
    return jnp.pad(v, (0, LANE - v.shape[0]))


def _pad_gate_cols(w):
    return jnp.pad(w, ((0, 0), (0, LANE - w.shape[1])))


def gdn_retention_mixer(hs_, mods, w_in, w_out, conv_w, a_log, dt_bias, gdn_gain, ret_gain, ln_g, ln_b, ctx_out):
    w = jnp.concatenate([w_in[:, :4 * GDN_W], w_in[:, 4 * GDN_W + GATE_COLS:],
                         _pad_gate_cols(w_in[:, 4 * GDN_W:4 * GDN_W + GATE_COLS])], axis=1).astype(BF16)
    gate_blk = (4 * GDN_W + 4 * RET_W) // LANE
    gp_gdn = jnp.stack([_gate_row(-jnp.exp(a_log)), _gate_row(dt_bias)])
    gp_ret = jnp.stack([_gate_row(retention_log_decay(0), retention_log_decay(1)), jnp.zeros((LANE,), F32)])
    wo = w_out.astype(BF16)
    projs, convs = [], []
    for (h, (sc, sh, _)), grid_w in zip(zip(hs_, mods), (hs_[0].shape[1], GRID_W)):
        p = in_projection(h, sc, sh, w)
        projs.append(p)
        convs.append(conv_prep(p, conv_w, 3 * GDN_W // LANE, grid_w, 2 * N_HEADS, Q_SCALE, 1.0))
    gdn_args = [((cv, 0), (cv, 1), (cv, 2), (p, gate_blk)) for p, cv in zip(projs, convs)]
    ret_args = [((p, 4), (p, 5), (p, 6), None) for p in projs]
    og = linear_scan_two_pass(gdn_args[0], gdn_args[1], gp_gdn, True)
    orr = linear_scan_two_pass(ret_args[0], ret_args[1], gp_ret, False, Q_SCALE)
    outs = []
    for i in range(2):
        if i == 0 and not ctx_out:
            outs.append(None)
            continue
        outs.append(_merge_call(_merge_even_kernel, "merge_even", [og[i], orr[i]], [(projs[i], 3), (projs[i], 7)],
                                [jnp.tile(gdn_gain, N_HEADS), ret_gain], [wo], hs_[i], mods[i][2], ln_g, ln_b))
    return outs


def mlstm_s5_mixer(hs_, mods, w_in, w_out, conv_w, gate_bias, mlstm_gain, lam_re, lam_im, log_dt,
                   b_re, b_im, c_re, c_im, d_skip, w_glu, b_glu, ln_g, ln_b, ctx_out):
    w = jnp.concatenate([w_in[:, :4 * MLSTM_W], _pad_gate_cols(w_in[:, 4 * MLSTM_W:4 * MLSTM_W + GATE_COLS])],
                        axis=1).astype(BF16)
    w_u = w_in[:, 4 * MLSTM_W + GATE_COLS:].astype(BF16)
    gate_blk = 4 * MLSTM_W // LANE
    gp = jnp.stack([_gate_row(gate_bias[0, 0], gate_bias[1, 0], gate_bias[0, 1], gate_bias[1, 1]),
                    jnp.zeros((LANE,), F32)])
    wo, wglu = w_out.astype(BF16), w_glu.astype(BF16)
    projs, us, args = [], [], []
    for (h, (sc, sh, _)), grid_w in zip(zip(hs_, mods), (hs_[0].shape[1], GRID_W)):
        p = in_projection(h, sc, sh, w)
        cv = conv_prep(p, conv_w, 2 * MLSTM_W // LANE, grid_w, 0, 1.0, Q_SCALE)
        projs.append(p)
        us.append(in_projection(h, sc, sh, w_u))
        args.append(((cv, 0), (cv, 1), (p, 2), (p, gate_blk)))
    om = mlstm_two_pass(args[0], args[1], gp)
    ys = s5_bidirectional(us[0], us[1], _s5_weights(lam_re, lam_im, log_dt, b_re, b_im, c_re, c_im))
    outs = []
    for i in range(2):
        if i == 0 and not ctx_out:
            outs.append(None)
            continue
        outs.append(_merge_call(_merge_odd_kernel, "merge_odd", [om[i]], [(ys[i], 0), (projs[i], 3), (us[i], 0)],
                                [mlstm_gain, d_skip, b_glu], [wglu, wo], hs_[i], mods[i][2], ln_g, ln_b))
    return outs


def kernel(x, c, ctx, c_ctx, w_mod, b_mod, ln1_g, ln1_b, ln2_g, ln2_b, w_router, w_gate, w_up, w_down,
           ev_w_in, ev_w_out, ev_conv, ev_a_log, ev_dt_bias, ev_gdn_norm, ev_ret_norm,
           od_w_in, od_w_out, od_conv, od_gate_bias, od_mlstm_norm, od_lam_re, od_lam_im, od_log_dt,
           od_b_re, od_b_im, od_c_re, od_c_im, od_d_skip, od_w_glu, od_b_glu):
    h_lat, h_ctx = x, ctx
    s_lat = jax.nn.silu(c)
    s_ctx = jax.nn.silu(c_ctx)
    experts = (w_gate.astype(BF16), w_up.astype(BF16), w_down.astype(BF16))
    for l in range(DEPTH):
        last = l == DEPTH - 1
        sh1, sc1, g1, sh2, sc2, g2 = jnp.split((s_lat @ w_mod[l] + b_mod[l])[:, None, :], 6, axis=-1)
        bc = lambda v: jnp.broadcast_to(v, (h_ctx.shape[0], 1, D_MODEL))
        csh1, csc1, cg1, csh2, csc2, cg2 = [bc(v) for v in jnp.split(s_ctx @ w_mod[l] + b_mod[l], 6, axis=-1)]
        streams = (h_ctx, h_lat)
        mods = ((csc1, csh1, cg1), (sc1, sh1, g1))
        if l % 2 == 0:
            e = l // 2
            h_ctx, h_lat = gdn_retention_mixer(streams, mods, ev_w_in[e], ev_w_out[e], ev_conv[e], ev_a_log[e],
                                               ev_dt_bias[e], ev_gdn_norm[e], ev_ret_norm[e], ln1_g[l], ln1_b[l],
                                               not last)
        else:
            o = l // 2
            h_ctx, h_lat = mlstm_s5_mixer(streams, mods, od_w_in[o], od_w_out[o], od_conv[o], od_gate_bias[o],
                                          od_mlstm_norm[o], od_lam_re[o], od_lam_im[o], od_log_dt[o],
                                          od_b_re[o], od_b_im[o], od_c_re[o], od_c_im[o], od_d_skip[o],
                                          od_w_glu[o], od_b_glu[o], ln1_g[l], ln1_b[l], not last)
        h_lat = moe_block(h_lat, sc2, sh2, g2, ln2_g[l], ln2_b[l], w_router[l], *experts, l)
        if not last:
            h_ctx = moe_block(h_ctx, csc2, csh2, cg2, ln2_g[l], ln2_b[l], w_router[l], *experts, l)
    return h_lat
```

```python
import functools

import jax
import jax.numpy as jnp
from jax import lax
from jax.experimental import pallas as pl
from jax.experimental.pallas import tpu as pltpu

D_MODEL = 1024
DEPTH = 4
GRID_W = 64
CHUNK = 64
CONV_K = 3
HEAD_DIM = D_MODEL // 8
GDN_HEADS = 4
RET_HEADS = 4
MLSTM_HEADS = 4
GDN_W = GDN_HEADS * HEAD_DIM
RET_W = RET_HEADS * HEAD_DIM
MLSTM_W = MLSTM_HEADS * HEAD_DIM
S5_CH = D_MODEL // 2
S5_GROUP = 16
S5_GROUPS = S5_CH // S5_GROUP
S5_STATE = 64
N_EXPERTS = 16
EXPERT_FF = 2 * D_MODEL
CAPACITY_FACTOR = 2
ALPHA = (2 * DEPTH) ** 0.25
EPS = 1e-5
L2_EPS = 1e-6
MASKED = -1e30
F32 = jnp.float32
BF16 = jnp.bfloat16

LANE = 128


S5_L = 16
S5_NB = S5_CH // LANE
S5_GPB = LANE // S5_GROUP
S5_SW = S5_GPB * S5_STATE
VMEM_LIMIT = 56 * 1024 * 1024


def _s5_expand_kernel(c_ref, o_ref, *, row_item_log2, col_item_log2):
    c = c_ref[...]
    nk, nc = c.shape[1], o_ref.shape[1]
    gmask = S5_GPB - 1
    k = lax.broadcasted_iota(jnp.int32, (nk, nc), 0)
    col = lax.broadcasted_iota(jnp.int32, (nk, nc), 1)
    src = ((col >> (col_item_log2 + S5_GPB.bit_length() - 1)) << col_item_log2) | (col & ((1 << col_item_log2) - 1))
    rep = jnp.where(k == src, 1.0, 0.0).astype(BF16)
    wide = jnp.dot(c, rep, preferred_element_type=F32)
    r = lax.broadcasted_iota(jnp.int32, wide.shape, 0)
    cc = lax.broadcasted_iota(jnp.int32, wide.shape, 1)
    same = ((r >> row_item_log2) & gmask) == ((cc >> col_item_log2) & gmask)
    o_ref[...] = jnp.where(same, wide, 0.0).astype(BF16)


def _s5_expand(c, row_item_log2, col_item_log2):
    nbk, rows, nk = c.shape
    return pl.pallas_call(
        functools.partial(_s5_expand_kernel, row_item_log2=row_item_log2, col_item_log2=col_item_log2),
        grid=(nbk,),
        in_specs=[pl.BlockSpec((None, rows, nk), lambda j: (j, 0, 0))],
        out_specs=pl.BlockSpec((None, rows, nk * S5_GPB), lambda j: (j, 0, 0)),
        out_shape=jax.ShapeDtypeStruct((nbk, rows, nk * S5_GPB), BF16),
        compiler_params=pltpu.CompilerParams(vmem_limit_bytes=VMEM_LIMIT),
        name="s5_expand",
    )(c)


def _s5_weights(lam_re, lam_im, log_dt, b_re, b_im, c_re, c_im):
    L, G = S5_L, S5_GROUPS
    hp = lax.Precision.HIGHEST
    taus = jnp.arange(L + 1, dtype=F32)[:, None, None]
    ks, ws, cas, ds = [], [], [], []
    for d in range(2):
        lr = jnp.minimum(lam_re[d], -1e-4)
        li = lam_im[d]
        dt = jnp.exp(log_dt[d])[:, None]
        mag = jnp.exp(lr * dt)
        ab_re, ab_im = mag * jnp.cos(li * dt), mag * jnp.sin(li * dt)
        xr, xi, den = ab_re - 1.0, ab_im, lr * lr + li * li
        f_re = (xr * lr + xi * li) / den
        f_im = (xi * lr - xr * li) / den
        bb_re = f_re[..., None] * b_re - f_im[..., None] * b_im
        bb_im = f_re[..., None] * b_im + f_im[..., None] * b_re
        pmag = jnp.exp(taus * (lr * dt))
        ar, ai = pmag * jnp.cos(taus * (li * dt)), pmag * jnp.sin(taus * (li * dt))
        wr = ar[..., None] * bb_re - ai[..., None] * bb_im
        wi = ar[..., None] * bb_im + ai[..., None] * bb_re
        k = (jnp.einsum('gop,tgpi->tgio', c_re, wr, precision=hp)
             - jnp.einsum('gop,tgpi->tgio', c_im, wi, precision=hp))
        car = c_re[None] * ar[:, :, None, :] - c_im[None] * ai[:, :, None, :]
        cai = c_re[None] * ai[:, :, None, :] + c_im[None] * ar[:, :, None, :]
        ks.append(k)
        ws.append((wr, wi))
        cas.append((car, cai))
        ds.append((ar[L], ai[L]))

    split_g = lambda t, g_axis: t.reshape(t.shape[:g_axis] + (S5_NB, S5_GPB) + t.shape[g_axis + 1:]).astype(BF16)
    compact = lambda t: t.reshape(S5_NB, L * LANE, -1)
    h_log2, p_log2 = S5_GROUP.bit_length() - 1, S5_STATE.bit_length() - 1

    kf, kb = ks
    kc = jnp.concatenate([kb[1:L][::-1], (kf[0] + kb[0])[None], kf[1:L]], axis=0)
    idx = (jnp.arange(L)[None, :] - jnp.arange(L)[:, None]) + (L - 1)
    tz = _s5_expand(compact(split_g(kc[idx], 2).transpose(2, 0, 3, 4, 1, 5)), h_log2, h_log2)

    pb_c, ca_c = [], []
    for d in range(2):
        wr, wi = ws[d]
        order = jnp.arange(L - 1, -1, -1) if d == 0 else jnp.arange(L)
        pb_c.append(jnp.stack([jnp.swapaxes(w[order], -1, -2) for w in (wr, wi)]))
        car, cai = cas[d]
        order = jnp.arange(1, L + 1) if d == 0 else jnp.arange(L, 0, -1)
        ca_c.append(jnp.stack([jnp.swapaxes(m, -1, -2) for m in (car[order], -cai[order])]))
    pb = _s5_expand(compact(split_g(jnp.stack(pb_c), 3).transpose(3, 2, 4, 5, 0, 1, 6)), h_log2, p_log2)
    ca = _s5_expand(compact(split_g(jnp.stack(ca_c), 3).transpose(3, 0, 1, 4, 5, 2, 6)), p_log2, h_log2)
    dr = jnp.stack([ds[0][0], ds[1][0]], 0).reshape(2, S5_NB, 1, S5_SW).transpose(1, 0, 2, 3).reshape(2 * S5_NB, 1, S5_SW)
    di = jnp.stack([ds[0][1], ds[1][1]], 0).reshape(2, S5_NB, 1, S5_SW).transpose(1, 0, 2, 3).reshape(2 * S5_NB, 1, S5_SW)
    return tz, pb, ca, dr, di


def _s5_p_kernel(u_ref, pb_ref, p_ref, *, nb, nc):
    res = jnp.dot(u_ref[...], pb_ref[...], preferred_element_type=F32)
    for b in range(nb):
        p_ref[:, b * 2 * S5_SW:(b + 1) * 2 * S5_SW] = res[b * nc:(b + 1) * nc]


def _s5_scan_kernel(p_ref, dr_ref, di_ref, s_ref, *, n_ctx, n_lat):
    rev = pl.program_id(0) % 2
    dr = dr_ref[...]
    di = di_ref[...]
    nbatch = p_ref.shape[1]

    def phase(base, n, carry):
        def body(step, carry):
            sr, si = carry
            row = base + jnp.where(rev == 0, step, n - 1 - step)
            s_ref[row, :, :S5_SW] = sr
            s_ref[row, :, S5_SW:] = si
            p = p_ref[row]
            nr = dr * sr - di * si + p[:, :S5_SW]
            ni = dr * si + di * sr + p[:, S5_SW:]
            return nr, ni
        return lax.fori_loop(0, n, body, carry)

    zero = jnp.zeros((nbatch, S5_SW), F32)
    carry = phase(0, n_ctx, (zero, zero))
    phase(n_ctx, n_lat, carry)


def _s5_y_kernel(u_ref, tz_ref, sf_ref, sb_ref, ca_ref, y_ref):
    y = jnp.dot(u_ref[...], tz_ref[...], preferred_element_type=F32)
    y += jnp.dot(sf_ref[...].astype(BF16), ca_ref[:2 * S5_SW, :], preferred_element_type=F32)
    y += jnp.dot(sb_ref[...].astype(BF16), ca_ref[2 * S5_SW:, :], preferred_element_type=F32)
    y_ref[...] = y


def s5_bidirectional(u_ctx, u_lat, weights):
    tz, pb, ca, dr, di = weights
    L = S5_L
    nb, t_ctx, _ = u_ctx.shape
    t_lat = u_lat.shape[1]
    assert t_ctx % L == 0 and t_lat % L == 0
    n_ctx, n_lat = t_ctx // L, t_lat // L
    nc = n_ctx + n_lat
    kw = L * LANE
    sw2 = 2 * S5_SW
    u = jnp.concatenate([u_ctx, u_lat], axis=1)
    ub = u.reshape(nb * nc, L, S5_NB, LANE).transpose(2, 0, 1, 3).reshape(S5_NB, nb * nc, kw).astype(BF16)

    p = pl.pallas_call(
        functools.partial(_s5_p_kernel, nb=nb, nc=nc),
        grid=(S5_NB, 2),
        in_specs=[pl.BlockSpec((None, nb * nc, kw), lambda j, d: (j, 0, 0)),
                  pl.BlockSpec((None, kw, sw2), lambda j, d: (j, 0, d))],
        out_specs=pl.BlockSpec((nc, nb * sw2), lambda j, d: (0, j * 2 + d)),
        out_shape=jax.ShapeDtypeStruct((nc, S5_NB * 2 * nb * sw2), F32),
        compiler_params=pltpu.CompilerParams(vmem_limit_bytes=VMEM_LIMIT),
        name="s5_chunk_inputs",
    )(ub, pb)

    p4 = p.reshape(nc, S5_NB * 2, nb, sw2)
    s4 = pl.pallas_call(
        functools.partial(_s5_scan_kernel, n_ctx=n_ctx, n_lat=n_lat),
        grid=(S5_NB * 2,),
        in_specs=[pl.BlockSpec((nc, None, nb, sw2), lambda g: (0, g, 0, 0)),
                  pl.BlockSpec((None, 1, S5_SW), lambda g: (g, 0, 0)),
                  pl.BlockSpec((None, 1, S5_SW), lambda g: (g, 0, 0))],
        out_specs=pl.BlockSpec((nc, None, nb, sw2), lambda g: (0, g, 0, 0)),
        out_shape=jax.ShapeDtypeStruct(p4.shape, F32),
        compiler_params=pltpu.CompilerParams(vmem_limit_bytes=VMEM_LIMIT),
        name="s5_state_scan",
    )(p4, dr, di)

    s2 = s4.reshape(nc, S5_NB * 2 * nb * sw2)
    yb = pl.pallas_call(
        _s5_y_kernel,
        grid=(S5_NB, nb),
        in_specs=[pl.BlockSpec((None, nc, kw), lambda j, b: (j, b, 0)),
                  pl.BlockSpec((None, kw, kw), lambda j, b: (j, 0, 0)),
                  pl.BlockSpec((nc, sw2), lambda j, b: (0, (j * 2) * nb + b)),
                  pl.BlockSpec((nc, sw2), lambda j, b: (0, (j * 2 + 1) * nb + b)),
                  pl.BlockSpec((None, 2 * sw2, kw), lambda j, b: (j, 0, 0))],
        out_specs=pl.BlockSpec((None, nc, kw), lambda j, b: (j, b, 0)),
        out_shape=jax.ShapeDtypeStruct((S5_NB, nb * nc, kw), F32),
        compiler_params=pltpu.CompilerParams(vmem_limit_bytes=VMEM_LIMIT),
        name="s5_output",
    )(ub, tz, s2, s2, ca)
    y = yb.reshape(S5_NB, nb, nc, L, LANE).transpose(1, 2, 3, 0, 4).reshape(nb, nc * L, S5_CH)
    return y[:, :t_ctx], y[:, t_ctx:]


ROUTE_TN = 1024
MOE_TN = 2048
SEL_ROWS = 128
FF_SPLIT = 4
FFN_ROWS = 1024
AFF_BITS = 30


def _route_kernel(h_ref, sc_ref, sh_ref, wr_ref, hm_ref, afft_ref, asp_ref):
    hm = h_ref[...] * (1.0 + sc_ref[...]) + sh_ref[...]
    hm_ref[...] = hm.astype(BF16)
    logits = jnp.dot(hm, wr_ref[...], precision=lax.Precision.HIGHEST, preferred_element_type=F32)
    lane = lax.broadcasted_iota(jnp.int32, logits.shape, 1)
    logits = jnp.where(lane < N_EXPERTS, logits, -jnp.inf)
    ex = jnp.exp(logits - jnp.max(logits, axis=1, keepdims=True))
    aff = ex / jnp.sum(ex, axis=1, keepdims=True)
    afft_ref[...] = aff.T[:N_EXPERTS, :]
    hi = aff.astype(BF16).astype(F32)
    mid = (aff - hi).astype(BF16).astype(F32)
    lo = (aff - hi - mid).astype(BF16).astype(F32)
    asp = hi + pltpu.roll(mid, N_EXPERTS, 1) + pltpu.roll(lo, 2 * N_EXPERTS, 1)
    asp_ref[...] = asp.astype(BF16)


def _select_kernel(aff_ref, pos_ref, *, nblk_log2, cap):
    a = aff_ref[...]
    r = a.shape[0]
    bits = pltpu.bitcast(a, jnp.int32)
    ri = lax.broadcasted_iota(jnp.int32, (r, r), 0)
    rj = lax.broadcasted_iota(jnp.int32, (r, r), 1)
    same = (ri >> nblk_log2) == (rj >> nblk_log2)
    gm = jnp.where(same, 1.0, 0.0).astype(BF16)
    lm = jnp.where(same & (rj < ri), 1.0, 0.0).astype(BF16)
    li = lax.broadcasted_iota(jnp.int32, (LANE, LANE), 0)
    lj = lax.broadcasted_iota(jnp.int32, (LANE, LANE), 1)
    um = jnp.where(li <= lj, 1.0, 0.0).astype(BF16)

    def group_count(mask):
        rc = jnp.sum(jnp.where(mask, 1.0, 0.0), axis=1, keepdims=True)
        gc = jnp.dot(gm, jnp.broadcast_to(rc, (r, LANE)).astype(BF16), preferred_element_type=F32)
        return gc[:, :1]

    def bisect(i, thr):
        cand = thr | jnp.left_shift(1, AFF_BITS - 1 - i)
        return jnp.where(group_count(bits >= cand) >= cap, cand, thr)

    thr = lax.fori_loop(0, AFF_BITS, bisect, jnp.zeros((r, 1), jnp.int32))

    def prefix(mask):
        x = jnp.where(mask, 1.0, 0.0)
        inc = jnp.dot(x.astype(BF16), um, preferred_element_type=F32)
        tot = jnp.broadcast_to(inc[:, LANE - 1:LANE], (r, LANE)).astype(BF16)
        return inc - x + jnp.dot(lm, tot, preferred_element_type=F32)

    gt = bits > thr
    eq = bits == thr
    need = cap - group_count(gt)
    sel = gt | (eq & (prefix(eq) < need))
    pos_ref[...] = jnp.where(sel, prefix(sel), -1.0).astype(jnp.int32)


def _onehot(pos_ref, cap):
    rows = lax.broadcasted_iota(jnp.int32, (cap, LANE), 0)
    blocks = [jnp.where(pos_ref[k:k + 1, :] == rows, 1.0, 0.0).astype(BF16) for k in range(pos_ref.shape[0])]
    return jnp.concatenate(blocks, axis=1)


def _ffn_kernel(pos_ref, hb_ref, asp_ref, wg_ref, wu_ref, wd_ref, ys_ref, xs_acc, g_acc, *, cap, nkt):
    e = pl.program_id(0)
    s = pl.program_id(2)

    @pl.when(s == 0)
    def _():
        xs_acc[...] = jnp.zeros_like(xs_acc)
        g_acc[...] = jnp.zeros_like(g_acc)

    rows = pl.ds(pl.multiple_of((s // nkt) * cap, cap), cap)
    oh = _onehot(pos_ref, cap)
    xs_acc[rows, :] += jnp.dot(oh, hb_ref[...], preferred_element_type=F32)
    g_acc[rows, :] += jnp.dot(oh, asp_ref[...], preferred_element_type=F32)

    @pl.when(s == pl.num_programs(2) - 1)
    def _():
        xs = xs_acc[...].astype(BF16)
        g = g_acc[...]
        lane = lax.broadcasted_iota(jnp.int32, g.shape, 1)
        gate = jnp.sum(jnp.where((lane & (N_EXPERTS - 1)) == e, g, 0.0), axis=1, keepdims=True)
        fw = EXPERT_FF // FF_SPLIT
        y = jnp.zeros(xs.shape, F32)
        for f in range(FF_SPLIT):
            hg = jnp.dot(xs, wg_ref[:, f * fw:(f + 1) * fw], preferred_element_type=F32)
            hu = jnp.dot(xs, wu_ref[:, f * fw:(f + 1) * fw], preferred_element_type=F32)
            hid = (hg * jax.nn.sigmoid(hg)) * hu
            y += jnp.dot(hid.astype(BF16), wd_ref[f * fw:(f + 1) * fw, :], preferred_element_type=F32)
        y = (y * gate).astype(BF16)
        for i in range(ys_ref.shape[0]):
            ys_ref[i] = y[i * cap:(i + 1) * cap]


def _ln_rows(z, g, b):
    mu = jnp.mean(z, axis=-1, keepdims=True)
    zc = z - mu
    var = jnp.mean(zc * zc, axis=-1, keepdims=True)
    return zc * lax.rsqrt(var + EPS) * g + b


def _combine_kernel(pos_ref, ys_ref, h_ref, g2_ref, lng_ref, lnb_ref, o_ref, acc, *, cap):
    e = pl.program_id(2)

    @pl.when(e == 0)
    def _():
        acc[...] = jnp.zeros_like(acc)

    oh = _onehot(pos_ref, cap)
    acc[...] += lax.dot_general(oh, ys_ref[...], (((0,), (0,)), ((), ())), preferred_element_type=F32)

    @pl.when(e == pl.num_programs(2) - 1)
    def _():
        z = ALPHA * h_ref[...] + g2_ref[...] * acc[...]
        o_ref[...] = _ln_rows(z, lng_ref[...], lnb_ref[...])


def moe_block(h, sc, sh, g2, ln_g, ln_b, w_router, wg, wu, wd, layer):
    nb, n, dm = h.shape
    cap = CAPACITY_FACTOR * n // N_EXPERTS
    nblk = n // LANE
    assert n % LANE == 0 and nblk & (nblk - 1) == 0
    tn_r = min(ROUTE_TN, n)
    tn = min(MOE_TN, n)
    wr = jnp.pad(w_router, ((0, 0), (0, LANE - N_EXPERTS)))
    row = lambda v: v.reshape(1, dm)

    hm, afft, asp = pl.pallas_call(
        _route_kernel,
        grid=(nb, n // tn_r),
        in_specs=[pl.BlockSpec((None, tn_r, dm), lambda b, t: (b, t, 0)),
                  pl.BlockSpec((None, 1, dm), lambda b, t: (b, 0, 0)),
                  pl.BlockSpec((None, 1, dm), lambda b, t: (b, 0, 0)),
                  pl.BlockSpec((dm, LANE), lambda b, t: (0, 0))],
        out_specs=[pl.BlockSpec((None, tn_r, dm), lambda b, t: (b, t, 0)),
                   pl.BlockSpec((None, N_EXPERTS, tn_r), lambda b, t: (b, 0, t)),
                   pl.BlockSpec((None, tn_r, LANE), lambda b, t: (b, t, 0))],
        out_shape=[jax.ShapeDtypeStruct((nb, n, dm), BF16),
                   jax.ShapeDtypeStruct((nb, N_EXPERTS, n), F32),
                   jax.ShapeDtypeStruct((nb, n, LANE), BF16)],
        name="moe_route",
    )(h, sc, sh, wr)

    rows_total = nb * N_EXPERTS * nblk
    rb = max(SEL_ROWS, N_EXPERTS * nblk)
    assert rows_total % rb == 0
    pos = pl.pallas_call(
        functools.partial(_select_kernel, nblk_log2=nblk.bit_length() - 1, cap=cap),
        grid=(rows_total // rb,),
        in_specs=[pl.BlockSpec((rb, LANE), lambda i: (i, 0))],
        out_specs=pl.BlockSpec((rb, LANE), lambda i: (i, 0)),
        out_shape=jax.ShapeDtypeStruct((rows_total, LANE), jnp.int32),
        name="moe_select",
    )(afft.reshape(rows_total, LANE))
    pos = pos.reshape(nb, N_EXPERTS, nblk, LANE)

    tb = tn // LANE
    nkt = n // tn
    group = min(nb, max(1, FFN_ROWS // cap))
    assert nb % group == 0
    sample = lambda bg, s: bg * group + s // nkt
    ys = pl.pallas_call(
        functools.partial(_ffn_kernel, cap=cap, nkt=nkt),
        grid=(N_EXPERTS, nb // group, group * nkt),
        in_specs=[pl.BlockSpec((None, None, tb, LANE), lambda e, bg, s: (sample(bg, s), e, s % nkt, 0)),
                  pl.BlockSpec((None, tn, dm), lambda e, bg, s: (sample(bg, s), s % nkt, 0)),
                  pl.BlockSpec((None, tn, LANE), lambda e, bg, s: (sample(bg, s), s % nkt, 0)),
                  pl.BlockSpec((None, None, dm, EXPERT_FF), lambda e, bg, s: (layer, e, 0, 0)),
                  pl.BlockSpec((None, None, dm, EXPERT_FF), lambda e, bg, s: (layer, e, 0, 0)),
                  pl.BlockSpec((None, None, EXPERT_FF, dm), lambda e, bg, s: (layer, e, 0, 0))],
        out_specs=pl.BlockSpec((group, None, cap, dm), lambda e, bg, s: (bg, e, 0, 0)),
        out_shape=jax.ShapeDtypeStruct((nb, N_EXPERTS, cap, dm), BF16),
        scratch_shapes=[pltpu.VMEM((group * cap, dm), F32), pltpu.VMEM((group * cap, LANE), F32)],
        compiler_params=pltpu.CompilerParams(vmem_limit_bytes=VMEM_LIMIT),
        name="moe_ffn",
    )(pos, hm, asp, wg, wu, wd)

    return pl.pallas_call(
        functools.partial(_combine_kernel, cap=cap),
        grid=(nb, n // tn, N_EXPERTS),
        in_specs=[pl.BlockSpec((None, None, tb, LANE), lambda b, t, e: (b, e, t, 0)),
                  pl.BlockSpec((None, None, cap, dm), lambda b, t, e: (b, e, 0, 0)),
                  pl.BlockSpec((None, tn, dm), lambda b, t, e: (b, t, 0)),
                  pl.BlockSpec((None, 1, dm), lambda b, t, e: (b, 0, 0)),
                  pl.BlockSpec((1, dm), lambda b, t, e: (0, 0)),
                  pl.BlockSpec((1, dm), lambda b, t, e: (0, 0))],
        out_specs=pl.BlockSpec((None, tn, dm), lambda b, t, e: (b, t, 0)),
        out_shape=jax.ShapeDtypeStruct((nb, n, dm), F32),
        scratch_shapes=[pltpu.VMEM((tn, dm), F32)],
        compiler_params=pltpu.CompilerParams(vmem_limit_bytes=VMEM_LIMIT),
        name="moe_combine",
    )(pos, ys, h, g2, row(ln_g), row(ln_b))


N_HEADS = 4
STACK = N_HEADS * CHUNK
CHUNK_LOG2 = CHUNK.bit_length() - 1


def _stack_heads(x):
    return jnp.concatenate([x[:, h * HEAD_DIM:(h + 1) * HEAD_DIM] for h in range(N_HEADS)], axis=0)


def _stack_cols(cols, width):
    return jnp.concatenate([jnp.broadcast_to(c, (CHUNK, width)) for c in cols], axis=0)


def _mxu(a, b, dims, passes):
    dg = lambda x, y: lax.dot_general(x, y, (dims, ((), ())), preferred_element_type=F32)
    a_hi, b_hi = a.astype(BF16), b.astype(BF16)
    out = dg(a_hi, b_hi)
    if passes >= 2:
        out = out + dg((a - a_hi.astype(F32)).astype(BF16), b_hi)
    if passes >= 3:
        out = out + dg(a_hi, (b - b_hi.astype(F32)).astype(BF16))
    return out


def _dot_nt(a, b, passes=1):
    return _mxu(a, b, ((1,), (1,)), passes)


def _dot_tn(a, b, passes=1):
    return _mxu(a, b, ((0,), (0,)), passes)


def _dot(a, b, passes=1):
    return _mxu(a, b, ((1,), (0,)), passes)


def _chunk_masks(rev):
    ri = lax.broadcasted_iota(jnp.int32, (STACK, STACK), 0)
    ci = lax.broadcasted_iota(jnp.int32, (STACK, STACK), 1)
    same = (ri >> CHUNK_LOG2) == (ci >> CHUNK_LOG2)
    ahead = jnp.where(rev, ci - ri, ri - ci)
    return same & (ahead >= 0), same & (ahead > 0)


def _scan_cumsum(gt, rev):
    ii = lax.broadcasted_iota(jnp.int32, (CHUNK, CHUNK), 0)
    jj = lax.broadcasted_iota(jnp.int32, (CHUNK, CHUNK), 1)
    tri = jnp.where(jnp.where(rev, jj - ii, ii - jj) >= 0, 1.0, 0.0)
    return jnp.dot(tri, gt, precision=lax.Precision.HIGHEST, preferred_element_type=F32)


def _unit_triangular_inverse(a):
    ri = lax.broadcasted_iota(jnp.int32, a.shape, 0)
    ci = lax.broadcasted_iota(jnp.int32, a.shape, 1)
    joins = lambda lvl: ((ri >> (lvl + 1)) == (ci >> (lvl + 1))) & ((ri >> lvl) != (ci >> lvl))
    t = jnp.where(ri == ci, 1.0, 0.0) - jnp.where(joins(0), a, 0.0)
    for lvl in range(1, CHUNK_LOG2):
        m = _dot(jnp.where(joins(lvl), a, 0.0), t)
        yield
        t = t - _dot(t, m)
        yield
    return t


def _interleave(chains):
    for _ in zip(*chains):
        pass


def _linear_scan_kernel(*refs, has_beta, k_scale):
    if has_beta:
        q_ref, k_ref, v_ref, gt_ref, gp_ref, s0_ref, o_ref, sfin_ref, s_scr = refs
    else:
        q_ref, k_ref, v_ref, gp_ref, s0_ref, o_ref, sfin_ref, s_scr = refs
    rev = pl.program_id(0) == 1
    n = pl.program_id(2)

    @pl.when(n == 0)
    def _():
        s_scr[...] = s0_ref[...]

    chains = []
    for i in range(q_ref.shape[0]):
        if has_beta:
            gt = _gdn_gates(gt_ref[i], gp_ref[...])
        else:
            gt = jnp.broadcast_to(gp_ref[0:1, :], (CHUNK, LANE))
        chains.append(_linear_chunk(q_ref.at[i], k_ref.at[i], v_ref.at[i], gt, o_ref.at[i], s_scr.at[i], rev,
                                    has_beta, k_scale))
    _interleave(chains)

    @pl.when(n == pl.num_programs(2) - 1)
    def _():
        sfin_ref[...] = s_scr[...]


def _softplus(x):
    return jnp.maximum(x, 0.0) + jnp.log1p(jnp.exp(-jnp.abs(x)))


def _gdn_gates(raw, gp):
    lane = lax.broadcasted_iota(jnp.int32, raw.shape, 1)
    return jnp.where(lane < 2 * N_HEADS, gp[0:1, :] * _softplus(raw + gp[1:2, :]), jax.nn.sigmoid(raw))


def _mlstm_gates(raw, gp):
    lane = lax.broadcasted_iota(jnp.int32, raw.shape, 1)
    x = raw + gp[0:1, :]
    return jnp.where(lane < 2 * N_HEADS, x, -_softplus(-x))


def _linear_chunk(q_ref, k_ref, v_ref, gt, o_ref, s_scr, rev, has_beta, k_scale):
    cum = _scan_cumsum(gt, rev)
    gcols, bcols, gtots = [], [], []
    for h in range(N_HEADS):
        gc = jnp.where(rev, cum[:, N_HEADS + h:N_HEADS + h + 1], cum[:, h:h + 1])
        gcols.append(gc)
        gtots.append(jnp.where(rev, gc[0:1], gc[CHUNK - 1:CHUNK]))
        bcols.append(jnp.where(rev, gt[:, 3 * N_HEADS + h:3 * N_HEADS + h + 1], gt[:, 2 * N_HEADS + h:2 * N_HEADS + h + 1]))
    cb = _stack_cols(gcols, STACK)
    diff = cb - cb.T
    incl, strict = _chunk_masks(rev)
    dec = jnp.exp(jnp.where(incl, diff, 0.0))
    gcb = cb[:, :HEAD_DIM]
    q_st, k_st, v_st = _stack_heads(q_ref[...]), _stack_heads(k_ref[...]) * k_scale, _stack_heads(v_ref[...])
    a_qk = _dot_nt(q_st, k_st) * jnp.where(incl, dec, 0.0)
    if has_beta:
        beta = _stack_cols(bcols, HEAD_DIM)
        kb = k_st * beta
        a = _dot_nt(kb, k_st) * jnp.where(strict, dec, 0.0)
        yield
        t_inv = yield from _unit_triangular_inverse(a)
        x = _dot(t_inv, jnp.concatenate([v_st * beta, kb * jnp.exp(gcb)], axis=1))
        u_st, w_st = x[:, :HEAD_DIM], x[:, HEAD_DIM:]
    else:
        u_st, w_st = v_st, None
    gtot = _stack_cols(gtots, HEAD_DIM)
    k_end = k_st * jnp.exp(gtot - gcb)
    q_dec = q_st * jnp.exp(gcb)
    hs = lambda t, h: t[h * CHUNK:(h + 1) * CHUNK]
    states = [s_scr[h] for h in range(N_HEADS)]
    q_s = [_dot(hs(q_dec, h), states[h]) for h in range(N_HEADS)]
    yield
    if has_beta:
        vn = jnp.concatenate([hs(u_st, h) - _dot(hs(w_st, h), states[h]) for h in range(N_HEADS)], axis=0)
    else:
        vn = u_st
    yield
    o_st = _dot(a_qk, vn)
    for h in range(N_HEADS):
        s_scr[h] = jnp.exp(gtots[h]) * states[h] + _dot_tn(hs(k_end, h), hs(vn, h))
    yield
    for h in range(N_HEADS):
        o_ref[:, h * HEAD_DIM:(h + 1) * HEAD_DIM] = hs(o_st, h) + q_s[h]
    yield


SCAN_NS = 4


HEADS_W = N_HEADS * HEAD_DIM


def _scan_specs(srcs, gates, nb, nchunk):
    ns = SCAN_NS
    assert nb % ns == 0
    cidx = lambda d, n: n + d * (nchunk - 1 - 2 * n)
    specs = [pl.BlockSpec((ns, CHUNK, HEADS_W), lambda d, b, n, c=c: (b, cidx(d, n), c)) for _, c in srcs]
    if gates is not None:
        specs.append(pl.BlockSpec((ns, CHUNK, LANE), lambda d, b, n, c=gates[1]: (b, cidx(d, n), c)))
    specs.append(pl.BlockSpec((2, LANE), lambda d, b, n: (0, 0)))
    ospec = pl.BlockSpec((None, ns, CHUNK, HEADS_W), lambda d, b, n: (d, b, cidx(d, n), 0))
    return specs, ospec


def linear_scan_bidir(q, k, v, gates, gp, s0, has_beta, k_scale=1.0):
    nb, t, _ = q[0].shape
    nchunk = t // CHUNK
    assert t % CHUNK == 0
    ns = SCAN_NS
    specs, ospec = _scan_specs((q, k, v), gates if has_beta else None, nb, nchunk)
    sspec = pl.BlockSpec((None, ns, N_HEADS, HEAD_DIM, HEAD_DIM), lambda d, b, n: (d, b, 0, 0, 0))
    args = [q[0], k[0], v[0]] + ([gates[0]] if has_beta else []) + [gp, s0]
    return pl.pallas_call(
        functools.partial(_linear_scan_kernel, has_beta=has_beta, k_scale=k_scale),
        grid=(2, nb // ns, nchunk),
        in_specs=specs + [sspec],
        out_specs=[ospec, sspec],
        out_shape=[jax.ShapeDtypeStruct((2, nb, t, HEADS_W), F32),
                   jax.ShapeDtypeStruct((2, nb, N_HEADS, HEAD_DIM, HEAD_DIM), F32)],
        scratch_shapes=[pltpu.VMEM((ns, N_HEADS, HEAD_DIM, HEAD_DIM), F32)],
        name="gdn_scan" if has_beta else "retention_scan",
    )(*args)


def linear_scan_two_pass(ctx_args, lat_args, gp, has_beta, k_scale=1.0):
    nb = ctx_args[0][0].shape[0]
    zero = jnp.zeros((2, nb, N_HEADS, HEAD_DIM, HEAD_DIM), F32)
    o_ctx, s_ctx = linear_scan_bidir(*ctx_args, gp, zero, has_beta, k_scale)
    o_lat, _ = linear_scan_bidir(*lat_args, gp, s_ctx, has_beta, k_scale)
    return o_ctx, o_lat


MLSTM_PASSES = 3


def _mlstm_scan_kernel(q_ref, k_ref, v_ref, gt_ref, gp_ref, c0_ref, n0_ref, m0_ref, o_ref, cfin_ref, nfin_ref,
                       mfin_ref, c_scr, n_scr, m_scr):
    rev = pl.program_id(0) == 1
    step = pl.program_id(2)

    @pl.when(step == 0)
    def _():
        c_scr[...] = c0_ref[...]
        n_scr[...] = n0_ref[...]
        m_scr[...] = m0_ref[...]

    _interleave([_mlstm_chunk(q_ref.at[i], k_ref.at[i], v_ref.at[i], _mlstm_gates(gt_ref[i], gp_ref[...]),
                              o_ref.at[i], c_scr.at[i], n_scr.at[i], m_scr.at[i], rev)
                 for i in range(q_ref.shape[0])])

    @pl.when(step == pl.num_programs(2) - 1)
    def _():
        cfin_ref[...] = c_scr[...]
        nfin_ref[...] = n_scr[...]
        mfin_ref[...] = m_scr[...]


def _mlstm_chunk(q_ref, k_ref, v_ref, gt, o_ref, c_scr, n_scr, m_scr, rev):
    cum = _scan_cumsum(gt, rev)
    pick = lambda t, c: jnp.where(rev, t[:, N_HEADS + c:N_HEADS + c + 1], t[:, c:c + 1])
    q_st, k_st, v_st = _stack_heads(q_ref[...]), _stack_heads(k_ref[...]), _stack_heads(v_ref[...])
    hs = lambda t, h: t[h * CHUNK:(h + 1) * CHUNK]
    qk = _dot_nt(q_st, k_st, MLSTM_PASSES)
    yield
    bcums, srcs, inters, qcs, qns = [], [], [], [], []
    for h in range(N_HEADS):
        ic = pick(gt, h)
        bcum = pick(cum, 2 * N_HEADS + h)
        b_end = jnp.where(rev, bcum[0:1], bcum[CHUNK - 1:CHUNK])
        c_prev, n_prev, m_prev = c_scr[h], n_scr[h], m_scr[h][:, :1]
        a = b_end - bcum + ic
        m_new = jnp.maximum(b_end + m_prev, jnp.max(a, axis=0, keepdims=True))
        w_state = jnp.exp(a - m_new)
        decay = jnp.exp(b_end + m_prev - m_new)
        kw = hs(k_st, h) * w_state
        c_scr[h] = decay * c_prev + _dot_tn(kw, hs(v_st, h), MLSTM_PASSES)
        n_scr[h] = decay * n_prev + jnp.sum(kw, axis=0, keepdims=True)
        m_scr[h] = jnp.broadcast_to(m_new, (1, HEAD_DIM))
        bcums.append(bcum)
        srcs.append(bcum - ic)
        inters.append(bcum + m_prev)
        qcs.append(_dot(hs(q_st, h), c_prev, MLSTM_PASSES))
        qns.append(jnp.sum(hs(q_st, h) * n_prev, axis=1, keepdims=True))
    yield
    incl, _ = _chunk_masks(rev)
    dlog = _stack_cols(bcums, STACK) - _stack_cols(srcs, STACK).T
    inter = jnp.concatenate(inters, axis=0)
    m_t = jnp.maximum(inter, jnp.max(jnp.where(incl, dlog, MASKED), axis=1, keepdims=True))
    s = qk * jnp.where(incl, jnp.exp(jnp.where(incl, dlog, 0.0) - m_t), 0.0)
    w_inter = jnp.exp(inter - m_t)
    yield
    num = _dot(s, v_st, MLSTM_PASSES) + w_inter * jnp.concatenate(qcs, axis=0)
    den = jnp.sum(s, axis=1, keepdims=True) + w_inter * jnp.concatenate(qns, axis=0)
    yield
    out = num / jnp.maximum(jnp.abs(den), jnp.exp(-m_t))
    for h in range(N_HEADS):
        o_ref[:, h * HEAD_DIM:(h + 1) * HEAD_DIM] = hs(out, h)
    yield


def mlstm_scan_bidir(q, k, v, gates, gp, state):
    nb, t, _ = q[0].shape
    nchunk = t // CHUNK
    assert t % CHUNK == 0
    ns = SCAN_NS
    specs, ospec = _scan_specs((q, k, v), gates, nb, nchunk)
    cspec = pl.BlockSpec((None, ns, N_HEADS, HEAD_DIM, HEAD_DIM), lambda d, b, n: (d, b, 0, 0, 0))
    vspec = pl.BlockSpec((None, ns, N_HEADS, 1, HEAD_DIM), lambda d, b, n: (d, b, 0, 0, 0))
    cshape = jax.ShapeDtypeStruct((2, nb, N_HEADS, HEAD_DIM, HEAD_DIM), F32)
    vshape = jax.ShapeDtypeStruct((2, nb, N_HEADS, 1, HEAD_DIM), F32)
    o, c, n, m = pl.pallas_call(
        _mlstm_scan_kernel,
        grid=(2, nb // ns, nchunk),
        in_specs=specs + [cspec, vspec, vspec],
        out_specs=[ospec, cspec, vspec, vspec],
        out_shape=[jax.ShapeDtypeStruct((2, nb, t, HEADS_W), F32), cshape, vshape, vshape],
        scratch_shapes=[pltpu.VMEM((ns, N_HEADS, HEAD_DIM, HEAD_DIM), F32), pltpu.VMEM((ns, N_HEADS, 1, HEAD_DIM), F32),
                        pltpu.VMEM((ns, N_HEADS, 1, HEAD_DIM), F32)],
        name="mlstm_scan",
    )(q[0], k[0], v[0], gates[0], gp, *state)
    return o, (c, n, m)


def mlstm_two_pass(ctx_args, lat_args, gp):
    nb = ctx_args[0][0].shape[0]
    zero = (jnp.zeros((2, nb, N_HEADS, HEAD_DIM, HEAD_DIM), F32), jnp.zeros((2, nb, N_HEADS, 1, HEAD_DIM), F32),
            jnp.zeros((2, nb, N_HEADS, 1, HEAD_DIM), F32))
    o_ctx, s_ctx = mlstm_scan_bidir(*ctx_args, gp, zero)
    o_lat, _ = mlstm_scan_bidir(*lat_args, gp, s_ctx)
    return o_ctx, o_lat


PROJ_TM = 1024
PROJ_TN_MAX = 2304


def _inproj_kernel(h_ref, sc_ref, sh_ref, w_ref, o_ref):
    hm = (h_ref[...] * (1.0 + sc_ref[...]) + sh_ref[...]).astype(BF16)
    o_ref[...] = jnp.dot(hm, w_ref[...], preferred_element_type=F32)


def in_projection(h, sc, sh, w):
    nb, n, dm = h.shape
    ncol = w.shape[1]
    tm = min(PROJ_TM, n)
    tn = max(c for c in range(LANE, min(ncol, PROJ_TN_MAX) + 1, LANE) if ncol % c == 0)
    return pl.pallas_call(
        _inproj_kernel,
        grid=(nb, n // tm, ncol // tn),
        in_specs=[pl.BlockSpec((None, tm, dm), lambda b, t, j: (b, t, 0)),
                  pl.BlockSpec((None, 1, dm), lambda b, t, j: (b, 0, 0)),
                  pl.BlockSpec((None, 1, dm), lambda b, t, j: (b, 0, 0)),
                  pl.BlockSpec((dm, tn), lambda b, t, j: (0, j))],
        out_specs=pl.BlockSpec((None, tm, tn), lambda b, t, j: (b, t, j)),
        out_shape=jax.ShapeDtypeStruct((nb, n, ncol), F32),
        name="in_projection",
    )(h, sc, sh, w)


def _conv_kernel(x_ref, w_ref, o_ref, *, grid_w, l2_blocks, q_scale, k_scale):
    x = x_ref[...]
    t = x.shape[0]
    w = w_ref[...]
    col = lax.broadcasted_iota(jnp.int32, x.shape, 0) & (grid_w - 1)
    left = jnp.where(col == 0, 0.0, pltpu.roll(x, 1, 0))
    right = jnp.where(col == grid_w - 1, 0.0, pltpu.roll(x, t - 1, 0))
    row = lambda kh: w[3 * kh:3 * kh + 1] * left + w[3 * kh + 1:3 * kh + 2] * x + w[3 * kh + 2:3 * kh + 3] * right
    acc = row(1)
    if grid_w < t:
        zero = jnp.zeros((grid_w, LANE), F32)
        acc = acc + jnp.concatenate([zero, row(0)[:t - grid_w]], axis=0) + jnp.concatenate([row(2)[grid_w:], zero], axis=0)
    y = acc * jax.nn.sigmoid(acc)
    c = pl.program_id(1)
    normed = y * lax.rsqrt(jnp.sum(y * y, axis=1, keepdims=True) + L2_EPS)
    y = jnp.where(c < l2_blocks, normed, y)
    o_ref[...] = y * jnp.where(c < N_HEADS, q_scale, jnp.where(c < 2 * N_HEADS, k_scale, 1.0))


def conv_prep(proj, conv_w, nblk, grid_w, l2_blocks, q_scale, k_scale):
    nb, t, _ = proj.shape
    assert grid_w & (grid_w - 1) == 0 and t % grid_w == 0
    return pl.pallas_call(
        functools.partial(_conv_kernel, grid_w=grid_w, l2_blocks=l2_blocks, q_scale=q_scale, k_scale=k_scale),
        grid=(nb, nblk),
        in_specs=[pl.BlockSpec((None, t, LANE), lambda b, c: (b, 0, c)),
                  pl.BlockSpec((CONV_K * CONV_K, LANE), lambda b, c: (0, c))],
        out_specs=pl.BlockSpec((None, t, LANE), lambda b, c: (b, 0, c)),
        out_shape=jax.ShapeDtypeStruct((nb, t, nblk * LANE), F32),
        compiler_params=pltpu.CompilerParams(vmem_limit_bytes=VMEM_LIMIT),
        name="conv_prep",
    )(proj, conv_w.reshape(CONV_K * CONV_K, -1))


def _head_norm(o, center):
    outs = []
    for h in range(N_HEADS):
        x = o[:, h * HEAD_DIM:(h + 1) * HEAD_DIM]
        if center:
            x = x - jnp.mean(x, axis=1, keepdims=True)
        outs.append(x * lax.rsqrt(jnp.mean(x * x, axis=1, keepdims=True) + EPS))
    return jnp.concatenate(outs, axis=1)


def _mix_out(y_a, y_b, wo_ref, h_ref, g1_ref, lng_ref, lnb_ref, o_ref):
    y = (jnp.dot(y_a.astype(BF16), wo_ref[:HEADS_W, :], preferred_element_type=F32)
         + jnp.dot(y_b.astype(BF16), wo_ref[HEADS_W:, :], preferred_element_type=F32))
    o_ref[...] = _ln_rows(ALPHA * h_ref[...] + g1_ref[...] * y, lng_ref[...], lnb_ref[...])


def _merge_even_kernel(og_ref, or_ref, za_ref, zr_ref, gg_ref, rg_ref, wo_ref, h_ref, g1_ref, lng_ref, lnb_ref, o_ref):
    za, zr = za_ref[...], zr_ref[...]
    y_g = _head_norm(og_ref[0] + og_ref[1], False) * gg_ref[...] * (za * jax.nn.sigmoid(za))
    y_r = _head_norm(or_ref[0] + or_ref[1], True) * rg_ref[...] * (zr * jax.nn.sigmoid(zr))
    _mix_out(y_g, y_r, wo_ref, h_ref, g1_ref, lng_ref, lnb_ref, o_ref)


def _merge_odd_kernel(om_ref, ys_ref, og_ref, u_ref, mg_ref, dsk_ref, bglu_ref, wglu_ref, wo_ref, h_ref, g1_ref,
                      lng_ref, lnb_ref, o_ref):
    y_m = _head_norm(om_ref[0] + om_ref[1], True) * mg_ref[...] * jax.nn.sigmoid(og_ref[...])
    y = jax.nn.gelu(ys_ref[...] + dsk_ref[...] * u_ref[...])
    y = y * jax.nn.sigmoid(jnp.dot(y.astype(BF16), wglu_ref[...], preferred_element_type=F32) + bglu_ref[...])
    _mix_out(y_m, y, wo_ref, h_ref, g1_ref, lng_ref, lnb_ref, o_ref)


def _merge_call(kernel_fn, name, scans, toks, rows, mats, h, g1, ln_g, ln_b):
    nb, n, dm = h.shape
    tm = min(PROJ_TM, n)
    full = lambda a: pl.BlockSpec(a.shape, lambda b, t: (0,) * a.ndim)
    rowv = lambda v: v.reshape(1, -1)
    specs, args = [], []
    for a in scans:
        specs.append(pl.BlockSpec((2, None, tm, HEADS_W), lambda b, t: (0, b, t, 0)))
        args.append(a)
    for a, c in toks:
        specs.append(pl.BlockSpec((None, tm, HEADS_W), lambda b, t, c=c: (b, t, c)))
        args.append(a)
    for v in rows:
        args.append(rowv(v))
        specs.append(full(args[-1]))
    for m in mats:
        args.append(m)
        specs.append(full(m))
    args += [h, g1, rowv(ln_g), rowv(ln_b)]
    specs += [pl.BlockSpec((None, tm, dm), lambda b, t: (b, t, 0)), pl.BlockSpec((None, 1, dm), lambda b, t: (b, 0, 0)),
              full(args[-2]), full(args[-1])]
    return pl.pallas_call(
        kernel_fn,
        grid=(nb, n // tm),
        in_specs=specs,
        out_specs=pl.BlockSpec((None, tm, dm), lambda b, t: (b, t, 0)),
        out_shape=jax.ShapeDtypeStruct((nb, n, dm), F32),
        compiler_params=pltpu.CompilerParams(vmem_limit_bytes=VMEM_LIMIT),
        name=name,
    )(*args)


def retention_log_decay(direction):
    expo = 5.0 + 2.0 * jnp.arange(RET_HEADS, dtype=F32) + direction
    return jnp.log1p(-jnp.exp2(-expo))


GATE_COLS = 4 * N_HEADS
Q_SCALE = HEAD_DIM ** -0.5


def _gate_row(*vals):
    v = jnp.concatenate([jnp.ravel(x) for x in vals])
    return jnp.pad(v, (0, LANE - v.shape[0]))


def _pad_gate_cols(w):
    return jnp.pad(w, ((0, 0), (0, LANE - w.shape[1])))


def gdn_retention_mixer(hs_, mods, w_in, w_out, conv_w, a_log, dt_bias, gdn_gain, ret_gain, ln_g, ln_b, ctx_out):
    w = jnp.concatenate([w_in[:, :4 * GDN_W], w_in[:, 4 * GDN_W + GATE_COLS:],
                         _pad_gate_cols(w_in[:, 4 * GDN_W:4 * GDN_W + GATE_COLS])], axis=1).astype(BF16)
    gate_blk = (4 * GDN_W + 4 * RET_W) // LANE
    gp_gdn = jnp.stack([_gate_row(-jnp.exp(a_log)), _gate_row(dt_bias)])
    gp_ret = jnp.stack([_gate_row(retention_log_decay(0), retention_log_decay(1)), jnp.zeros((LANE,), F32)])
    wo = w_out.astype(BF16)
    projs, convs = [], []
    for (h, (sc, sh, _)), grid_w in zip(zip(hs_, mods), (hs_[0].shape[1], GRID_W)):
        p = in_projection(h, sc, sh, w)
        projs.append(p)
        convs.append(conv_prep(p, conv_w, 3 * GDN_W // LANE, grid_w, 2 * N_HEADS, Q_SCALE, 1.0))
    gdn_args = [((cv, 0), (cv, 1), (cv, 2), (p, gate_blk)) for p, cv in zip(projs, convs)]
    ret_args = [((p, 4), (p, 5), (p, 6), None) for p in projs]
    og = linear_scan_two_pass(gdn_args[0], gdn_args[1], gp_gdn, True)
    orr = linear_scan_two_pass(ret_args[0], ret_args[1], gp_ret, False, Q_SCALE)
    outs = []
    for i in range(2):
        if i == 0 and not ctx_out:
            outs.append(None)
            continue
        outs.append(_merge_call(_merge_even_kernel, "merge_even", [og[i], orr[i]], [(projs[i], 3), (projs[i], 7)],
                                [jnp.tile(gdn_gain, N_HEADS), ret_gain], [wo], hs_[i], mods[i][2], ln_g, ln_b))
    return outs


def mlstm_s5_mixer(hs_, mods, w_in, w_out, conv_w, gate_bias, mlstm_gain, lam_re, lam_im, log_dt,
                   b_re, b_im, c_re, c_im, d_skip, w_glu, b_glu, ln_g, ln_b, ctx_out):
    w = jnp.concatenate([w_in[:, :4 * MLSTM_W], _pad_gate_cols(w_in[:, 4 * MLSTM_W:4 * MLSTM_W + GATE_COLS])],
                        axis=1).astype(BF16)
    w_u = w_in[:, 4 * MLSTM_W + GATE_COLS:].astype(BF16)
    gate_blk = 4 * MLSTM_W // LANE
    gp = jnp.stack([_gate_row(gate_bias[0, 0], gate_bias[1, 0], gate_bias[0, 1], gate_bias[1, 1]),
                    jnp.zeros((LANE,), F32)])
    wo, wglu = w_out.astype(BF16), w_glu.astype(BF16)
    projs, us, args = [], [], []
    for (h, (sc, sh, _)), grid_w in zip(zip(hs_, mods), (hs_[0].shape[1], GRID_W)):
        p = in_projection(h, sc, sh, w)
        cv = conv_prep(p, conv_w, 2 * MLSTM_W // LANE, grid_w, 0, 1.0, Q_SCALE)
        projs.append(p)
        us.append(in_projection(h, sc, sh, w_u))
        args.append(((cv, 0), (cv, 1), (p, 2), (p, gate_blk)))
    om = mlstm_two_pass(args[0], args[1], gp)
    ys = s5_bidirectional(us[0], us[1], _s5_weights(lam_re, lam_im, log_dt, b_re, b_im, c_re, c_im))
    outs = []
    for i in range(2):
        if i == 0 and not ctx_out:
            outs.append(None)
            continue
        outs.append(_merge_call(_merge_odd_kernel, "merge_odd", [om[i]], [(ys[i], 0), (projs[i], 3), (us[i], 0)],
                                [mlstm_gain, d_skip, b_glu], [wglu, wo], hs_[i], mods[i][2], ln_g, ln_b))
    return outs


def kernel(x, c, ctx, c_ctx, w_mod, b_mod, ln1_g, ln1_b, ln2_g, ln2_b, w_router, w_gate, w_up, w_down,
           ev_w_in, ev_w_out, ev_conv, ev_a_log, ev_dt_bias, ev_gdn_norm, ev_ret_norm,
           od_w_in, od_w_out, od_conv, od_gate_bias, od_mlstm_norm, od_lam_re, od_lam_im, od_log_dt,
           od_b_re, od_b_im, od_c_re, od_c_im, od_d_skip, od_w_glu, od_b_glu):
    h_lat, h_ctx = x, ctx
    s_lat = jax.nn.silu(c)
    s_ctx = jax.nn.silu(c_ctx)
    experts = (w_gate.astype(BF16), w_up.astype(BF16), w_down.astype(BF16))
    for l in range(DEPTH):
        last = l == DEPTH - 1
        sh1, sc1, g1, sh2, sc2, g2 = jnp.split((s_lat @ w_mod[l] + b_mod[l])[:, None, :], 6, axis=-1)
        bc = lambda v: jnp.broadcast_to(v, (h_ctx.shape[0], 1, D_MODEL))
        csh1, csc1, cg1, csh2, csc2, cg2 = [bc(v) for v in jnp.split(s_ctx @ w_mod[l] + b_mod[l], 6, axis=-1)]
        streams = (h_ctx, h_lat)
        mods = ((csc1, csh1, cg1), (sc1, sh1, g1))
        if l % 2 == 0:
            e = l // 2
            h_ctx, h_lat = gdn_retention_mixer(streams, mods, ev_w_in[e], ev_w_out[e], ev_conv[e], ev_a_log[e],
                                               ev_dt_bias[e], ev_gdn_norm[e], ev_ret_norm[e], ln1_g[l], ln1_b[l],
                                               not last)
        else:
            o = l // 2
            h_ctx, h_lat = mlstm_s5_mixer(streams, mods, od_w_in[o], od_w_out[o], od_conv[o], od_gate_bias[o],
                                          od_mlstm_norm[o], od_lam_re[o], od_lam_im[o], od_log_dt[o],
                                          od_b_re[o], od_b_im[o], od_c_re[o], od_c_im[o], od_d_skip[o],
                                          od_w_glu[o], od_b_glu[o], ln1_g[l], ln1_b[l], not last)
        h_lat = moe_block(h_lat, sc2, sh2, g2, ln2_g[l], ln2_b[l], w_router[l], *experts, l)
        if not last:
            h_ctx = moe_block(h_ctx, csc2, csh2, cg2, ln2_g[l], ln2_b[l], w_router[l], *experts, l)
    return h_lat
```

```python
import functools

import jax
import jax.numpy as jnp
from jax import lax
from jax.experimental import pallas as pl
from jax.experimental.pallas import tpu as pltpu

D_MODEL = 1024
DEPTH = 4
GRID_W = 64
CHUNK = 64
CONV_K = 3
HEAD_DIM = D_MODEL // 8
GDN_HEADS = 4
RET_HEADS = 4
MLSTM_HEADS = 4
GDN_W = GDN_HEADS * HEAD_DIM
RET_W = RET_HEADS * HEAD_DIM
MLSTM_W = MLSTM_HEADS * HEAD_DIM
S5_CH = D_MODEL // 2
S5_GROUP = 16
S5_GROUPS = S5_CH // S5_GROUP
S5_STATE = 64
N_EXPERTS = 16
EXPERT_FF = 2 * D_MODEL
CAPACITY_FACTOR = 2
ALPHA = (2 * DEPTH) ** 0.25
EPS = 1e-5
L2_EPS = 1e-6
MASKED = -1e30
F32 = jnp.float32
BF16 = jnp.bfloat16

LANE = 128


S5_L = 16
S5_NB = S5_CH // LANE
S5_GPB = LANE // S5_GROUP
S5_SW = S5_GPB * S5_STATE
VMEM_LIMIT = 56 * 1024 * 1024


def _s5_expand_kernel(c_ref, o_ref, *, row_item_log2, col_item_log2):
    c = c_ref[...]
    nk, nc = c.shape[1], o_ref.shape[1]
    gmask = S5_GPB - 1
    k = lax.broadcasted_iota(jnp.int32, (nk, nc), 0)
    col = lax.broadcasted_iota(jnp.int32, (nk, nc), 1)
    src = ((col >> (col_item_log2 + S5_GPB.bit_length() - 1)) << col_item_log2) | (col & ((1 << col_item_log2) - 1))
    rep = jnp.where(k == src, 1.0, 0.0).astype(BF16)
    wide = jnp.dot(c, rep, preferred_element_type=F32)
    r = lax.broadcasted_iota(jnp.int32, wide.shape, 0)
    cc = lax.broadcasted_iota(jnp.int32, wide.shape, 1)
    same = ((r >> row_item_log2) & gmask) == ((cc >> col_item_log2) & gmask)
    o_ref[...] = jnp.where(same, wide, 0.0).astype(BF16)


def _s5_expand(c, row_item_log2, col_item_log2):
    nbk, rows, nk = c.shape
    return pl.pallas_call(
        functools.partial(_s5_expand_kernel, row_item_log2=row_item_log2, col_item_log2=col_item_log2),
        grid=(nbk,),
        in_specs=[pl.BlockSpec((None, rows, nk), lambda j: (j, 0, 0))],
        out_specs=pl.BlockSpec((None, rows, nk * S5_GPB), lambda j: (j, 0, 0)),
        out_shape=jax.ShapeDtypeStruct((nbk, rows, nk * S5_GPB), BF16),
        compiler_params=pltpu.CompilerParams(vmem_limit_bytes=VMEM_LIMIT),
        name="s5_expand",
    )(c)


def _s5_weights(lam_re, lam_im, log_dt, b_re, b_im, c_re, c_im):
    L, G = S5_L, S5_GROUPS
    hp = lax.Precision.HIGHEST
    taus = jnp.arange(L + 1, dtype=F32)[:, None, None]
    ks, ws, cas, ds = [], [], [], []
    for d in range(2):
        lr = jnp.minimum(lam_re[d], -1e-4)
        li = lam_im[d]
        dt = jnp.exp(log_dt[d])[:, None]
        mag = jnp.exp(lr * dt)
        ab_re, ab_im = mag * jnp.cos(li * dt), mag * jnp.sin(li * dt)
        xr, xi, den = ab_re - 1.0, ab_im, lr * lr + li * li
        f_re = (xr * lr + xi * li) / den
        f_im = (xi * lr - xr * li) / den
        bb_re = f_re[..., None] * b_re - f_im[..., None] * b_im
        bb_im = f_re[..., None] * b_im + f_im[..., None] * b_re
        pmag = jnp.exp(taus * (lr * dt))
        ar, ai = pmag * jnp.cos(taus * (li * dt)), pmag * jnp.sin(taus * (li * dt))
        wr = ar[..., None] * bb_re - ai[..., None] * bb_im
        wi = ar[..., None] * bb_im + ai[..., None] * bb_re
        k = (jnp.einsum('gop,tgpi->tgio', c_re, wr, precision=hp)
             - jnp.einsum('gop,tgpi->tgio', c_im, wi, precision=hp))
        car = c_re[None] * ar[:, :, None, :] - c_im[None] * ai[:, :, None, :]
        cai = c_re[None] * ai[:, :, None, :] + c_im[None] * ar[:, :, None, :]
        ks.append(k)
        ws.append((wr, wi))
        cas.append((car, cai))
        ds.append((ar[L], ai[L]))

    split_g = lambda t, g_axis: t.reshape(t.shape[:g_axis] + (S5_NB, S5_GPB) + t.shape[g_axis + 1:]).astype(BF16)
    compact = lambda t: t.reshape(S5_NB, L * LANE, -1)
    h_log2, p_log2 = S5_GROUP.bit_length() - 1, S5_STATE.bit_length() - 1

    kf, kb = ks
    kc = jnp.concatenate([kb[1:L][::-1], (kf[0] + kb[0])[None], kf[1:L]], axis=0)
    idx = (jnp.arange(L)[None, :] - jnp.arange(L)[:, None]) + (L - 1)
    tz = _s5_expand(compact(split_g(kc[idx], 2).transpose(2, 0, 3, 4, 1, 5)), h_log2, h_log2)

    pb_c, ca_c = [], []
    for d in range(2):
        wr, wi = ws[d]
        order = jnp.arange(L - 1, -1, -1) if d == 0 else jnp.arange(L)
        pb_c.append(jnp.stack([jnp.swapaxes(w[order], -1, -2) for w in (wr, wi)]))
        car, cai = cas[d]
        order = jnp.arange(1, L + 1) if d == 0 else jnp.arange(L, 0, -1)
        ca_c.append(jnp.stack([jnp.swapaxes(m, -1, -2) for m in (car[order], -cai[order])]))
    pb = _s5_expand(compact(split_g(jnp.stack(pb_c), 3).transpose(3, 2, 4, 5, 0, 1, 6)), h_log2, p_log2)
    ca = _s5_expand(compact(split_g(jnp.stack(ca_c), 3).transpose(3, 0, 1, 4, 5, 2, 6)), p_log2, h_log2)
    dr = jnp.stack([ds[0][0], ds[1][0]], 0).reshape(2, S5_NB, 1, S5_SW).transpose(1, 0, 2, 3).reshape(2 * S5_NB, 1, S5_SW)
    di = jnp.stack([ds[0][1], ds[1][1]], 0).reshape(2, S5_NB, 1, S5_SW).transpose(1, 0, 2, 3).reshape(2 * S5_NB, 1, S5_SW)
    return tz, pb, ca, dr, di


def _s5_p_kernel(u_ref, pb_ref, p_ref, *, nb, nc):
    res = jnp.dot(u_ref[...], pb_ref[...], preferred_element_type=F32)
    for b in range(nb):
        p_ref[:, b * 2 * S5_SW:(b + 1) * 2 * S5_SW] = res[b * nc:(b + 1) * nc]


def _s5_scan_kernel(p_ref, dr_ref, di_ref, s_ref, *, n_ctx, n_lat):
    rev = pl.program_id(0) % 2
    dr = dr_ref[...]
    di = di_ref[...]
    sw2 = 2 * S5_SW
    nbatch = p_ref.shape[1] // sw2

    def phase(base, n, carry):
        def body(step, carry):
            row = pl.ds(base + jnp.where(rev == 0, step, n - 1 - step), 1)
            p = p_ref[row, :]
            out = []
            for b, (sr, si) in enumerate(carry):
                s_ref[row, b * sw2:b * sw2 + S5_SW] = sr
                s_ref[row, b * sw2 + S5_SW:(b + 1) * sw2] = si
                out.append((dr * sr - di * si + p[:, b * sw2:b * sw2 + S5_SW],
                            dr * si + di * sr + p[:, b * sw2 + S5_SW:(b + 1) * sw2]))
            return tuple(out)
        return lax.fori_loop(0, n, body, carry)

    zero = jnp.zeros((1, S5_SW), F32)
    carry = phase(0, n_ctx, tuple((zero, zero) for _ in range(nbatch)))
    phase(n_ctx, n_lat, carry)


def _s5_y_kernel(u_ref, tz_ref, sf_ref, sb_ref, ca_ref, y_ref):
    y = jnp.dot(u_ref[...], tz_ref[...], preferred_element_type=F32)
    y += jnp.dot(sf_ref[...].astype(BF16), ca_ref[:2 * S5_SW, :], preferred_element_type=F32)
    y += jnp.dot(sb_ref[...].astype(BF16), ca_ref[2 * S5_SW:, :], preferred_element_type=F32)
    y_ref[...] = y


def s5_bidirectional(u_ctx, u_lat, weights):
    tz, pb, ca, dr, di = weights
    L = S5_L
    nb, t_ctx, _ = u_ctx.shape
    t_lat = u_lat.shape[1]
    assert t_ctx % L == 0 and t_lat % L == 0
    n_ctx, n_lat = t_ctx // L, t_lat // L
    nc = n_ctx + n_lat
    kw = L * LANE
    sw2 = 2 * S5_SW
    u = jnp.concatenate([u_ctx, u_lat], axis=1)
    ub = u.reshape(nb * nc, L, S5_NB, LANE).transpose(2, 0, 1, 3).reshape(S5_NB, nb * nc, kw).astype(BF16)

    p = pl.pallas_call(
        functools.partial(_s5_p_kernel, nb=nb, nc=nc),
        grid=(S5_NB, 2),
        in_specs=[pl.BlockSpec((None, nb * nc, kw), lambda j, d: (j, 0, 0)),
                  pl.BlockSpec((None, kw, sw2), lambda j, d: (j, 0, d))],
        out_specs=pl.BlockSpec((nc, nb * sw2), lambda j, d: (0, j * 2 + d)),
        out_shape=jax.ShapeDtypeStruct((nc, S5_NB * 2 * nb * sw2), F32),
        compiler_params=pltpu.CompilerParams(vmem_limit_bytes=VMEM_LIMIT),
        name="s5_chunk_inputs",
    )(ub, pb)

    s2 = pl.pallas_call(
        functools.partial(_s5_scan_kernel, n_ctx=n_ctx, n_lat=n_lat),
        grid=(S5_NB * 2,),
        in_specs=[pl.BlockSpec((nc, nb * sw2), lambda g: (0, g)),
                  pl.BlockSpec((None, 1, S5_SW), lambda g: (g, 0, 0)),
                  pl.BlockSpec((None, 1, S5_SW), lambda g: (g, 0, 0))],
        out_specs=pl.BlockSpec((nc, nb * sw2), lambda g: (0, g)),
        out_shape=jax.ShapeDtypeStruct(p.shape, F32),
        compiler_params=pltpu.CompilerParams(vmem_limit_bytes=VMEM_LIMIT),
        name="s5_state_scan",
    )(p, dr, di)

    yb = pl.pallas_call(
        _s5_y_kernel,
        grid=(S5_NB, nb),
        in_specs=[pl.BlockSpec((None, nc, kw), lambda j, b: (j, b, 0)),
                  pl.BlockSpec((None, kw, kw), lambda j, b: (j, 0, 0)),
                  pl.BlockSpec((nc, sw2), lambda j, b: (0, (j * 2) * nb + b)),
                  pl.BlockSpec((nc, sw2), lambda j, b: (0, (j * 2 + 1) * nb + b)),
                  pl.BlockSpec((None, 2 * sw2, kw), lambda j, b: (j, 0, 0))],
        out_specs=pl.BlockSpec((None, nc, kw), lambda j, b: (j, b, 0)),
        out_shape=jax.ShapeDtypeStruct((S5_NB, nb * nc, kw), F32),
        compiler_params=pltpu.CompilerParams(vmem_limit_bytes=VMEM_LIMIT),
        name="s5_output",
    )(ub, tz, s2, s2, ca)
    y = yb.reshape(S5_NB, nb, nc, L, LANE).transpose(1, 2, 3, 0, 4).reshape(nb, nc * L, S5_CH)
    return y[:, :t_ctx], y[:, t_ctx:]


ROUTE_TN = 1024
MOE_TN = 2048
SEL_ROWS = 128
FF_SPLIT = 4
FFN_ROWS = 1024
AFF_BITS = 30


def _route_kernel(h_ref, sc_ref, sh_ref, wr_ref, hm_ref, afft_ref, asp_ref):
    hm = h_ref[...] * (1.0 + sc_ref[...]) + sh_ref[...]
    hm_ref[...] = hm.astype(BF16)
    logits = jnp.dot(hm, wr_ref[...], precision=lax.Precision.HIGHEST, preferred_element_type=F32)
    lane = lax.broadcasted_iota(jnp.int32, logits.shape, 1)
    logits = jnp.where(lane < N_EXPERTS, logits, -jnp.inf)
    ex = jnp.exp(logits - jnp.max(logits, axis=1, keepdims=True))
    aff = ex / jnp.sum(ex, axis=1, keepdims=True)
    afft_ref[...] = aff.T[:N_EXPERTS, :]
    hi = aff.astype(BF16).astype(F32)
    mid = (aff - hi).astype(BF16).astype(F32)
    lo = (aff - hi - mid).astype(BF16).astype(F32)
    asp = hi + pltpu.roll(mid, N_EXPERTS, 1) + pltpu.roll(lo, 2 * N_EXPERTS, 1)
    asp_ref[...] = asp.astype(BF16)


def _select_kernel(aff_ref, pos_ref, *, nblk_log2, cap):
    a = aff_ref[...]
    r = a.shape[0]
    bits = pltpu.bitcast(a, jnp.int32)
    ri = lax.broadcasted_iota(jnp.int32, (r, r), 0)
    rj = lax.broadcasted_iota(jnp.int32, (r, r), 1)
    same = (ri >> nblk_log2) == (rj >> nblk_log2)
    gm = jnp.where(same, 1.0, 0.0).astype(BF16)
    lm = jnp.where(same & (rj < ri), 1.0, 0.0).astype(BF16)
    li = lax.broadcasted_iota(jnp.int32, (LANE, LANE), 0)
    lj = lax.broadcasted_iota(jnp.int32, (LANE, LANE), 1)
    um = jnp.where(li <= lj, 1.0, 0.0).astype(BF16)

    def group_count(mask):
        rc = jnp.sum(jnp.where(mask, 1.0, 0.0), axis=1, keepdims=True)
        gc = jnp.dot(gm, jnp.broadcast_to(rc, (r, LANE)).astype(BF16), preferred_element_type=F32)
        return gc[:, :1]

    def bisect(i, thr):
        cand = thr | jnp.left_shift(1, AFF_BITS - 1 - i)
        return jnp.where(group_count(bits >= cand) >= cap, cand, thr)

    thr = lax.fori_loop(0, AFF_BITS, bisect, jnp.zeros((r, 1), jnp.int32))

    def prefix(mask):
        x = jnp.where(mask, 1.0, 0.0)
        inc = jnp.dot(x.astype(BF16), um, preferred_element_type=F32)
        tot = jnp.broadcast_to(inc[:, LANE - 1:LANE], (r, LANE)).astype(BF16)
        return inc - x + jnp.dot(lm, tot, preferred_element_type=F32)

    gt = bits > thr
    eq = bits == thr
    need = cap - group_count(gt)
    sel = gt | (eq & (prefix(eq) < need))
    pos_ref[...] = jnp.where(sel, prefix(sel), -1.0).astype(jnp.int32)


def _onehot(pos_ref, cap):
    rows = lax.broadcasted_iota(jnp.int32, (cap, LANE), 0)
    blocks = [jnp.where(pos_ref[k:k + 1, :] == rows, 1.0, 0.0).astype(BF16) for k in range(pos_ref.shape[0])]
    return jnp.concatenate(blocks, axis=1)


def _ffn_kernel(pos_ref, hb_ref, asp_ref, wg_ref, wu_ref, wd_ref, ys_ref, xs_acc, g_acc, *, cap, nkt):
    e = pl.program_id(0)
    s = pl.program_id(2)

    @pl.when(s == 0)
    def _():
        xs_acc[...] = jnp.zeros_like(xs_acc)
        g_acc[...] = jnp.zeros_like(g_acc)

    rows = pl.ds(pl.multiple_of((s // nkt) * cap, cap), cap)
    oh = _onehot(pos_ref, cap)
    xs_acc[rows, :] += jnp.dot(oh, hb_ref[...], preferred_element_type=F32)
    g_acc[rows, :] += jnp.dot(oh, asp_ref[...], preferred_element_type=F32)

    @pl.when(s == pl.num_programs(2) - 1)
    def _():
        xs = xs_acc[...].astype(BF16)
        g = g_acc[...]
        lane = lax.broadcasted_iota(jnp.int32, g.shape, 1)
        gate = jnp.sum(jnp.where((lane & (N_EXPERTS - 1)) == e, g, 0.0), axis=1, keepdims=True)
        fw = EXPERT_FF // FF_SPLIT
        y = jnp.zeros(xs.shape, F32)
        for f in range(FF_SPLIT):
            hg = jnp.dot(xs, wg_ref[:, f * fw:(f + 1) * fw], preferred_element_type=F32)
            hu = jnp.dot(xs, wu_ref[:, f * fw:(f + 1) * fw], preferred_element_type=F32)
            hid = (hg * jax.nn.sigmoid(hg)) * hu
            y += jnp.dot(hid.astype(BF16), wd_ref[f * fw:(f + 1) * fw, :], preferred_element_type=F32)
        y = (y * gate).astype(BF16)
        for i in range(ys_ref.shape[0]):
            ys_ref[i] = y[i * cap:(i + 1) * cap]


def _ln_rows(z, g, b):
    mu = jnp.mean(z, axis=-1, keepdims=True)
    zc = z - mu
    var = jnp.mean(zc * zc, axis=-1, keepdims=True)
    return zc * lax.rsqrt(var + EPS) * g + b


def _combine_kernel(pos_ref, ys_ref, h_ref, g2_ref, lng_ref, lnb_ref, o_ref, acc, *, cap):
    e = pl.program_id(2)

    @pl.when(e == 0)
    def _():
        acc[...] = jnp.zeros_like(acc)

    oh = _onehot(pos_ref, cap)
    acc[...] += lax.dot_general(oh, ys_ref[...], (((0,), (0,)), ((), ())), preferred_element_type=F32)

    @pl.when(e == pl.num_programs(2) - 1)
    def _():
        z = ALPHA * h_ref[...] + g2_ref[...] * acc[...]
        o_ref[...] = _ln_rows(z, lng_ref[...], lnb_ref[...])


def moe_block(h, sc, sh, g2, ln_g, ln_b, w_router, wg, wu, wd, layer):
    nb, n, dm = h.shape
    cap = CAPACITY_FACTOR * n // N_EXPERTS
    nblk = n // LANE
    assert n % LANE == 0 and nblk & (nblk - 1) == 0
    tn_r = min(ROUTE_TN, n)
    tn = min(MOE_TN, n)
    wr = jnp.pad(w_router, ((0, 0), (0, LANE - N_EXPERTS)))
    row = lambda v: v.reshape(1, dm)

    hm, afft, asp = pl.pallas_call(
        _route_kernel,
        grid=(nb, n // tn_r),
        in_specs=[pl.BlockSpec((None, tn_r, dm), lambda b, t: (b, t, 0)),
                  pl.BlockSpec((None, 1, dm), lambda b, t: (b, 0, 0)),
                  pl.BlockSpec((None, 1, dm), lambda b, t: (b, 0, 0)),
                  pl.BlockSpec((dm, LANE), lambda b, t: (0, 0))],
        out_specs=[pl.BlockSpec((None, tn_r, dm), lambda b, t: (b, t, 0)),
                   pl.BlockSpec((None, N_EXPERTS, tn_r), lambda b, t: (b, 0, t)),
                   pl.BlockSpec((None, tn_r, LANE), lambda b, t: (b, t, 0))],
        out_shape=[jax.ShapeDtypeStruct((nb, n, dm), BF16),
                   jax.ShapeDtypeStruct((nb, N_EXPERTS, n), F32),
                   jax.ShapeDtypeStruct((nb, n, LANE), BF16)],
        name="moe_route",
    )(h, sc, sh, wr)

    rows_total = nb * N_EXPERTS * nblk
    rb = max(SEL_ROWS, N_EXPERTS * nblk)
    assert rows_total % rb == 0
    pos = pl.pallas_call(
        functools.partial(_select_kernel, nblk_log2=nblk.bit_length() - 1, cap=cap),
        grid=(rows_total // rb,),
        in_specs=[pl.BlockSpec((rb, LANE), lambda i: (i, 0))],
        out_specs=pl.BlockSpec((rb, LANE), lambda i: (i, 0)),
        out_shape=jax.ShapeDtypeStruct((rows_total, LANE), jnp.int32),
        name="moe_select",
    )(afft.reshape(rows_total, LANE))
    pos = pos.reshape(nb, N_EXPERTS, nblk, LANE)

    tb = tn // LANE
    nkt = n // tn
    group = min(nb, max(1, FFN_ROWS // cap))
    assert nb % group == 0
    sample = lambda bg, s: bg * group + s // nkt
    ys = pl.pallas_call(
        functools.partial(_ffn_kernel, cap=cap, nkt=nkt),
        grid=(N_EXPERTS, nb // group, group * nkt),
        in_specs=[pl.BlockSpec((None, None, tb, LANE), lambda e, bg, s: (sample(bg, s), e, s % nkt, 0)),
                  pl.BlockSpec((None, tn, dm), lambda e, bg, s: (sample(bg, s), s % nkt, 0)),
                  pl.BlockSpec((None, tn, LANE), lambda e, bg, s: (sample(bg, s), s % nkt, 0)),
                  pl.BlockSpec((None, None, dm, EXPERT_FF), lambda e, bg, s: (layer, e, 0, 0)),
                  pl.BlockSpec((None, None, dm, EXPERT_FF), lambda e, bg, s: (layer, e, 0, 0)),
                  pl.BlockSpec((None, None, EXPERT_FF, dm), lambda e, bg, s: (layer, e, 0, 0))],
        out_specs=pl.BlockSpec((group, None, cap, dm), lambda e, bg, s: (bg, e, 0, 0)),
        out_shape=jax.ShapeDtypeStruct((nb, N_EXPERTS, cap, dm), BF16),
        scratch_shapes=[pltpu.VMEM((group * cap, dm), F32), pltpu.VMEM((group * cap, LANE), F32)],
        compiler_params=pltpu.CompilerParams(vmem_limit_bytes=VMEM_LIMIT),
        name="moe_ffn",
    )(pos, hm, asp, wg, wu, wd)

    return pl.pallas_call(
        functools.partial(_combine_kernel, cap=cap),
        grid=(nb, n // tn, N_EXPERTS),
        in_specs=[pl.BlockSpec((None, None, tb, LANE), lambda b, t, e: (b, e, t, 0)),
                  pl.BlockSpec((None, None, cap, dm), lambda b, t, e: (b, e, 0, 0)),
                  pl.BlockSpec((None, tn, dm), lambda b, t, e: (b, t, 0)),
                  pl.BlockSpec((None, 1, dm), lambda b, t, e: (b, 0, 0)),
                  pl.BlockSpec((1, dm), lambda b, t, e: (0, 0)),
                  pl.BlockSpec((1, dm), lambda b, t, e: (0, 0))],
        out_specs=pl.BlockSpec((None, tn, dm), lambda b, t, e: (b, t, 0)),
        out_shape=jax.ShapeDtypeStruct((nb, n, dm), F32),
        scratch_shapes=[pltpu.VMEM((tn, dm), F32)],
        compiler_params=pltpu.CompilerParams(vmem_limit_bytes=VMEM_LIMIT),
        name="moe_combine",
    )(pos, ys, h, g2, row(ln_g), row(ln_b))


N_HEADS = 4
STACK = N_HEADS * CHUNK
CHUNK_LOG2 = CHUNK.bit_length() - 1


def _stack_heads(x):
    return jnp.concatenate([x[:, h * HEAD_DIM:(h + 1) * HEAD_DIM] for h in range(N_HEADS)], axis=0)


def _stack_cols(cols, width):
    return jnp.concatenate([jnp.broadcast_to(c, (CHUNK, width)) for c in cols], axis=0)


def _mxu(a, b, dims, passes):
    dg = lambda x, y: lax.dot_general(x, y, (dims, ((), ())), preferred_element_type=F32)
    a_hi, b_hi = a.astype(BF16), b.astype(BF16)
    out = dg(a_hi, b_hi)
    if passes >= 2:
        out = out + dg((a - a_hi.astype(F32)).astype(BF16), b_hi)
    if passes >= 3:
        out = out + dg(a_hi, (b - b_hi.astype(F32)).astype(BF16))
    return out


def _dot_nt(a, b, passes=1):
    return _mxu(a, b, ((1,), (1,)), passes)


def _dot_tn(a, b, passes=1):
    return _mxu(a, b, ((0,), (0,)), passes)


def _dot(a, b, passes=1):
    return _mxu(a, b, ((1,), (0,)), passes)


def _chunk_masks(rev):
    ri = lax.broadcasted_iota(jnp.int32, (STACK, STACK), 0)
    ci = lax.broadcasted_iota(jnp.int32, (STACK, STACK), 1)
    same = (ri >> CHUNK_LOG2) == (ci >> CHUNK_LOG2)
    ahead = jnp.where(rev, ci - ri, ri - ci)
    return same & (ahead >= 0), same & (ahead > 0)


def _scan_cumsum(gt, rev):
    ii = lax.broadcasted_iota(jnp.int32, (CHUNK, CHUNK), 0)
    jj = lax.broadcasted_iota(jnp.int32, (CHUNK, CHUNK), 1)
    tri = jnp.where(jnp.where(rev, jj - ii, ii - jj) >= 0, 1.0, 0.0)
    return jnp.dot(tri, gt, precision=lax.Precision.HIGHEST, preferred_element_type=F32)


def _unit_triangular_inverse(a):
    ri = lax.broadcasted_iota(jnp.int32, a.shape, 0)
    ci = lax.broadcasted_iota(jnp.int32, a.shape, 1)
    joins = lambda lvl: ((ri >> (lvl + 1)) == (ci >> (lvl + 1))) & ((ri >> lvl) != (ci >> lvl))
    t = jnp.where(ri == ci, 1.0, 0.0) - jnp.where(joins(0), a, 0.0)
    for lvl in range(1, CHUNK_LOG2):
        m = _dot(jnp.where(joins(lvl), a, 0.0), t)
        yield
        t = t - _dot(t, m)
        yield
    return t


def _interleave(chains):
    for _ in zip(*chains):
        pass


def _linear_scan_kernel(*refs, has_beta, k_scale):
    if has_beta:
        q_ref, k_ref, v_ref, gt_ref, gp_ref, s0_ref, o_ref, sfin_ref, s_scr = refs
    else:
        q_ref, k_ref, v_ref, gp_ref, s0_ref, o_ref, sfin_ref, s_scr = refs
    rev = pl.program_id(0) == 1
    n = pl.program_id(2)

    @pl.when(n == 0)
    def _():
        s_scr[...] = s0_ref[...]

    chains = []
    for i in range(q_ref.shape[0]):
        if has_beta:
            gt = _gdn_gates(gt_ref[i], gp_ref[...])
        else:
            gt = jnp.broadcast_to(gp_ref[0:1, :], (CHUNK, LANE))
        chains.append(_linear_chunk(q_ref.at[i], k_ref.at[i], v_ref.at[i], gt, o_ref.at[i], s_scr.at[i], rev,
                                    has_beta, k_scale))
    _interleave(chains)

    @pl.when(n == pl.num_programs(2) - 1)
    def _():
        sfin_ref[...] = s_scr[...]


def _softplus(x):
    return jnp.maximum(x, 0.0) + jnp.log1p(jnp.exp(-jnp.abs(x)))


def _gdn_gates(raw, gp):
    lane = lax.broadcasted_iota(jnp.int32, raw.shape, 1)
    return jnp.where(lane < 2 * N_HEADS, gp[0:1, :] * _softplus(raw + gp[1:2, :]), jax.nn.sigmoid(raw))


def _mlstm_gates(raw, gp):
    lane = lax.broadcasted_iota(jnp.int32, raw.shape, 1)
    x = raw + gp[0:1, :]
    return jnp.where(lane < 2 * N_HEADS, x, -_softplus(-x))


def _linear_chunk(q_ref, k_ref, v_ref, gt, o_ref, s_scr, rev, has_beta, k_scale):
    cum = _scan_cumsum(gt, rev)
    gcols, bcols, gtots = [], [], []
    for h in range(N_HEADS):
        gc = jnp.where(rev, cum[:, N_HEADS + h:N_HEADS + h + 1], cum[:, h:h + 1])
        gcols.append(gc)
        gtots.append(jnp.where(rev, gc[0:1], gc[CHUNK - 1:CHUNK]))
        bcols.append(jnp.where(rev, gt[:, 3 * N_HEADS + h:3 * N_HEADS + h + 1], gt[:, 2 * N_HEADS + h:2 * N_HEADS + h + 1]))
    cb = _stack_cols(gcols, STACK)
    diff = cb - cb.T
    incl, strict = _chunk_masks(rev)
    dec = jnp.exp(jnp.where(incl, diff, 0.0))
    gcb = cb[:, :HEAD_DIM]
    q_st, k_st, v_st = _stack_heads(q_ref[...]), _stack_heads(k_ref[...]) * k_scale, _stack_heads(v_ref[...])
    a_qk = _dot_nt(q_st, k_st) * jnp.where(incl, dec, 0.0)
    if has_beta:
        beta = _stack_cols(bcols, HEAD_DIM)
        kb = k_st * beta
        a = _dot_nt(kb, k_st) * jnp.where(strict, dec, 0.0)
        yield
        t_inv = yield from _unit_triangular_inverse(a)
        x = _dot(t_inv, jnp.concatenate([v_st * beta, kb * jnp.exp(gcb)], axis=1))
        u_st, w_st = x[:, :HEAD_DIM], x[:, HEAD_DIM:]
    else:
        u_st, w_st = v_st, None
    gtot = _stack_cols(gtots, HEAD_DIM)
    k_end = k_st * jnp.exp(gtot - gcb)
    q_dec = q_st * jnp.exp(gcb)
    hs = lambda t, h: t[h * CHUNK:(h + 1) * CHUNK]
    states = [s_scr[h] for h in range(N_HEADS)]
    q_s = [_dot(hs(q_dec, h), states[h]) for h in range(N_HEADS)]
    yield
    if has_beta:
        vn = jnp.concatenate([hs(u_st, h) - _dot(hs(w_st, h), states[h]) for h in range(N_HEADS)], axis=0)
    else:
        vn = u_st
    yield
    o_st = _dot(a_qk, vn)
    for h in range(N_HEADS):
        s_scr[h] = jnp.exp(gtots[h]) * states[h] + _dot_tn(hs(k_end, h), hs(vn, h))
    yield
    for h in range(N_HEADS):
        o_ref[:, h * HEAD_DIM:(h + 1) * HEAD_DIM] = hs(o_st, h) + q_s[h]
    yield


SCAN_NS = 4


HEADS_W = N_HEADS * HEAD_DIM


def _scan_specs(srcs, gates, nb, nchunk):
    ns = SCAN_NS
    assert nb % ns == 0
    cidx = lambda d, n: n + d * (nchunk - 1 - 2 * n)
    specs = [pl.BlockSpec((ns, CHUNK, HEADS_W), lambda d, b, n, c=c: (b, cidx(d, n), c)) for _, c in srcs]
    if gates is not None:
        specs.append(pl.BlockSpec((ns, CHUNK, LANE), lambda d, b, n, c=gates[1]: (b, cidx(d, n), c)))
    specs.append(pl.BlockSpec((2, LANE), lambda d, b, n: (0, 0)))
    ospec = pl.BlockSpec((None, ns, CHUNK, HEADS_W), lambda d, b, n: (d, b, cidx(d, n), 0))
    return specs, ospec


def linear_scan_bidir(q, k, v, gates, gp, s0, has_beta, k_scale=1.0):
    nb, t, _ = q[0].shape
    nchunk = t // CHUNK
    assert t % CHUNK == 0
    ns = SCAN_NS
    specs, ospec = _scan_specs((q, k, v), gates if has_beta else None, nb, nchunk)
    sspec = pl.BlockSpec((None, ns, N_HEADS, HEAD_DIM, HEAD_DIM), lambda d, b, n: (d, b, 0, 0, 0))
    args = [q[0], k[0], v[0]] + ([gates[0]] if has_beta else []) + [gp, s0]
    return pl.pallas_call(
        functools.partial(_linear_scan_kernel, has_beta=has_beta, k_scale=k_scale),
        grid=(2, nb // ns, nchunk),
        in_specs=specs + [sspec],
        out_specs=[ospec, sspec],
        out_shape=[jax.ShapeDtypeStruct((2, nb, t, HEADS_W), F32),
                   jax.ShapeDtypeStruct((2, nb, N_HEADS, HEAD_DIM, HEAD_DIM), F32)],
        scratch_shapes=[pltpu.VMEM((ns, N_HEADS, HEAD_DIM, HEAD_DIM), F32)],
        name="gdn_scan" if has_beta else "retention_scan",
    )(*args)


def linear_scan_two_pass(ctx_args, lat_args, gp, has_beta, k_scale=1.0):
    nb = ctx_args[0][0].shape[0]
    zero = jnp.zeros((2, nb, N_HEADS, HEAD_DIM, HEAD_DIM), F32)
    o_ctx, s_ctx = linear_scan_bidir(*ctx_args, gp, zero, has_beta, k_scale)
    o_lat, _ = linear_scan_bidir(*lat_args, gp, s_ctx, has_beta, k_scale)
    return o_ctx, o_lat


MLSTM_PASSES = 3


def _mlstm_scan_kernel(q_ref, k_ref, v_ref, gt_ref, gp_ref, c0_ref, n0_ref, m0_ref, o_ref, cfin_ref, nfin_ref,
                       mfin_ref, c_scr, n_scr, m_scr):
    rev = pl.program_id(0) == 1
    step = pl.program_id(2)

    @pl.when(step == 0)
    def _():
        c_scr[...] = c0_ref[...]
        n_scr[...] = n0_ref[...]
        m_scr[...] = m0_ref[...]

    _interleave([_mlstm_chunk(q_ref.at[i], k_ref.at[i], v_ref.at[i], _mlstm_gates(gt_ref[i], gp_ref[...]),
                              o_ref.at[i], c_scr.at[i], n_scr.at[i], m_scr.at[i], rev)
                 for i in range(q_ref.shape[0])])

    @pl.when(step == pl.num_programs(2) - 1)
    def _():
        cfin_ref[...] = c_scr[...]
        nfin_ref[...] = n_scr[...]
        mfin_ref[...] = m_scr[...]


def _mlstm_chunk(q_ref, k_ref, v_ref, gt, o_ref, c_scr, n_scr, m_scr, rev):
    cum = _scan_cumsum(gt, rev)
    pick = lambda t, c: jnp.where(rev, t[:, N_HEADS + c:N_HEADS + c + 1], t[:, c:c + 1])
    q_st, k_st, v_st = _stack_heads(q_ref[...]), _stack_heads(k_ref[...]), _stack_heads(v_ref[...])
    hs = lambda t, h: t[h * CHUNK:(h + 1) * CHUNK]
    qk = _dot_nt(q_st, k_st, MLSTM_PASSES)
    yield
    bcums, srcs, inters, qcs, qns = [], [], [], [], []
    for h in range(N_HEADS):
        ic = pick(gt, h)
        bcum = pick(cum, 2 * N_HEADS + h)
        b_end = jnp.where(rev, bcum[0:1], bcum[CHUNK - 1:CHUNK])
        c_prev, n_prev, m_prev = c_scr[h], n_scr[h], m_scr[h][:, :1]
        a = b_end - bcum + ic
        m_new = jnp.maximum(b_end + m_prev, jnp.max(a, axis=0, keepdims=True))
        w_state = jnp.exp(a - m_new)
        decay = jnp.exp(b_end + m_prev - m_new)
        kw = hs(k_st, h) * w_state
        c_scr[h] = decay * c_prev + _dot_tn(kw, hs(v_st, h), MLSTM_PASSES)
        n_scr[h] = decay * n_prev + jnp.sum(kw, axis=0, keepdims=True)
        m_scr[h] = jnp.broadcast_to(m_new, (1, HEAD_DIM))
        bcums.append(bcum)
        srcs.append(bcum - ic)
        inters.append(bcum + m_prev)
        qcs.append(_dot(hs(q_st, h), c_prev, MLSTM_PASSES))
        qns.append(jnp.sum(hs(q_st, h) * n_prev, axis=1, keepdims=True))
    yield
    incl, _ = _chunk_masks(rev)
    dlog = _stack_cols(bcums, STACK) - _stack_cols(srcs, STACK).T
    inter = jnp.concatenate(inters, axis=0)
    m_t = jnp.maximum(inter, jnp.max(jnp.where(incl, dlog, MASKED), axis=1, keepdims=True))
    s = qk * jnp.where(incl, jnp.exp(jnp.where(incl, dlog, 0.0) - m_t), 0.0)
    w_inter = jnp.exp(inter - m_t)
    yield
    num = _dot(s, v_st, MLSTM_PASSES) + w_inter * jnp.concatenate(qcs, axis=0)
    den = jnp.sum(s, axis=1, keepdims=True) + w_inter * jnp.concatenate(qns, axis=0)
    yield
    out = num / jnp.maximum(jnp.abs(den), jnp.exp(-m_t))
    for h in range(N_HEADS):
        o_ref[:, h * HEAD_DIM:(h + 1) * HEAD_DIM] = hs(out, h)
    yield


def mlstm_scan_bidir(q, k, v, gates, gp, state):
    nb, t, _ = q[0].shape
    nchunk = t // CHUNK
    assert t % CHUNK == 0
    ns = SCAN_NS
    specs, ospec = _scan_specs((q, k, v), gates, nb, nchunk)
    cspec = pl.BlockSpec((None, ns, N_HEADS, HEAD_DIM, HEAD_DIM), lambda d, b, n: (d, b, 0, 0, 0))
    vspec = pl.BlockSpec((None, ns, N_HEADS, 1, HEAD_DIM), lambda d, b, n: (d, b, 0, 0, 0))
    cshape = jax.ShapeDtypeStruct((2, nb, N_HEADS, HEAD_DIM, HEAD_DIM), F32)
    vshape = jax.ShapeDtypeStruct((2, nb, N_HEADS, 1, HEAD_DIM), F32)
    o, c, n, m = pl.pallas_call(
        _mlstm_scan_kernel,
        grid=(2, nb // ns, nchunk),
        in_specs=specs + [cspec, vspec, vspec],
        out_specs=[ospec, cspec, vspec, vspec],
        out_shape=[jax.ShapeDtypeStruct((2, nb, t, HEADS_W), F32), cshape, vshape, vshape],
        scratch_shapes=[pltpu.VMEM((ns, N_HEADS, HEAD_DIM, HEAD_DIM), F32), pltpu.VMEM((ns, N_HEADS, 1, HEAD_DIM), F32),
                        pltpu.VMEM((ns, N_HEADS, 1, HEAD_DIM), F32)],
        name="mlstm_scan",
    )(q[0], k[0], v[0], gates[0], gp, *state)
    return o, (c, n, m)


def mlstm_two_pass(ctx_args, lat_args, gp):
    nb = ctx_args[0][0].shape[0]
    zero = (jnp.zeros((2, nb, N_HEADS, HEAD_DIM, HEAD_DIM), F32), jnp.zeros((2, nb, N_HEADS, 1, HEAD_DIM), F32),
            jnp.zeros((2, nb, N_HEADS, 1, HEAD_DIM), F32))
    o_ctx, s_ctx = mlstm_scan_bidir(*ctx_args, gp, zero)
    o_lat, _ = mlstm_scan_bidir(*lat_args, gp, s_ctx)
    return o_ctx, o_lat


PROJ_TM = 1024
PROJ_TN_MAX = 2304


def _inproj_kernel(h_ref, sc_ref, sh_ref, w_ref, o_ref):
    hm = (h_ref[...] * (1.0 + sc_ref[...]) + sh_ref[...]).astype(BF16)
    o_ref[...] = jnp.dot(hm, w_ref[...], preferred_element_type=F32)


def in_projection(h, sc, sh, w):
    nb, n, dm = h.shape
    ncol = w.shape[1]
    tm = min(PROJ_TM, n)
    tn = max(c for c in range(LANE, min(ncol, PROJ_TN_MAX) + 1, LANE) if ncol % c == 0)
    return pl.pallas_call(
        _inproj_kernel,
        grid=(nb, n // tm, ncol // tn),
        in_specs=[pl.BlockSpec((None, tm, dm), lambda b, t, j: (b, t, 0)),
                  pl.BlockSpec((None, 1, dm), lambda b, t, j: (b, 0, 0)),
                  pl.BlockSpec((None, 1, dm), lambda b, t, j: (b, 0, 0)),
                  pl.BlockSpec((dm, tn), lambda b, t, j: (0, j))],
        out_specs=pl.BlockSpec((None, tm, tn), lambda b, t, j: (b, t, j)),
        out_shape=jax.ShapeDtypeStruct((nb, n, ncol), F32),
        name="in_projection",
    )(h, sc, sh, w)


def _conv_kernel(x_ref, w_ref, o_ref, *, grid_w, l2_blocks, q_scale, k_scale):
    x = x_ref[...]
    t = x.shape[0]
    w = w_ref[...]
    col = lax.broadcasted_iota(jnp.int32, x.shape, 0) & (grid_w - 1)
    left = jnp.where(col == 0, 0.0, pltpu.roll(x, 1, 0))
    right = jnp.where(col == grid_w - 1, 0.0, pltpu.roll(x, t - 1, 0))
    row = lambda kh: w[3 * kh:3 * kh + 1] * left + w[3 * kh + 1:3 * kh + 2] * x + w[3 * kh + 2:3 * kh + 3] * right
    acc = row(1)
    if grid_w < t:
        zero = jnp.zeros((grid_w, LANE), F32)
        acc = acc + jnp.concatenate([zero, row(0)[:t - grid_w]], axis=0) + jnp.concatenate([row(2)[grid_w:], zero], axis=0)
    y = acc * jax.nn.sigmoid(acc)
    c = pl.program_id(1)
    normed = y * lax.rsqrt(jnp.sum(y * y, axis=1, keepdims=True) + L2_EPS)
    y = jnp.where(c < l2_blocks, normed, y)
    o_ref[...] = y * jnp.where(c < N_HEADS, q_scale, jnp.where(c < 2 * N_HEADS, k_scale, 1.0))


def conv_prep(proj, conv_w, nblk, grid_w, l2_blocks, q_scale, k_scale):
    nb, t, _ = proj.shape
    assert grid_w & (grid_w - 1) == 0 and t % grid_w == 0
    return pl.pallas_call(
        functools.partial(_conv_kernel, grid_w=grid_w, l2_blocks=l2_blocks, q_scale=q_scale, k_scale=k_scale),
        grid=(nb, nblk),
        in_specs=[pl.BlockSpec((None, t, LANE), lambda b, c: (b, 0, c)),
                  pl.BlockSpec((CONV_K * CONV_K, LANE), lambda b, c: (0, c))],
        out_specs=pl.BlockSpec((None, t, LANE), lambda b, c: (b, 0, c)),
        out_shape=jax.ShapeDtypeStruct((nb, t, nblk * LANE), F32),
        compiler_params=pltpu.CompilerParams(vmem_limit_bytes=VMEM_LIMIT),
        name="conv_prep",
    )(proj, conv_w.reshape(CONV_K * CONV_K, -1))


def _head_norm(o, center):
    outs = []
    for h in range(N_HEADS):
        x = o[:, h * HEAD_DIM:(h + 1) * HEAD_DIM]
        if center:
            x = x - jnp.mean(x, axis=1, keepdims=True)
        outs.append(x * lax.rsqrt(jnp.mean(x * x, axis=1, keepdims=True) + EPS))
    return jnp.concatenate(outs, axis=1)


def _mix_out(y_a, y_b, wo_ref, h_ref, g1_ref, lng_ref, lnb_ref, o_ref):
    y = (jnp.dot(y_a.astype(BF16), wo_ref[:HEADS_W, :], preferred_element_type=F32)
         + jnp.dot(y_b.astype(BF16), wo_ref[HEADS_W:, :], preferred_element_type=F32))
    o_ref[...] = _ln_rows(ALPHA * h_ref[...] + g1_ref[...] * y, lng_ref[...], lnb_ref[...])


def _merge_even_kernel(og_ref, or_ref, za_ref, zr_ref, gg_ref, rg_ref, wo_ref, h_ref, g1_ref, lng_ref, lnb_ref, o_ref):
    za, zr = za_ref[...], zr_ref[...]
    y_g = _head_norm(og_ref[0] + og_ref[1], False) * gg_ref[...] * (za * jax.nn.sigmoid(za))
    y_r = _head_norm(or_ref[0] + or_ref[1], True) * rg_ref[...] * (zr * jax.nn.sigmoid(zr))
    _mix_out(y_g, y_r, wo_ref, h_ref, g1_ref, lng_ref, lnb_ref, o_ref)


def _merge_odd_kernel(om_ref, ys_ref, og_ref, u_ref, mg_ref, dsk_ref, bglu_ref, wglu_ref, wo_ref, h_ref, g1_ref,
                      lng_ref, lnb_ref, o_ref):
    y_m = _head_norm(om_ref[0] + om_ref[1], True) * mg_ref[...] * jax.nn.sigmoid(og_ref[...])
    y = jax.nn.gelu(ys_ref[...] + dsk_ref[...] * u_ref[...])
    y = y * jax.nn.sigmoid(jnp.dot(y.astype(BF16), wglu_ref[...], preferred_element_type=F32) + bglu_ref[...])
    _mix_out(y_m, y, wo_ref, h_ref, g1_ref, lng_ref, lnb_ref, o_ref)


def _merge_call(kernel_fn, name, scans, toks, rows, mats, h, g1, ln_g, ln_b):
    nb, n, dm = h.shape
    tm = min(PROJ_TM, n)
    full = lambda a: pl.BlockSpec(a.shape, lambda b, t: (0,) * a.ndim)
    rowv = lambda v: v.reshape(1, -1)
    specs, args = [], []
    for a in scans:
        specs.append(pl.BlockSpec((2, None, tm, HEADS_W), lambda b, t: (0, b, t, 0)))
        args.append(a)
    for a, c in toks:
        specs.append(pl.BlockSpec((None, tm, HEADS_W), lambda b, t, c=c: (b, t, c)))
        args.append(a)
    for v in rows:
        args.append(rowv(v))
        specs.append(full(args[-1]))
    for m in mats:
        args.append(m)
        specs.append(full(m))
    args += [h, g1, rowv(ln_g), rowv(ln_b)]
    specs += [pl.BlockSpec((None, tm, dm), lambda b, t: (b, t, 0)), pl.BlockSpec((None, 1, dm), lambda b, t: (b, 0, 0)),
              full(args[-2]), full(args[-1])]
    return pl.pallas_call(
        kernel_fn,
        grid=(nb, n // tm),
        in_specs=specs,
        out_specs=pl.BlockSpec((None, tm, dm), lambda b, t: (b, t, 0)),
        out_shape=jax.ShapeDtypeStruct((nb, n, dm), F32),
        compiler_params=pltpu.CompilerParams(vmem_limit_bytes=VMEM_LIMIT),
        name=name,
    )(*args)


def retention_log_decay(direction):
    expo = 5.0 + 2.0 * jnp.arange(RET_HEADS, dtype=F32) + direction
    return jnp.log1p(-jnp.exp2(-expo))


GATE_COLS = 4 * N_HEADS
Q_SCALE = HEAD_DIM ** -0.5


def _gate_row(*vals):
    v = jnp.concatenate([jnp.ravel(x) for x in vals])
    return jnp.pad(v, (0, LANE - v.shape[0]))


def _pad_gate_cols(w):
    return jnp.pad(w, ((0, 0), (0, LANE - w.shape[1])))


def gdn_retention_mixer(hs_, mods, w_in, w_out, conv_w, a_log, dt_bias, gdn_gain, ret_gain, ln_g, ln_b, ctx_out):
    w = jnp.concatenate([w_in[:, :4 * GDN_W], w_in[:, 4 * GDN_W + GATE_COLS:],
                         _pad_gate_cols(w_in[:, 4 * GDN_W:4 * GDN_W + GATE_COLS])], axis=1).astype(BF16)
    gate_blk = (4 * GDN_W + 4 * RET_W) // LANE
    gp_gdn = jnp.stack([_gate_row(-jnp.exp(a_log)), _gate_row(dt_bias)])
    gp_ret = jnp.stack([_gate_row(retention_log_decay(0), retention_log_decay(1)), jnp.zeros((LANE,), F32)])
    wo = w_out.astype(BF16)
    projs, convs = [], []
    for (h, (sc, sh, _)), grid_w in zip(zip(hs_, mods), (hs_[0].shape[1], GRID_W)):
        p = in_projection(h, sc, sh, w)
        projs.append(p)
        convs.append(conv_prep(p, conv_w, 3 * GDN_W // LANE, grid_w, 2 * N_HEADS, Q_SCALE, 1.0))
    gdn_args = [((cv, 0), (cv, 1), (cv, 2), (p, gate_blk)) for p, cv in zip(projs, convs)]
    ret_args = [((p, 4), (p, 5), (p, 6), None) for p in projs]
    og = linear_scan_two_pass(gdn_args[0], gdn_args[1], gp_gdn, True)
    orr = linear_scan_two_pass(ret_args[0], ret_args[1], gp_ret, False, Q_SCALE)
    outs = []
    for i in range(2):
        if i == 0 and not ctx_out:
            outs.append(None)
            continue
        outs.append(_merge_call(_merge_even_kernel, "merge_even", [og[i], orr[i]], [(projs[i], 3), (projs[i], 7)],
                                [jnp.tile(gdn_gain, N_HEADS), ret_gain], [wo], hs_[i], mods[i][2], ln_g, ln_b))
    return outs


def mlstm_s5_mixer(hs_, mods, w_in, w_out, conv_w, gate_bias, mlstm_gain, lam_re, lam_im, log_dt,
                   b_re, b_im, c_re, c_im, d_skip, w_glu, b_glu, ln_g, ln_b, ctx_out):
    w = jnp.concatenate([w_in[:, :4 * MLSTM_W], _pad_gate_cols(w_in[:, 4 * MLSTM_W:4 * MLSTM_W + GATE_COLS])],
                        axis=1).astype(BF16)
    w_u = w_in[:, 4 * MLSTM_W + GATE_COLS:].astype(BF16)
    gate_blk = 4 * MLSTM_W // LANE
    gp = jnp.stack([_gate_row(gate_bias[0, 0], gate_bias[1, 0], gate_bias[0, 1], gate_bias[1, 1]),
                    jnp.zeros((LANE,), F32)])
    wo, wglu = w_out.astype(BF16), w_glu.astype(BF16)
    projs, us, args = [], [], []
    for (h, (sc, sh, _)), grid_w in zip(zip(hs_, mods), (hs_[0].shape[1], GRID_W)):
        p = in_projection(h, sc, sh, w)
        cv = conv_prep(p, conv_w, 2 * MLSTM_W // LANE, grid_w, 0, 1.0, Q_SCALE)
        projs.append(p)
        us.append(in_projection(h, sc, sh, w_u))
        args.append(((cv, 0), (cv, 1), (p, 2), (p, gate_blk)))
    om = mlstm_two_pass(args[0], args[1], gp)
    ys = s5_bidirectional(us[0], us[1], _s5_weights(lam_re, lam_im, log_dt, b_re, b_im, c_re, c_im))
    outs = []
    for i in range(2):
        if i == 0 and not ctx_out:
            outs.append(None)
            continue
        outs.append(_merge_call(_merge_odd_kernel, "merge_odd", [om[i]], [(ys[i], 0), (projs[i], 3), (us[i], 0)],
                                [mlstm_gain, d_skip, b_glu], [wglu, wo], hs_[i], mods[i][2], ln_g, ln_b))
    return outs


def kernel(x, c, ctx, c_ctx, w_mod, b_mod, ln1_g, ln1_b, ln2_g, ln2_b, w_router, w_gate, w_up, w_down,
           ev_w_in, ev_w_out, ev_conv, ev_a_log, ev_dt_bias, ev_gdn_norm, ev_ret_norm,
           od_w_in, od_w_out, od_conv, od_gate_bias, od_mlstm_norm, od_lam_re, od_lam_im, od_log_dt,
           od_b_re, od_b_im, od_c_re, od_c_im, od_d_skip, od_w_glu, od_b_glu):
    h_lat, h_ctx = x, ctx
    s_lat = jax.nn.silu(c)
    s_ctx = jax.nn.silu(c_ctx)
    experts = (w_gate.astype(BF16), w_up.astype(BF16), w_down.astype(BF16))
    for l in range(DEPTH):
        last = l == DEPTH - 1
        sh1, sc1, g1, sh2, sc2, g2 = jnp.split((s_lat @ w_mod[l] + b_mod[l])[:, None, :], 6, axis=-1)
        bc = lambda v: jnp.broadcast_to(v, (h_ctx.shape[0], 1, D_MODEL))
        csh1, csc1, cg1, csh2, csc2, cg2 = [bc(v) for v in jnp.split(s_ctx @ w_mod[l] + b_mod[l], 6, axis=-1)]
        streams = (h_ctx, h_lat)
        mods = ((csc1, csh1, cg1), (sc1, sh1, g1))
        if l % 2 == 0:
            e = l // 2
            h_ctx, h_lat = gdn_retention_mixer(streams, mods, ev_w_in[e], ev_w_out[e], ev_conv[e], ev_a_log[e],
                                               ev_dt_bias[e], ev_gdn_norm[e], ev_ret_norm[e], ln1_g[l], ln1_b[l],
                                               not last)
        else:
            o = l // 2
            h_ctx, h_lat = mlstm_s5_mixer(streams, mods, od_w_in[o], od_w_out[o], od_conv[o], od_gate_bias[o],
                                          od_mlstm_norm[o], od_lam_re[o], od_lam_im[o], od_log_dt[o],
                                          od_b_re[o], od_b_im[o], od_c_re[o], od_c_im[o], od_d_skip[o],
                                          od_w_glu[o], od_b_glu[o], ln1_g[l], ln1_b[l], not last)
        h_lat = moe_block(h_lat, sc2, sh2, g2, ln2_g[l], ln2_b[l], w_router[l], *experts, l)
        if not last:
            h_ctx = moe_block(h_ctx, csc2, csh2, cg2, ln2_g[l], ln2_b[l], w_router[l], *experts, l)
    return h_lat
```
